```python
import math
import jax, jax.numpy as jnp
from jax import lax
import numpy as np

D_MODEL = 2048
BATCH = 8
SEQ = 2048
DEPTH = 1

N_MEM = 256
HEAD_DIM = 128
ROPE_THETA = 500000.0
ROT_DIM = HEAD_DIM // 4
EPS = 1e-6
NEG = -1e30
DSWA_GROUPS = ((128, 1), (512, 4), (2048, 16))
DSWA_HEADS_PER_GROUP = 2
DSWA_HEADS = DSWA_HEADS_PER_GROUP * len(DSWA_GROUPS)
DSWA_WIDTH = DSWA_HEADS * HEAD_DIM
DSWA_OUT = DSWA_HEADS_PER_GROUP * HEAD_DIM
HY_WIDTH = 3 * D_MODEL // 8
HY_ORDER = 2
HY_SHORT = 3
HY_EMB = 33
HY_FILTER_HIDDEN = 64
HY_FAST_DECAY = 0.3
HY_SLOW_DECAY = 1.5
HY_TARGET = 1e-2
MEM_HEADS = 4
MEM_WIDTH = MEM_HEADS * HEAD_DIM
N_BRANCH = 3
D_FF = 256 * ((8 * D_MODEL // 3 + 255) // 256)
IN_WIDTH = 3 * DSWA_WIDTH + (HY_ORDER + 1) * HY_WIDTH + MEM_WIDTH + N_BRANCH * D_MODEL
IN_SPLITS = (DSWA_WIDTH, 2 * DSWA_WIDTH, 3 * DSWA_WIDTH,
             3 * DSWA_WIDTH + (HY_ORDER + 1) * HY_WIDTH,
             3 * DSWA_WIDTH + (HY_ORDER + 1) * HY_WIDTH + MEM_WIDTH)

kernel_name = 'hybrid_gated_dilated_hyena_memory_encoder_layer'

F32 = jnp.float32


def rmsnorm(x, g):
    xf = x.astype(F32)
    y = xf * lax.rsqrt(jnp.mean(xf * xf, axis=-1, keepdims=True) + EPS)
    return (y * g.astype(F32)).astype(x.dtype)


def swiglu(h, w_in, w_down):
    a, b = jnp.split(h @ w_in, 2, axis=-1)
    return (jax.nn.silu(a) * b) @ w_down


def rope_tables(seq):
    inv = jnp.power(ROPE_THETA, -jnp.arange(0, ROT_DIM, 2, dtype=F32) / ROT_DIM)
    ang = jnp.arange(seq, dtype=F32)[:, None] * inv[None, :]
    return jnp.cos(ang), jnp.sin(ang)


def partial_rope(t, cos, sin):
    tf = t.astype(F32)
    half = ROT_DIM // 2
    t1, t2 = tf[..., :half], tf[..., half:ROT_DIM]
    c, s = cos[None, :, None, :], sin[None, :, None, :]
    out = jnp.concatenate([t1 * c - t2 * s, t2 * c + t1 * s, tf[..., ROT_DIM:]], axis=-1)
    return out.astype(t.dtype)


def dilated_window_attn(q, k, v, window, dilation):
    B, S, H, Dh = q.shape
    half = window // (2 * dilation)
    L = S // dilation
    nb = -(-L // half)
    Lp = nb * half

    def sub(t):
        return t.astype(F32).reshape(B, L, dilation, H, Dh).transpose(0, 2, 1, 3, 4)

    qs = jnp.pad(sub(q), ((0, 0), (0, 0), (0, Lp - L), (0, 0), (0, 0)))
    qs = qs.reshape(B, dilation, nb, half, H, Dh)

    def windows(t):
        tp = jnp.pad(sub(t), ((0, 0), (0, 0), (half, Lp - L + half), (0, 0), (0, 0)))
        tp = tp.reshape(B, dilation, nb + 2, half, H, Dh)
        return jnp.concatenate([tp[:, :, :-2], tp[:, :, 1:-1], tp[:, :, 2:]], axis=3)

    kw, vw = windows(k), windows(v)
    s = jnp.einsum('bdnqhe,bdnkhe->bdnhqk', qs, kw) / math.sqrt(Dh)
    qi = jnp.arange(half)[:, None]
    kj = jnp.arange(3 * half)[None, :]
    rel = kj - half - qi
    kpos = jnp.arange(nb)[:, None, None] * half - half + kj[None]
    valid = (jnp.abs(rel) <= half)[None] & (kpos >= 0) & (kpos < L)
    s = jnp.where(valid[None, None, :, None], s, NEG)
    m = jnp.max(s, axis=-1, keepdims=True)
    p = jnp.exp(s - m)
    den = jnp.sum(p, axis=-1, keepdims=True)
    o = jnp.einsum('bdnhqk,bdnkhe->bdnhqe', p, vw) / den
    lse = (m + jnp.log(den))[..., 0]
    o = o.transpose(0, 1, 2, 4, 3, 5).reshape(B, dilation, Lp, H, Dh)[:, :, :L]
    o = o.transpose(0, 2, 1, 3, 4).reshape(B, S, H, Dh)
    lse = lse.transpose(0, 1, 2, 4, 3).reshape(B, dilation, Lp, H)[:, :, :L]
    lse = lse.transpose(0, 2, 1, 3).reshape(B, S, H)
    return o, lse


def hyena_positional_features(L):
    bands = (HY_EMB - 1) // 2
    t = jnp.linspace(0.0, 1.0, L, dtype=F32)[:, None]
    w = 2.0 * math.pi * jnp.arange(L, dtype=F32)[:, None] / L
    f = jnp.linspace(1e-4, bands - 1, bands, dtype=F32)[None, :]
    return jnp.concatenate([t, jnp.cos(f * w), -jnp.sin(f * w)], axis=-1)


def hyena_filters(z, w1, b1, w2, b2, w3, b3, w4, freq):
    L = z.shape[0]
    fr = freq.astype(F32)
    act = lambda u: jnp.sin(fr * u)
    hh = act(z @ w1.astype(F32) + b1.astype(F32))
    hh = act(hh @ w2.astype(F32) + b2.astype(F32))
    hh = act(hh @ w3.astype(F32) + b3.astype(F32))
    h = (hh @ w4.astype(F32)).reshape(L, HY_ORDER, 2, HY_WIDTH)
    max_decay = math.log(HY_TARGET) / HY_FAST_DECAY
    min_decay = math.log(HY_TARGET) / HY_SLOW_DECAY
    deltas = jnp.linspace(min_decay, max_decay, HY_WIDTH, dtype=F32)
    t = jnp.linspace(0.0, 1.0, L, dtype=F32)[:, None]
    decay = jnp.exp(-t * jnp.abs(deltas)[None, :])
    h = h * decay[:, None, None, :]
    return h.transpose(1, 2, 0, 3)


def bidir_fftconv(u, h, bias):
    L = u.shape[1]
    n = 2 * L
    uf = u.astype(F32)
    ud = jnp.stack([uf, uf[:, ::-1]], axis=0)
    U = jnp.fft.rfft(ud, n=n, axis=2)
    Hf = jnp.fft.rfft(h, n=n, axis=1)[:, None]
    y = jnp.fft.irfft(U * Hf, n=n, axis=2)[:, :, :L]
    return y[0] + y[1][:, ::-1] + uf * bias.astype(F32)


def hyena_mixer(u, conv_w, conv_b, filters, bias):
    S = u.shape[1]
    pad = HY_SHORT // 2
    up = jnp.pad(u, ((0, 0), (pad, pad), (0, 0)))
    uc = sum(up[:, j:j + S] * conv_w[j] for j in range(HY_SHORT)) + conv_b
    parts = jnp.split(uc, HY_ORDER + 1, axis=-1)
    z = parts[0]
    for o in range(HY_ORDER):
        z = parts[o + 1].astype(F32) * bidir_fftconv(z, filters[o], bias[o])
    return z.astype(u.dtype)


def head_rmsnorm(t, g):
    tf = t.astype(F32)
    y = tf * lax.rsqrt(jnp.mean(tf * tf, axis=-1, keepdims=True) + EPS)
    return (y * g.astype(F32)).astype(t.dtype)


def mem_attention(q, mem_n, w_kv, gq, gk):
    B, S, _ = q.shape
    M = mem_n.shape[1]
    k, v = jnp.split(mem_n @ w_kv, 2, axis=-1)
    q = head_rmsnorm(q.reshape(B, S, MEM_HEADS, HEAD_DIM), gq).astype(F32)
    k = head_rmsnorm(k.reshape(B, M, MEM_HEADS, HEAD_DIM), gk).astype(F32)
    v = v.reshape(B, M, MEM_HEADS, HEAD_DIM).astype(F32)
    s = jnp.einsum('bshe,bmhe->bhsm', q, k) / math.sqrt(HEAD_DIM)
    p = jax.nn.softmax(s, axis=-1)
    o = jnp.einsum('bhsm,bmhe->bshe', p, v)
    return o.reshape(B, S, MEM_WIDTH)


def setup_inputs(seed: int = 0) -> dict:
    key = jax.random.key(seed)
    ks = iter(jax.random.split(key, 48))

    def nrm(shape, scale):
        return scale * jax.random.normal(next(ks), shape, jnp.float32)

    def gain(shape):
        return 1.0 + 0.02 * jax.random.normal(next(ks), shape, jnp.float32)

    Lr = DEPTH
    Hf = HY_FILTER_HIDDEN
    return {
        'x': nrm((BATCH, SEQ, D_MODEL), 1.0),
        'mem': nrm((BATCH, N_MEM, D_MODEL), 1.0),
        'g_ff1': gain((Lr, D_MODEL)),
        'w_ff1_in': nrm((Lr, D_MODEL, 2 * D_FF), D_MODEL ** -0.5),
        'w_ff1_out': nrm((Lr, D_FF, D_MODEL), D_FF ** -0.5),
        'g_mix': gain((Lr, D_MODEL)),
        'w_in': nrm((Lr, D_MODEL, IN_WIDTH), D_MODEL ** -0.5),
        'a_gq': gain((Lr, HEAD_DIM)),
        'a_gk': gain((Lr, HEAD_DIM)),
        'hy_conv_w': nrm((Lr, HY_SHORT, (HY_ORDER + 1) * HY_WIDTH), HY_SHORT ** -0.5),
        'hy_conv_b': nrm((Lr, (HY_ORDER + 1) * HY_WIDTH), 0.02),
        'hy_f_w1': nrm((Lr, HY_EMB, Hf), HY_EMB ** -0.5),
        'hy_f_b1': nrm((Lr, Hf), 0.02),
        'hy_f_w2': nrm((Lr, Hf, Hf), Hf ** -0.5),
        'hy_f_b2': nrm((Lr, Hf), 0.02),
        'hy_f_w3': nrm((Lr, Hf, Hf), Hf ** -0.5),
        'hy_f_b3': nrm((Lr, Hf), 0.02),
        'hy_f_w4': nrm((Lr, Hf, HY_ORDER * 2 * HY_WIDTH), 0.1 * Hf ** -0.5),
        'hy_f_freq': gain((Lr, Hf)),
        'hy_bias': nrm((Lr, HY_ORDER, HY_WIDTH), 0.1),
        'g_mem': gain((Lr, D_MODEL)),
        'w_mem_kv': nrm((Lr, D_MODEL, 2 * MEM_WIDTH), D_MODEL ** -0.5),
        'm_gq': gain((Lr, HEAD_DIM)),
        'm_gk': gain((Lr, HEAD_DIM)),
        'w_br_a': nrm((Lr, DSWA_OUT, D_MODEL), DSWA_OUT ** -0.5),
        'w_br_b': nrm((Lr, HY_WIDTH, D_MODEL), HY_WIDTH ** -0.5),
        'w_br_c': nrm((Lr, MEM_WIDTH, D_MODEL), MEM_WIDTH ** -0.5),
        'w_out': nrm((Lr, D_MODEL, D_MODEL), D_MODEL ** -0.5),
        'g_ff2': gain((Lr, D_MODEL)),
        'w_ff2_in': nrm((Lr, D_MODEL, 2 * D_FF), D_MODEL ** -0.5),
        'w_ff2_out': nrm((Lr, D_FF, D_MODEL), D_FF ** -0.5),
        'g_post': gain((Lr, D_MODEL)),
    }


def reference(x, mem, g_ff1, w_ff1_in, w_ff1_out, g_mix, w_in, a_gq, a_gk,
              hy_conv_w, hy_conv_b, hy_f_w1, hy_f_b1, hy_f_w2, hy_f_b2, hy_f_w3, hy_f_b3,
              hy_f_w4, hy_f_freq, hy_bias, g_mem, w_mem_kv, m_gq, m_gk,
              w_br_a, w_br_b, w_br_c, w_out, g_ff2, w_ff2_in, w_ff2_out, g_post):
    B, S, _ = x.shape
    cos, sin = rope_tables(S)
    hy_z = hyena_positional_features(S)
    for l in range(DEPTH):
        x = x + 0.5 * swiglu(rmsnorm(x, g_ff1[l]), w_ff1_in[l], w_ff1_out[l])

        h = rmsnorm(x, g_mix[l])
        proj = h @ w_in[l]
        a_q, a_k, a_v, hy_u, m_q, gate_logits = jnp.split(proj, IN_SPLITS, axis=-1)

        a_q = partial_rope(head_rmsnorm(a_q.reshape(B, S, DSWA_HEADS, HEAD_DIM), a_gq[l]), cos, sin)
        a_k = partial_rope(head_rmsnorm(a_k.reshape(B, S, DSWA_HEADS, HEAD_DIM), a_gk[l]), cos, sin)
        a_v = a_v.reshape(B, S, DSWA_HEADS, HEAD_DIM)
        outs, lses = [], []
        for g, (win, dil) in enumerate(DSWA_GROUPS):
            sl = slice(g * DSWA_HEADS_PER_GROUP, (g + 1) * DSWA_HEADS_PER_GROUP)
            o, lse = dilated_window_attn(a_q[:, :, sl], a_k[:, :, sl], a_v[:, :, sl], win, dil)
            outs.append(o)
            lses.append(lse)
        alpha = jax.nn.softmax(jnp.stack(lses, axis=0), axis=0)[..., None]
        y_a = jnp.sum(alpha * jnp.stack(outs, axis=0), axis=0).reshape(B, S, DSWA_OUT).astype(x.dtype)

        filt = hyena_filters(hy_z, hy_f_w1[l], hy_f_b1[l], hy_f_w2[l], hy_f_b2[l],
                             hy_f_w3[l], hy_f_b3[l], hy_f_w4[l], hy_f_freq[l])
        y_b = hyena_mixer(hy_u, hy_conv_w[l], hy_conv_b[l], filt, hy_bias[l])

        y_c = mem_attention(m_q, rmsnorm(mem, g_mem[l]), w_mem_kv[l], m_gq[l], m_gk[l]).astype(x.dtype)

        ga, gb, gc = jnp.split(jax.nn.sigmoid(gate_logits.astype(F32)), N_BRANCH, axis=-1)
        merged = ga * (y_a @ w_br_a[l]) + gb * (y_b @ w_br_b[l]) + gc * (y_c @ w_br_c[l])
        x = x + merged.astype(x.dtype) @ w_out[l]

        x = x + 0.5 * swiglu(rmsnorm(x, g_ff2[l]), w_ff2_in[l], w_ff2_out[l])
        x = rmsnorm(x, g_post[l])
    return x
```

```python
import functools
import math

import jax
import jax.numpy as jnp
from jax import lax
from jax.experimental import pallas as pl
from jax.experimental.pallas import tpu as pltpu

F32 = jnp.float32
BF16 = jnp.bfloat16

HEAD_DIM = 128
ROPE_THETA = 500000.0
ROT_DIM = HEAD_DIM // 4
EPS = 1e-6
NEG = -1e30
DSWA_GROUPS = ((128, 1), (512, 4), (2048, 16))
DSWA_HEADS_PER_GROUP = 2
MEM_HEADS = 4
HY_ORDER = 2
HY_SHORT = 3
HY_FAST_DECAY = 0.3
HY_SLOW_DECAY = 1.5
HY_TARGET = 1e-2

LANES = 128
VMEM_LIMIT_BYTES = 56 * 1024 * 1024

FFN_TM = 512
FFN_TF = 512
PROJ_TM = 1024
PROJ_TN = 512
ATT_TQ = 256
MEM_TQ = 1024
DFT_TF = 256
FILT_TW = 384
MERGE_TM = 512
MERGE_TN = 512


def _cparams(sem):
    return pltpu.CompilerParams(dimension_semantics=sem, vmem_limit_bytes=VMEM_LIMIT_BYTES)


def _rms(x, g):
    return x * lax.rsqrt(jnp.mean(x * x, axis=-1, keepdims=True) + EPS) * g


def _ffn_kernel(x_ref, g_ref, wa_ref, wb_ref, wd_ref, gp_ref, o_ref, h_ref, acc_ref, *, nj, final_norm):
    j = pl.program_id(1)

    @pl.when(j == 0)
    def _():
        h_ref[...] = _rms(x_ref[...], g_ref[...]).astype(BF16)
        acc_ref[...] = jnp.zeros_like(acc_ref)

    h = h_ref[...]
    a = jnp.dot(h, wa_ref[...], preferred_element_type=F32)
    b = jnp.dot(h, wb_ref[...], preferred_element_type=F32)
    act = (a * jax.nn.sigmoid(a) * b).astype(BF16)
    acc_ref[...] += jnp.dot(act, wd_ref[...], preferred_element_type=F32)

    @pl.when(j == nj - 1)
    def _():
        y = x_ref[...] + 0.5 * acc_ref[...]
        if final_norm:
            y = _rms(y, gp_ref[...])
        o_ref[...] = y


def _ffn(x, g, w_in, w_out, g_post, final_norm):
    t, d = x.shape
    d_ff = w_out.shape[0]
    nj = d_ff // FFN_TF
    kern = functools.partial(_ffn_kernel, nj=nj, final_norm=final_norm)
    return pl.pallas_call(
        kern,
        grid=(t // FFN_TM, nj),
        in_specs=[
            pl.BlockSpec((FFN_TM, d), lambda i, j: (i, 0)),
            pl.BlockSpec((1, d), lambda i, j: (0, 0)),
            pl.BlockSpec((d, FFN_TF), lambda i, j: (0, j)),
            pl.BlockSpec((d, FFN_TF), lambda i, j: (0, j + nj)),
            pl.BlockSpec((FFN_TF, d), lambda i, j: (j, 0)),
            pl.BlockSpec((1, d), lambda i, j: (0, 0)),
        ],
        out_specs=pl.BlockSpec((FFN_TM, d), lambda i, j: (i, 0)),
        out_shape=jax.ShapeDtypeStruct((t, d), F32),
        scratch_shapes=[pltpu.VMEM((FFN_TM, d), BF16), pltpu.VMEM((FFN_TM, d), F32)],
        compiler_params=_cparams(("parallel", "arbitrary")),
        name="ffn_final" if final_norm else "ffn",
    )(x, g, w_in, w_in, w_out, g_post)


def _mixproj_kernel(x_ref, g_ref, w_ref, gain_ref, cos_ref, sin_ref, o_ref, h_ref, *, n_rope, n_plain):
    j = pl.program_id(1)

    @pl.when(j == 0)
    def _():
        h_ref[...] = _rms(x_ref[...], g_ref[...]).astype(BF16)

    acc = jnp.dot(h_ref[...], w_ref[...], preferred_element_type=F32)
    heads = PROJ_TN // HEAD_DIM

    def head_norm(s):
        t = acc[:, s * HEAD_DIM:(s + 1) * HEAD_DIM]
        return _rms(t, gain_ref[:, s * HEAD_DIM:(s + 1) * HEAD_DIM])

    @pl.when(j < n_rope)
    def _():
        lane = lax.broadcasted_iota(jnp.int32, (PROJ_TM, HEAD_DIM), 1)
        first = lane < (ROT_DIM // 2)
        cos = cos_ref[...]
        sin = sin_ref[...]
        for s in range(heads):
            t = head_norm(s)
            partner = jnp.where(first, pltpu.roll(t, HEAD_DIM - ROT_DIM // 2, axis=1),
                                pltpu.roll(t, ROT_DIM // 2, axis=1))
            o_ref[:, s * HEAD_DIM:(s + 1) * HEAD_DIM] = (t * cos + partner * sin).astype(BF16)

    @pl.when((j >= n_rope) & (j < n_rope + n_plain))
    def _():
        o_ref[...] = acc.astype(BF16)

    @pl.when(j >= n_rope + n_plain)
    def _():
        for s in range(heads):
            o_ref[:, s * HEAD_DIM:(s + 1) * HEAD_DIM] = head_norm(s).astype(BF16)


def _mixproj(x, g, w, gain_cols, cos_t, sin_t, n_cols, n_rope, n_plain):
    t, d = x.shape
    seq = cos_t.shape[0]
    nj = n_cols // PROJ_TN
    per_seq = seq // PROJ_TM
    kern = functools.partial(_mixproj_kernel, n_rope=n_rope, n_plain=n_plain)
    return pl.pallas_call(
        kern,
        grid=(t // PROJ_TM, nj),
        in_specs=[
            pl.BlockSpec((PROJ_TM, d), lambda i, j: (i, 0)),
            pl.BlockSpec((1, d), lambda i, j: (0, 0)),
            pl.BlockSpec((d, PROJ_TN), lambda i, j: (0, j)),
            pl.BlockSpec((1, PROJ_TN), lambda i, j: (0, j)),
            pl.BlockSpec((PROJ_TM, HEAD_DIM), lambda i, j: (i % per_seq, 0)),
            pl.BlockSpec((PROJ_TM, HEAD_DIM), lambda i, j: (i % per_seq, 0)),
        ],
        out_specs=pl.BlockSpec((PROJ_TM, PROJ_TN), lambda i, j: (i, j)),
        out_shape=jax.ShapeDtypeStruct((t, n_cols), BF16),
        scratch_shapes=[pltpu.VMEM((PROJ_TM, d), BF16)],
        compiler_params=_cparams(("parallel", "arbitrary")),
        name="mixproj",
    )(x, g, w, gain_cols, cos_t, sin_t)


def _attn_windows(seq):
    out = []
    for win, _ in DSWA_GROUPS:
        reach = win // 2
        back = min(seq, -(-reach // ATT_TQ) * ATT_TQ)
        out.append(back)
    return out


def _attn_kernel(*refs, seq, backs):
    qkv = refs[:9]
    tabs = refs[9:12]
    o_ref = refs[12]
    for qb in range(seq // ATT_TQ):
        i0 = qb * ATT_TQ
        scores, values = [], []
        for g in range(3):
            q_ref, k_ref, v_ref = qkv[3 * g:3 * g + 3]
            back = backs[g]
            lo = max(0, i0 - back)
            hi = min(seq, i0 + ATT_TQ + back)
            c0 = lo - (i0 - back)
            s = lax.dot_general(q_ref[i0:i0 + ATT_TQ, :], k_ref[lo:hi, :], (((1,), (1,)), ((), ())),
                                preferred_element_type=F32)
            scores.append(s + tabs[g][:, c0:c0 + (hi - lo)])
            values.append(v_ref[lo:hi, :])
        m = scores[0].max(axis=-1, keepdims=True)
        for s in scores[1:]:
            m = jnp.maximum(m, s.max(axis=-1, keepdims=True))
        den = jnp.zeros((ATT_TQ, 1), F32)
        acc = jnp.zeros((ATT_TQ, HEAD_DIM), F32)
        for s, v in zip(scores, values):
            p = jnp.exp(s - m)
            den = den + p.sum(axis=-1, keepdims=True)
            acc = acc + jnp.dot(p.astype(BF16), v, preferred_element_type=F32)
        o_ref[i0:i0 + ATT_TQ, :] = (acc / den).astype(BF16)


def _attn_tables(seq, backs):
    tabs = []
    for (win, dil), back in zip(DSWA_GROUPS, backs):
        width = 2 * back + ATT_TQ
        row = jnp.arange(ATT_TQ, dtype=jnp.int32)[:, None]
        col = jnp.arange(width, dtype=jnp.int32)[None, :]
        rel = col - back - row
        ok = (jnp.abs(rel) <= win // 2) & ((rel & (dil - 1)) == 0)
        tabs.append(jnp.where(ok, 0.0, NEG).astype(F32))
    return tabs


def _attn(p5, batch, seq, col_q, col_k, col_v):
    backs = _attn_windows(seq)
    tabs = _attn_tables(seq, backs)
    hpg = DSWA_HEADS_PER_GROUP
    in_specs = []
    for g in range(3):
        for base in (col_q, col_k, col_v):
            in_specs.append(pl.BlockSpec((None, seq, HEAD_DIM),
                                         functools.partial(lambda b, h, c: (b, 0, c + h), c=base + g * hpg)))
    for tab in tabs:
        in_specs.append(pl.BlockSpec(tab.shape, lambda b, h: (0, 0)))
    kern = functools.partial(_attn_kernel, seq=seq, backs=tuple(backs))
    return pl.pallas_call(
        kern,
        grid=(batch, hpg),
        in_specs=in_specs,
        out_specs=pl.BlockSpec((None, seq, HEAD_DIM), lambda b, h: (b, 0, h)),
        out_shape=jax.ShapeDtypeStruct((batch, seq, hpg * HEAD_DIM), BF16),
        compiler_params=_cparams(("parallel", "parallel")),
        name="attn",
    )(*([p5] * 9), *tabs)


def _memattn_kernel(q_ref, mem_ref, g_ref, wkv_ref, gk_ref, o_ref, k_ref, v_ref):
    width = MEM_HEADS * HEAD_DIM

    @pl.when(pl.program_id(1) == 0)
    def _():
        mn = _rms(mem_ref[...], g_ref[...]).astype(BF16)
        kv = jnp.dot(mn, wkv_ref[...], preferred_element_type=F32)
        for h in range(MEM_HEADS):
            sl = slice(h * HEAD_DIM, (h + 1) * HEAD_DIM)
            k_ref[:, sl] = _rms(kv[:, sl], gk_ref[...]).astype(BF16)
        v_ref[...] = kv[:, width:].astype(BF16)

    for h in range(MEM_HEADS):
        sl = slice(h * HEAD_DIM, (h + 1) * HEAD_DIM)
        s = lax.dot_general(q_ref[:, sl], k_ref[:, sl], (((1,), (1,)), ((), ())), preferred_element_type=F32)
        p = jnp.exp(s - s.max(axis=-1, keepdims=True))
        den = p.sum(axis=-1, keepdims=True)
        o = jnp.dot(p.astype(BF16), v_ref[:, sl], preferred_element_type=F32)
        o_ref[:, sl] = (o / den).astype(BF16)


def _memattn(p5, mem, g_mem, w_kv, gk, col_block):
    batch, seq, _ = p5.shape
    n_mem, d = mem.shape[1:]
    width = MEM_HEADS * HEAD_DIM
    return pl.pallas_call(
        _memattn_kernel,
        grid=(batch, seq // MEM_TQ),
        in_specs=[
            pl.BlockSpec((None, MEM_TQ, width), lambda b, i: (b, i, col_block)),
            pl.BlockSpec((None, n_mem, d), lambda b, i: (b, 0, 0)),
            pl.BlockSpec((1, d), lambda b, i: (0, 0)),
            pl.BlockSpec((d, 2 * width), lambda b, i: (0, 0)),
            pl.BlockSpec((1, HEAD_DIM), lambda b, i: (0, 0)),
        ],
        out_specs=pl.BlockSpec((None, MEM_TQ, width), lambda b, i: (b, i, 0)),
        out_shape=jax.ShapeDtypeStruct((batch, seq, width), BF16),
        scratch_shapes=[pltpu.VMEM((n_mem, width), BF16), pltpu.VMEM((n_mem, width), BF16)],
        compiler_params=_cparams(("parallel", "arbitrary")),
        name="memattn",
    )(p5, mem, g_mem, w_kv, gk)


def _dft_kernel(f_ref, g_ref, *, seq):
    t = pl.program_id(0)
    n2 = 4 * seq
    step = 2.0 * math.pi / n2
    f_row = t * DFT_TF + lax.broadcasted_iota(jnp.int32, (DFT_TF, seq), 0)
    s_col = lax.broadcasted_iota(jnp.int32, (DFT_TF, seq), 1)
    ang = (((2 * f_row + 1) * s_col) & (n2 - 1)).astype(F32) * step
    f_ref[:DFT_TF, :] = jnp.cos(ang).astype(BF16)
    f_ref[DFT_TF:, :] = (-jnp.sin(ang)).astype(BF16)
    s_row = lax.broadcasted_iota(jnp.int32, (seq, DFT_TF), 0)
    f_col = t * DFT_TF + lax.broadcasted_iota(jnp.int32, (seq, DFT_TF), 1)
    ang_t = (((2 * f_col + 1) * s_row) & (n2 - 1)).astype(F32) * step
    g_ref[:, :DFT_TF] = jnp.cos(ang_t).astype(BF16)
    g_ref[:, DFT_TF:] = (-jnp.sin(ang_t)).astype(BF16)


def _dft_tables(seq):
    nf = seq // DFT_TF
    return pl.pallas_call(
        functools.partial(_dft_kernel, seq=seq),
        grid=(nf,),
        out_specs=[pl.BlockSpec((2 * DFT_TF, seq), lambda t: (t, 0)),
                   pl.BlockSpec((seq, 2 * DFT_TF), lambda t: (0, t))],
        out_shape=[jax.ShapeDtypeStruct((2 * seq, seq), BF16), jax.ShapeDtypeStruct((seq, 2 * seq), BF16)],
        compiler_params=_cparams(("parallel",)),
        name="dft_tables",
    )()


def _filter_mlp_kernel(z_ref, w1_ref, b1_ref, w2_ref, b2_ref, w3_ref, b3_ref, w4_ref, fr_ref, t_ref, d_ref, o_ref):
    hp = lax.Precision.HIGHEST
    fr = fr_ref[...]
    hh = jnp.sin(fr * (jnp.dot(z_ref[...], w1_ref[...], precision=hp, preferred_element_type=F32) + b1_ref[...]))
    hh = jnp.sin(fr * (jnp.dot(hh, w2_ref[...], precision=hp, preferred_element_type=F32) + b2_ref[...]))
    hh = jnp.sin(fr * (jnp.dot(hh, w3_ref[...], precision=hp, preferred_element_type=F32) + b3_ref[...]))
    h = jnp.dot(hh, w4_ref[...], precision=hp, preferred_element_type=F32)
    o_ref[...] = h * jnp.exp(-t_ref[...] * d_ref[...])


def _filter_mlp(z, w1, b1, w2, b2, w3, b3, w4, freq, t_col, absd):
    seq = z.shape[0]
    tl = 512
    n_out = w4.shape[1]
    full = lambda a: pl.BlockSpec(a.shape, lambda i: (0, 0))
    return pl.pallas_call(
        _filter_mlp_kernel,
        grid=(seq // tl,),
        in_specs=[pl.BlockSpec((tl, z.shape[1]), lambda i: (i, 0)),
                  full(w1), full(b1), full(w2), full(b2), full(w3), full(b3), full(w4), full(freq),
                  pl.BlockSpec((tl, 1), lambda i: (i, 0)), full(absd)],
        out_specs=pl.BlockSpec((tl, n_out), lambda i: (i, 0)),
        out_shape=jax.ShapeDtypeStruct((seq, n_out), F32),
        compiler_params=_cparams(("parallel",)),
        name="hyena_filter_mlp",
    )(z, w1, b1, w2, b2, w3, b3, w4, freq, t_col, absd)


def _filter_dft_kernel(hf_ref, hb_ref, bias_ref, f_ref, kre_ref, kim_ref, a_hi, a_lo, n_hi, n_lo, *, seq):
    @pl.when(pl.program_id(2) == 0)
    def _():
        hf = hf_ref[...]
        hb = hb_ref[...]
        row = lax.broadcasted_iota(jnp.int32, hf.shape, 0)
        a = hf + hb + jnp.where(row == 0, bias_ref[...], 0.0)
        nb = hf - hb
        a_hi[...] = a.astype(BF16)
        a_lo[...] = (a - a_hi[...].astype(F32)).astype(BF16)
        n_hi[...] = nb.astype(BF16)
        n_lo[...] = (nb - n_hi[...].astype(F32)).astype(BF16)

    scale = 1.0 / seq
    fc = f_ref[:DFT_TF, :]
    fs = f_ref[DFT_TF:, :]
    kre = jnp.dot(fc, a_hi[...], preferred_element_type=F32) + jnp.dot(fc, a_lo[...], preferred_element_type=F32)
    kim = jnp.dot(fs, n_hi[...], preferred_element_type=F32) + jnp.dot(fs, n_lo[...], preferred_element_type=F32)
    kre_ref[...] = kre * scale
    kim_ref[...] = kim * scale


def _filter_dft(hfilt, bias, f_tab, width):
    seq = hfilt.shape[0]
    nf = seq // DFT_TF
    kern = functools.partial(_filter_dft_kernel, seq=seq)
    out = jax.ShapeDtypeStruct((HY_ORDER, seq, width), F32)
    wt = FILT_TW
    nw = width // wt
    return pl.pallas_call(
        kern,
        grid=(HY_ORDER, nw, nf),
        in_specs=[pl.BlockSpec((seq, wt), lambda o, c, f: (0, 2 * o * nw + c)),
                  pl.BlockSpec((seq, wt), lambda o, c, f: (0, (2 * o + 1) * nw + c)),
                  pl.BlockSpec((None, 1, wt), lambda o, c, f: (o, 0, c)),
                  pl.BlockSpec((2 * DFT_TF, seq), lambda o, c, f: (f, 0))],
        out_specs=[pl.BlockSpec((None, DFT_TF, wt), lambda o, c, f: (o, f, c)),
                   pl.BlockSpec((None, DFT_TF, wt), lambda o, c, f: (o, f, c))],
        out_shape=[out, out],
        scratch_shapes=[pltpu.VMEM((seq, wt), BF16)] * 4,
        compiler_params=_cparams(("parallel", "parallel", "arbitrary")),
        name="hyena_filter_dft",
    )(hfilt, hfilt, bias, f_tab)


def _short_conv(u_ref, w_ref, b_ref, part, width):
    u = u_ref[...].astype(F32)
    seq = u.shape[0]
    row = lax.broadcasted_iota(jnp.int32, u.shape, 0)
    prev = jnp.where(row == 0, 0.0, pltpu.roll(u, 1, axis=0))
    nxt = jnp.where(row == seq - 1, 0.0, pltpu.roll(u, seq - 1, axis=0))
    sl = slice(part * width, (part + 1) * width)
    return prev * w_ref[0:1, sl] + u * w_ref[1:2, sl] + nxt * w_ref[2:3, sl] + b_ref[:, sl]


def _hyena_kernel(u0_ref, u1_ref, u2_ref, w_ref, b_ref, f_ref, g_ref, kre_ref, kim_ref, o_ref, z_ref, acc_ref,
                  *, nf, width):
    j = pl.program_id(1)

    @pl.when(j == 0)
    def _():
        z_ref[...] = _short_conv(u0_ref, w_ref, b_ref, 0, width).astype(BF16)
        acc_ref[...] = jnp.zeros_like(acc_ref)

    uv = jnp.dot(f_ref[...], z_ref[...], preferred_element_type=F32)
    re, im = uv[:DFT_TF], uv[DFT_TF:]
    kre, kim = kre_ref[...], kim_ref[...]
    y = jnp.concatenate([re * kre - im * kim, re * kim + im * kre], axis=0).astype(BF16)
    acc_ref[...] += jnp.dot(g_ref[...], y, preferred_element_type=F32)

    @pl.when(j == nf - 1)
    def _():
        z_ref[...] = (_short_conv(u1_ref, w_ref, b_ref, 1, width) * acc_ref[...]).astype(BF16)
        acc_ref[...] = jnp.zeros_like(acc_ref)

    @pl.when(j == 2 * nf - 1)
    def _():
        o_ref[...] = (_short_conv(u2_ref, w_ref, b_ref, 2, width) * acc_ref[...]).astype(BF16)


def _hyena(p5, conv_w, conv_b, f_tab, g_tab, kre, kim, col_block, width):
    batch, seq, _ = p5.shape
    nf = seq // DFT_TF
    kern = functools.partial(_hyena_kernel, nf=nf, width=width)
    u_spec = lambda part: pl.BlockSpec((None, seq, width), lambda b, j: (b, 0, col_block + part))
    return pl.pallas_call(
        kern,
        grid=(batch, HY_ORDER * nf),
        in_specs=[u_spec(0), u_spec(1), u_spec(2),
                  pl.BlockSpec(conv_w.shape, lambda b, j: (0, 0)),
                  pl.BlockSpec(conv_b.shape, lambda b, j: (0, 0)),
                  pl.BlockSpec((2 * DFT_TF, seq), lambda b, j: (j % nf, 0)),
                  pl.BlockSpec((seq, 2 * DFT_TF), lambda b, j: (0, j % nf)),
                  pl.BlockSpec((None, DFT_TF, width), lambda b, j: (j // nf, j % nf, 0)),
                  pl.BlockSpec((None, DFT_TF, width), lambda b, j: (j // nf, j % nf, 0))],
        out_specs=pl.BlockSpec((None, seq, width), lambda b, j: (b, 0, 0)),
        out_shape=jax.ShapeDtypeStruct((batch, seq, width), BF16),
        scratch_shapes=[pltpu.VMEM((seq, width), BF16), pltpu.VMEM((seq, width), F32)],
        compiler_params=_cparams(("parallel", "arbitrary")),
        name="hyena_conv",
    )(p5, p5, p5, conv_w, conv_b, f_tab, g_tab, kre, kim)


def _merge_kernel(x_ref, g_ref, ya_ref, yb_ref, yc_ref, wga_ref, wgb_ref, wgc_ref, wa_ref, wb_ref, wc_ref,
                  wo_ref, o_ref, h_ref, acc_ref, *, nj):
    j = pl.program_id(1)

    @pl.when(j == 0)
    def _():
        h_ref[...] = _rms(x_ref[...], g_ref[...]).astype(BF16)
        acc_ref[...] = jnp.zeros_like(acc_ref)

    h = h_ref[...]

    def branch(wg_ref, y_ref, w_ref):
        gate = jax.nn.sigmoid(jnp.dot(h, wg_ref[...], preferred_element_type=F32))
        return gate * jnp.dot(y_ref[...], w_ref[...], preferred_element_type=F32)

    merged = branch(wga_ref, ya_ref, wa_ref) + branch(wgb_ref, yb_ref, wb_ref) + branch(wgc_ref, yc_ref, wc_ref)
    acc_ref[...] += jnp.dot(merged.astype(BF16), wo_ref[...], preferred_element_type=F32)

    @pl.when(j == nj - 1)
    def _():
        o_ref[...] = x_ref[...] + acc_ref[...]


def _merge(x, g, ya, yb, yc, w_gate, gate_col, w_a, w_b, w_c, w_o):
    t, d = x.shape
    nj = d // MERGE_TN
    g0 = gate_col // MERGE_TN
    rows = lambda a: pl.BlockSpec((MERGE_TM, a.shape[1]), lambda i, j: (i, 0))
    cols = lambda a, off: pl.BlockSpec((a.shape[0], MERGE_TN), lambda i, j: (0, j + off))
    return pl.pallas_call(
        functools.partial(_merge_kernel, nj=nj),
        grid=(t // MERGE_TM, nj),
        in_specs=[rows(x), pl.BlockSpec((1, d), lambda i, j: (0, 0)), rows(ya), rows(yb), rows(yc),
                  cols(w_gate, g0), cols(w_gate, g0 + nj), cols(w_gate, g0 + 2 * nj),
                  cols(w_a, 0), cols(w_b, 0), cols(w_c, 0),
                  pl.BlockSpec((MERGE_TN, d), lambda i, j: (j, 0))],
        out_specs=pl.BlockSpec((MERGE_TM, d), lambda i, j: (i, 0)),
        out_shape=jax.ShapeDtypeStruct((t, d), F32),
        scratch_shapes=[pltpu.VMEM((MERGE_TM, d), BF16), pltpu.VMEM((MERGE_TM, d), F32)],
        compiler_params=_cparams(("parallel", "arbitrary")),
        name="merge",
    )(x, g, ya, yb, yc, w_gate, w_gate, w_gate, w_a, w_b, w_c, w_o)


def _rope_tables(seq):
    half = ROT_DIM // 2
    inv = jnp.power(ROPE_THETA, -jnp.arange(0, ROT_DIM, 2, dtype=F32) / ROT_DIM)
    ang = jnp.arange(seq, dtype=F32)[:, None] * inv[None, :]
    cos, sin = jnp.cos(ang), jnp.sin(ang)
    ones = jnp.ones((seq, HEAD_DIM - ROT_DIM), F32)
    cos_t = jnp.concatenate([cos, cos, ones], axis=1)
    sin_t = jnp.concatenate([-sin, sin, 0.0 * ones], axis=1)
    assert cos_t.shape == (seq, HEAD_DIM) and half * 2 == ROT_DIM
    return cos_t, sin_t


def _hyena_positional_features(seq, emb):
    bands = (emb - 1) // 2
    t = jnp.linspace(0.0, 1.0, seq, dtype=F32)[:, None]
    w = 2.0 * math.pi * jnp.arange(seq, dtype=F32)[:, None] / seq
    f = jnp.linspace(1e-4, bands - 1, bands, dtype=F32)[None, :]
    return jnp.concatenate([t, jnp.cos(f * w), -jnp.sin(f * w)], axis=-1)


def _pad_to(a, shape):
    return jnp.pad(a, [(0, s - d) for s, d in zip(shape, a.shape)])


def kernel(x, mem, g_ff1, w_ff1_in, w_ff1_out, g_mix, w_in, a_gq, a_gk, hy_conv_w, hy_conv_b, hy_f_w1, hy_f_b1,
           hy_f_w2, hy_f_b2, hy_f_w3, hy_f_b3, hy_f_w4, hy_f_freq, hy_bias, g_mem, w_mem_kv, m_gq, m_gk,
           w_br_a, w_br_b, w_br_c, w_out, g_ff2, w_ff2_in, w_ff2_out, g_post):
    batch, seq, d = x.shape
    depth = g_ff1.shape[0]
    dswa_width = len(DSWA_GROUPS) * DSWA_HEADS_PER_GROUP * HEAD_DIM
    hy_width = hy_bias.shape[-1]
    mem_width = MEM_HEADS * HEAD_DIM
    n_cols = 3 * dswa_width + (HY_ORDER + 1) * hy_width + mem_width
    n_rope = 2 * dswa_width // PROJ_TN
    n_plain = (n_cols - mem_width) // PROJ_TN - n_rope
    assert 2 * dswa_width % PROJ_TN == 0 and mem_width == PROJ_TN and n_cols % PROJ_TN == 0
    assert 3 * dswa_width % hy_width == 0 and n_cols % MERGE_TN == 0 and seq & (seq - 1) == 0
    scale = 1.0 / math.sqrt(HEAD_DIM)

    cos_t, sin_t = _rope_tables(seq)
    f_tab, g_tab = _dft_tables(seq)
    emb, hidden = hy_f_w1.shape[1:]
    z_feat = _pad_to(_hyena_positional_features(seq, emb), (seq, LANES))
    t_col = jnp.linspace(0.0, 1.0, seq, dtype=F32)[:, None]
    deltas = jnp.linspace(math.log(HY_TARGET) / HY_SLOW_DECAY, math.log(HY_TARGET) / HY_FAST_DECAY, hy_width, dtype=F32)
    absd = jnp.tile(jnp.abs(deltas)[None, :], (1, HY_ORDER * 2))
    row = lambda v: v.reshape(1, -1)

    xt = x.reshape(batch * seq, d)
    for l in range(depth):
        xt = _ffn(xt, row(g_ff1[l]), w_ff1_in[l].astype(BF16), w_ff1_out[l].astype(BF16), row(g_post[l]), False)

        heads = dswa_width // HEAD_DIM
        gain_cols = jnp.concatenate([jnp.tile(a_gq[l], heads) * scale, jnp.tile(a_gk[l], heads),
                                     jnp.ones((n_cols - 2 * dswa_width - mem_width,), F32),
                                     jnp.tile(m_gq[l], MEM_HEADS) * scale])[None, :]
        w_in_b = w_in[l].astype(BF16)
        p5 = _mixproj(xt, row(g_mix[l]), w_in_b, gain_cols, cos_t, sin_t, n_cols, n_rope, n_plain)
        p5 = p5.reshape(batch, seq, n_cols)

        blk = dswa_width // HEAD_DIM
        y_a = _attn(p5, batch, seq, 0, blk, 2 * blk)

        pad_h = lambda a, shape: _pad_to(a.astype(F32), shape)
        hfilt = _filter_mlp(z_feat, pad_h(hy_f_w1[l], (LANES, LANES)), pad_h(row(hy_f_b1[l]), (1, LANES)),
                            pad_h(hy_f_w2[l], (LANES, LANES)), pad_h(row(hy_f_b2[l]), (1, LANES)),
                            pad_h(hy_f_w3[l], (LANES, LANES)), pad_h(row(hy_f_b3[l]), (1, LANES)),
                            pad_h(hy_f_w4[l], (LANES, hy_f_w4.shape[-1])), pad_h(row(hy_f_freq[l]), (1, LANES)),
                            t_col, absd)
        kre, kim = _filter_dft(hfilt, hy_bias[l].reshape(HY_ORDER, 1, hy_width), f_tab, hy_width)
        y_b = _hyena(p5, hy_conv_w[l], row(hy_conv_b[l]), f_tab, g_tab, kre, kim, 3 * dswa_width // hy_width, hy_width)

        y_c = _memattn(p5, mem, row(g_mem[l]), w_mem_kv[l].astype(BF16), row(m_gk[l]), (n_cols - mem_width) // mem_width)

        t = batch * seq
        xt = _merge(xt, row(g_mix[l]), y_a.reshape(t, -1), y_b.reshape(t, -1), y_c.reshape(t, -1),
                    w_in_b, n_cols, w_br_a[l].astype(BF16), w_br_b[l].astype(BF16), w_br_c[l].astype(BF16),
                    w_out[l].astype(BF16))

        xt = _ffn(xt, row(g_ff2[l]), w_ff2_in[l].astype(BF16), w_ff2_out[l].astype(BF16), row(g_post[l]), True)
    return xt.reshape(batch, seq, d)
```

```python
import functools
import math

import jax
import jax.numpy as jnp
from jax import lax
from jax.experimental import pallas as pl
from jax.experimental.pallas import tpu as pltpu

F32 = jnp.float32
BF16 = jnp.bfloat16

HEAD_DIM = 128
ROPE_THETA = 500000.0
ROT_DIM = HEAD_DIM // 4
EPS = 1e-6
NEG = -1e30
DSWA_GROUPS = ((128, 1), (512, 4), (2048, 16))
DSWA_HEADS_PER_GROUP = 2
MEM_HEADS = 4
HY_ORDER = 2
HY_SHORT = 3
HY_FAST_DECAY = 0.3
HY_SLOW_DECAY = 1.5
HY_TARGET = 1e-2

LANES = 128
VMEM_LIMIT_BYTES = 56 * 1024 * 1024

FFN_TM = 1024
FFN_TF = 512
PROJ_TM = 512
PROJ_TN = 512
ATT_TQ = 256
MEM_TQ = 1024
DFT_TF = 256
FILT_TW = 384
MERGE_TM = 1024
MERGE_TN = 256


def _cparams(sem):
    return pltpu.CompilerParams(dimension_semantics=sem, vmem_limit_bytes=VMEM_LIMIT_BYTES)


def _rms(x, g):
    return x * lax.rsqrt(jnp.mean(x * x, axis=-1, keepdims=True) + EPS) * g


def _ffn_kernel(x_ref, g_ref, wa_ref, wb_ref, wd_ref, gp_ref, o_ref, h_ref, *, nj, final_norm):
    j = pl.program_id(1)

    @pl.when(j == 0)
    def _():
        x = x_ref[...]
        h_ref[...] = _rms(x, g_ref[...]).astype(BF16)
        o_ref[...] = x

    h = h_ref[...]
    a = jnp.dot(h, wa_ref[...], preferred_element_type=F32)
    b = jnp.dot(h, wb_ref[...], preferred_element_type=F32)
    act = (0.5 * a * jax.nn.sigmoid(a) * b).astype(BF16)
    o_ref[...] += jnp.dot(act, wd_ref[...], preferred_element_type=F32)

    if final_norm:
        @pl.when(j == nj - 1)
        def _():
            o_ref[...] = _rms(o_ref[...], gp_ref[...])


def _ffn(x, g, w_in, w_out, g_post, final_norm):
    t, d = x.shape
    d_ff = w_out.shape[0]
    nj = d_ff // FFN_TF
    kern = functools.partial(_ffn_kernel, nj=nj, final_norm=final_norm)
    return pl.pallas_call(
        kern,
        grid=(t // FFN_TM, nj),
        in_specs=[
            pl.BlockSpec((FFN_TM, d), lambda i, j: (i, 0), pipeline_mode=pl.Buffered(1)),
            pl.BlockSpec((1, d), lambda i, j: (0, 0)),
            pl.BlockSpec((d, FFN_TF), lambda i, j: (0, j)),
            pl.BlockSpec((d, FFN_TF), lambda i, j: (0, j + nj)),
            pl.BlockSpec((FFN_TF, d), lambda i, j: (j, 0)),
            pl.BlockSpec((1, d), lambda i, j: (0, 0)),
        ],
        out_specs=pl.BlockSpec((FFN_TM, d), lambda i, j: (i, 0)),
        out_shape=jax.ShapeDtypeStruct((t, d), F32),
        scratch_shapes=[pltpu.VMEM((FFN_TM, d), BF16)],
        compiler_params=_cparams(("parallel", "arbitrary")),
        name="ffn_final" if final_norm else "ffn",
    )(x, g, w_in, w_in, w_out, g_post)


def _mixproj_kernel(x0_ref, xn_ref, g_ref, w_ref, gain_ref, cos_ref, sin_ref, o_ref, h_ref, *, n_rope, n_plain):
    i = pl.program_id(0)
    slot = i % 2

    @pl.when(i == 0)
    def _():
        h_ref[0] = _rms(x0_ref[...], g_ref[...]).astype(BF16)

    h_ref[1 - slot] = _rms(xn_ref[...], g_ref[...]).astype(BF16)
    h = h_ref[slot]
    lane = lax.broadcasted_iota(jnp.int32, (PROJ_TM, HEAD_DIM), 1)
    first = lane < (ROT_DIM // 2)
    for j in range(w_ref.shape[1] // PROJ_TN):
        acc = jnp.dot(h, w_ref[:, j * PROJ_TN:(j + 1) * PROJ_TN], preferred_element_type=F32)
        plain = n_rope <= j < n_rope + n_plain
        for s in range(PROJ_TN // HEAD_DIM):
            sl = slice(j * PROJ_TN + s * HEAD_DIM, j * PROJ_TN + (s + 1) * HEAD_DIM)
            t = acc[:, s * HEAD_DIM:(s + 1) * HEAD_DIM]
            if not plain:
                t = _rms(t, gain_ref[:, sl])
            if j < n_rope:
                partner = jnp.where(first, pltpu.roll(t, HEAD_DIM - ROT_DIM // 2, axis=1),
                                    pltpu.roll(t, ROT_DIM // 2, axis=1))
                t = t * cos_ref[...] + partner * sin_ref[...]
            o_ref[:, sl] = t.astype(BF16)


def _mixproj(x, g, w, gain_cols, cos_t, sin_t, n_cols, n_rope, n_plain):
    t, d = x.shape
    n_row = t // PROJ_TM
    per_seq = cos_t.shape[0] // PROJ_TM
    kern = functools.partial(_mixproj_kernel, n_rope=n_rope, n_plain=n_plain)
    once = pl.Buffered(1)
    return pl.pallas_call(
        kern,
        grid=(n_row,),
        in_specs=[
            pl.BlockSpec((PROJ_TM, d), lambda i: (0, 0), pipeline_mode=once),
            pl.BlockSpec((PROJ_TM, d), lambda i: (jnp.minimum(i + 1, n_row - 1), 0)),
            pl.BlockSpec((1, d), lambda i: (0, 0)),
            pl.BlockSpec((d, n_cols), lambda i: (0, 0), pipeline_mode=once),
            pl.BlockSpec((1, n_cols), lambda i: (0, 0)),
            pl.BlockSpec((PROJ_TM, HEAD_DIM), lambda i: (i % per_seq, 0)),
            pl.BlockSpec((PROJ_TM, HEAD_DIM), lambda i: (i % per_seq, 0)),
        ],
        out_specs=pl.BlockSpec((PROJ_TM, n_cols), lambda i: (i, 0)),
        out_shape=jax.ShapeDtypeStruct((t, n_cols), BF16),
        scratch_shapes=[pltpu.VMEM((2, PROJ_TM, d), BF16)],
        compiler_params=_cparams(("arbitrary",)),
        name="mixproj",
    )(x, x, g, w, gain_cols, cos_t, sin_t)


def _attn_windows(seq):
    out = []
    for win, _ in DSWA_GROUPS:
        reach = win // 2
        back = min(seq, -(-reach // ATT_TQ) * ATT_TQ)
        out.append(back)
    return out


def _attn_kernel(*refs, seq, backs):
    qkv = refs[:9]
    tabs = refs[9:12]
    o_ref = refs[12]
    for qb in range(seq // ATT_TQ):
        i0 = qb * ATT_TQ
        scores, values = [], []
        for g in range(3):
            q_ref, k_ref, v_ref = qkv[3 * g:3 * g + 3]
            back = backs[g]
            lo = max(0, i0 - back)
            hi = min(seq, i0 + ATT_TQ + back)
            c0 = lo - (i0 - back)
            s = lax.dot_general(q_ref[i0:i0 + ATT_TQ, :], k_ref[lo:hi, :], (((1,), (1,)), ((), ())),
                                preferred_element_type=F32)
            scores.append(s + tabs[g][:, c0:c0 + (hi - lo)])
            values.append(v_ref[lo:hi, :])
        m = scores[0].max(axis=-1, keepdims=True)
        for s in scores[1:]:
            m = jnp.maximum(m, s.max(axis=-1, keepdims=True))
        den = jnp.zeros((ATT_TQ, 1), F32)
        acc = jnp.zeros((ATT_TQ, HEAD_DIM), F32)
        for s, v in zip(scores, values):
            p = jnp.exp(s - m)
            den = den + p.sum(axis=-1, keepdims=True)
            acc = acc + jnp.dot(p.astype(BF16), v, preferred_element_type=F32)
        o_ref[i0:i0 + ATT_TQ, :] = (acc / den).astype(BF16)


def _attn_tables(seq, backs):
    tabs = []
    for (win, dil), back in zip(DSWA_GROUPS, backs):
        width = 2 * back + ATT_TQ
        row = jnp.arange(ATT_TQ, dtype=jnp.int32)[:, None]
        col = jnp.arange(width, dtype=jnp.int32)[None, :]
        rel = col - back - row
        ok = (jnp.abs(rel) <= win // 2) & ((rel & (dil - 1)) == 0)
        tabs.append(jnp.where(ok, 0.0, NEG).astype(F32))
    return tabs


def _attn(p5, batch, seq, col_q, col_k, col_v):
    backs = _attn_windows(seq)
    tabs = _attn_tables(seq, backs)
    hpg = DSWA_HEADS_PER_GROUP
    in_specs = []
    for g in range(3):
        for base in (col_q, col_k, col_v):
            in_specs.append(pl.BlockSpec((None, seq, HEAD_DIM),
                                         functools.partial(lambda b, h, c: (b, 0, c + h), c=base + g * hpg)))
    for tab in tabs:
        in_specs.append(pl.BlockSpec(tab.shape, lambda b, h: (0, 0)))
    kern = functools.partial(_attn_kernel, seq=seq, backs=tuple(backs))
    return pl.pallas_call(
        kern,
        grid=(batch, hpg),
        in_specs=in_specs,
        out_specs=pl.BlockSpec((None, seq, HEAD_DIM), lambda b, h: (b, 0, h)),
        out_shape=jax.ShapeDtypeStruct((batch, seq, hpg * HEAD_DIM), BF16),
        compiler_params=_cparams(("parallel", "parallel")),
        name="attn",
    )(*([p5] * 9), *tabs)


def _memattn_kernel(q_ref, mem_ref, g_ref, wkv_ref, gk_ref, o_ref, k_ref, v_ref):
    width = MEM_HEADS * HEAD_DIM

    @pl.when(pl.program_id(1) == 0)
    def _():
        mn = _rms(mem_ref[...], g_ref[...]).astype(BF16)
        kv = jnp.dot(mn, wkv_ref[...], preferred_element_type=F32)
        for h in range(MEM_HEADS):
            sl = slice(h * HEAD_DIM, (h + 1) * HEAD_DIM)
            k_ref[:, sl] = _rms(kv[:, sl], gk_ref[...]).astype(BF16)
        v_ref[...] = kv[:, width:].astype(BF16)

    for h in range(MEM_HEADS):
        sl = slice(h * HEAD_DIM, (h + 1) * HEAD_DIM)
        s = lax.dot_general(q_ref[:, sl], k_ref[:, sl], (((1,), (1,)), ((), ())), preferred_element_type=F32)
        p = jnp.exp(s - s.max(axis=-1, keepdims=True))
        den = p.sum(axis=-1, keepdims=True)
        o = jnp.dot(p.astype(BF16), v_ref[:, sl], preferred_element_type=F32)
        o_ref[:, sl] = (o / den).astype(BF16)


def _memattn(p5, mem, g_mem, w_kv, gk, col_block):
    batch, seq, _ = p5.shape
    n_mem, d = mem.shape[1:]
    width = MEM_HEADS * HEAD_DIM
    return pl.pallas_call(
        _memattn_kernel,
        grid=(batch, seq // MEM_TQ),
        in_specs=[
            pl.BlockSpec((None, MEM_TQ, width), lambda b, i: (b, i, col_block)),
            pl.BlockSpec((None, n_mem, d), lambda b, i: (b, 0, 0)),
            pl.BlockSpec((1, d), lambda b, i: (0, 0)),
            pl.BlockSpec((d, 2 * width), lambda b, i: (0, 0)),
            pl.BlockSpec((1, HEAD_DIM), lambda b, i: (0, 0)),
        ],
        out_specs=pl.BlockSpec((None, MEM_TQ, width), lambda b, i: (b, i, 0)),
        out_shape=jax.ShapeDtypeStruct((batch, seq, width), BF16),
        scratch_shapes=[pltpu.VMEM((n_mem, width), BF16), pltpu.VMEM((n_mem, width), BF16)],
        compiler_params=_cparams(("parallel", "arbitrary")),
        name="memattn",
    )(p5, mem, g_mem, w_kv, gk)


def _dft_kernel(f_ref, g_ref, cb, sb, cbt, sbt, cac, sac, *, seq):
    t = pl.program_id(0)
    n2 = 4 * seq
    theta = 2.0 * math.pi / n2

    def trig(m):
        ang = (m & (n2 - 1)).astype(F32) * theta
        return jnp.cos(ang), jnp.sin(ang)

    @pl.when(t == 0)
    def _():
        f_lo = lax.broadcasted_iota(jnp.int32, (DFT_TF, seq), 0)
        s = lax.broadcasted_iota(jnp.int32, (DFT_TF, seq), 1)
        cb[...], sb[...] = trig((2 * f_lo + 1) * s)
        s = lax.broadcasted_iota(jnp.int32, (seq, DFT_TF), 0)
        f_lo = lax.broadcasted_iota(jnp.int32, (seq, DFT_TF), 1)
        cbt[...], sbt[...] = trig((2 * f_lo + 1) * s)
        s = lax.broadcasted_iota(jnp.int32, (seq, LANES), 0)
        tile = lax.broadcasted_iota(jnp.int32, (seq, LANES), 1)
        cac[...], sac[...] = trig(2 * DFT_TF * tile * s)

    s = lax.broadcasted_iota(jnp.int32, (1, seq), 1)
    ca, sa = trig(2 * DFT_TF * t * s)
    f_ref[:DFT_TF, :] = (ca * cb[...] - sa * sb[...]).astype(BF16)
    f_ref[DFT_TF:, :] = (-(sa * cb[...] + ca * sb[...])).astype(BF16)
    pick = lax.broadcasted_iota(jnp.int32, (seq, LANES), 1) == t
    ca = jnp.sum(jnp.where(pick, cac[...], 0.0), axis=-1, keepdims=True)
    sa = jnp.sum(jnp.where(pick, sac[...], 0.0), axis=-1, keepdims=True)
    g_ref[:, :DFT_TF] = (ca * cbt[...] - sa * sbt[...]).astype(BF16)
    g_ref[:, DFT_TF:] = (-(sa * cbt[...] + ca * sbt[...])).astype(BF16)


def _dft_tables(seq):
    nf = seq // DFT_TF
    assert nf <= LANES
    return pl.pallas_call(
        functools.partial(_dft_kernel, seq=seq),
        grid=(nf,),
        out_specs=[pl.BlockSpec((2 * DFT_TF, seq), lambda t: (t, 0)),
                   pl.BlockSpec((seq, 2 * DFT_TF), lambda t: (0, t))],
        out_shape=[jax.ShapeDtypeStruct((2 * seq, seq), BF16), jax.ShapeDtypeStruct((seq, 2 * seq), BF16)],
        scratch_shapes=[pltpu.VMEM((DFT_TF, seq), F32)] * 2 + [pltpu.VMEM((seq, DFT_TF), F32)] * 2
                       + [pltpu.VMEM((seq, LANES), F32)] * 2,
        compiler_params=_cparams(("arbitrary",)),
        name="dft_tables",
    )()


def _filter_mlp_kernel(z_ref, w1_ref, b1_ref, w2_ref, b2_ref, w3_ref, b3_ref, w4_ref, fr_ref, t_ref, d_ref, o_ref):
    hp = lax.Precision.HIGHEST
    fr = fr_ref[...]
    hh = jnp.sin(fr * (jnp.dot(z_ref[...], w1_ref[...], precision=hp, preferred_element_type=F32) + b1_ref[...]))
    hh = jnp.sin(fr * (jnp.dot(hh, w2_ref[...], precision=hp, preferred_element_type=F32) + b2_ref[...]))
    hh = jnp.sin(fr * (jnp.dot(hh, w3_ref[...], precision=hp, preferred_element_type=F32) + b3_ref[...]))
    h = jnp.dot(hh, w4_ref[...], precision=hp, preferred_element_type=F32)
    o_ref[...] = h * jnp.exp(-t_ref[...] * d_ref[...])


def _filter_mlp(z, w1, b1, w2, b2, w3, b3, w4, freq, t_col, absd):
    seq = z.shape[0]
    tl = 512
    n_out = w4.shape[1]
    full = lambda a: pl.BlockSpec(a.shape, lambda i: (0, 0))
    return pl.pallas_call(
        _filter_mlp_kernel,
        grid=(seq // tl,),
        in_specs=[pl.BlockSpec((tl, z.shape[1]), lambda i: (i, 0)),
                  full(w1), full(b1), full(w2), full(b2), full(w3), full(b3), full(w4), full(freq),
                  pl.BlockSpec((tl, 1), lambda i: (i, 0)), full(absd)],
        out_specs=pl.BlockSpec((tl, n_out), lambda i: (i, 0)),
        out_shape=jax.ShapeDtypeStruct((seq, n_out), F32),
        compiler_params=_cparams(("parallel",)),
        name="hyena_filter_mlp",
    )(z, w1, b1, w2, b2, w3, b3, w4, freq, t_col, absd)


def _filter_dft_kernel(hf_ref, hb_ref, bias_ref, f_ref, kre_ref, kim_ref, a_hi, a_lo, n_hi, n_lo, *, seq):
    @pl.when(pl.program_id(2) == 0)
    def _():
        hf = hf_ref[...]
        hb = hb_ref[...]
        row = lax.broadcasted_iota(jnp.int32, hf.shape, 0)
        a = hf + hb + jnp.where(row == 0, bias_ref[...], 0.0)
        nb = hf - hb
        a_hi[...] = a.astype(BF16)
        a_lo[...] = (a - a_hi[...].astype(F32)).astype(BF16)
        n_hi[...] = nb.astype(BF16)
        n_lo[...] = (nb - n_hi[...].astype(F32)).astype(BF16)

    scale = 1.0 / seq
    fc = f_ref[:DFT_TF, :]
    fs = f_ref[DFT_TF:, :]
    kre = jnp.dot(fc, a_hi[...], preferred_element_type=F32) + jnp.dot(fc, a_lo[...], preferred_element_type=F32)
    kim = jnp.dot(fs, n_hi[...], preferred_element_type=F32) + jnp.dot(fs, n_lo[...], preferred_element_type=F32)
    kre_ref[...] = kre * scale
    kim_ref[...] = kim * scale


def _filter_dft(hfilt, bias, f_tab, width):
    seq = hfilt.shape[0]
    nf = seq // DFT_TF
    kern = functools.partial(_filter_dft_kernel, seq=seq)
    out = jax.ShapeDtypeStruct((HY_ORDER, seq, width), F32)
    wt = FILT_TW
    nw = width // wt
    return pl.pallas_call(
        kern,
        grid=(HY_ORDER, nw, nf),
        in_specs=[pl.BlockSpec((seq, wt), lambda o, c, f: (0, 2 * o * nw + c)),
                  pl.BlockSpec((seq, wt), lambda o, c, f: (0, (2 * o + 1) * nw + c)),
                  pl.BlockSpec((None, 1, wt), lambda o, c, f: (o, 0, c)),
                  pl.BlockSpec((2 * DFT_TF, seq), lambda o, c, f: (f, 0))],
        out_specs=[pl.BlockSpec((None, DFT_TF, wt), lambda o, c, f: (o, f, c)),
                   pl.BlockSpec((None, DFT_TF, wt), lambda o, c, f: (o, f, c))],
        out_shape=[out, out],
        scratch_shapes=[pltpu.VMEM((seq, wt), BF16)] * 4,
        compiler_params=_cparams(("parallel", "parallel", "arbitrary")),
        name="hyena_filter_dft",
    )(hfilt, hfilt, bias, f_tab)


def _short_conv(u_ref, w_ref, b_ref, part, width):
    u = u_ref[...].astype(F32)
    seq = u.shape[0]
    row = lax.broadcasted_iota(jnp.int32, u.shape, 0)
    prev = jnp.where(row == 0, 0.0, pltpu.roll(u, 1, axis=0))
    nxt = jnp.where(row == seq - 1, 0.0, pltpu.roll(u, seq - 1, axis=0))
    sl = slice(part * width, (part + 1) * width)
    return prev * w_ref[0:1, sl] + u * w_ref[1:2, sl] + nxt * w_ref[2:3, sl] + b_ref[:, sl]


def _hyena_kernel(u0_ref, u1_ref, u2_ref, w_ref, b_ref, f_ref, g_ref, kre_ref, kim_ref, o_ref, z_ref, acc_ref,
                  *, nf, width):
    j = pl.program_id(1)

    @pl.when(j == 0)
    def _():
        z_ref[...] = _short_conv(u0_ref, w_ref, b_ref, 0, width).astype(BF16)
        acc_ref[...] = jnp.zeros_like(acc_ref)

    uv = jnp.dot(f_ref[...], z_ref[...], preferred_element_type=F32)
    re, im = uv[:DFT_TF], uv[DFT_TF:]
    kre, kim = kre_ref[...], kim_ref[...]
    y = jnp.concatenate([re * kre - im * kim, re * kim + im * kre], axis=0).astype(BF16)
    acc_ref[...] += jnp.dot(g_ref[...], y, preferred_element_type=F32)

    @pl.when(j == nf - 1)
    def _():
        z_ref[...] = (_short_conv(u1_ref, w_ref, b_ref, 1, width) * acc_ref[...]).astype(BF16)
        acc_ref[...] = jnp.zeros_like(acc_ref)

    @pl.when(j == 2 * nf - 1)
    def _():
        o_ref[...] = (_short_conv(u2_ref, w_ref, b_ref, 2, width) * acc_ref[...]).astype(BF16)


def _hyena(p5, conv_w, conv_b, f_tab, g_tab, kre, kim, col_block, width):
    batch, seq, _ = p5.shape
    nf = seq // DFT_TF
    kern = functools.partial(_hyena_kernel, nf=nf, width=width)
    u_spec = lambda part: pl.BlockSpec((None, seq, width), lambda b, j: (b, 0, col_block + part))
    return pl.pallas_call(
        kern,
        grid=(batch, HY_ORDER * nf),
        in_specs=[u_spec(0), u_spec(1), u_spec(2),
                  pl.BlockSpec(conv_w.shape, lambda b, j: (0, 0)),
                  pl.BlockSpec(conv_b.shape, lambda b, j: (0, 0)),
                  pl.BlockSpec((2 * DFT_TF, seq), lambda b, j: (j % nf, 0)),
                  pl.BlockSpec((seq, 2 * DFT_TF), lambda b, j: (0, j % nf)),
                  pl.BlockSpec((None, DFT_TF, width), lambda b, j: (j // nf, j % nf, 0)),
                  pl.BlockSpec((None, DFT_TF, width), lambda b, j: (j // nf, j % nf, 0))],
        out_specs=pl.BlockSpec((None, seq, width), lambda b, j: (b, 0, 0)),
        out_shape=jax.ShapeDtypeStruct((batch, seq, width), BF16),
        scratch_shapes=[pltpu.VMEM((seq, width), BF16), pltpu.VMEM((seq, width), F32)],
        compiler_params=_cparams(("parallel", "arbitrary")),
        name="hyena_conv",
    )(p5, p5, p5, conv_w, conv_b, f_tab, g_tab, kre, kim)


def _merge_kernel(x_ref, g_ref, ya_ref, yb_ref, yc_ref, wga_ref, wgb_ref, wgc_ref, wa_ref, wb_ref, wc_ref,
                  wo_ref, o_ref, h_ref, m_ref, *, nj):
    s = pl.program_id(0)
    slot = s % 2

    @pl.when(s == 0)
    def _():
        m_ref[1] = jnp.zeros(m_ref.shape[1:], BF16)
        o_ref[...] = jnp.zeros_like(o_ref)

    @pl.when(s % nj == 0)
    def _():
        h_ref[...] = _rms(x_ref[...], g_ref[...]).astype(BF16)

    first = (s - 1) % nj == 0
    base = jnp.where(first, x_ref[...], o_ref[...])
    o_ref[...] = base + jnp.dot(m_ref[1 - slot], wo_ref[...], preferred_element_type=F32)

    h = h_ref[...]

    def branch(wg_ref, y_ref, w_ref):
        gate = jax.nn.sigmoid(jnp.dot(h, wg_ref[...], preferred_element_type=F32))
        return gate * jnp.dot(y_ref[...], w_ref[...], preferred_element_type=F32)

    merged = branch(wga_ref, ya_ref, wa_ref) + branch(wgb_ref, yb_ref, wb_ref) + branch(wgc_ref, yc_ref, wc_ref)
    m_ref[slot] = merged.astype(BF16)


def _merge(x, g, ya, yb, yc, w_gate, gate_col, w_a, w_b, w_c, w_o):
    t, d = x.shape
    nj = d // MERGE_TN
    g0 = gate_col // MERGE_TN
    n = (t // MERGE_TM) * nj
    cur = lambda s: jnp.minimum(s, n - 1)
    prev = lambda s: jnp.maximum(s - 1, 0)
    rows = lambda a: pl.BlockSpec((MERGE_TM, a.shape[1]), lambda s: (cur(s) // nj, 0))
    cols = lambda a, off: pl.BlockSpec((a.shape[0], MERGE_TN), lambda s: (0, cur(s) % nj + off))
    return pl.pallas_call(
        functools.partial(_merge_kernel, nj=nj),
        grid=(n + 1,),
        in_specs=[pl.BlockSpec((MERGE_TM, d), lambda s: (cur(s) // nj, 0), pipeline_mode=pl.Buffered(1)),
                  pl.BlockSpec((1, d), lambda s: (0, 0)), rows(ya), rows(yb), rows(yc),
                  cols(w_gate, g0), cols(w_gate, g0 + nj), cols(w_gate, g0 + 2 * nj),
                  cols(w_a, 0), cols(w_b, 0), cols(w_c, 0),
                  pl.BlockSpec((MERGE_TN, d), lambda s: (prev(s) % nj, 0))],
        out_specs=pl.BlockSpec((MERGE_TM, d), lambda s: (prev(s) // nj, 0)),
        out_shape=jax.ShapeDtypeStruct((t, d), F32),
        scratch_shapes=[pltpu.VMEM((MERGE_TM, d), BF16), pltpu.VMEM((2, MERGE_TM, MERGE_TN), BF16)],
        compiler_params=_cparams(("arbitrary",)),
        name="merge",
    )(x, g, ya, yb, yc, w_gate, w_gate, w_gate, w_a, w_b, w_c, w_o)


def _rope_tables(seq):
    half = ROT_DIM // 2
    inv = jnp.power(ROPE_THETA, -jnp.arange(0, ROT_DIM, 2, dtype=F32) / ROT_DIM)
    ang = jnp.arange(seq, dtype=F32)[:, None] * inv[None, :]
    cos, sin = jnp.cos(ang), jnp.sin(ang)
    ones = jnp.ones((seq, HEAD_DIM - ROT_DIM), F32)
    cos_t = jnp.concatenate([cos, cos, ones], axis=1)
    sin_t = jnp.concatenate([-sin, sin, 0.0 * ones], axis=1)
    assert cos_t.shape == (seq, HEAD_DIM) and half * 2 == ROT_DIM
    return cos_t, sin_t


def _hyena_positional_features(seq, emb):
    bands = (emb - 1) // 2
    t = jnp.linspace(0.0, 1.0, seq, dtype=F32)[:, None]
    w = 2.0 * math.pi * jnp.arange(seq, dtype=F32)[:, None] / seq
    f = jnp.linspace(1e-4, bands - 1, bands, dtype=F32)[None, :]
    return jnp.concatenate([t, jnp.cos(f * w), -jnp.sin(f * w)], axis=-1)


def _pad_to(a, shape):
    return jnp.pad(a, [(0, s - d) for s, d in zip(shape, a.shape)])


def kernel(x, mem, g_ff1, w_ff1_in, w_ff1_out, g_mix, w_in, a_gq, a_gk, hy_conv_w, hy_conv_b, hy_f_w1, hy_f_b1,
           hy_f_w2, hy_f_b2, hy_f_w3, hy_f_b3, hy_f_w4, hy_f_freq, hy_bias, g_mem, w_mem_kv, m_gq, m_gk,
           w_br_a, w_br_b, w_br_c, w_out, g_ff2, w_ff2_in, w_ff2_out, g_post):
    batch, seq, d = x.shape
    depth = g_ff1.shape[0]
    dswa_width = len(DSWA_GROUPS) * DSWA_HEADS_PER_GROUP * HEAD_DIM
    hy_width = hy_bias.shape[-1]
    mem_width = MEM_HEADS * HEAD_DIM
    n_cols = 3 * dswa_width + (HY_ORDER + 1) * hy_width + mem_width
    n_rope = 2 * dswa_width // PROJ_TN
    n_plain = (n_cols - mem_width) // PROJ_TN - n_rope
    assert 2 * dswa_width % PROJ_TN == 0 and mem_width == PROJ_TN and n_cols % PROJ_TN == 0
    assert 3 * dswa_width % hy_width == 0 and n_cols % MERGE_TN == 0 and seq & (seq - 1) == 0
    scale = 1.0 / math.sqrt(HEAD_DIM)

    cos_t, sin_t = _rope_tables(seq)
    f_tab, g_tab = _dft_tables(seq)
    emb, hidden = hy_f_w1.shape[1:]
    z_feat = _pad_to(_hyena_positional_features(seq, emb), (seq, LANES))
    t_col = jnp.linspace(0.0, 1.0, seq, dtype=F32)[:, None]
    deltas = jnp.linspace(math.log(HY_TARGET) / HY_SLOW_DECAY, math.log(HY_TARGET) / HY_FAST_DECAY, hy_width, dtype=F32)
    absd = jnp.tile(jnp.abs(deltas)[None, :], (1, HY_ORDER * 2))
    row = lambda v: v.reshape(1, -1)

    xt = x.reshape(batch * seq, d)
    for l in range(depth):
        xt = _ffn(xt, row(g_ff1[l]), w_ff1_in[l].astype(BF16), w_ff1_out[l].astype(BF16), row(g_post[l]), False)

        heads = dswa_width // HEAD_DIM
        gain_cols = jnp.concatenate([jnp.tile(a_gq[l], heads) * scale, jnp.tile(a_gk[l], heads),
                                     jnp.ones((n_cols - 2 * dswa_width - mem_width,), F32),
                                     jnp.tile(m_gq[l], MEM_HEADS) * scale])[None, :]
        w_in_b = w_in[l].astype(BF16)
        p5 = _mixproj(xt, row(g_mix[l]), w_in_b, gain_cols, cos_t, sin_t, n_cols, n_rope, n_plain)
        p5 = p5.reshape(batch, seq, n_cols)

        blk = dswa_width // HEAD_DIM
        y_a = _attn(p5, batch, seq, 0, blk, 2 * blk)

        pad_h = lambda a, shape: _pad_to(a.astype(F32), shape)
        hfilt = _filter_mlp(z_feat, pad_h(hy_f_w1[l], (LANES, LANES)), pad_h(row(hy_f_b1[l]), (1, LANES)),
                            pad_h(hy_f_w2[l], (LANES, LANES)), pad_h(row(hy_f_b2[l]), (1, LANES)),
                            pad_h(hy_f_w3[l], (LANES, LANES)), pad_h(row(hy_f_b3[l]), (1, LANES)),
                            pad_h(hy_f_w4[l], (LANES, hy_f_w4.shape[-1])), pad_h(row(hy_f_freq[l]), (1, LANES)),
                            t_col, absd)
        kre, kim = _filter_dft(hfilt, hy_bias[l].reshape(HY_ORDER, 1, hy_width), f_tab, hy_width)
        y_b = _hyena(p5, hy_conv_w[l], row(hy_conv_b[l]), f_tab, g_tab, kre, kim, 3 * dswa_width // hy_width, hy_width)

        y_c = _memattn(p5, mem, row(g_mem[l]), w_mem_kv[l].astype(BF16), row(m_gk[l]), (n_cols - mem_width) // mem_width)

        t = batch * seq
        xt = _merge(xt, row(g_mix[l]), y_a.reshape(t, -1), y_b.reshape(t, -1), y_c.reshape(t, -1),
                    w_in_b, n_cols, w_br_a[l].astype(BF16), w_br_b[l].astype(BF16), w_br_c[l].astype(BF16),
                    w_out[l].astype(BF16))

        xt = _ffn(xt, row(g_ff2[l]), w_ff2_in[l].astype(BF16), w_ff2_out[l].astype(BF16), row(g_post[l]), True)
    return xt.reshape(batch, seq, d)
```

```python
import functools
import math

import jax
import jax.numpy as jnp
from jax import lax
from jax.experimental import pallas as pl
from jax.experimental.pallas import tpu as pltpu

F32 = jnp.float32
BF16 = jnp.bfloat16

HEAD_DIM = 128
ROPE_THETA = 500000.0
ROT_DIM = HEAD_DIM // 4
EPS = 1e-6
NEG = -1e30
DSWA_GROUPS = ((128, 1), (512, 4), (2048, 16))
DSWA_HEADS_PER_GROUP = 2
MEM_HEADS = 4
HY_ORDER = 2
HY_SHORT = 3
HY_FAST_DECAY = 0.3
HY_SLOW_DECAY = 1.5
HY_TARGET = 1e-2

LANES = 128
SUBLANES = 8
VMEM_LIMIT_BYTES = 60 * 1024 * 1024

FFN_TM = 1024
FFN_TF = 512
PROJ_TM = 512
PROJ_TN = 512
ATT_TQ = 256
MEM_TQ = 1024
DFT_TF = 256
FILT_TW = 384
MERGE_TM = 1024
MERGE_TN = 256


def _cparams(sem):
    return pltpu.CompilerParams(dimension_semantics=sem, vmem_limit_bytes=VMEM_LIMIT_BYTES)


def _rms(x, g):
    return x * lax.rsqrt(jnp.mean(x * x, axis=-1, keepdims=True) + EPS) * g


def _ffn_kernel(x_ref, g_ref, wa_ref, wb_ref, wd_ref, gp_ref, o_ref, h_ref, *, nj, final_norm):
    j = pl.program_id(1)

    @pl.when(j == 0)
    def _():
        x = x_ref[...]
        h_ref[...] = _rms(x, g_ref[...]).astype(BF16)
        o_ref[...] = x

    h = h_ref[...]
    a = jnp.dot(h, wa_ref[...], preferred_element_type=F32)
    b = jnp.dot(h, wb_ref[...], preferred_element_type=F32)
    act = (0.5 * a * jax.nn.sigmoid(a) * b).astype(BF16)
    o_ref[...] += jnp.dot(act, wd_ref[...], preferred_element_type=F32)

    if final_norm:
        @pl.when(j == nj - 1)
        def _():
            o_ref[...] = _rms(o_ref[...], gp_ref[...])


def _ffn(x, g, w_in, w_out, g_post, final_norm):
    t, d = x.shape
    d_ff = w_out.shape[0]
    nj = d_ff // FFN_TF
    kern = functools.partial(_ffn_kernel, nj=nj, final_norm=final_norm)
    return pl.pallas_call(
        kern,
        grid=(t // FFN_TM, nj),
        in_specs=[
            pl.BlockSpec((FFN_TM, d), lambda i, j: (i, 0)),
            pl.BlockSpec((1, d), lambda i, j: (0, 0)),
            pl.BlockSpec((d, FFN_TF), lambda i, j: (0, j)),
            pl.BlockSpec((d, FFN_TF), lambda i, j: (0, j + nj)),
            pl.BlockSpec((FFN_TF, d), lambda i, j: (j, 0)),
            pl.BlockSpec((1, d), lambda i, j: (0, 0)),
        ],
        out_specs=pl.BlockSpec((FFN_TM, d), lambda i, j: (i, 0)),
        out_shape=jax.ShapeDtypeStruct((t, d), F32),
        scratch_shapes=[pltpu.VMEM((FFN_TM, d), BF16)],
        compiler_params=_cparams(("parallel", "arbitrary")),
        name="ffn_final" if final_norm else "ffn",
    )(x, g, w_in, w_in, w_out, g_post)


def _mixproj_kernel(x0_ref, xn_ref, g_ref, w_ref, gain_ref, cos_ref, sin_ref, o_ref, h_ref, *, n_rope, n_plain):
    i = pl.program_id(0)
    slot = i % 2

    @pl.when(i == 0)
    def _():
        h_ref[0] = _rms(x0_ref[...], g_ref[...]).astype(BF16)

    h_ref[1 - slot] = _rms(xn_ref[...], g_ref[...]).astype(BF16)
    h = h_ref[slot]
    lane = lax.broadcasted_iota(jnp.int32, (PROJ_TM, HEAD_DIM), 1)
    first = lane < (ROT_DIM // 2)
    for j in range(w_ref.shape[1] // PROJ_TN):
        acc = jnp.dot(h, w_ref[:, j * PROJ_TN:(j + 1) * PROJ_TN], preferred_element_type=F32)
        plain = n_rope <= j < n_rope + n_plain
        for s in range(PROJ_TN // HEAD_DIM):
            sl = slice(j * PROJ_TN + s * HEAD_DIM, j * PROJ_TN + (s + 1) * HEAD_DIM)
            t = acc[:, s * HEAD_DIM:(s + 1) * HEAD_DIM]
            if not plain:
                t = _rms(t, gain_ref[:, sl])
            if j < n_rope:
                partner = jnp.where(first, pltpu.roll(t, HEAD_DIM - ROT_DIM // 2, axis=1),
                                    pltpu.roll(t, ROT_DIM // 2, axis=1))
                t = t * cos_ref[...] + partner * sin_ref[...]
            o_ref[:, sl] = t.astype(BF16)


def _mixproj(x, g, w, gain_cols, cos_t, sin_t, n_cols, n_rope, n_plain):
    t, d = x.shape
    n_row = t // PROJ_TM
    per_seq = cos_t.shape[0] // PROJ_TM
    kern = functools.partial(_mixproj_kernel, n_rope=n_rope, n_plain=n_plain)
    once = pl.Buffered(1)
    return pl.pallas_call(
        kern,
        grid=(n_row,),
        in_specs=[
            pl.BlockSpec((PROJ_TM, d), lambda i: (0, 0), pipeline_mode=once),
            pl.BlockSpec((PROJ_TM, d), lambda i: (jnp.minimum(i + 1, n_row - 1), 0)),
            pl.BlockSpec((1, d), lambda i: (0, 0)),
            pl.BlockSpec((d, n_cols), lambda i: (0, 0), pipeline_mode=once),
            pl.BlockSpec((1, n_cols), lambda i: (0, 0)),
            pl.BlockSpec((PROJ_TM, HEAD_DIM), lambda i: (i % per_seq, 0)),
            pl.BlockSpec((PROJ_TM, HEAD_DIM), lambda i: (i % per_seq, 0)),
        ],
        out_specs=pl.BlockSpec((PROJ_TM, n_cols), lambda i: (i, 0)),
        out_shape=jax.ShapeDtypeStruct((t, n_cols), BF16),
        scratch_shapes=[pltpu.VMEM((2, PROJ_TM, d), BF16)],
        compiler_params=_cparams(("arbitrary",)),
        name="mixproj",
    )(x, x, g, w, gain_cols, cos_t, sin_t)


def _attn_windows(seq):
    out = []
    for win, _ in DSWA_GROUPS:
        reach = win // 2
        back = min(seq, -(-reach // ATT_TQ) * ATT_TQ)
        out.append(back)
    return out


def _attn_kernel(*refs, seq, backs):
    qkv = refs[:9]
    tabs = refs[9:12]
    o_ref = refs[12]
    for qb in range(seq // ATT_TQ):
        i0 = qb * ATT_TQ
        scores, values = [], []
        for g in range(3):
            q_ref, k_ref, v_ref = qkv[3 * g:3 * g + 3]
            back = backs[g]
            lo = max(0, i0 - back)
            hi = min(seq, i0 + ATT_TQ + back)
            c0 = lo - (i0 - back)
            s = lax.dot_general(q_ref[i0:i0 + ATT_TQ, :], k_ref[lo:hi, :], (((1,), (1,)), ((), ())),
                                preferred_element_type=F32)
            scores.append(s + tabs[g][:, c0:c0 + (hi - lo)])
            values.append(v_ref[lo:hi, :])
        m = scores[0].max(axis=-1, keepdims=True)
        for s in scores[1:]:
            m = jnp.maximum(m, s.max(axis=-1, keepdims=True))
        den = jnp.zeros((ATT_TQ, 1), F32)
        acc = jnp.zeros((ATT_TQ, HEAD_DIM), F32)
        for s, v in zip(scores, values):
            p = jnp.exp(s - m)
            den = den + p.sum(axis=-1, keepdims=True)
            acc = acc + jnp.dot(p.astype(BF16), v, preferred_element_type=F32)
        o_ref[i0:i0 + ATT_TQ, :] = (acc / den).astype(BF16)


def _attn_tables(seq, backs):
    tabs = []
    for (win, dil), back in zip(DSWA_GROUPS, backs):
        width = 2 * back + ATT_TQ
        row = jnp.arange(ATT_TQ, dtype=jnp.int32)[:, None]
        col = jnp.arange(width, dtype=jnp.int32)[None, :]
        rel = col - back - row
        ok = (jnp.abs(rel) <= win // 2) & ((rel & (dil - 1)) == 0)
        tabs.append(jnp.where(ok, 0.0, NEG).astype(F32))
    return tabs


def _attn(p5, batch, seq, col_q, col_k, col_v):
    backs = _attn_windows(seq)
    tabs = _attn_tables(seq, backs)
    hpg = DSWA_HEADS_PER_GROUP
    in_specs = []
    for g in range(3):
        for base in (col_q, col_k, col_v):
            in_specs.append(pl.BlockSpec((None, seq, HEAD_DIM),
                                         functools.partial(lambda b, h, c: (b, 0, c + h), c=base + g * hpg)))
    for tab in tabs:
        in_specs.append(pl.BlockSpec(tab.shape, lambda b, h: (0, 0)))
    kern = functools.partial(_attn_kernel, seq=seq, backs=tuple(backs))
    return pl.pallas_call(
        kern,
        grid=(batch, hpg),
        in_specs=in_specs,
        out_specs=pl.BlockSpec((None, seq, HEAD_DIM), lambda b, h: (b, 0, h)),
        out_shape=jax.ShapeDtypeStruct((batch, seq, hpg * HEAD_DIM), BF16),
        compiler_params=_cparams(("parallel", "parallel")),
        name="attn",
    )(*([p5] * 9), *tabs)


def _memattn_kernel(q_ref, mem_ref, g_ref, wkv_ref, gk_ref, o_ref, k_ref, v_ref):
    width = MEM_HEADS * HEAD_DIM

    @pl.when(pl.program_id(1) == 0)
    def _():
        mn = _rms(mem_ref[...], g_ref[...]).astype(BF16)
        kv = jnp.dot(mn, wkv_ref[...], preferred_element_type=F32)
        for h in range(MEM_HEADS):
            sl = slice(h * HEAD_DIM, (h + 1) * HEAD_DIM)
            k_ref[:, sl] = _rms(kv[:, sl], gk_ref[...]).astype(BF16)
        v_ref[...] = kv[:, width:].astype(BF16)

    for h in range(MEM_HEADS):
        sl = slice(h * HEAD_DIM, (h + 1) * HEAD_DIM)
        s = lax.dot_general(q_ref[:, sl], k_ref[:, sl], (((1,), (1,)), ((), ())), preferred_element_type=F32)
        p = jnp.exp(s - s.max(axis=-1, keepdims=True))
        den = p.sum(axis=-1, keepdims=True)
        o = jnp.dot(p.astype(BF16), v_ref[:, sl], preferred_element_type=F32)
        o_ref[:, sl] = (o / den).astype(BF16)


def _memattn(p5, mem, g_mem, w_kv, gk, col_block):
    batch, seq, _ = p5.shape
    n_mem, d = mem.shape[1:]
    width = MEM_HEADS * HEAD_DIM
    return pl.pallas_call(
        _memattn_kernel,
        grid=(batch, seq // MEM_TQ),
        in_specs=[
            pl.BlockSpec((None, MEM_TQ, width), lambda b, i: (b, i, col_block)),
            pl.BlockSpec((None, n_mem, d), lambda b, i: (b, 0, 0)),
            pl.BlockSpec((1, d), lambda b, i: (0, 0)),
            pl.BlockSpec((d, 2 * width), lambda b, i: (0, 0)),
            pl.BlockSpec((1, HEAD_DIM), lambda b, i: (0, 0)),
        ],
        out_specs=pl.BlockSpec((None, MEM_TQ, width), lambda b, i: (b, i, 0)),
        out_shape=jax.ShapeDtypeStruct((batch, seq, width), BF16),
        scratch_shapes=[pltpu.VMEM((n_mem, width), BF16), pltpu.VMEM((n_mem, width), BF16)],
        compiler_params=_cparams(("parallel", "arbitrary")),
        name="memattn",
    )(p5, mem, g_mem, w_kv, gk)


def _dft_kernel(f_ref, g_ref, cb, sb, cbt, sbt, cac, sac, *, seq):
    t = pl.program_id(0)
    n2 = 4 * seq
    theta = 2.0 * math.pi / n2

    def trig(m):
        ang = (m & (n2 - 1)).astype(F32) * theta
        return jnp.cos(ang), jnp.sin(ang)

    @pl.when(t == 0)
    def _():
        f_lo = lax.broadcasted_iota(jnp.int32, (DFT_TF, seq), 0)
        s = lax.broadcasted_iota(jnp.int32, (DFT_TF, seq), 1)
        cb[...], sb[...] = trig((2 * f_lo + 1) * s)
        s = lax.broadcasted_iota(jnp.int32, (seq, DFT_TF), 0)
        f_lo = lax.broadcasted_iota(jnp.int32, (seq, DFT_TF), 1)
        cbt[...], sbt[...] = trig((2 * f_lo + 1) * s)
        s = lax.broadcasted_iota(jnp.int32, (seq, LANES), 0)
        tile = lax.broadcasted_iota(jnp.int32, (seq, LANES), 1)
        cac[...], sac[...] = trig(2 * DFT_TF * tile * s)

    s = lax.broadcasted_iota(jnp.int32, (1, seq), 1)
    ca, sa = trig(2 * DFT_TF * t * s)
    f_ref[:DFT_TF, :] = (ca * cb[...] - sa * sb[...]).astype(BF16)
    f_ref[DFT_TF:, :] = (-(sa * cb[...] + ca * sb[...])).astype(BF16)
    pick = lax.broadcasted_iota(jnp.int32, (seq, LANES), 1) == t
    ca = jnp.sum(jnp.where(pick, cac[...], 0.0), axis=-1, keepdims=True)
    sa = jnp.sum(jnp.where(pick, sac[...], 0.0), axis=-1, keepdims=True)
    g_ref[:, :DFT_TF] = (ca * cbt[...] - sa * sbt[...]).astype(BF16)
    g_ref[:, DFT_TF:] = (-(sa * cbt[...] + ca * sbt[...])).astype(BF16)


def _dft_tables(seq):
    nf = seq // DFT_TF
    assert nf <= LANES
    return pl.pallas_call(
        functools.partial(_dft_kernel, seq=seq),
        grid=(nf,),
        out_specs=[pl.BlockSpec((2 * DFT_TF, seq), lambda t: (t, 0)),
                   pl.BlockSpec((seq, 2 * DFT_TF), lambda t: (0, t))],
        out_shape=[jax.ShapeDtypeStruct((2 * seq, seq), BF16), jax.ShapeDtypeStruct((seq, 2 * seq), BF16)],
        scratch_shapes=[pltpu.VMEM((DFT_TF, seq), F32)] * 2 + [pltpu.VMEM((seq, DFT_TF), F32)] * 2
                       + [pltpu.VMEM((seq, LANES), F32)] * 2,
        compiler_params=_cparams(("arbitrary",)),
        name="dft_tables",
    )()


def _filter_mlp_kernel(z_ref, w1_ref, b1_ref, w2_ref, b2_ref, w3_ref, b3_ref, w4_ref, fr_ref, t_ref, d_ref, o_ref):
    hp = lax.Precision.HIGHEST
    fr = fr_ref[...]
    hh = jnp.sin(fr * (jnp.dot(z_ref[...], w1_ref[...], precision=hp, preferred_element_type=F32) + b1_ref[...]))
    hh = jnp.sin(fr * (jnp.dot(hh, w2_ref[...], precision=hp, preferred_element_type=F32) + b2_ref[...]))
    hh = jnp.sin(fr * (jnp.dot(hh, w3_ref[...], precision=hp, preferred_element_type=F32) + b3_ref[...]))
    h = jnp.dot(hh, w4_ref[...], precision=hp, preferred_element_type=F32)
    decay = jnp.exp(-t_ref[...] * d_ref[...])
    width = decay.shape[1]
    for c in range(h.shape[1] // width):
        o_ref[:, c * width:(c + 1) * width] = h[:, c * width:(c + 1) * width] * decay


def _filter_mlp(z, w1, b1, w2, b2, w3, b3, w4, freq, t_col, absd):
    seq = z.shape[0]
    tl = 512
    n_out = w4.shape[1]
    full = lambda a: pl.BlockSpec(a.shape, lambda i: (0, 0))
    return pl.pallas_call(
        _filter_mlp_kernel,
        grid=(seq // tl,),
        in_specs=[pl.BlockSpec((tl, z.shape[1]), lambda i: (i, 0)),
                  full(w1), full(b1), full(w2), full(b2), full(w3), full(b3), full(w4), full(freq),
                  pl.BlockSpec((tl, 1), lambda i: (i, 0)), full(absd)],
        out_specs=pl.BlockSpec((tl, n_out), lambda i: (i, 0)),
        out_shape=jax.ShapeDtypeStruct((seq, n_out), F32),
        compiler_params=_cparams(("parallel",)),
        name="hyena_filter_mlp",
    )(z, w1, b1, w2, b2, w3, b3, w4, freq, t_col, absd)


def _filter_dft_kernel(hf_ref, hb_ref, bias_ref, f_ref, kre_ref, kim_ref, a_hi, a_lo, n_hi, n_lo, *, seq):
    @pl.when(pl.program_id(2) == 0)
    def _():
        hf = hf_ref[...]
        hb = hb_ref[...]
        row = lax.broadcasted_iota(jnp.int32, hf.shape, 0)
        a = hf + hb + jnp.where(row == 0, bias_ref[...], 0.0)
        nb = hf - hb
        a_hi[...] = a.astype(BF16)
        a_lo[...] = (a - a_hi[...].astype(F32)).astype(BF16)
        n_hi[...] = nb.astype(BF16)
        n_lo[...] = (nb - n_hi[...].astype(F32)).astype(BF16)

    scale = 1.0 / seq
    fc = f_ref[:DFT_TF, :]
    fs = f_ref[DFT_TF:, :]
    kre = jnp.dot(fc, a_hi[...], preferred_element_type=F32) + jnp.dot(fc, a_lo[...], preferred_element_type=F32)
    kim = jnp.dot(fs, n_hi[...], preferred_element_type=F32) + jnp.dot(fs, n_lo[...], preferred_element_type=F32)
    kre_ref[...] = kre * scale
    kim_ref[...] = kim * scale


def _filter_dft(hfilt, bias, f_tab, width):
    seq = hfilt.shape[0]
    nf = seq // DFT_TF
    kern = functools.partial(_filter_dft_kernel, seq=seq)
    out = jax.ShapeDtypeStruct((HY_ORDER, seq, width), F32)
    wt = FILT_TW
    nw = width // wt
    return pl.pallas_call(
        kern,
        grid=(HY_ORDER, nw, nf),
        in_specs=[pl.BlockSpec((seq, wt), lambda o, c, f: (0, 2 * o * nw + c)),
                  pl.BlockSpec((seq, wt), lambda o, c, f: (0, (2 * o + 1) * nw + c)),
                  pl.BlockSpec((None, 1, wt), lambda o, c, f: (o, 0, c)),
                  pl.BlockSpec((2 * DFT_TF, seq), lambda o, c, f: (f, 0))],
        out_specs=[pl.BlockSpec((None, DFT_TF, wt), lambda o, c, f: (o, f, c)),
                   pl.BlockSpec((None, DFT_TF, wt), lambda o, c, f: (o, f, c))],
        out_shape=[out, out],
        scratch_shapes=[pltpu.VMEM((seq, wt), BF16)] * 4,
        compiler_params=_cparams(("parallel", "parallel", "arbitrary")),
        name="hyena_filter_dft",
    )(hfilt, hfilt, bias, f_tab)


def _short_conv(u_ref, w_ref, b_ref, part, width):
    u = u_ref[...].astype(F32)
    seq = u.shape[0]
    sl = slice(part * width, (part + 1) * width)
    w0, w1, w2, b = w_ref[0:1, sl], w_ref[1:2, sl], w_ref[2:3, sl], b_ref[:, sl]
    prev = pltpu.roll(u, 1, axis=0)
    nxt = pltpu.roll(u, seq - 1, axis=0)
    out = prev * w0 + u * w1 + nxt * w2 + b
    sub = SUBLANES
    row = lax.broadcasted_iota(jnp.int32, (sub, width), 0)
    head = jnp.where(row == 0, u[:sub] * w1 + nxt[:sub] * w2 + b, out[:sub])
    tail = jnp.where(row == sub - 1, prev[-sub:] * w0 + u[-sub:] * w1 + b, out[-sub:])
    return jnp.concatenate([head, out[sub:-sub], tail], axis=0)


def _hyena_kernel(u0_ref, u1_ref, u2_ref, w_ref, b_ref, f_ref, g_ref, kre_ref, kim_ref, o_ref, z_ref, acc_ref,
                  *, nf, width):
    j = pl.program_id(1)

    @pl.when(j == 0)
    def _():
        z_ref[...] = _short_conv(u0_ref, w_ref, b_ref, 0, width).astype(BF16)
        acc_ref[...] = jnp.zeros_like(acc_ref)

    uv = jnp.dot(f_ref[...], z_ref[...], preferred_element_type=F32)
    re, im = uv[:DFT_TF], uv[DFT_TF:]
    kre, kim = kre_ref[...], kim_ref[...]
    y = jnp.concatenate([re * kre - im * kim, re * kim + im * kre], axis=0).astype(BF16)
    acc_ref[...] += jnp.dot(g_ref[...], y, preferred_element_type=F32)

    @pl.when(j == nf - 1)
    def _():
        z_ref[...] = (_short_conv(u1_ref, w_ref, b_ref, 1, width) * acc_ref[...]).astype(BF16)
        acc_ref[...] = jnp.zeros_like(acc_ref)

    @pl.when(j == 2 * nf - 1)
    def _():
        o_ref[...] = (_short_conv(u2_ref, w_ref, b_ref, 2, width) * acc_ref[...]).astype(BF16)


def _hyena(p5, conv_w, conv_b, f_tab, g_tab, kre, kim, col_block, width):
    batch, seq, _ = p5.shape
    nf = seq // DFT_TF
    kern = functools.partial(_hyena_kernel, nf=nf, width=width)
    u_spec = lambda part: pl.BlockSpec((None, seq, width), lambda b, j: (b, 0, col_block + part))
    return pl.pallas_call(
        kern,
        grid=(batch, HY_ORDER * nf),
        in_specs=[u_spec(0), u_spec(1), u_spec(2),
                  pl.BlockSpec(conv_w.shape, lambda b, j: (0, 0)),
                  pl.BlockSpec(conv_b.shape, lambda b, j: (0, 0)),
                  pl.BlockSpec((2 * DFT_TF, seq), lambda b, j: (j % nf, 0)),
                  pl.BlockSpec((seq, 2 * DFT_TF), lambda b, j: (0, j % nf)),
                  pl.BlockSpec((None, DFT_TF, width), lambda b, j: (j // nf, j % nf, 0)),
                  pl.BlockSpec((None, DFT_TF, width), lambda b, j: (j // nf, j % nf, 0))],
        out_specs=pl.BlockSpec((None, seq, width), lambda b, j: (b, 0, 0)),
        out_shape=jax.ShapeDtypeStruct((batch, seq, width), BF16),
        scratch_shapes=[pltpu.VMEM((seq, width), BF16), pltpu.VMEM((seq, width), F32)],
        compiler_params=_cparams(("parallel", "arbitrary")),
        name="hyena_conv",
    )(p5, p5, p5, conv_w, conv_b, f_tab, g_tab, kre, kim)


def _merge_kernel(x_ref, g_ref, ya_ref, yb_ref, yc_ref, wga_ref, wgb_ref, wgc_ref, wa_ref, wb_ref, wc_ref,
                  wo_ref, o_ref, h_ref, m_ref, *, nj):
    s = pl.program_id(0)
    slot = s % 2

    @pl.when(s == 0)
    def _():
        m_ref[1] = jnp.zeros(m_ref.shape[1:], BF16)
        o_ref[...] = jnp.zeros_like(o_ref)

    @pl.when(s % nj == 0)
    def _():
        h_ref[...] = _rms(x_ref[...], g_ref[...]).astype(BF16)

    first = (s - 1) % nj == 0
    base = jnp.where(first, x_ref[...], o_ref[...])
    o_ref[...] = base + jnp.dot(m_ref[1 - slot], wo_ref[...], preferred_element_type=F32)

    h = h_ref[...]

    def branch(wg_ref, y_ref, w_ref):
        gate = jax.nn.sigmoid(jnp.dot(h, wg_ref[...], preferred_element_type=F32))
        return gate * jnp.dot(y_ref[...], w_ref[...], preferred_element_type=F32)

    merged = branch(wga_ref, ya_ref, wa_ref) + branch(wgb_ref, yb_ref, wb_ref) + branch(wgc_ref, yc_ref, wc_ref)
    m_ref[slot] = merged.astype(BF16)


def _merge(x, g, ya, yb, yc, w_gate, gate_col, w_a, w_b, w_c, w_o):
    t, d = x.shape
    nj = d // MERGE_TN
    g0 = gate_col // MERGE_TN
    n = (t // MERGE_TM) * nj
    cur = lambda s: jnp.minimum(s, n - 1)
    prev = lambda s: jnp.maximum(s - 1, 0)
    rows = lambda a: pl.BlockSpec((MERGE_TM, a.shape[1]), lambda s: (cur(s) // nj, 0))
    cols = lambda a, off: pl.BlockSpec((a.shape[0], MERGE_TN), lambda s: (0, cur(s) % nj + off))
    return pl.pallas_call(
        functools.partial(_merge_kernel, nj=nj),
        grid=(n + 1,),
        in_specs=[pl.BlockSpec((MERGE_TM, d), lambda s: (cur(s) // nj, 0)),
                  pl.BlockSpec((1, d), lambda s: (0, 0)), rows(ya), rows(yb), rows(yc),
                  cols(w_gate, g0), cols(w_gate, g0 + nj), cols(w_gate, g0 + 2 * nj),
                  cols(w_a, 0), cols(w_b, 0), cols(w_c, 0),
                  pl.BlockSpec((MERGE_TN, d), lambda s: (prev(s) % nj, 0))],
        out_specs=pl.BlockSpec((MERGE_TM, d), lambda s: (prev(s) // nj, 0)),
        out_shape=jax.ShapeDtypeStruct((t, d), F32),
        scratch_shapes=[pltpu.VMEM((MERGE_TM, d), BF16), pltpu.VMEM((2, MERGE_TM, MERGE_TN), BF16)],
        compiler_params=_cparams(("arbitrary",)),
        name="merge",
    )(x, g, ya, yb, yc, w_gate, w_gate, w_gate, w_a, w_b, w_c, w_o)


def _rope_tables(seq):
    half = ROT_DIM // 2
    inv = jnp.power(ROPE_THETA, -jnp.arange(0, ROT_DIM, 2, dtype=F32) / ROT_DIM)
    ang = jnp.arange(seq, dtype=F32)[:, None] * inv[None, :]
    cos, sin = jnp.cos(ang), jnp.sin(ang)
    ones = jnp.ones((seq, HEAD_DIM - ROT_DIM), F32)
    cos_t = jnp.concatenate([cos, cos, ones], axis=1)
    sin_t = jnp.concatenate([-sin, sin, 0.0 * ones], axis=1)
    assert cos_t.shape == (seq, HEAD_DIM) and half * 2 == ROT_DIM
    return cos_t, sin_t


def _hyena_positional_features(seq, emb):
    bands = (emb - 1) // 2
    t = jnp.linspace(0.0, 1.0, seq, dtype=F32)[:, None]
    w = 2.0 * math.pi * jnp.arange(seq, dtype=F32)[:, None] / seq
    f = jnp.linspace(1e-4, bands - 1, bands, dtype=F32)[None, :]
    return jnp.concatenate([t, jnp.cos(f * w), -jnp.sin(f * w)], axis=-1)


def _pad_to(a, shape):
    return jnp.pad(a, [(0, s - d) for s, d in zip(shape, a.shape)])


def kernel(x, mem, g_ff1, w_ff1_in, w_ff1_out, g_mix, w_in, a_gq, a_gk, hy_conv_w, hy_conv_b, hy_f_w1, hy_f_b1,
           hy_f_w2, hy_f_b2, hy_f_w3, hy_f_b3, hy_f_w4, hy_f_freq, hy_bias, g_mem, w_mem_kv, m_gq, m_gk,
           w_br_a, w_br_b, w_br_c, w_out, g_ff2, w_ff2_in, w_ff2_out, g_post):
    batch, seq, d = x.shape
    depth = g_ff1.shape[0]
    dswa_width = len(DSWA_GROUPS) * DSWA_HEADS_PER_GROUP * HEAD_DIM
    hy_width = hy_bias.shape[-1]
    mem_width = MEM_HEADS * HEAD_DIM
    n_cols = 3 * dswa_width + (HY_ORDER + 1) * hy_width + mem_width
    n_rope = 2 * dswa_width // PROJ_TN
    n_plain = (n_cols - mem_width) // PROJ_TN - n_rope
    assert 2 * dswa_width % PROJ_TN == 0 and mem_width == PROJ_TN and n_cols % PROJ_TN == 0
    assert 3 * dswa_width % hy_width == 0 and n_cols % MERGE_TN == 0 and seq & (seq - 1) == 0
    scale = 1.0 / math.sqrt(HEAD_DIM)

    cos_t, sin_t = _rope_tables(seq)
    f_tab, g_tab = _dft_tables(seq)
    emb, hidden = hy_f_w1.shape[1:]
    z_feat = _pad_to(_hyena_positional_features(seq, emb), (seq, LANES))
    t_col = jnp.linspace(0.0, 1.0, seq, dtype=F32)[:, None]
    deltas = jnp.linspace(math.log(HY_TARGET) / HY_SLOW_DECAY, math.log(HY_TARGET) / HY_FAST_DECAY, hy_width, dtype=F32)
    absd = jnp.abs(deltas)[None, :]
    row = lambda v: v.reshape(1, -1)

    xt = x.reshape(batch * seq, d)
    for l in range(depth):
        xt = _ffn(xt, row(g_ff1[l]), w_ff1_in[l].astype(BF16), w_ff1_out[l].astype(BF16), row(g_post[l]), False)

        heads = dswa_width // HEAD_DIM
        gain_cols = jnp.concatenate([jnp.tile(a_gq[l], heads) * scale, jnp.tile(a_gk[l], heads),
                                     jnp.ones((n_cols - 2 * dswa_width - mem_width,), F32),
                                     jnp.tile(m_gq[l], MEM_HEADS) * scale])[None, :]
        w_in_b = w_in[l].astype(BF16)
        p5 = _mixproj(xt, row(g_mix[l]), w_in_b, gain_cols, cos_t, sin_t, n_cols, n_rope, n_plain)
        p5 = p5.reshape(batch, seq, n_cols)

        blk = dswa_width // HEAD_DIM
        y_a = _attn(p5, batch, seq, 0, blk, 2 * blk)

        pad_h = lambda a, shape: _pad_to(a.astype(F32), shape)
        hfilt = _filter_mlp(z_feat, pad_h(hy_f_w1[l], (LANES, LANES)), pad_h(row(hy_f_b1[l]), (1, LANES)),
                            pad_h(hy_f_w2[l], (LANES, LANES)), pad_h(row(hy_f_b2[l]), (1, LANES)),
                            pad_h(hy_f_w3[l], (LANES, LANES)), pad_h(row(hy_f_b3[l]), (1, LANES)),
                            pad_h(hy_f_w4[l], (LANES, hy_f_w4.shape[-1])), pad_h(row(hy_f_freq[l]), (1, LANES)),
                            t_col, absd)
        kre, kim = _filter_dft(hfilt, hy_bias[l].reshape(HY_ORDER, 1, hy_width), f_tab, hy_width)
        y_b = _hyena(p5, hy_conv_w[l], row(hy_conv_b[l]), f_tab, g_tab, kre, kim, 3 * dswa_width // hy_width, hy_width)

        y_c = _memattn(p5, mem, row(g_mem[l]), w_mem_kv[l].astype(BF16), row(m_gk[l]), (n_cols - mem_width) // mem_width)

        t = batch * seq
        xt = _merge(xt, row(g_mix[l]), y_a.reshape(t, -1), y_b.reshape(t, -1), y_c.reshape(t, -1),
                    w_in_b, n_cols, w_br_a[l].astype(BF16), w_br_b[l].astype(BF16), w_br_c[l].astype(BF16),
                    w_out[l].astype(BF16))

        xt = _ffn(xt, row(g_ff2[l]), w_ff2_in[l].astype(BF16), w_ff2_out[l].astype(BF16), row(g_post[l]), True)
    return xt.reshape(batch, seq, d)
```

```python
import functools
import math

import jax
import jax.numpy as jnp
from jax import lax
from jax.experimental import pallas as pl
from jax.experimental.pallas import tpu as pltpu

F32 = jnp.float32
BF16 = jnp.bfloat16

HEAD_DIM = 128
ROPE_THETA = 500000.0
ROT_DIM = HEAD_DIM // 4
EPS = 1e-6
NEG = -1e30
DSWA_GROUPS = ((128, 1), (512, 4), (2048, 16))
DSWA_HEADS_PER_GROUP = 2
MEM_HEADS = 4
HY_ORDER = 2
HY_SHORT = 3
HY_FAST_DECAY = 0.3
HY_SLOW_DECAY = 1.5
HY_TARGET = 1e-2

LANES = 128
SUBLANES = 8
BF16_ROWS = 16
VMEM_LIMIT_BYTES = 60 * 1024 * 1024

FFN_TM = 1024
FFN_TF = 512
PROJ_TM = 512
PROJ_TN = 512
ATT_TQ = 128
ATT_BACK = 128
MEM_TQ = 1024
DFT_TF = 256
FILT_TW = 384
MERGE_TM = 1024
MERGE_TN = 256


def _cparams(sem):
    return pltpu.CompilerParams(dimension_semantics=sem, vmem_limit_bytes=VMEM_LIMIT_BYTES)


def _rms(x, g):
    return x * lax.rsqrt(jnp.mean(x * x, axis=-1, keepdims=True) + EPS) * g


def _ffn_kernel(x_ref, g_ref, wa_ref, wb_ref, wd_ref, gp_ref, *rest, nj, final_norm, n_cast):
    cast_in, o_ref, cast_out, h_ref = rest[:n_cast], rest[n_cast], rest[n_cast + 1:2 * n_cast + 1], rest[-1]
    j = pl.program_id(1)

    for src, dst in zip(cast_in, cast_out):
        dst[...] = src[...].astype(BF16)

    @pl.when(j == 0)
    def _():
        x = x_ref[...]
        h_ref[...] = _rms(x, g_ref[...]).astype(BF16)
        o_ref[...] = x

    h = h_ref[...]
    a = jnp.dot(h, wa_ref[...], preferred_element_type=F32)
    b = jnp.dot(h, wb_ref[...], preferred_element_type=F32)
    act = (0.5 * a * jax.nn.sigmoid(a) * b).astype(BF16)
    o_ref[...] += jnp.dot(act, wd_ref[...], preferred_element_type=F32)

    if final_norm:
        @pl.when(j == nj - 1)
        def _():
            o_ref[...] = _rms(o_ref[...], gp_ref[...])


def _cast_rows(n_rows, steps):
    rows = BF16_ROWS
    while n_rows % rows or n_rows // rows > steps:
        rows += BF16_ROWS
    return rows


def _ffn(x, g, w_in, w_out, g_post, final_norm, casts=()):
    t, d = x.shape
    d_ff = w_out.shape[0]
    nj = d_ff // FFN_TF
    steps = (t // FFN_TM) * nj
    cast_specs = []
    for a in casts:
        rows = _cast_rows(a.shape[0], steps)
        n = a.shape[0] // rows
        cast_specs.append(pl.BlockSpec((rows, a.shape[1]), functools.partial(
            lambda i, j, n: ((i * nj + j) * n // steps, 0), n=n)))
    kern = functools.partial(_ffn_kernel, nj=nj, final_norm=final_norm, n_cast=len(casts))
    out = pl.pallas_call(
        kern,
        grid=(t // FFN_TM, nj),
        in_specs=[
            pl.BlockSpec((FFN_TM, d), lambda i, j: (i, 0)),
            pl.BlockSpec((1, d), lambda i, j: (0, 0)),
            pl.BlockSpec((d, FFN_TF), lambda i, j: (0, j)),
            pl.BlockSpec((d, FFN_TF), lambda i, j: (0, j + nj)),
            pl.BlockSpec((FFN_TF, d), lambda i, j: (j, 0)),
            pl.BlockSpec((1, d), lambda i, j: (0, 0)),
        ] + cast_specs,
        out_specs=[pl.BlockSpec((FFN_TM, d), lambda i, j: (i, 0))] + cast_specs,
        out_shape=[jax.ShapeDtypeStruct((t, d), F32)] + [jax.ShapeDtypeStruct(a.shape, BF16) for a in casts],
        scratch_shapes=[pltpu.VMEM((FFN_TM, d), BF16)],
        compiler_params=_cparams(("arbitrary", "arbitrary")),
        name="ffn_final" if final_norm else "ffn",
    )(x, g, w_in, w_in, w_out, g_post, *casts)
    return out[0], tuple(out[1:])


def _mixproj_kernel(x0_ref, xn_ref, g_ref, w_ref, gain_ref, cos_ref, sin_ref, o_ref, h_ref, *, n_rope, n_plain):
    i = pl.program_id(0)
    slot = i % 2

    @pl.when(i == 0)
    def _():
        h_ref[0] = _rms(x0_ref[...], g_ref[...]).astype(BF16)

    h_ref[1 - slot] = _rms(xn_ref[...], g_ref[...]).astype(BF16)
    h = h_ref[slot]
    lane = lax.broadcasted_iota(jnp.int32, (PROJ_TM, HEAD_DIM), 1)
    first = lane < (ROT_DIM // 2)
    for j in range(w_ref.shape[1] // PROJ_TN):
        acc = jnp.dot(h, w_ref[:, j * PROJ_TN:(j + 1) * PROJ_TN], preferred_element_type=F32)
        plain = n_rope <= j < n_rope + n_plain
        for s in range(PROJ_TN // HEAD_DIM):
            sl = slice(j * PROJ_TN + s * HEAD_DIM, j * PROJ_TN + (s + 1) * HEAD_DIM)
            t = acc[:, s * HEAD_DIM:(s + 1) * HEAD_DIM]
            if not plain:
                t = _rms(t, gain_ref[:, sl])
            if j < n_rope:
                partner = jnp.where(first, pltpu.roll(t, HEAD_DIM - ROT_DIM // 2, axis=1),
                                    pltpu.roll(t, ROT_DIM // 2, axis=1))
                t = t * cos_ref[...] + partner * sin_ref[...]
            o_ref[:, sl] = t.astype(BF16)


def _mixproj(x, g, w, gain_cols, cos_t, sin_t, n_cols, n_rope, n_plain):
    t, d = x.shape
    n_row = t // PROJ_TM
    per_seq = cos_t.shape[0] // PROJ_TM
    kern = functools.partial(_mixproj_kernel, n_rope=n_rope, n_plain=n_plain)
    once = pl.Buffered(1)
    return pl.pallas_call(
        kern,
        grid=(n_row,),
        in_specs=[
            pl.BlockSpec((PROJ_TM, d), lambda i: (0, 0), pipeline_mode=once),
            pl.BlockSpec((PROJ_TM, d), lambda i: (jnp.minimum(i + 1, n_row - 1), 0)),
            pl.BlockSpec((1, d), lambda i: (0, 0)),
            pl.BlockSpec((d, n_cols), lambda i: (0, 0), pipeline_mode=once),
            pl.BlockSpec((1, n_cols), lambda i: (0, 0)),
            pl.BlockSpec((PROJ_TM, HEAD_DIM), lambda i: (i % per_seq, 0)),
            pl.BlockSpec((PROJ_TM, HEAD_DIM), lambda i: (i % per_seq, 0)),
        ],
        out_specs=pl.BlockSpec((PROJ_TM, n_cols), lambda i: (i, 0)),
        out_shape=jax.ShapeDtypeStruct((t, n_cols), BF16),
        scratch_shapes=[pltpu.VMEM((2, PROJ_TM, d), BF16)],
        compiler_params=_cparams(("arbitrary",)),
        name="mixproj",
    )(x, x, g, w, gain_cols, cos_t, sin_t)


def _banded_tiles(q, k, v, length):
    tiles = []
    for qb in range(length // ATT_TQ):
        i0 = qb * ATT_TQ
        lo = max(0, i0 - ATT_BACK)
        hi = min(length, i0 + ATT_TQ + ATT_BACK)
        tiles.append((i0, q[i0:i0 + ATT_TQ], k[lo:hi], v[lo:hi], lo - (i0 - ATT_BACK)))
    return tiles


def _banded_attention(tiles, tab_ref):
    scores = [lax.dot_general(q, k, (((1,), (1,)), ((), ())), preferred_element_type=F32)
              + tab_ref[:, c0:c0 + k.shape[0]] for _, q, k, _, c0 in tiles]
    maxes = [s.max(axis=-1, keepdims=True) for s in scores]
    probs = [jnp.exp(s - m) for s, m in zip(scores, maxes)]
    dens = [p.sum(axis=-1, keepdims=True) for p in probs]
    outs = [jnp.dot(p.astype(BF16), t[3], preferred_element_type=F32) / d for p, t, d in zip(probs, tiles, dens)]
    return [(o, m + jnp.log(d)) for o, m, d in zip(outs, maxes, dens)]


def _attn_kernel(*refs, seq):
    qkv, tab_ref, o_ref = refs[:9], refs[9], refs[10]
    qf, kf, vf = refs[11:14]
    outs, lses = refs[14:16], refs[16:18]
    dilated = [(g, dil) for g, (_, dil) in enumerate(DSWA_GROUPS) if dil > 1]
    plain = [g for g, (_, dil) in enumerate(DSWA_GROUPS) if dil == 1]
    assert len(dilated) == len(outs) and len(plain) == 1

    for slot, (g, dil) in enumerate(dilated):
        length = seq // dil
        for src, dst in zip(qkv[3 * g:3 * g + 3], (qf, kf, vf)):
            dst[...] = src[...].astype(F32)
        tiles, rows = [], []
        for r in range(dil):
            q, k, v = (ref[pl.ds(r, length, stride=dil), :].astype(BF16) for ref in (qf, kf, vf))
            sub = _banded_tiles(q, k, v, length)
            tiles += sub
            rows += [pl.ds(t[0] * dil + r, ATT_TQ, stride=dil) for t in sub]
        for rw, (o, lse) in zip(rows, _banded_attention(tiles, tab_ref)):
            outs[slot][rw, :] = o
            lses[slot][rw, :] = jnp.broadcast_to(lse, (ATT_TQ, HEAD_DIM))

    g = plain[0]
    tiles = _banded_tiles(qkv[3 * g][...], qkv[3 * g + 1][...], qkv[3 * g + 2][...], seq)
    for (i0, *_), (o, lse) in zip(tiles, _banded_attention(tiles, tab_ref)):
        rw = slice(i0, i0 + ATT_TQ)
        parts = [(o, jnp.broadcast_to(lse, (ATT_TQ, HEAD_DIM)))] + [(a[rw, :], b[rw, :]) for a, b in zip(outs, lses)]
        m = functools.reduce(jnp.maximum, [l for _, l in parts])
        num = jnp.zeros((ATT_TQ, HEAD_DIM), F32)
        den = jnp.zeros((ATT_TQ, HEAD_DIM), F32)
        for a, l in parts:
            w = jnp.exp(l - m)
            num = num + w * a
            den = den + w
        o_ref[rw, :] = (num / den).astype(BF16)


def _attn_table():
    width = 2 * ATT_BACK + ATT_TQ
    row = jnp.arange(ATT_TQ, dtype=jnp.int32)[:, None]
    col = jnp.arange(width, dtype=jnp.int32)[None, :]
    half = DSWA_GROUPS[0][0] // (2 * DSWA_GROUPS[0][1])
    return jnp.where(jnp.abs(col - ATT_BACK - row) <= half, 0.0, NEG).astype(F32)


def _attn(p5, batch, seq, col_q, col_k, col_v):
    halves = {win // (2 * dil) for win, dil in DSWA_GROUPS}
    assert len(halves) == 1 and halves.pop() <= ATT_BACK
    assert all(seq % dil == 0 and (seq // dil) % ATT_TQ == 0 for _, dil in DSWA_GROUPS)
    tab = _attn_table()
    hpg = DSWA_HEADS_PER_GROUP
    in_specs = []
    for g in range(len(DSWA_GROUPS)):
        for base in (col_q, col_k, col_v):
            in_specs.append(pl.BlockSpec((None, seq, HEAD_DIM),
                                         functools.partial(lambda b, h, c: (b, 0, c + h), c=base + g * hpg)))
    in_specs.append(pl.BlockSpec(tab.shape, lambda b, h: (0, 0)))
    return pl.pallas_call(
        functools.partial(_attn_kernel, seq=seq),
        grid=(batch, hpg),
        in_specs=in_specs,
        out_specs=pl.BlockSpec((None, seq, HEAD_DIM), lambda b, h: (b, 0, h)),
        out_shape=jax.ShapeDtypeStruct((batch, seq, hpg * HEAD_DIM), BF16),
        scratch_shapes=[pltpu.VMEM((seq, HEAD_DIM), F32)] * 7,
        compiler_params=_cparams(("parallel", "parallel")),
        name="attn",
    )(*([p5] * 9), tab)


def _memattn_kernel(q_ref, mem_ref, g_ref, wkv_ref, gk_ref, o_ref, k_ref, v_ref):
    width = MEM_HEADS * HEAD_DIM

    @pl.when(pl.program_id(1) == 0)
    def _():
        mn = _rms(mem_ref[...], g_ref[...]).astype(BF16)
        kv = jnp.dot(mn, wkv_ref[...], preferred_element_type=F32)
        for h in range(MEM_HEADS):
            sl = slice(h * HEAD_DIM, (h + 1) * HEAD_DIM)
            k_ref[:, sl] = _rms(kv[:, sl], gk_ref[...]).astype(BF16)
        v_ref[...] = kv[:, width:].astype(BF16)

    for h in range(MEM_HEADS):
        sl = slice(h * HEAD_DIM, (h + 1) * HEAD_DIM)
        s = lax.dot_general(q_ref[:, sl], k_ref[:, sl], (((1,), (1,)), ((), ())), preferred_element_type=F32)
        p = jnp.exp(s - s.max(axis=-1, keepdims=True))
        den = p.sum(axis=-1, keepdims=True)
        o = jnp.dot(p.astype(BF16), v_ref[:, sl], preferred_element_type=F32)
        o_ref[:, sl] = (o / den).astype(BF16)


def _memattn(p5, mem, g_mem, w_kv, gk, col_block):
    batch, seq, _ = p5.shape
    n_mem, d = mem.shape[1:]
    width = MEM_HEADS * HEAD_DIM
    return pl.pallas_call(
        _memattn_kernel,
        grid=(batch, seq // MEM_TQ),
        in_specs=[
            pl.BlockSpec((None, MEM_TQ, width), lambda b, i: (b, i, col_block)),
            pl.BlockSpec((None, n_mem, d), lambda b, i: (b, 0, 0)),
            pl.BlockSpec((1, d), lambda b, i: (0, 0)),
            pl.BlockSpec((d, 2 * width), lambda b, i: (0, 0)),
            pl.BlockSpec((1, HEAD_DIM), lambda b, i: (0, 0)),
        ],
        out_specs=pl.BlockSpec((None, MEM_TQ, width), lambda b, i: (b, i, 0)),
        out_shape=jax.ShapeDtypeStruct((batch, seq, width), BF16),
        scratch_shapes=[pltpu.VMEM((n_mem, width), BF16), pltpu.VMEM((n_mem, width), BF16)],
        compiler_params=_cparams(("parallel", "arbitrary")),
        name="memattn",
    )(p5, mem, g_mem, w_kv, gk)


def _dft_kernel(f_ref, g_ref, cb, sb, cbt, sbt, cac, sac, *, seq):
    t = pl.program_id(0)
    n2 = 4 * seq
    theta = 2.0 * math.pi / n2

    def trig(m):
        ang = (m & (n2 - 1)).astype(F32) * theta
        return jnp.cos(ang), jnp.sin(ang)

    @pl.when(t == 0)
    def _():
        f_lo = lax.broadcasted_iota(jnp.int32, (DFT_TF, seq), 0)
        s = lax.broadcasted_iota(jnp.int32, (DFT_TF, seq), 1)
        cb[...], sb[...] = trig((2 * f_lo + 1) * s)
        s = lax.broadcasted_iota(jnp.int32, (seq, DFT_TF), 0)
        f_lo = lax.broadcasted_iota(jnp.int32, (seq, DFT_TF), 1)
        cbt[...], sbt[...] = trig((2 * f_lo + 1) * s)
        s = lax.broadcasted_iota(jnp.int32, (seq, LANES), 0)
        tile = lax.broadcasted_iota(jnp.int32, (seq, LANES), 1)
        cac[...], sac[...] = trig(2 * DFT_TF * tile * s)

    s = lax.broadcasted_iota(jnp.int32, (1, seq), 1)
    ca, sa = trig(2 * DFT_TF * t * s)
    f_ref[:DFT_TF, :] = (ca * cb[...] - sa * sb[...]).astype(BF16)
    f_ref[DFT_TF:, :] = (-(sa * cb[...] + ca * sb[...])).astype(BF16)
    pick = lax.broadcasted_iota(jnp.int32, (seq, LANES), 1) == t
    ca = jnp.sum(jnp.where(pick, cac[...], 0.0), axis=-1, keepdims=True)
    sa = jnp.sum(jnp.where(pick, sac[...], 0.0), axis=-1, keepdims=True)
    g_ref[:, :DFT_TF] = (ca * cbt[...] - sa * sbt[...]).astype(BF16)
    g_ref[:, DFT_TF:] = (-(sa * cbt[...] + ca * sbt[...])).astype(BF16)


def _dft_tables(seq):
    nf = seq // DFT_TF
    assert nf <= LANES
    return pl.pallas_call(
        functools.partial(_dft_kernel, seq=seq),
        grid=(nf,),
        out_specs=[pl.BlockSpec((2 * DFT_TF, seq), lambda t: (t, 0)),
                   pl.BlockSpec((seq, 2 * DFT_TF), lambda t: (0, t))],
        out_shape=[jax.ShapeDtypeStruct((2 * seq, seq), BF16), jax.ShapeDtypeStruct((seq, 2 * seq), BF16)],
        scratch_shapes=[pltpu.VMEM((DFT_TF, seq), F32)] * 2 + [pltpu.VMEM((seq, DFT_TF), F32)] * 2
                       + [pltpu.VMEM((seq, LANES), F32)] * 2,
        compiler_params=_cparams(("arbitrary",)),
        name="dft_tables",
    )()


def _filter_mlp_kernel(z_ref, w1_ref, b1_ref, w2_ref, b2_ref, w3_ref, b3_ref, w4_ref, fr_ref, t_ref, d_ref, o_ref):
    hp = lax.Precision.HIGHEST
    fr = fr_ref[...]
    hh = jnp.sin(fr * (jnp.dot(z_ref[...], w1_ref[...], precision=hp, preferred_element_type=F32) + b1_ref[...]))
    hh = jnp.sin(fr * (jnp.dot(hh, w2_ref[...], precision=hp, preferred_element_type=F32) + b2_ref[...]))
    hh = jnp.sin(fr * (jnp.dot(hh, w3_ref[...], precision=hp, preferred_element_type=F32) + b3_ref[...]))
    h = jnp.dot(hh, w4_ref[...], precision=hp, preferred_element_type=F32)
    decay = jnp.exp(-t_ref[...] * d_ref[...])
    width = decay.shape[1]
    for c in range(h.shape[1] // width):
        o_ref[:, c * width:(c + 1) * width] = h[:, c * width:(c + 1) * width] * decay


def _filter_mlp(z, w1, b1, w2, b2, w3, b3, w4, freq, t_col, absd):
    seq = z.shape[0]
    tl = 512
    n_out = w4.shape[1]
    full = lambda a: pl.BlockSpec(a.shape, lambda i: (0, 0))
    return pl.pallas_call(
        _filter_mlp_kernel,
        grid=(seq // tl,),
        in_specs=[pl.BlockSpec((tl, z.shape[1]), lambda i: (i, 0)),
                  full(w1), full(b1), full(w2), full(b2), full(w3), full(b3), full(w4), full(freq),
                  pl.BlockSpec((tl, 1), lambda i: (i, 0)), full(absd)],
        out_specs=pl.BlockSpec((tl, n_out), lambda i: (i, 0)),
        out_shape=jax.ShapeDtypeStruct((seq, n_out), F32),
        compiler_params=_cparams(("parallel",)),
        name="hyena_filter_mlp",
    )(z, w1, b1, w2, b2, w3, b3, w4, freq, t_col, absd)


def _filter_dft_kernel(hf_ref, hb_ref, bias_ref, f_ref, kre_ref, kim_ref, a_hi, a_lo, n_hi, n_lo, *, seq):
    @pl.when(pl.program_id(2) == 0)
    def _():
        hf = hf_ref[...]
        hb = hb_ref[...]
        row = lax.broadcasted_iota(jnp.int32, hf.shape, 0)
        a = hf + hb + jnp.where(row == 0, bias_ref[...], 0.0)
        nb = hf - hb
        a_hi[...] = a.astype(BF16)
        a_lo[...] = (a - a_hi[...].astype(F32)).astype(BF16)
        n_hi[...] = nb.astype(BF16)
        n_lo[...] = (nb - n_hi[...].astype(F32)).astype(BF16)

    scale = 1.0 / seq
    fc = f_ref[:DFT_TF, :]
    fs = f_ref[DFT_TF:, :]
    kre = jnp.dot(fc, a_hi[...], preferred_element_type=F32) + jnp.dot(fc, a_lo[...], preferred_element_type=F32)
    kim = jnp.dot(fs, n_hi[...], preferred_element_type=F32) + jnp.dot(fs, n_lo[...], preferred_element_type=F32)
    kre_ref[...] = kre * scale
    kim_ref[...] = kim * scale


def _filter_dft(hfilt, bias, f_tab, width):
    seq = hfilt.shape[0]
    nf = seq // DFT_TF
    kern = functools.partial(_filter_dft_kernel, seq=seq)
    out = jax.ShapeDtypeStruct((HY_ORDER, seq, width), F32)
    wt = FILT_TW
    nw = width // wt
    return pl.pallas_call(
        kern,
        grid=(HY_ORDER, nw, nf),
        in_specs=[pl.BlockSpec((seq, wt), lambda o, c, f: (0, 2 * o * nw + c)),
                  pl.BlockSpec((seq, wt), lambda o, c, f: (0, (2 * o + 1) * nw + c)),
                  pl.BlockSpec((None, 1, wt), lambda o, c, f: (o, 0, c)),
                  pl.BlockSpec((2 * DFT_TF, seq), lambda o, c, f: (f, 0))],
        out_specs=[pl.BlockSpec((None, DFT_TF, wt), lambda o, c, f: (o, f, c)),
                   pl.BlockSpec((None, DFT_TF, wt), lambda o, c, f: (o, f, c))],
        out_shape=[out, out],
        scratch_shapes=[pltpu.VMEM((seq, wt), BF16)] * 4,
        compiler_params=_cparams(("parallel", "parallel", "arbitrary")),
        name="hyena_filter_dft",
    )(hfilt, hfilt, bias, f_tab)


def _short_conv(u_ref, w_ref, b_ref, part, width):
    u = u_ref[...].astype(F32)
    seq = u.shape[0]
    sl = slice(part * width, (part + 1) * width)
    w0, w1, w2, b = w_ref[0:1, sl], w_ref[1:2, sl], w_ref[2:3, sl], b_ref[:, sl]
    prev = pltpu.roll(u, 1, axis=0)
    nxt = pltpu.roll(u, seq - 1, axis=0)
    out = prev * w0 + u * w1 + nxt * w2 + b
    sub = SUBLANES
    row = lax.broadcasted_iota(jnp.int32, (sub, width), 0)
    head = jnp.where(row == 0, u[:sub] * w1 + nxt[:sub] * w2 + b, out[:sub])
    tail = jnp.where(row == sub - 1, prev[-sub:] * w0 + u[-sub:] * w1 + b, out[-sub:])
    return jnp.concatenate([head, out[sub:-sub], tail], axis=0)


def _hyena_kernel(u0_ref, u1_ref, u2_ref, w_ref, b_ref, f_ref, g_ref, kre_ref, kim_ref, o_ref, z_ref, acc_ref,
                  *, nf, width):
    j = pl.program_id(1)

    @pl.when(j == 0)
    def _():
        z_ref[...] = _short_conv(u0_ref, w_ref, b_ref, 0, width).astype(BF16)
        acc_ref[...] = jnp.zeros_like(acc_ref)

    uv = jnp.dot(f_ref[...], z_ref[...], preferred_element_type=F32)
    re, im = uv[:DFT_TF], uv[DFT_TF:]
    kre, kim = kre_ref[...], kim_ref[...]
    y = jnp.concatenate([re * kre - im * kim, re * kim + im * kre], axis=0).astype(BF16)
    acc_ref[...] += jnp.dot(g_ref[...], y, preferred_element_type=F32)

    @pl.when(j == nf - 1)
    def _():
        z_ref[...] = (_short_conv(u1_ref, w_ref, b_ref, 1, width) * acc_ref[...]).astype(BF16)
        acc_ref[...] = jnp.zeros_like(acc_ref)

    @pl.when(j == 2 * nf - 1)
    def _():
        o_ref[...] = (_short_conv(u2_ref, w_ref, b_ref, 2, width) * acc_ref[...]).astype(BF16)


def _hyena(p5, conv_w, conv_b, f_tab, g_tab, kre, kim, col_block, width):
    batch, seq, _ = p5.shape
    nf = seq // DFT_TF
    kern = functools.partial(_hyena_kernel, nf=nf, width=width)
    u_spec = lambda part: pl.BlockSpec((None, seq, width), lambda b, j: (b, 0, col_block + part))
    return pl.pallas_call(
        kern,
        grid=(batch, HY_ORDER * nf),
        in_specs=[u_spec(0), u_spec(1), u_spec(2),
                  pl.BlockSpec(conv_w.shape, lambda b, j: (0, 0)),
                  pl.BlockSpec(conv_b.shape, lambda b, j: (0, 0)),
                  pl.BlockSpec((2 * DFT_TF, seq), lambda b, j: (j % nf, 0)),
                  pl.BlockSpec((seq, 2 * DFT_TF), lambda b, j: (0, j % nf)),
                  pl.BlockSpec((None, DFT_TF, width), lambda b, j: (j // nf, j % nf, 0)),
                  pl.BlockSpec((None, DFT_TF, width), lambda b, j: (j // nf, j % nf, 0))],
        out_specs=pl.BlockSpec((None, seq, width), lambda b, j: (b, 0, 0)),
        out_shape=jax.ShapeDtypeStruct((batch, seq, width), BF16),
        scratch_shapes=[pltpu.VMEM((seq, width), BF16), pltpu.VMEM((seq, width), F32)],
        compiler_params=_cparams(("parallel", "arbitrary")),
        name="hyena_conv",
    )(p5, p5, p5, conv_w, conv_b, f_tab, g_tab, kre, kim)


def _merge_kernel(x_ref, g_ref, ya_ref, yb_ref, yc_ref, wga_ref, wgb_ref, wgc_ref, wa_ref, wb_ref, wc_ref,
                  wo_ref, o_ref, h_ref, m_ref, *, nj):
    s = pl.program_id(0)
    slot = s % 2

    @pl.when(s == 0)
    def _():
        m_ref[1] = jnp.zeros(m_ref.shape[1:], BF16)
        o_ref[...] = jnp.zeros_like(o_ref)

    @pl.when(s % nj == 0)
    def _():
        h_ref[...] = _rms(x_ref[...], g_ref[...]).astype(BF16)

    first = (s - 1) % nj == 0
    base = jnp.where(first, x_ref[...], o_ref[...])
    o_ref[...] = base + jnp.dot(m_ref[1 - slot], wo_ref[...], preferred_element_type=F32)

    h = h_ref[...]

    def branch(wg_ref, y_ref, w_ref):
        gate = jax.nn.sigmoid(jnp.dot(h, wg_ref[...], preferred_element_type=F32))
        return gate * jnp.dot(y_ref[...], w_ref[...], preferred_element_type=F32)

    merged = branch(wga_ref, ya_ref, wa_ref) + branch(wgb_ref, yb_ref, wb_ref) + branch(wgc_ref, yc_ref, wc_ref)
    m_ref[slot] = merged.astype(BF16)


def _merge(x, g, ya, yb, yc, w_gate, gate_col, w_a, w_b, w_c, w_o):
    t, d = x.shape
    nj = d // MERGE_TN
    g0 = gate_col // MERGE_TN
    n = (t // MERGE_TM) * nj
    cur = lambda s: jnp.minimum(s, n - 1)
    prev = lambda s: jnp.maximum(s - 1, 0)
    rows = lambda a: pl.BlockSpec((MERGE_TM, a.shape[1]), lambda s: (cur(s) // nj, 0))
    cols = lambda a, off: pl.BlockSpec((a.shape[0], MERGE_TN), lambda s: (0, cur(s) % nj + off))
    return pl.pallas_call(
        functools.partial(_merge_kernel, nj=nj),
        grid=(n + 1,),
        in_specs=[pl.BlockSpec((MERGE_TM, d), lambda s: (cur(s) // nj, 0)),
                  pl.BlockSpec((1, d), lambda s: (0, 0)), rows(ya), rows(yb), rows(yc),
                  cols(w_gate, g0), cols(w_gate, g0 + nj), cols(w_gate, g0 + 2 * nj),
                  cols(w_a, 0), cols(w_b, 0), cols(w_c, 0),
                  pl.BlockSpec((MERGE_TN, d), lambda s: (prev(s) % nj, 0))],
        out_specs=pl.BlockSpec((MERGE_TM, d), lambda s: (prev(s) // nj, 0)),
        out_shape=jax.ShapeDtypeStruct((t, d), F32),
        scratch_shapes=[pltpu.VMEM((MERGE_TM, d), BF16), pltpu.VMEM((2, MERGE_TM, MERGE_TN), BF16)],
        compiler_params=_cparams(("arbitrary",)),
        name="merge",
    )(x, g, ya, yb, yc, w_gate, w_gate, w_gate, w_a, w_b, w_c, w_o)


def _rope_tables(seq):
    half = ROT_DIM // 2
    inv = jnp.power(ROPE_THETA, -jnp.arange(0, ROT_DIM, 2, dtype=F32) / ROT_DIM)
    ang = jnp.arange(seq, dtype=F32)[:, None] * inv[None, :]
    cos, sin = jnp.cos(ang), jnp.sin(ang)
    ones = jnp.ones((seq, HEAD_DIM - ROT_DIM), F32)
    cos_t = jnp.concatenate([cos, cos, ones], axis=1)
    sin_t = jnp.concatenate([-sin, sin, 0.0 * ones], axis=1)
    assert cos_t.shape == (seq, HEAD_DIM) and half * 2 == ROT_DIM
    return cos_t, sin_t


def _hyena_positional_features(seq, emb):
    bands = (emb - 1) // 2
    t = jnp.linspace(0.0, 1.0, seq, dtype=F32)[:, None]
    w = 2.0 * math.pi * jnp.arange(seq, dtype=F32)[:, None] / seq
    f = jnp.linspace(1e-4, bands - 1, bands, dtype=F32)[None, :]
    return jnp.concatenate([t, jnp.cos(f * w), -jnp.sin(f * w)], axis=-1)


def _pad_to(a, shape):
    return jnp.pad(a, [(0, s - d) for s, d in zip(shape, a.shape)])


def kernel(x, mem, g_ff1, w_ff1_in, w_ff1_out, g_mix, w_in, a_gq, a_gk, hy_conv_w, hy_conv_b, hy_f_w1, hy_f_b1,
           hy_f_w2, hy_f_b2, hy_f_w3, hy_f_b3, hy_f_w4, hy_f_freq, hy_bias, g_mem, w_mem_kv, m_gq, m_gk,
           w_br_a, w_br_b, w_br_c, w_out, g_ff2, w_ff2_in, w_ff2_out, g_post):
    batch, seq, d = x.shape
    depth = g_ff1.shape[0]
    dswa_width = len(DSWA_GROUPS) * DSWA_HEADS_PER_GROUP * HEAD_DIM
    hy_width = hy_bias.shape[-1]
    mem_width = MEM_HEADS * HEAD_DIM
    n_cols = 3 * dswa_width + (HY_ORDER + 1) * hy_width + mem_width
    n_rope = 2 * dswa_width // PROJ_TN
    n_plain = (n_cols - mem_width) // PROJ_TN - n_rope
    assert 2 * dswa_width % PROJ_TN == 0 and mem_width == PROJ_TN and n_cols % PROJ_TN == 0
    assert 3 * dswa_width % hy_width == 0 and n_cols % MERGE_TN == 0 and seq & (seq - 1) == 0
    scale = 1.0 / math.sqrt(HEAD_DIM)

    cos_t, sin_t = _rope_tables(seq)
    f_tab, g_tab = _dft_tables(seq)
    emb, hidden = hy_f_w1.shape[1:]
    z_feat = _pad_to(_hyena_positional_features(seq, emb), (seq, LANES))
    t_col = jnp.linspace(0.0, 1.0, seq, dtype=F32)[:, None]
    deltas = jnp.linspace(math.log(HY_TARGET) / HY_SLOW_DECAY, math.log(HY_TARGET) / HY_FAST_DECAY, hy_width, dtype=F32)
    absd = jnp.abs(deltas)[None, :]
    row = lambda v: v.reshape(1, -1)

    xt = x.reshape(batch * seq, d)
    for l in range(depth):
        xt, (w_ff2_in_b, w_ff2_out_b, w_in_b) = _ffn(
            xt, row(g_ff1[l]), w_ff1_in[l].astype(BF16), w_ff1_out[l].astype(BF16), row(g_post[l]), False,
            casts=(w_ff2_in[l], w_ff2_out[l], w_in[l]))

        heads = dswa_width // HEAD_DIM
        gain_cols = jnp.concatenate([jnp.tile(a_gq[l], heads) * scale, jnp.tile(a_gk[l], heads),
                                     jnp.ones((n_cols - 2 * dswa_width - mem_width,), F32),
                                     jnp.tile(m_gq[l], MEM_HEADS) * scale])[None, :]
        p5 = _mixproj(xt, row(g_mix[l]), w_in_b, gain_cols, cos_t, sin_t, n_cols, n_rope, n_plain)
        p5 = p5.reshape(batch, seq, n_cols)

        blk = dswa_width // HEAD_DIM
        y_a = _attn(p5, batch, seq, 0, blk, 2 * blk)

        pad_h = lambda a, shape: _pad_to(a.astype(F32), shape)
        hfilt = _filter_mlp(z_feat, pad_h(hy_f_w1[l], (LANES, LANES)), pad_h(row(hy_f_b1[l]), (1, LANES)),
                            pad_h(hy_f_w2[l], (LANES, LANES)), pad_h(row(hy_f_b2[l]), (1, LANES)),
                            pad_h(hy_f_w3[l], (LANES, LANES)), pad_h(row(hy_f_b3[l]), (1, LANES)),
                            pad_h(hy_f_w4[l], (LANES, hy_f_w4.shape[-1])), pad_h(row(hy_f_freq[l]), (1, LANES)),
                            t_col, absd)
        kre, kim = _filter_dft(hfilt, hy_bias[l].reshape(HY_ORDER, 1, hy_width), f_tab, hy_width)
        y_b = _hyena(p5, hy_conv_w[l], row(hy_conv_b[l]), f_tab, g_tab, kre, kim, 3 * dswa_width // hy_width, hy_width)

        y_c = _memattn(p5, mem, row(g_mem[l]), w_mem_kv[l].astype(BF16), row(m_gk[l]), (n_cols - mem_width) // mem_width)

        t = batch * seq
        xt = _merge(xt, row(g_mix[l]), y_a.reshape(t, -1), y_b.reshape(t, -1), y_c.reshape(t, -1),
                    w_in_b, n_cols, w_br_a[l].astype(BF16), w_br_b[l].astype(BF16), w_br_c[l].astype(BF16),
                    w_out[l].astype(BF16))

        xt, _ = _ffn(xt, row(g_ff2[l]), w_ff2_in_b, w_ff2_out_b, row(g_post[l]), True)
    return xt.reshape(batch, seq, d)
```

```python
import functools
import math

import jax
import jax.numpy as jnp
from jax import lax
from jax.experimental import pallas as pl
from jax.experimental.pallas import tpu as pltpu

F32 = jnp.float32
BF16 = jnp.bfloat16

HEAD_DIM = 128
ROPE_THETA = 500000.0
ROT_DIM = HEAD_DIM // 4
EPS = 1e-6
NEG = -1e30
DSWA_GROUPS = ((128, 1), (512, 4), (2048, 16))
DSWA_HEADS_PER_GROUP = 2
MEM_HEADS = 4
HY_ORDER = 2
HY_SHORT = 3
HY_FAST_DECAY = 0.3
HY_SLOW_DECAY = 1.5
HY_TARGET = 1e-2

LANES = 128
SUBLANES = 8
BF16_ROWS = 16
VMEM_LIMIT_BYTES = 60 * 1024 * 1024

FFN_TM = 1024
FFN_TF = 512
PROJ_TM = 512
PROJ_TN = 512
ATT_TQ = 128
ATT_BACK = 128
MEM_TQ = 1024
DFT_TF = 256
MERGE_TM = 1024
MERGE_TN = 256


def _cparams(sem):
    return pltpu.CompilerParams(dimension_semantics=sem, vmem_limit_bytes=VMEM_LIMIT_BYTES)


def _rms(x, g):
    return x * lax.rsqrt(jnp.mean(x * x, axis=-1, keepdims=True) + EPS) * g


def _ffn_kernel(x_ref, g_ref, wa_ref, wb_ref, wd_ref, gp_ref, *rest, nj, final_norm, n_cast):
    cast_in, o_ref, cast_out, h_ref = rest[:n_cast], rest[n_cast], rest[n_cast + 1:2 * n_cast + 1], rest[-1]
    j = pl.program_id(1)

    for src, dst in zip(cast_in, cast_out):
        dst[...] = src[...].astype(BF16)

    @pl.when(j == 0)
    def _():
        x = x_ref[...]
        h_ref[...] = _rms(x, g_ref[...]).astype(BF16)
        o_ref[...] = x

    h = h_ref[...]
    a = jnp.dot(h, wa_ref[...], preferred_element_type=F32)
    b = jnp.dot(h, wb_ref[...], preferred_element_type=F32)
    act = (0.5 * a * jax.nn.sigmoid(a) * b).astype(BF16)
    o_ref[...] += jnp.dot(act, wd_ref[...], preferred_element_type=F32)

    if final_norm:
        @pl.when(j == nj - 1)
        def _():
            o_ref[...] = _rms(o_ref[...], gp_ref[...])


def _cast_specs(casts, steps, flat):
    specs = []
    for a in casts:
        rows = BF16_ROWS
        while a.shape[0] % rows or a.shape[0] // rows > steps:
            rows += BF16_ROWS
        n = a.shape[0] // rows
        specs.append(pl.BlockSpec((rows, a.shape[1]), functools.partial(
            lambda *idx, n: (flat(*idx) * n // steps, 0), n=n)))
    return specs


def _ffn(x, g, w_in, w_out, g_post, final_norm, casts=()):
    t, d = x.shape
    d_ff = w_out.shape[0]
    nj = d_ff // FFN_TF
    cast_specs = _cast_specs(casts, (t // FFN_TM) * nj, lambda i, j: i * nj + j)
    kern = functools.partial(_ffn_kernel, nj=nj, final_norm=final_norm, n_cast=len(casts))
    out = pl.pallas_call(
        kern,
        grid=(t // FFN_TM, nj),
        in_specs=[
            pl.BlockSpec((FFN_TM, d), lambda i, j: (i, 0)),
            pl.BlockSpec((1, d), lambda i, j: (0, 0)),
            pl.BlockSpec((d, FFN_TF), lambda i, j: (0, j)),
            pl.BlockSpec((d, FFN_TF), lambda i, j: (0, j + nj)),
            pl.BlockSpec((FFN_TF, d), lambda i, j: (j, 0)),
            pl.BlockSpec((1, d), lambda i, j: (0, 0)),
        ] + cast_specs,
        out_specs=[pl.BlockSpec((FFN_TM, d), lambda i, j: (i, 0))] + cast_specs,
        out_shape=[jax.ShapeDtypeStruct((t, d), F32)] + [jax.ShapeDtypeStruct(a.shape, BF16) for a in casts],
        scratch_shapes=[pltpu.VMEM((FFN_TM, d), BF16)],
        compiler_params=_cparams(("arbitrary", "arbitrary")),
        name="ffn_final" if final_norm else "ffn",
    )(x, g, w_in, w_in, w_out, g_post, *casts)
    return out[0], tuple(out[1:])


def _mixproj_kernel(x0_ref, xn_ref, g_ref, w_ref, gain_ref, cos_ref, sin_ref, o_ref, h_ref, *, n_rope, n_plain):
    i = pl.program_id(0)
    slot = i % 2

    @pl.when(i == 0)
    def _():
        h_ref[0] = _rms(x0_ref[...], g_ref[...]).astype(BF16)

    h_ref[1 - slot] = _rms(xn_ref[...], g_ref[...]).astype(BF16)
    h = h_ref[slot]
    lane = lax.broadcasted_iota(jnp.int32, (PROJ_TM, HEAD_DIM), 1)
    first = lane < (ROT_DIM // 2)
    for j in range(w_ref.shape[1] // PROJ_TN):
        acc = jnp.dot(h, w_ref[:, j * PROJ_TN:(j + 1) * PROJ_TN], preferred_element_type=F32)
        plain = n_rope <= j < n_rope + n_plain
        for s in range(PROJ_TN // HEAD_DIM):
            sl = slice(j * PROJ_TN + s * HEAD_DIM, j * PROJ_TN + (s + 1) * HEAD_DIM)
            t = acc[:, s * HEAD_DIM:(s + 1) * HEAD_DIM]
            if not plain:
                t = _rms(t, gain_ref[:, sl])
            if j < n_rope:
                partner = jnp.where(first, pltpu.roll(t, HEAD_DIM - ROT_DIM // 2, axis=1),
                                    pltpu.roll(t, ROT_DIM // 2, axis=1))
                t = t * cos_ref[...] + partner * sin_ref[...]
            o_ref[:, sl] = t.astype(BF16)


def _mixproj(x, g, w, gain_cols, cos_t, sin_t, n_cols, n_rope, n_plain):
    t, d = x.shape
    n_row = t // PROJ_TM
    per_seq = cos_t.shape[0] // PROJ_TM
    kern = functools.partial(_mixproj_kernel, n_rope=n_rope, n_plain=n_plain)
    once = pl.Buffered(1)
    return pl.pallas_call(
        kern,
        grid=(n_row,),
        in_specs=[
            pl.BlockSpec((PROJ_TM, d), lambda i: (0, 0), pipeline_mode=once),
            pl.BlockSpec((PROJ_TM, d), lambda i: (jnp.minimum(i + 1, n_row - 1), 0)),
            pl.BlockSpec((1, d), lambda i: (0, 0)),
            pl.BlockSpec((d, n_cols), lambda i: (0, 0), pipeline_mode=once),
            pl.BlockSpec((1, n_cols), lambda i: (0, 0)),
            pl.BlockSpec((PROJ_TM, HEAD_DIM), lambda i: (i % per_seq, 0)),
            pl.BlockSpec((PROJ_TM, HEAD_DIM), lambda i: (i % per_seq, 0)),
        ],
        out_specs=pl.BlockSpec((PROJ_TM, n_cols), lambda i: (i, 0)),
        out_shape=jax.ShapeDtypeStruct((t, n_cols), BF16),
        scratch_shapes=[pltpu.VMEM((2, PROJ_TM, d), BF16)],
        compiler_params=_cparams(("arbitrary",)),
        name="mixproj",
    )(x, x, g, w, gain_cols, cos_t, sin_t)


def _banded_tiles(q, k, v, length):
    tiles = []
    for qb in range(length // ATT_TQ):
        i0 = qb * ATT_TQ
        lo = max(0, i0 - ATT_BACK)
        hi = min(length, i0 + ATT_TQ + ATT_BACK)
        tiles.append((i0, q[i0:i0 + ATT_TQ], k[lo:hi], v[lo:hi], lo - (i0 - ATT_BACK)))
    return tiles


def _banded_attention(tiles, tab_ref):
    scores = [lax.dot_general(q, k, (((1,), (1,)), ((), ())), preferred_element_type=F32)
              + tab_ref[:, c0:c0 + k.shape[0]] for _, q, k, _, c0 in tiles]
    maxes = [s.max(axis=-1, keepdims=True) for s in scores]
    probs = [jnp.exp(s - m) for s, m in zip(scores, maxes)]
    dens = [p.sum(axis=-1, keepdims=True) for p in probs]
    outs = [jnp.dot(p.astype(BF16), t[3], preferred_element_type=F32) / d for p, t, d in zip(probs, tiles, dens)]
    return [(o, m + jnp.log(d)) for o, m, d in zip(outs, maxes, dens)]


def _attn_kernel(*refs, seq, n_cast):
    qkv, tab_ref = refs[:9], refs[9]
    cast_in, o_ref, cast_out = refs[10:10 + n_cast], refs[10 + n_cast], refs[11 + n_cast:11 + 2 * n_cast]
    qf, kf, vf = refs[-7:-4]
    outs, lses = refs[-4:-2], refs[-2:]

    for src, dst in zip(cast_in, cast_out):
        dst[...] = src[...].astype(BF16)

    dilated = [(g, dil) for g, (_, dil) in enumerate(DSWA_GROUPS) if dil > 1]
    plain = [g for g, (_, dil) in enumerate(DSWA_GROUPS) if dil == 1]
    assert len(dilated) == len(outs) and len(plain) == 1

    for slot, (g, dil) in enumerate(dilated):
        length = seq // dil
        for src, dst in zip(qkv[3 * g:3 * g + 3], (qf, kf, vf)):
            dst[...] = src[...].astype(F32)
        tiles, rows = [], []
        for r in range(dil):
            q, k, v = (ref[pl.ds(r, length, stride=dil), :].astype(BF16) for ref in (qf, kf, vf))
            sub = _banded_tiles(q, k, v, length)
            tiles += sub
            rows += [pl.ds(t[0] * dil + r, ATT_TQ, stride=dil) for t in sub]
        for rw, (o, lse) in zip(rows, _banded_attention(tiles, tab_ref)):
            outs[slot][rw, :] = o
            lses[slot][rw, :] = jnp.broadcast_to(lse, (ATT_TQ, HEAD_DIM))

    g = plain[0]
    tiles = _banded_tiles(qkv[3 * g][...], qkv[3 * g + 1][...], qkv[3 * g + 2][...], seq)
    for (i0, *_), (o, lse) in zip(tiles, _banded_attention(tiles, tab_ref)):
        rw = slice(i0, i0 + ATT_TQ)
        parts = [(o, jnp.broadcast_to(lse, (ATT_TQ, HEAD_DIM)))] + [(a[rw, :], b[rw, :]) for a, b in zip(outs, lses)]
        m = functools.reduce(jnp.maximum, [l for _, l in parts])
        num = jnp.zeros((ATT_TQ, HEAD_DIM), F32)
        den = jnp.zeros((ATT_TQ, HEAD_DIM), F32)
        for a, l in parts:
            w = jnp.exp(l - m)
            num = num + w * a
            den = den + w
        o_ref[rw, :] = (num / den).astype(BF16)


def _attn_table():
    width = 2 * ATT_BACK + ATT_TQ
    row = jnp.arange(ATT_TQ, dtype=jnp.int32)[:, None]
    col = jnp.arange(width, dtype=jnp.int32)[None, :]
    half = DSWA_GROUPS[0][0] // (2 * DSWA_GROUPS[0][1])
    return jnp.where(jnp.abs(col - ATT_BACK - row) <= half, 0.0, NEG).astype(F32)


def _attn(p5, batch, seq, col_q, col_k, col_v, casts=()):
    halves = {win // (2 * dil) for win, dil in DSWA_GROUPS}
    assert len(halves) == 1 and halves.pop() <= ATT_BACK
    assert all(seq % dil == 0 and (seq // dil) % ATT_TQ == 0 for _, dil in DSWA_GROUPS)
    tab = _attn_table()
    hpg = DSWA_HEADS_PER_GROUP
    in_specs = []
    for g in range(len(DSWA_GROUPS)):
        for base in (col_q, col_k, col_v):
            in_specs.append(pl.BlockSpec((None, seq, HEAD_DIM),
                                         functools.partial(lambda b, h, c: (b, 0, c + h), c=base + g * hpg)))
    in_specs.append(pl.BlockSpec(tab.shape, lambda b, h: (0, 0)))
    cast_specs = _cast_specs(casts, batch * hpg, lambda b, h: b * hpg + h)
    out = pl.pallas_call(
        functools.partial(_attn_kernel, seq=seq, n_cast=len(casts)),
        grid=(batch, hpg),
        in_specs=in_specs + cast_specs,
        out_specs=[pl.BlockSpec((None, seq, HEAD_DIM), lambda b, h: (b, 0, h))] + cast_specs,
        out_shape=[jax.ShapeDtypeStruct((batch, seq, hpg * HEAD_DIM), BF16)]
                  + [jax.ShapeDtypeStruct(a.shape, BF16) for a in casts],
        scratch_shapes=[pltpu.VMEM((seq, HEAD_DIM), F32)] * 7,
        compiler_params=_cparams(("arbitrary", "arbitrary")),
        name="attn",
    )(*([p5] * 9), tab, *casts)
    return out[0], tuple(out[1:])


def _memattn_kernel(q_ref, mem_ref, g_ref, wkv_ref, gk_ref, o_ref, k_ref, v_ref):
    width = MEM_HEADS * HEAD_DIM

    @pl.when(pl.program_id(1) == 0)
    def _():
        mn = _rms(mem_ref[...], g_ref[...]).astype(BF16)
        kv = jnp.dot(mn, wkv_ref[...], preferred_element_type=F32)
        for h in range(MEM_HEADS):
            sl = slice(h * HEAD_DIM, (h + 1) * HEAD_DIM)
            k_ref[:, sl] = _rms(kv[:, sl], gk_ref[...]).astype(BF16)
        v_ref[...] = kv[:, width:].astype(BF16)

    for h in range(MEM_HEADS):
        sl = slice(h * HEAD_DIM, (h + 1) * HEAD_DIM)
        s = lax.dot_general(q_ref[:, sl], k_ref[:, sl], (((1,), (1,)), ((), ())), preferred_element_type=F32)
        p = jnp.exp(s - s.max(axis=-1, keepdims=True))
        den = p.sum(axis=-1, keepdims=True)
        o = jnp.dot(p.astype(BF16), v_ref[:, sl], preferred_element_type=F32)
        o_ref[:, sl] = (o / den).astype(BF16)


def _memattn(p5, mem, g_mem, w_kv, gk, col_block):
    batch, seq, _ = p5.shape
    n_mem, d = mem.shape[1:]
    width = MEM_HEADS * HEAD_DIM
    return pl.pallas_call(
        _memattn_kernel,
        grid=(batch, seq // MEM_TQ),
        in_specs=[
            pl.BlockSpec((None, MEM_TQ, width), lambda b, i: (b, i, col_block)),
            pl.BlockSpec((None, n_mem, d), lambda b, i: (b, 0, 0)),
            pl.BlockSpec((1, d), lambda b, i: (0, 0)),
            pl.BlockSpec((d, 2 * width), lambda b, i: (0, 0)),
            pl.BlockSpec((1, HEAD_DIM), lambda b, i: (0, 0)),
        ],
        out_specs=pl.BlockSpec((None, MEM_TQ, width), lambda b, i: (b, i, 0)),
        out_shape=jax.ShapeDtypeStruct((batch, seq, width), BF16),
        scratch_shapes=[pltpu.VMEM((n_mem, width), BF16), pltpu.VMEM((n_mem, width), BF16)],
        compiler_params=_cparams(("parallel", "arbitrary")),
        name="memattn",
    )(p5, mem, g_mem, w_kv, gk)


def _dft_kernel(f_ref, g_ref, cb, sb, cbt, sbt, cac, sac, *, seq):
    t = pl.program_id(0)
    n2 = 4 * seq
    theta = 2.0 * math.pi / n2

    def trig(m):
        ang = (m & (n2 - 1)).astype(F32) * theta
        return jnp.cos(ang), jnp.sin(ang)

    @pl.when(t == 0)
    def _():
        f_lo = lax.broadcasted_iota(jnp.int32, (DFT_TF, seq), 0)
        s = lax.broadcasted_iota(jnp.int32, (DFT_TF, seq), 1)
        cb[...], sb[...] = trig((2 * f_lo + 1) * s)
        s = lax.broadcasted_iota(jnp.int32, (seq, DFT_TF), 0)
        f_lo = lax.broadcasted_iota(jnp.int32, (seq, DFT_TF), 1)
        cbt[...], sbt[...] = trig((2 * f_lo + 1) * s)
        s = lax.broadcasted_iota(jnp.int32, (seq, LANES), 0)
        tile = lax.broadcasted_iota(jnp.int32, (seq, LANES), 1)
        cac[...], sac[...] = trig(2 * DFT_TF * tile * s)

    s = lax.broadcasted_iota(jnp.int32, (1, seq), 1)
    ca, sa = trig(2 * DFT_TF * t * s)
    f_ref[:DFT_TF, :] = (ca * cb[...] - sa * sb[...]).astype(BF16)
    f_ref[DFT_TF:, :] = (-(sa * cb[...] + ca * sb[...])).astype(BF16)
    pick = lax.broadcasted_iota(jnp.int32, (seq, LANES), 1) == t
    ca = jnp.sum(jnp.where(pick, cac[...], 0.0), axis=-1, keepdims=True)
    sa = jnp.sum(jnp.where(pick, sac[...], 0.0), axis=-1, keepdims=True)
    g_ref[:, :DFT_TF] = (ca * cbt[...] - sa * sbt[...]).astype(BF16)
    g_ref[:, DFT_TF:] = (-(sa * cbt[...] + ca * sbt[...])).astype(BF16)


def _dft_tables(seq):
    nf = seq // DFT_TF
    assert nf <= LANES
    return pl.pallas_call(
        functools.partial(_dft_kernel, seq=seq),
        grid=(nf,),
        out_specs=[pl.BlockSpec((2 * DFT_TF, seq), lambda t: (t, 0)),
                   pl.BlockSpec((seq, 2 * DFT_TF), lambda t: (0, t))],
        out_shape=[jax.ShapeDtypeStruct((2 * seq, seq), BF16), jax.ShapeDtypeStruct((seq, 2 * seq), BF16)],
        scratch_shapes=[pltpu.VMEM((DFT_TF, seq), F32)] * 2 + [pltpu.VMEM((seq, DFT_TF), F32)] * 2
                       + [pltpu.VMEM((seq, LANES), F32)] * 2,
        compiler_params=_cparams(("arbitrary",)),
        name="dft_tables",
    )()


def _filter_mlp_kernel(z_ref, w1_ref, b1_ref, w2_ref, b2_ref, w3_ref, b3_ref, w4_ref, fr_ref, t_ref, d_ref, o_ref):
    hp = lax.Precision.HIGHEST
    fr = fr_ref[...]
    hh = jnp.sin(fr * (jnp.dot(z_ref[...], w1_ref[...], precision=hp, preferred_element_type=F32) + b1_ref[...]))
    hh = jnp.sin(fr * (jnp.dot(hh, w2_ref[...], precision=hp, preferred_element_type=F32) + b2_ref[...]))
    hh = jnp.sin(fr * (jnp.dot(hh, w3_ref[...], precision=hp, preferred_element_type=F32) + b3_ref[...]))
    h = jnp.dot(hh, w4_ref[...], precision=hp, preferred_element_type=F32)
    decay = jnp.exp(-t_ref[...] * d_ref[...])
    width = decay.shape[1]
    for c in range(h.shape[1] // width):
        o_ref[:, c * width:(c + 1) * width] = h[:, c * width:(c + 1) * width] * decay


def _filter_mlp(z, w1, b1, w2, b2, w3, b3, w4, freq, t_col, absd):
    seq = z.shape[0]
    tl = 512
    n_out = w4.shape[1]
    full = lambda a: pl.BlockSpec(a.shape, lambda i: (0, 0))
    return pl.pallas_call(
        _filter_mlp_kernel,
        grid=(seq // tl,),
        in_specs=[pl.BlockSpec((tl, z.shape[1]), lambda i: (i, 0)),
                  full(w1), full(b1), full(w2), full(b2), full(w3), full(b3), full(w4), full(freq),
                  pl.BlockSpec((tl, 1), lambda i: (i, 0)), full(absd)],
        out_specs=pl.BlockSpec((tl, n_out), lambda i: (i, 0)),
        out_shape=jax.ShapeDtypeStruct((seq, n_out), F32),
        compiler_params=_cparams(("parallel",)),
        name="hyena_filter_mlp",
    )(z, w1, b1, w2, b2, w3, b3, w4, freq, t_col, absd)


def _filter_dft_kernel(hf_ref, hb_ref, bias_ref, f_ref, *rest, seq, n_cast):
    cast_in, (kre_ref, kim_ref), cast_out = rest[:n_cast], rest[n_cast:n_cast + 2], rest[n_cast + 2:2 * n_cast + 2]
    a_ref, n_ref = rest[-2:]

    for src, dst in zip(cast_in, cast_out):
        dst[...] = src[...].astype(BF16)

    @pl.when(pl.program_id(1) == 0)
    def _():
        hf = hf_ref[...]
        hb = hb_ref[...]
        row = lax.broadcasted_iota(jnp.int32, hf.shape, 0)
        a_ref[...] = (hf + hb + jnp.where(row == 0, bias_ref[...], 0.0)).astype(BF16)
        n_ref[...] = (hf - hb).astype(BF16)

    scale = 1.0 / seq
    kre_ref[...] = jnp.dot(f_ref[:DFT_TF, :], a_ref[...], preferred_element_type=F32) * scale
    kim_ref[...] = jnp.dot(f_ref[DFT_TF:, :], n_ref[...], preferred_element_type=F32) * scale


def _filter_dft(hfilt, bias, f_tab, width, casts=()):
    seq = hfilt.shape[0]
    nf = seq // DFT_TF
    cast_specs = _cast_specs(casts, HY_ORDER * nf, lambda o, f: o * nf + f)
    kern = functools.partial(_filter_dft_kernel, seq=seq, n_cast=len(casts))
    spec = jax.ShapeDtypeStruct((HY_ORDER, seq, width), F32)
    once = pl.Buffered(1)
    out = pl.pallas_call(
        kern,
        grid=(HY_ORDER, nf),
        in_specs=[pl.BlockSpec((seq, width), lambda o, f: (0, 2 * o), pipeline_mode=once),
                  pl.BlockSpec((seq, width), lambda o, f: (0, 2 * o + 1), pipeline_mode=once),
                  pl.BlockSpec((None, 1, width), lambda o, f: (o, 0, 0)),
                  pl.BlockSpec((2 * DFT_TF, seq), lambda o, f: (f, 0))] + cast_specs,
        out_specs=[pl.BlockSpec((None, DFT_TF, width), lambda o, f: (o, f, 0)),
                   pl.BlockSpec((None, DFT_TF, width), lambda o, f: (o, f, 0))] + cast_specs,
        out_shape=[spec, spec] + [jax.ShapeDtypeStruct(a.shape, BF16) for a in casts],
        scratch_shapes=[pltpu.VMEM((seq, width), BF16)] * 2,
        compiler_params=_cparams(("arbitrary", "arbitrary")),
        name="hyena_filter_dft",
    )(hfilt, hfilt, bias, f_tab, *casts)
    return out[0], out[1], tuple(out[2:])


def _short_conv(u_ref, w_ref, b_ref, part, width):
    u = u_ref[...].astype(F32)
    seq = u.shape[0]
    sl = slice(part * width, (part + 1) * width)
    w0, w1, w2, b = w_ref[0:1, sl], w_ref[1:2, sl], w_ref[2:3, sl], b_ref[:, sl]
    prev = pltpu.roll(u, 1, axis=0)
    nxt = pltpu.roll(u, seq - 1, axis=0)
    out = prev * w0 + u * w1 + nxt * w2 + b
    sub = SUBLANES
    row = lax.broadcasted_iota(jnp.int32, (sub, width), 0)
    head = jnp.where(row == 0, u[:sub] * w1 + nxt[:sub] * w2 + b, out[:sub])
    tail = jnp.where(row == sub - 1, prev[-sub:] * w0 + u[-sub:] * w1 + b, out[-sub:])
    return jnp.concatenate([head, out[sub:-sub], tail], axis=0)


def _hyena_kernel(u0_ref, u1_ref, u2_ref, w_ref, b_ref, f_ref, g_ref, kre_ref, kim_ref, o_ref, z_ref, acc_ref,
                  *, nf, width):
    j = pl.program_id(1)

    @pl.when(j == 0)
    def _():
        z_ref[...] = _short_conv(u0_ref, w_ref, b_ref, 0, width).astype(BF16)
        acc_ref[...] = jnp.zeros_like(acc_ref)

    uv = jnp.dot(f_ref[...], z_ref[...], preferred_element_type=F32)
    re, im = uv[:DFT_TF], uv[DFT_TF:]
    kre, kim = kre_ref[...], kim_ref[...]
    y = jnp.concatenate([re * kre - im * kim, re * kim + im * kre], axis=0).astype(BF16)
    acc_ref[...] += jnp.dot(g_ref[...], y, preferred_element_type=F32)

    @pl.when(j == nf - 1)
    def _():
        z_ref[...] = (_short_conv(u1_ref, w_ref, b_ref, 1, width) * acc_ref[...]).astype(BF16)
        acc_ref[...] = jnp.zeros_like(acc_ref)

    @pl.when(j == 2 * nf - 1)
    def _():
        o_ref[...] = (_short_conv(u2_ref, w_ref, b_ref, 2, width) * acc_ref[...]).astype(BF16)


def _hyena(p5, conv_w, conv_b, f_tab, g_tab, kre, kim, col_block, width):
    batch, seq, _ = p5.shape
    nf = seq // DFT_TF
    kern = functools.partial(_hyena_kernel, nf=nf, width=width)
    u_spec = lambda part: pl.BlockSpec((None, seq, width), lambda b, j: (b, 0, col_block + part))
    return pl.pallas_call(
        kern,
        grid=(batch, HY_ORDER * nf),
        in_specs=[u_spec(0), u_spec(1), u_spec(2),
                  pl.BlockSpec(conv_w.shape, lambda b, j: (0, 0)),
                  pl.BlockSpec(conv_b.shape, lambda b, j: (0, 0)),
                  pl.BlockSpec((2 * DFT_TF, seq), lambda b, j: (j % nf, 0)),
                  pl.BlockSpec((seq, 2 * DFT_TF), lambda b, j: (0, j % nf)),
                  pl.BlockSpec((None, DFT_TF, width), lambda b, j: (j // nf, j % nf, 0)),
                  pl.BlockSpec((None, DFT_TF, width), lambda b, j: (j // nf, j % nf, 0))],
        out_specs=pl.BlockSpec((None, seq, width), lambda b, j: (b, 0, 0)),
        out_shape=jax.ShapeDtypeStruct((batch, seq, width), BF16),
        scratch_shapes=[pltpu.VMEM((seq, width), BF16), pltpu.VMEM((seq, width), F32)],
        compiler_params=_cparams(("parallel", "arbitrary")),
        name="hyena_conv",
    )(p5, p5, p5, conv_w, conv_b, f_tab, g_tab, kre, kim)


def _merge_kernel(x_ref, g_ref, ya_ref, yb_ref, yc_ref, wga_ref, wgb_ref, wgc_ref, wa_ref, wb_ref, wc_ref,
                  wo_ref, o_ref, h_ref, m_ref, *, nj):
    s = pl.program_id(0)
    slot = s % 2

    @pl.when(s == 0)
    def _():
        m_ref[1] = jnp.zeros(m_ref.shape[1:], BF16)
        o_ref[...] = jnp.zeros_like(o_ref)

    @pl.when(s % nj == 0)
    def _():
        h_ref[...] = _rms(x_ref[...], g_ref[...]).astype(BF16)

    first = (s - 1) % nj == 0
    base = jnp.where(first, x_ref[...], o_ref[...])
    o_ref[...] = base + jnp.dot(m_ref[1 - slot], wo_ref[...], preferred_element_type=F32)

    h = h_ref[...]

    def branch(wg_ref, y_ref, w_ref):
        gate = jax.nn.sigmoid(jnp.dot(h, wg_ref[...], preferred_element_type=F32))
        return gate * jnp.dot(y_ref[...], w_ref[...], preferred_element_type=F32)

    merged = branch(wga_ref, ya_ref, wa_ref) + branch(wgb_ref, yb_ref, wb_ref) + branch(wgc_ref, yc_ref, wc_ref)
    m_ref[slot] = merged.astype(BF16)


def _merge(x, g, ya, yb, yc, w_gate, gate_col, w_a, w_b, w_c, w_o):
    t, d = x.shape
    nj = d // MERGE_TN
    g0 = gate_col // MERGE_TN
    n = (t // MERGE_TM) * nj
    cur = lambda s: jnp.minimum(s, n - 1)
    prev = lambda s: jnp.maximum(s - 1, 0)
    rows = lambda a: pl.BlockSpec((MERGE_TM, a.shape[1]), lambda s: (cur(s) // nj, 0))
    cols = lambda a, off: pl.BlockSpec((a.shape[0], MERGE_TN), lambda s: (0, cur(s) % nj + off))
    return pl.pallas_call(
        functools.partial(_merge_kernel, nj=nj),
        grid=(n + 1,),
        in_specs=[pl.BlockSpec((MERGE_TM, d), lambda s: (cur(s) // nj, 0)),
                  pl.BlockSpec((1, d), lambda s: (0, 0)), rows(ya), rows(yb), rows(yc),
                  cols(w_gate, g0), cols(w_gate, g0 + nj), cols(w_gate, g0 + 2 * nj),
                  cols(w_a, 0), cols(w_b, 0), cols(w_c, 0),
                  pl.BlockSpec((MERGE_TN, d), lambda s: (prev(s) % nj, 0))],
        out_specs=pl.BlockSpec((MERGE_TM, d), lambda s: (prev(s) // nj, 0)),
        out_shape=jax.ShapeDtypeStruct((t, d), F32),
        scratch_shapes=[pltpu.VMEM((MERGE_TM, d), BF16), pltpu.VMEM((2, MERGE_TM, MERGE_TN), BF16)],
        compiler_params=_cparams(("arbitrary",)),
        name="merge",
    )(x, g, ya, yb, yc, w_gate, w_gate, w_gate, w_a, w_b, w_c, w_o)


def _rope_tables(seq):
    half = ROT_DIM // 2
    inv = jnp.power(ROPE_THETA, -jnp.arange(0, ROT_DIM, 2, dtype=F32) / ROT_DIM)
    ang = jnp.arange(seq, dtype=F32)[:, None] * inv[None, :]
    cos, sin = jnp.cos(ang), jnp.sin(ang)
    ones = jnp.ones((seq, HEAD_DIM - ROT_DIM), F32)
    cos_t = jnp.concatenate([cos, cos, ones], axis=1)
    sin_t = jnp.concatenate([-sin, sin, 0.0 * ones], axis=1)
    assert cos_t.shape == (seq, HEAD_DIM) and half * 2 == ROT_DIM
    return cos_t, sin_t


def _hyena_positional_features(seq, emb):
    bands = (emb - 1) // 2
    t = jnp.linspace(0.0, 1.0, seq, dtype=F32)[:, None]
    w = 2.0 * math.pi * jnp.arange(seq, dtype=F32)[:, None] / seq
    f = jnp.linspace(1e-4, bands - 1, bands, dtype=F32)[None, :]
    return jnp.concatenate([t, jnp.cos(f * w), -jnp.sin(f * w)], axis=-1)


def _pad_to(a, shape):
    return jnp.pad(a, [(0, s - d) for s, d in zip(shape, a.shape)])


def kernel(x, mem, g_ff1, w_ff1_in, w_ff1_out, g_mix, w_in, a_gq, a_gk, hy_conv_w, hy_conv_b, hy_f_w1, hy_f_b1,
           hy_f_w2, hy_f_b2, hy_f_w3, hy_f_b3, hy_f_w4, hy_f_freq, hy_bias, g_mem, w_mem_kv, m_gq, m_gk,
           w_br_a, w_br_b, w_br_c, w_out, g_ff2, w_ff2_in, w_ff2_out, g_post):
    batch, seq, d = x.shape
    depth = g_ff1.shape[0]
    dswa_width = len(DSWA_GROUPS) * DSWA_HEADS_PER_GROUP * HEAD_DIM
    hy_width = hy_bias.shape[-1]
    mem_width = MEM_HEADS * HEAD_DIM
    n_cols = 3 * dswa_width + (HY_ORDER + 1) * hy_width + mem_width
    n_rope = 2 * dswa_width // PROJ_TN
    n_plain = (n_cols - mem_width) // PROJ_TN - n_rope
    assert 2 * dswa_width % PROJ_TN == 0 and mem_width == PROJ_TN and n_cols % PROJ_TN == 0
    assert 3 * dswa_width % hy_width == 0 and n_cols % MERGE_TN == 0 and seq & (seq - 1) == 0
    scale = 1.0 / math.sqrt(HEAD_DIM)

    cos_t, sin_t = _rope_tables(seq)
    f_tab, g_tab = _dft_tables(seq)
    emb, hidden = hy_f_w1.shape[1:]
    z_feat = _pad_to(_hyena_positional_features(seq, emb), (seq, LANES))
    t_col = jnp.linspace(0.0, 1.0, seq, dtype=F32)[:, None]
    deltas = jnp.linspace(math.log(HY_TARGET) / HY_SLOW_DECAY, math.log(HY_TARGET) / HY_FAST_DECAY, hy_width, dtype=F32)
    absd = jnp.abs(deltas)[None, :]
    row = lambda v: v.reshape(1, -1)

    xt = x.reshape(batch * seq, d)
    for l in range(depth):
        pad_h = lambda a, shape: _pad_to(a.astype(F32), shape)
        hfilt = _filter_mlp(z_feat, pad_h(hy_f_w1[l], (LANES, LANES)), pad_h(row(hy_f_b1[l]), (1, LANES)),
                            pad_h(hy_f_w2[l], (LANES, LANES)), pad_h(row(hy_f_b2[l]), (1, LANES)),
                            pad_h(hy_f_w3[l], (LANES, LANES)), pad_h(row(hy_f_b3[l]), (1, LANES)),
                            pad_h(hy_f_w4[l], (LANES, hy_f_w4.shape[-1])), pad_h(row(hy_f_freq[l]), (1, LANES)),
                            t_col, absd)
        kre, kim, (w_ff1_in_b, w_ff1_out_b) = _filter_dft(
            hfilt, hy_bias[l].reshape(HY_ORDER, 1, hy_width), f_tab, hy_width, casts=(w_ff1_in[l], w_ff1_out[l]))

        xt, (w_ff2_in_b, w_ff2_out_b, w_in_b) = _ffn(
            xt, row(g_ff1[l]), w_ff1_in_b, w_ff1_out_b, row(g_post[l]), False,
            casts=(w_ff2_in[l], w_ff2_out[l], w_in[l]))

        heads = dswa_width // HEAD_DIM
        gain_cols = jnp.concatenate([jnp.tile(a_gq[l], heads) * scale, jnp.tile(a_gk[l], heads),
                                     jnp.ones((n_cols - 2 * dswa_width - mem_width,), F32),
                                     jnp.tile(m_gq[l], MEM_HEADS) * scale])[None, :]
        p5 = _mixproj(xt, row(g_mix[l]), w_in_b, gain_cols, cos_t, sin_t, n_cols, n_rope, n_plain)
        p5 = p5.reshape(batch, seq, n_cols)

        blk = dswa_width // HEAD_DIM
        y_a, (w_out_b, w_br_a_b, w_br_b_b, w_br_c_b, w_mem_kv_b) = _attn(
            p5, batch, seq, 0, blk, 2 * blk, casts=(w_out[l], w_br_a[l], w_br_b[l], w_br_c[l], w_mem_kv[l]))

        y_b = _hyena(p5, hy_conv_w[l], row(hy_conv_b[l]), f_tab, g_tab, kre, kim, 3 * dswa_width // hy_width, hy_width)

        y_c = _memattn(p5, mem, row(g_mem[l]), w_mem_kv_b, row(m_gk[l]), (n_cols - mem_width) // mem_width)

        t = batch * seq
        xt = _merge(xt, row(g_mix[l]), y_a.reshape(t, -1), y_b.reshape(t, -1), y_c.reshape(t, -1),
                    w_in_b, n_cols, w_br_a_b, w_br_b_b, w_br_c_b, w_out_b)

        xt, _ = _ffn(xt, row(g_ff2[l]), w_ff2_in_b, w_ff2_out_b, row(g_post[l]), True)
    return xt.reshape(batch, seq, d)
```

```python
import functools
import math

import jax
import jax.numpy as jnp
from jax import lax
from jax.experimental import pallas as pl
from jax.experimental.pallas import tpu as pltpu

F32 = jnp.float32
BF16 = jnp.bfloat16

HEAD_DIM = 128
ROPE_THETA = 500000.0
ROT_DIM = HEAD_DIM // 4
EPS = 1e-6
NEG = -1e30
DSWA_GROUPS = ((128, 1), (512, 4), (2048, 16))
DSWA_HEADS_PER_GROUP = 2
MEM_HEADS = 4
HY_ORDER = 2
HY_SHORT = 3
HY_FAST_DECAY = 0.3
HY_SLOW_DECAY = 1.5
HY_TARGET = 1e-2

LANES = 128
SUBLANES = 8
BF16_ROWS = 16
VMEM_LIMIT_BYTES = 60 * 1024 * 1024

FFN_TM = 1024
FFN_TF = 512
PROJ_TM = 512
PROJ_TN = 512
ATT_TQ = 128
ATT_BACK = 128
MEM_TQ = 1024
DFT_TF = 256
FILT_TF = 128
HY_SPLIT = 2
MERGE_TM = 1024
MERGE_TN = 256


def _cparams(sem):
    return pltpu.CompilerParams(dimension_semantics=sem, vmem_limit_bytes=VMEM_LIMIT_BYTES)


def _rms(x, g):
    return x * lax.rsqrt(jnp.mean(x * x, axis=-1, keepdims=True) + EPS) * g


def _ffn_kernel(x_ref, g_ref, wa_ref, wb_ref, wd_ref, gp_ref, *rest, nj, final_norm, n_cast):
    cast_in, o_ref, cast_out, h_ref = rest[:n_cast], rest[n_cast], rest[n_cast + 1:2 * n_cast + 1], rest[-1]
    j = pl.program_id(1)

    for src, dst in zip(cast_in, cast_out):
        dst[...] = src[...].astype(BF16)

    @pl.when(j == 0)
    def _():
        x = x_ref[...]
        h_ref[...] = _rms(x, g_ref[...]).astype(BF16)
        o_ref[...] = x

    h = h_ref[...]
    a = jnp.dot(h, wa_ref[...], preferred_element_type=F32)
    b = jnp.dot(h, wb_ref[...], preferred_element_type=F32)
    act = (0.5 * a * jax.nn.sigmoid(a) * b).astype(BF16)
    o_ref[...] += jnp.dot(act, wd_ref[...], preferred_element_type=F32)

    if final_norm:
        @pl.when(j == nj - 1)
        def _():
            o_ref[...] = _rms(o_ref[...], gp_ref[...])


def _cast_specs(casts, steps, flat):
    specs = []
    for a in casts:
        rows = BF16_ROWS
        while a.shape[0] % rows or a.shape[0] // rows > steps:
            rows += BF16_ROWS
        n = a.shape[0] // rows
        specs.append(pl.BlockSpec((rows, a.shape[1]), functools.partial(
            lambda *idx, n: (flat(*idx) * n // steps, 0), n=n)))
    return specs


def _ffn(x, g, w_in, w_out, g_post, final_norm, casts=()):
    t, d = x.shape
    d_ff = w_out.shape[0]
    nj = d_ff // FFN_TF
    cast_specs = _cast_specs(casts, (t // FFN_TM) * nj, lambda i, j: i * nj + j)
    kern = functools.partial(_ffn_kernel, nj=nj, final_norm=final_norm, n_cast=len(casts))
    out = pl.pallas_call(
        kern,
        grid=(t // FFN_TM, nj),
        in_specs=[
            pl.BlockSpec((FFN_TM, d), lambda i, j: (i, 0)),
            pl.BlockSpec((1, d), lambda i, j: (0, 0)),
            pl.BlockSpec((d, FFN_TF), lambda i, j: (0, j)),
            pl.BlockSpec((d, FFN_TF), lambda i, j: (0, j + nj)),
            pl.BlockSpec((FFN_TF, d), lambda i, j: (j, 0)),
            pl.BlockSpec((1, d), lambda i, j: (0, 0)),
        ] + cast_specs,
        out_specs=[pl.BlockSpec((FFN_TM, d), lambda i, j: (i, 0))] + cast_specs,
        out_shape=[jax.ShapeDtypeStruct((t, d), F32)] + [jax.ShapeDtypeStruct(a.shape, BF16) for a in casts],
        scratch_shapes=[pltpu.VMEM((FFN_TM, d), BF16)],
        compiler_params=_cparams(("arbitrary", "arbitrary")),
        name="ffn_final" if final_norm else "ffn",
    )(x, g, w_in, w_in, w_out, g_post, *casts)
    return out[0], tuple(out[1:])


def _mixproj_kernel(x0_ref, xn_ref, g_ref, w_ref, gain_ref, cos_ref, sin_ref, o_ref, h_ref, *, n_rope, n_plain):
    i = pl.program_id(0)
    slot = i % 2

    @pl.when(i == 0)
    def _():
        h_ref[0] = _rms(x0_ref[...], g_ref[...]).astype(BF16)

    h_ref[1 - slot] = _rms(xn_ref[...], g_ref[...]).astype(BF16)
    h = h_ref[slot]
    lane = lax.broadcasted_iota(jnp.int32, (PROJ_TM, HEAD_DIM), 1)
    first = lane < (ROT_DIM // 2)
    for j in range(w_ref.shape[1] // PROJ_TN):
        acc = jnp.dot(h, w_ref[:, j * PROJ_TN:(j + 1) * PROJ_TN], preferred_element_type=F32)
        plain = n_rope <= j < n_rope + n_plain
        for s in range(PROJ_TN // HEAD_DIM):
            sl = slice(j * PROJ_TN + s * HEAD_DIM, j * PROJ_TN + (s + 1) * HEAD_DIM)
            t = acc[:, s * HEAD_DIM:(s + 1) * HEAD_DIM]
            if not plain:
                t = _rms(t, gain_ref[:, sl])
            if j < n_rope:
                partner = jnp.where(first, pltpu.roll(t, HEAD_DIM - ROT_DIM // 2, axis=1),
                                    pltpu.roll(t, ROT_DIM // 2, axis=1))
                t = t * cos_ref[...] + partner * sin_ref[...]
            o_ref[:, sl] = t.astype(BF16)


def _mixproj(x, g, w, gain_cols, cos_t, sin_t, n_cols, n_rope, n_plain):
    t, d = x.shape
    n_row = t // PROJ_TM
    per_seq = cos_t.shape[0] // PROJ_TM
    kern = functools.partial(_mixproj_kernel, n_rope=n_rope, n_plain=n_plain)
    once = pl.Buffered(1)
    return pl.pallas_call(
        kern,
        grid=(n_row,),
        in_specs=[
            pl.BlockSpec((PROJ_TM, d), lambda i: (0, 0), pipeline_mode=once),
            pl.BlockSpec((PROJ_TM, d), lambda i: (jnp.minimum(i + 1, n_row - 1), 0)),
            pl.BlockSpec((1, d), lambda i: (0, 0)),
            pl.BlockSpec((d, n_cols), lambda i: (0, 0), pipeline_mode=once),
            pl.BlockSpec((1, n_cols), lambda i: (0, 0)),
            pl.BlockSpec((PROJ_TM, HEAD_DIM), lambda i: (i % per_seq, 0)),
            pl.BlockSpec((PROJ_TM, HEAD_DIM), lambda i: (i % per_seq, 0)),
        ],
        out_specs=pl.BlockSpec((PROJ_TM, n_cols), lambda i: (i, 0)),
        out_shape=jax.ShapeDtypeStruct((t, n_cols), BF16),
        scratch_shapes=[pltpu.VMEM((2, PROJ_TM, d), BF16)],
        compiler_params=_cparams(("arbitrary",)),
        name="mixproj",
    )(x, x, g, w, gain_cols, cos_t, sin_t)


def _banded_tiles(q, k, v, length):
    tiles = []
    for qb in range(length // ATT_TQ):
        i0 = qb * ATT_TQ
        lo = max(0, i0 - ATT_BACK)
        hi = min(length, i0 + ATT_TQ + ATT_BACK)
        tiles.append((i0, q[i0:i0 + ATT_TQ], k[lo:hi], v[lo:hi], lo - (i0 - ATT_BACK)))
    return tiles


def _banded_attention(tiles, tab_ref):
    scores = [lax.dot_general(q, k, (((1,), (1,)), ((), ())), preferred_element_type=F32)
              + tab_ref[:, c0:c0 + k.shape[0]] for _, q, k, _, c0 in tiles]
    maxes = [s.max(axis=-1, keepdims=True) for s in scores]
    probs = [jnp.exp(s - m) for s, m in zip(scores, maxes)]
    dens = [p.sum(axis=-1, keepdims=True) for p in probs]
    outs = [jnp.dot(p.astype(BF16), t[3], preferred_element_type=F32) / d for p, t, d in zip(probs, tiles, dens)]
    return [(o, m + jnp.log(d)) for o, m, d in zip(outs, maxes, dens)]


def _attn_kernel(*refs, seq, n_cast):
    qkv, tab_ref = refs[:9], refs[9]
    cast_in, o_ref, cast_out = refs[10:10 + n_cast], refs[10 + n_cast], refs[11 + n_cast:11 + 2 * n_cast]
    qf, kf, vf = refs[-7:-4]
    outs, lses = refs[-4:-2], refs[-2:]

    for src, dst in zip(cast_in, cast_out):
        dst[...] = src[...].astype(BF16)

    dilated = [(g, dil) for g, (_, dil) in enumerate(DSWA_GROUPS) if dil > 1]
    plain = [g for g, (_, dil) in enumerate(DSWA_GROUPS) if dil == 1]
    assert len(dilated) == len(outs) and len(plain) == 1

    for slot, (g, dil) in enumerate(dilated):
        length = seq // dil
        for src, dst in zip(qkv[3 * g:3 * g + 3], (qf, kf, vf)):
            dst[...] = src[...].astype(F32)
        tiles, rows = [], []
        for r in range(dil):
            q, k, v = (ref[pl.ds(r, length, stride=dil), :].astype(BF16) for ref in (qf, kf, vf))
            sub = _banded_tiles(q, k, v, length)
            tiles += sub
            rows += [pl.ds(t[0] * dil + r, ATT_TQ, stride=dil) for t in sub]
        for rw, (o, lse) in zip(rows, _banded_attention(tiles, tab_ref)):
            outs[slot][rw, :] = o
            lses[slot][rw, :] = jnp.broadcast_to(lse, (ATT_TQ, HEAD_DIM))

    g = plain[0]
    tiles = _banded_tiles(qkv[3 * g][...], qkv[3 * g + 1][...], qkv[3 * g + 2][...], seq)
    for (i0, *_), (o, lse) in zip(tiles, _banded_attention(tiles, tab_ref)):
        rw = slice(i0, i0 + ATT_TQ)
        parts = [(o, jnp.broadcast_to(lse, (ATT_TQ, HEAD_DIM)))] + [(a[rw, :], b[rw, :]) for a, b in zip(outs, lses)]
        m = functools.reduce(jnp.maximum, [l for _, l in parts])
        num = jnp.zeros((ATT_TQ, HEAD_DIM), F32)
        den = jnp.zeros((ATT_TQ, HEAD_DIM), F32)
        for a, l in parts:
            w = jnp.exp(l - m)
            num = num + w * a
            den = den + w
        o_ref[rw, :] = (num / den).astype(BF16)


def _attn_table():
    width = 2 * ATT_BACK + ATT_TQ
    row = jnp.arange(ATT_TQ, dtype=jnp.int32)[:, None]
    col = jnp.arange(width, dtype=jnp.int32)[None, :]
    half = DSWA_GROUPS[0][0] // (2 * DSWA_GROUPS[0][1])
    return jnp.where(jnp.abs(col - ATT_BACK - row) <= half, 0.0, NEG).astype(F32)


def _attn(p5, batch, seq, col_q, col_k, col_v, casts=()):
    halves = {win // (2 * dil) for win, dil in DSWA_GROUPS}
    assert len(halves) == 1 and halves.pop() <= ATT_BACK
    assert all(seq % dil == 0 and (seq // dil) % ATT_TQ == 0 for _, dil in DSWA_GROUPS)
    tab = _attn_table()
    hpg = DSWA_HEADS_PER_GROUP
    in_specs = []
    for g in range(len(DSWA_GROUPS)):
        for base in (col_q, col_k, col_v):
            in_specs.append(pl.BlockSpec((None, seq, HEAD_DIM),
                                         functools.partial(lambda b, h, c: (b, 0, c + h), c=base + g * hpg)))
    in_specs.append(pl.BlockSpec(tab.shape, lambda b, h: (0, 0)))
    cast_specs = _cast_specs(casts, batch * hpg, lambda b, h: b * hpg + h)
    out = pl.pallas_call(
        functools.partial(_attn_kernel, seq=seq, n_cast=len(casts)),
        grid=(batch, hpg),
        in_specs=in_specs + cast_specs,
        out_specs=[pl.BlockSpec((None, seq, HEAD_DIM), lambda b, h: (b, 0, h))] + cast_specs,
        out_shape=[jax.ShapeDtypeStruct((batch, seq, hpg * HEAD_DIM), BF16)]
                  + [jax.ShapeDtypeStruct(a.shape, BF16) for a in casts],
        scratch_shapes=[pltpu.VMEM((seq, HEAD_DIM), F32)] * 7,
        compiler_params=_cparams(("arbitrary", "arbitrary")),
        name="attn",
    )(*([p5] * 9), tab, *casts)
    return out[0], tuple(out[1:])


def _memattn_kernel(q_ref, mem_ref, g_ref, wkv_ref, gk_ref, o_ref, k_ref, v_ref):
    width = MEM_HEADS * HEAD_DIM

    @pl.when(pl.program_id(1) == 0)
    def _():
        mn = _rms(mem_ref[...], g_ref[...]).astype(BF16)
        kv = jnp.dot(mn, wkv_ref[...], preferred_element_type=F32)
        for h in range(MEM_HEADS):
            sl = slice(h * HEAD_DIM, (h + 1) * HEAD_DIM)
            k_ref[:, sl] = _rms(kv[:, sl], gk_ref[...]).astype(BF16)
        v_ref[...] = kv[:, width:].astype(BF16)

    for h in range(MEM_HEADS):
        sl = slice(h * HEAD_DIM, (h + 1) * HEAD_DIM)
        s = lax.dot_general(q_ref[:, sl], k_ref[:, sl], (((1,), (1,)), ((), ())), preferred_element_type=F32)
        p = jnp.exp(s - s.max(axis=-1, keepdims=True))
        den = p.sum(axis=-1, keepdims=True)
        o = jnp.dot(p.astype(BF16), v_ref[:, sl], preferred_element_type=F32)
        o_ref[:, sl] = (o / den).astype(BF16)


def _memattn(p5, mem, g_mem, w_kv, gk, col_block):
    batch, seq, _ = p5.shape
    n_mem, d = mem.shape[1:]
    width = MEM_HEADS * HEAD_DIM
    return pl.pallas_call(
        _memattn_kernel,
        grid=(batch, seq // MEM_TQ),
        in_specs=[
            pl.BlockSpec((None, MEM_TQ, width), lambda b, i: (b, i, col_block)),
            pl.BlockSpec((None, n_mem, d), lambda b, i: (b, 0, 0)),
            pl.BlockSpec((1, d), lambda b, i: (0, 0)),
            pl.BlockSpec((d, 2 * width), lambda b, i: (0, 0)),
            pl.BlockSpec((1, HEAD_DIM), lambda b, i: (0, 0)),
        ],
        out_specs=pl.BlockSpec((None, MEM_TQ, width), lambda b, i: (b, i, 0)),
        out_shape=jax.ShapeDtypeStruct((batch, seq, width), BF16),
        scratch_shapes=[pltpu.VMEM((n_mem, width), BF16), pltpu.VMEM((n_mem, width), BF16)],
        compiler_params=_cparams(("parallel", "arbitrary")),
        name="memattn",
    )(p5, mem, g_mem, w_kv, gk)


def _dft_kernel(f_ref, g_ref, cb, sb, cbt, sbt, cac, sac, *, seq):
    t = pl.program_id(0)
    n2 = 4 * seq
    theta = 2.0 * math.pi / n2

    def trig(m):
        ang = (m & (n2 - 1)).astype(F32) * theta
        return jnp.cos(ang), jnp.sin(ang)

    @pl.when(t == 0)
    def _():
        f_lo = lax.broadcasted_iota(jnp.int32, (DFT_TF, seq), 0)
        s = lax.broadcasted_iota(jnp.int32, (DFT_TF, seq), 1)
        cb[...], sb[...] = trig((2 * f_lo + 1) * s)
        s = lax.broadcasted_iota(jnp.int32, (seq, DFT_TF), 0)
        f_lo = lax.broadcasted_iota(jnp.int32, (seq, DFT_TF), 1)
        cbt[...], sbt[...] = trig((2 * f_lo + 1) * s)
        s = lax.broadcasted_iota(jnp.int32, (seq, LANES), 0)
        tile = lax.broadcasted_iota(jnp.int32, (seq, LANES), 1)
        cac[...], sac[...] = trig(2 * DFT_TF * tile * s)

    s = lax.broadcasted_iota(jnp.int32, (1, seq), 1)
    ca, sa = trig(2 * DFT_TF * t * s)
    f_ref[:DFT_TF, :] = (ca * cb[...] - sa * sb[...]).astype(BF16)
    f_ref[DFT_TF:, :] = (-(sa * cb[...] + ca * sb[...])).astype(BF16)
    pick = lax.broadcasted_iota(jnp.int32, (seq, LANES), 1) == t
    ca = jnp.sum(jnp.where(pick, cac[...], 0.0), axis=-1, keepdims=True)
    sa = jnp.sum(jnp.where(pick, sac[...], 0.0), axis=-1, keepdims=True)
    g_ref[:, :DFT_TF] = (ca * cbt[...] - sa * sbt[...]).astype(BF16)
    g_ref[:, DFT_TF:] = (-(sa * cbt[...] + ca * sbt[...])).astype(BF16)


def _dft_tables(seq):
    nf = seq // DFT_TF
    assert nf <= LANES
    return pl.pallas_call(
        functools.partial(_dft_kernel, seq=seq),
        grid=(nf,),
        out_specs=[pl.BlockSpec((2 * DFT_TF, seq), lambda t: (t, 0)),
                   pl.BlockSpec((seq, 2 * DFT_TF), lambda t: (0, t))],
        out_shape=[jax.ShapeDtypeStruct((2 * seq, seq), BF16), jax.ShapeDtypeStruct((seq, 2 * seq), BF16)],
        scratch_shapes=[pltpu.VMEM((DFT_TF, seq), F32)] * 2 + [pltpu.VMEM((seq, DFT_TF), F32)] * 2
                       + [pltpu.VMEM((seq, LANES), F32)] * 2,
        compiler_params=_cparams(("arbitrary",)),
        name="dft_tables",
    )()


def _filter_mlp_kernel(z_ref, w1_ref, b1_ref, w2_ref, b2_ref, w3_ref, b3_ref, w4_ref, fr_ref, t_ref, d_ref, o_ref):
    hp = lax.Precision.HIGHEST
    fr = fr_ref[...]
    hh = jnp.sin(fr * (jnp.dot(z_ref[...], w1_ref[...], precision=hp, preferred_element_type=F32) + b1_ref[...]))
    hh = jnp.sin(fr * (jnp.dot(hh, w2_ref[...], precision=hp, preferred_element_type=F32) + b2_ref[...]))
    hh = jnp.sin(fr * (jnp.dot(hh, w3_ref[...], precision=hp, preferred_element_type=F32) + b3_ref[...]))
    h = jnp.dot(hh, w4_ref[...], precision=hp, preferred_element_type=F32)
    decay = jnp.exp(-t_ref[...] * d_ref[...])
    width = decay.shape[1]
    for c in range(h.shape[1] // width):
        o_ref[:, c * width:(c + 1) * width] = h[:, c * width:(c + 1) * width] * decay


def _filter_mlp(z, w1, b1, w2, b2, w3, b3, w4, freq, t_col, absd):
    seq = z.shape[0]
    tl = 512
    n_out = w4.shape[1]
    full = lambda a: pl.BlockSpec(a.shape, lambda i: (0, 0))
    return pl.pallas_call(
        _filter_mlp_kernel,
        grid=(seq // tl,),
        in_specs=[pl.BlockSpec((tl, z.shape[1]), lambda i: (i, 0)),
                  full(w1), full(b1), full(w2), full(b2), full(w3), full(b3), full(w4), full(freq),
                  pl.BlockSpec((tl, 1), lambda i: (i, 0)), full(absd)],
        out_specs=pl.BlockSpec((tl, n_out), lambda i: (i, 0)),
        out_shape=jax.ShapeDtypeStruct((seq, n_out), F32),
        compiler_params=_cparams(("parallel",)),
        name="hyena_filter_mlp",
    )(z, w1, b1, w2, b2, w3, b3, w4, freq, t_col, absd)


def _filter_dft_kernel(hf_ref, hb_ref, bias_ref, fc_ref, fs_ref, *rest, half, n_cast):
    cast_in, k_ref, cast_out, r_ref = rest[:n_cast], rest[n_cast], rest[n_cast + 1:2 * n_cast + 1], rest[-1]
    width = hf_ref.shape[1]

    for src, dst in zip(cast_in, cast_out):
        dst[...] = src[...].astype(BF16)

    @pl.when(pl.program_id(1) == 0)
    def _():
        for c, ref in enumerate((hf_ref, hb_ref)):
            for part in range(HY_SPLIT):
                col = (HY_SPLIT * c + part) * width
                r_ref[:, col:col + width] = ref[part * half:(part + 1) * half, :].astype(BF16)

    r = r_ref[...]
    re = jnp.dot(fc_ref[...], r, preferred_element_type=F32)
    im = jnp.dot(fs_ref[...], r, preferred_element_type=F32)
    f0r, f1r, b0r, b1r = (re[:, n * width:(n + 1) * width] for n in range(4))
    f0i, f1i, b0i, b1i = (im[:, n * width:(n + 1) * width] for n in range(4))
    row = lax.broadcasted_iota(jnp.int32, (FILT_TF, width), 0)
    sgn = jnp.where((row & 1) == 0, 1.0, -1.0)
    scale = 1.0 / half
    k_ref[0] = (f0r + b0r + bias_ref[...]) * scale
    k_ref[1] = (f0i - b0i) * scale
    k_ref[2] = (f1r - sgn * f0i) * scale
    k_ref[3] = (f1i + sgn * (f0r - hf_ref[0:1, :])) * scale
    k_ref[4] = (b1r - sgn * b0i) * scale
    k_ref[5] = -(b1i + sgn * (b0r - hb_ref[0:1, :])) * scale


def _filter_dft(hfilt, bias, f_tab, width, casts=()):
    seq = hfilt.shape[0]
    half = seq // HY_SPLIT
    nq = half // FILT_TF
    sub = DFT_TF // FILT_TF
    assert HY_SPLIT == 2 and FILT_TF % 2 == 0
    cast_specs = _cast_specs(casts, HY_ORDER * nq, lambda o, q: o * nq + q)
    kern = functools.partial(_filter_dft_kernel, half=half, n_cast=len(casts))
    once = pl.Buffered(1)
    out = pl.pallas_call(
        kern,
        grid=(HY_ORDER, nq),
        in_specs=[pl.BlockSpec((seq, width), lambda o, q: (0, 2 * o), pipeline_mode=once),
                  pl.BlockSpec((seq, width), lambda o, q: (0, 2 * o + 1), pipeline_mode=once),
                  pl.BlockSpec((None, 1, width), lambda o, q: (o, 0, 0)),
                  pl.BlockSpec((FILT_TF, half), lambda o, q: ((q // sub) * 2 * sub + q % sub, 0)),
                  pl.BlockSpec((FILT_TF, half), lambda o, q: ((q // sub) * 2 * sub + sub + q % sub, 0))] + cast_specs,
        out_specs=[pl.BlockSpec((None, 6, FILT_TF, width), lambda o, q: (o, 0, q, 0))] + cast_specs,
        out_shape=[jax.ShapeDtypeStruct((HY_ORDER, 6, half, width), F32)]
                  + [jax.ShapeDtypeStruct(a.shape, BF16) for a in casts],
        scratch_shapes=[pltpu.VMEM((half, 2 * HY_SPLIT * width), BF16)],
        compiler_params=_cparams(("arbitrary", "arbitrary")),
        name="hyena_filter_dft",
    )(hfilt, hfilt, bias, f_tab, f_tab, *casts)
    return out[0], tuple(out[1:])


def _short_conv(u_ref, w_ref, b_ref, part, width):
    u = u_ref[...].astype(F32)
    seq = u.shape[0]
    sl = slice(part * width, (part + 1) * width)
    w0, w1, w2, b = w_ref[0:1, sl], w_ref[1:2, sl], w_ref[2:3, sl], b_ref[:, sl]
    prev = pltpu.roll(u, 1, axis=0)
    nxt = pltpu.roll(u, seq - 1, axis=0)
    out = prev * w0 + u * w1 + nxt * w2 + b
    sub = SUBLANES
    row = lax.broadcasted_iota(jnp.int32, (sub, width), 0)
    head = jnp.where(row == 0, u[:sub] * w1 + nxt[:sub] * w2 + b, out[:sub])
    tail = jnp.where(row == sub - 1, prev[-sub:] * w0 + u[-sub:] * w1 + b, out[-sub:])
    return jnp.concatenate([head, out[sub:-sub], tail], axis=0)


def _hyena_kernel(u0_ref, u1_ref, u2_ref, w_ref, b_ref, f_ref, g_ref, k_ref, o_ref, z_ref, acc_ref,
                  *, nf, width, half):
    j = pl.program_id(1)

    def put_halves(v):
        z_ref[:, :width] = v[:half].astype(BF16)
        z_ref[:, width:] = v[half:].astype(BF16)

    def conv_result():
        return jnp.concatenate([acc_ref[:, :width], acc_ref[:, width:]], axis=0)

    @pl.when(j == 0)
    def _():
        put_halves(_short_conv(u0_ref, w_ref, b_ref, 0, width))
        acc_ref[...] = jnp.zeros_like(acc_ref)

    uv = jnp.dot(f_ref[...], z_ref[...], preferred_element_type=F32)
    re0, re1, im0, im1 = uv[:DFT_TF, :width], uv[:DFT_TF, width:], uv[DFT_TF:, :width], uv[DFT_TF:, width:]
    k0r, k0i, kpr, kpi, kmr, kmi = (k_ref[n] for n in range(6))
    y0r = re0 * k0r - im0 * k0i + re1 * kmr - im1 * kmi
    y0i = re0 * k0i + im0 * k0r + re1 * kmi + im1 * kmr
    y1r = re0 * kpr - im0 * kpi + re1 * k0r - im1 * k0i
    y1i = re0 * kpi + im0 * kpr + re1 * k0i + im1 * k0r
    y = jnp.concatenate([jnp.concatenate([y0r, y1r], axis=1), jnp.concatenate([y0i, y1i], axis=1)], axis=0)
    acc_ref[...] += jnp.dot(g_ref[...], y.astype(BF16), preferred_element_type=F32)

    @pl.when(j == nf - 1)
    def _():
        put_halves(_short_conv(u1_ref, w_ref, b_ref, 1, width) * conv_result())
        acc_ref[...] = jnp.zeros_like(acc_ref)

    @pl.when(j == 2 * nf - 1)
    def _():
        o_ref[...] = (_short_conv(u2_ref, w_ref, b_ref, 2, width) * conv_result()).astype(BF16)


def _hyena(p5, conv_w, conv_b, f_tab, g_tab, spectra, col_block, width):
    batch, seq, _ = p5.shape
    half = seq // HY_SPLIT
    nf = half // DFT_TF
    assert HY_SPLIT == 2 and HY_ORDER == 2
    kern = functools.partial(_hyena_kernel, nf=nf, width=width, half=half)
    u_spec = lambda part: pl.BlockSpec((None, seq, width), lambda b, j: (b, 0, col_block + part))
    return pl.pallas_call(
        kern,
        grid=(batch, HY_ORDER * nf),
        in_specs=[u_spec(0), u_spec(1), u_spec(2),
                  pl.BlockSpec(conv_w.shape, lambda b, j: (0, 0)),
                  pl.BlockSpec(conv_b.shape, lambda b, j: (0, 0)),
                  pl.BlockSpec((2 * DFT_TF, half), lambda b, j: (j % nf, 0)),
                  pl.BlockSpec((half, 2 * DFT_TF), lambda b, j: (0, j % nf)),
                  pl.BlockSpec((None, 6, DFT_TF, width), lambda b, j: (j // nf, 0, j % nf, 0))],
        out_specs=pl.BlockSpec((None, seq, width), lambda b, j: (b, 0, 0)),
        out_shape=jax.ShapeDtypeStruct((batch, seq, width), BF16),
        scratch_shapes=[pltpu.VMEM((half, HY_SPLIT * width), BF16), pltpu.VMEM((half, HY_SPLIT * width), F32)],
        compiler_params=_cparams(("parallel", "arbitrary")),
        name="hyena_conv",
    )(p5, p5, p5, conv_w, conv_b, f_tab, g_tab, spectra)


def _merge_kernel(x_ref, g_ref, ya_ref, yb_ref, yc_ref, wga_ref, wgb_ref, wgc_ref, wa_ref, wb_ref, wc_ref,
                  wo_ref, o_ref, h_ref, m_ref, *, nj):
    s = pl.program_id(0)
    slot = s % 2

    @pl.when(s == 0)
    def _():
        m_ref[1] = jnp.zeros(m_ref.shape[1:], BF16)
        o_ref[...] = jnp.zeros_like(o_ref)

    @pl.when(s % nj == 0)
    def _():
        h_ref[...] = _rms(x_ref[...], g_ref[...]).astype(BF16)

    first = (s - 1) % nj == 0
    base = jnp.where(first, x_ref[...], o_ref[...])
    o_ref[...] = base + jnp.dot(m_ref[1 - slot], wo_ref[...], preferred_element_type=F32)

    h = h_ref[...]

    def branch(wg_ref, y_ref, w_ref):
        gate = jax.nn.sigmoid(jnp.dot(h, wg_ref[...], preferred_element_type=F32))
        return gate * jnp.dot(y_ref[...], w_ref[...], preferred_element_type=F32)

    merged = branch(wga_ref, ya_ref, wa_ref) + branch(wgb_ref, yb_ref, wb_ref) + branch(wgc_ref, yc_ref, wc_ref)
    m_ref[slot] = merged.astype(BF16)


def _merge(x, g, ya, yb, yc, w_gate, gate_col, w_a, w_b, w_c, w_o):
    t, d = x.shape
    nj = d // MERGE_TN
    g0 = gate_col // MERGE_TN
    n = (t // MERGE_TM) * nj
    cur = lambda s: jnp.minimum(s, n - 1)
    prev = lambda s: jnp.maximum(s - 1, 0)
    rows = lambda a: pl.BlockSpec((MERGE_TM, a.shape[1]), lambda s: (cur(s) // nj, 0))
    cols = lambda a, off: pl.BlockSpec((a.shape[0], MERGE_TN), lambda s: (0, cur(s) % nj + off))
    return pl.pallas_call(
        functools.partial(_merge_kernel, nj=nj),
        grid=(n + 1,),
        in_specs=[pl.BlockSpec((MERGE_TM, d), lambda s: (cur(s) // nj, 0)),
                  pl.BlockSpec((1, d), lambda s: (0, 0)), rows(ya), rows(yb), rows(yc),
                  cols(w_gate, g0), cols(w_gate, g0 + nj), cols(w_gate, g0 + 2 * nj),
                  cols(w_a, 0), cols(w_b, 0), cols(w_c, 0),
                  pl.BlockSpec((MERGE_TN, d), lambda s: (prev(s) % nj, 0))],
        out_specs=pl.BlockSpec((MERGE_TM, d), lambda s: (prev(s) // nj, 0)),
        out_shape=jax.ShapeDtypeStruct((t, d), F32),
        scratch_shapes=[pltpu.VMEM((MERGE_TM, d), BF16), pltpu.VMEM((2, MERGE_TM, MERGE_TN), BF16)],
        compiler_params=_cparams(("arbitrary",)),
        name="merge",
    )(x, g, ya, yb, yc, w_gate, w_gate, w_gate, w_a, w_b, w_c, w_o)


def _rope_tables(seq):
    half = ROT_DIM // 2
    inv = jnp.power(ROPE_THETA, -jnp.arange(0, ROT_DIM, 2, dtype=F32) / ROT_DIM)
    ang = jnp.arange(seq, dtype=F32)[:, None] * inv[None, :]
    cos, sin = jnp.cos(ang), jnp.sin(ang)
    ones = jnp.ones((seq, HEAD_DIM - ROT_DIM), F32)
    cos_t = jnp.concatenate([cos, cos, ones], axis=1)
    sin_t = jnp.concatenate([-sin, sin, 0.0 * ones], axis=1)
    assert cos_t.shape == (seq, HEAD_DIM) and half * 2 == ROT_DIM
    return cos_t, sin_t


def _hyena_positional_features(seq, emb):
    bands = (emb - 1) // 2
    t = jnp.linspace(0.0, 1.0, seq, dtype=F32)[:, None]
    w = 2.0 * math.pi * jnp.arange(seq, dtype=F32)[:, None] / seq
    f = jnp.linspace(1e-4, bands - 1, bands, dtype=F32)[None, :]
    return jnp.concatenate([t, jnp.cos(f * w), -jnp.sin(f * w)], axis=-1)


def _pad_to(a, shape):
    return jnp.pad(a, [(0, s - d) for s, d in zip(shape, a.shape)])


def kernel(x, mem, g_ff1, w_ff1_in, w_ff1_out, g_mix, w_in, a_gq, a_gk, hy_conv_w, hy_conv_b, hy_f_w1, hy_f_b1,
           hy_f_w2, hy_f_b2, hy_f_w3, hy_f_b3, hy_f_w4, hy_f_freq, hy_bias, g_mem, w_mem_kv, m_gq, m_gk,
           w_br_a, w_br_b, w_br_c, w_out, g_ff2, w_ff2_in, w_ff2_out, g_post):
    batch, seq, d = x.shape
    depth = g_ff1.shape[0]
    dswa_width = len(DSWA_GROUPS) * DSWA_HEADS_PER_GROUP * HEAD_DIM
    hy_width = hy_bias.shape[-1]
    mem_width = MEM_HEADS * HEAD_DIM
    n_cols = 3 * dswa_width + (HY_ORDER + 1) * hy_width + mem_width
    n_rope = 2 * dswa_width // PROJ_TN
    n_plain = (n_cols - mem_width) // PROJ_TN - n_rope
    assert 2 * dswa_width % PROJ_TN == 0 and mem_width == PROJ_TN and n_cols % PROJ_TN == 0
    assert 3 * dswa_width % hy_width == 0 and n_cols % MERGE_TN == 0 and seq & (seq - 1) == 0
    scale = 1.0 / math.sqrt(HEAD_DIM)

    cos_t, sin_t = _rope_tables(seq)
    f_tab, g_tab = _dft_tables(seq // HY_SPLIT)
    emb, hidden = hy_f_w1.shape[1:]
    z_feat = _pad_to(_hyena_positional_features(seq, emb), (seq, LANES))
    t_col = jnp.linspace(0.0, 1.0, seq, dtype=F32)[:, None]
    deltas = jnp.linspace(math.log(HY_TARGET) / HY_SLOW_DECAY, math.log(HY_TARGET) / HY_FAST_DECAY, hy_width, dtype=F32)
    absd = jnp.abs(deltas)[None, :]
    row = lambda v: v.reshape(1, -1)

    xt = x.reshape(batch * seq, d)
    for l in range(depth):
        pad_h = lambda a, shape: _pad_to(a.astype(F32), shape)
        hfilt = _filter_mlp(z_feat, pad_h(hy_f_w1[l], (LANES, LANES)), pad_h(row(hy_f_b1[l]), (1, LANES)),
                            pad_h(hy_f_w2[l], (LANES, LANES)), pad_h(row(hy_f_b2[l]), (1, LANES)),
                            pad_h(hy_f_w3[l], (LANES, LANES)), pad_h(row(hy_f_b3[l]), (1, LANES)),
                            pad_h(hy_f_w4[l], (LANES, hy_f_w4.shape[-1])), pad_h(row(hy_f_freq[l]), (1, LANES)),
                            t_col, absd)
        spectra, (w_ff1_in_b, w_ff1_out_b) = _filter_dft(
            hfilt, hy_bias[l].reshape(HY_ORDER, 1, hy_width), f_tab, hy_width, casts=(w_ff1_in[l], w_ff1_out[l]))

        xt, (w_ff2_in_b, w_ff2_out_b, w_in_b) = _ffn(
            xt, row(g_ff1[l]), w_ff1_in_b, w_ff1_out_b, row(g_post[l]), False,
            casts=(w_ff2_in[l], w_ff2_out[l], w_in[l]))

        heads = dswa_width // HEAD_DIM
        gain_cols = jnp.concatenate([jnp.tile(a_gq[l], heads) * scale, jnp.tile(a_gk[l], heads),
                                     jnp.ones((n_cols - 2 * dswa_width - mem_width,), F32),
                                     jnp.tile(m_gq[l], MEM_HEADS) * scale])[None, :]
        p5 = _mixproj(xt, row(g_mix[l]), w_in_b, gain_cols, cos_t, sin_t, n_cols, n_rope, n_plain)
        p5 = p5.reshape(batch, seq, n_cols)

        blk = dswa_width // HEAD_DIM
        y_a, (w_out_b, w_br_a_b, w_br_b_b, w_br_c_b, w_mem_kv_b) = _attn(
            p5, batch, seq, 0, blk, 2 * blk, casts=(w_out[l], w_br_a[l], w_br_b[l], w_br_c[l], w_mem_kv[l]))

        y_b = _hyena(p5, hy_conv_w[l], row(hy_conv_b[l]), f_tab, g_tab, spectra, 3 * dswa_width // hy_width, hy_width)

        y_c = _memattn(p5, mem, row(g_mem[l]), w_mem_kv_b, row(m_gk[l]), (n_cols - mem_width) // mem_width)

        t = batch * seq
        xt = _merge(xt, row(g_mix[l]), y_a.reshape(t, -1), y_b.reshape(t, -1), y_c.reshape(t, -1),
                    w_in_b, n_cols, w_br_a_b, w_br_b_b, w_br_c_b, w_out_b)

        xt, _ = _ffn(xt, row(g_ff2[l]), w_ff2_in_b, w_ff2_out_b, row(g_post[l]), True)
    return xt.reshape(batch, seq, d)
```

```python
import functools
import math

import jax
import jax.numpy as jnp
from jax import lax
from jax.experimental import pallas as pl
from jax.experimental.pallas import tpu as pltpu

F32 = jnp.float32
BF16 = jnp.bfloat16

HEAD_DIM = 128
ROPE_THETA = 500000.0
ROT_DIM = HEAD_DIM // 4
EPS = 1e-6
NEG = -1e30
DSWA_GROUPS = ((128, 1), (512, 4), (2048, 16))
DSWA_HEADS_PER_GROUP = 2
MEM_HEADS = 4
HY_ORDER = 2
HY_SHORT = 3
HY_FAST_DECAY = 0.3
HY_SLOW_DECAY = 1.5
HY_TARGET = 1e-2

LANES = 128
SUBLANES = 8
BF16_ROWS = 16
VMEM_LIMIT_BYTES = 60 * 1024 * 1024

FFN_TM = 1024
FFN_TF = 512
PROJ_TM = 512
PROJ_TN = 512
ATT_TQ = 128
ATT_BACK = 128
MEM_TQ = 1024
DFT_TF = 128
HY_SPLIT = 4
FILT_CAST_STEPS = 32
MERGE_TM = 1024
MERGE_TN = 256


def _cparams(sem):
    return pltpu.CompilerParams(dimension_semantics=sem, vmem_limit_bytes=VMEM_LIMIT_BYTES)


def _rms(x, g):
    return x * lax.rsqrt(jnp.mean(x * x, axis=-1, keepdims=True) + EPS) * g


def _ffn_kernel(x_ref, g_ref, wa_ref, wb_ref, wd_ref, gp_ref, *rest, nj, final_norm, n_cast):
    cast_in, o_ref, cast_out, h_ref = rest[:n_cast], rest[n_cast], rest[n_cast + 1:2 * n_cast + 1], rest[-1]
    j = pl.program_id(1)

    for src, dst in zip(cast_in, cast_out):
        dst[...] = src[...].astype(BF16)

    @pl.when(j == 0)
    def _():
        x = x_ref[...]
        h_ref[...] = _rms(x, g_ref[...]).astype(BF16)
        o_ref[...] = x

    h = h_ref[...]
    a = jnp.dot(h, wa_ref[...], preferred_element_type=F32)
    b = jnp.dot(h, wb_ref[...], preferred_element_type=F32)
    act = (0.5 * a * jax.nn.sigmoid(a) * b).astype(BF16)
    o_ref[...] += jnp.dot(act, wd_ref[...], preferred_element_type=F32)

    if final_norm:
        @pl.when(j == nj - 1)
        def _():
            o_ref[...] = _rms(o_ref[...], gp_ref[...])


def _cast_specs(casts, steps, flat):
    specs = []
    for a in casts:
        rows = BF16_ROWS
        while a.shape[0] % rows or a.shape[0] // rows > steps:
            rows += BF16_ROWS
        n = a.shape[0] // rows
        specs.append(pl.BlockSpec((rows, a.shape[1]), functools.partial(
            lambda *idx, n: (flat(*idx) * n // steps, 0), n=n)))
    return specs


def _ffn(x, g, w_in, w_out, g_post, final_norm, casts=()):
    t, d = x.shape
    d_ff = w_out.shape[0]
    nj = d_ff // FFN_TF
    cast_specs = _cast_specs(casts, (t // FFN_TM) * nj, lambda i, j: i * nj + j)
    kern = functools.partial(_ffn_kernel, nj=nj, final_norm=final_norm, n_cast=len(casts))
    out = pl.pallas_call(
        kern,
        grid=(t // FFN_TM, nj),
        in_specs=[
            pl.BlockSpec((FFN_TM, d), lambda i, j: (i, 0)),
            pl.BlockSpec((1, d), lambda i, j: (0, 0)),
            pl.BlockSpec((d, FFN_TF), lambda i, j: (0, j)),
            pl.BlockSpec((d, FFN_TF), lambda i, j: (0, j + nj)),
            pl.BlockSpec((FFN_TF, d), lambda i, j: (j, 0)),
            pl.BlockSpec((1, d), lambda i, j: (0, 0)),
        ] + cast_specs,
        out_specs=[pl.BlockSpec((FFN_TM, d), lambda i, j: (i, 0))] + cast_specs,
        out_shape=[jax.ShapeDtypeStruct((t, d), F32)] + [jax.ShapeDtypeStruct(a.shape, BF16) for a in casts],
        scratch_shapes=[pltpu.VMEM((FFN_TM, d), BF16)],
        compiler_params=_cparams(("arbitrary", "arbitrary")),
        name="ffn_final" if final_norm else "ffn",
    )(x, g, w_in, w_in, w_out, g_post, *casts)
    return out[0], tuple(out[1:])


def _mixproj_kernel(x0_ref, xn_ref, g_ref, w_ref, gain_ref, cos_ref, sin_ref, o_ref, h_ref, *, n_rope, n_plain):
    i = pl.program_id(0)
    slot = i % 2

    @pl.when(i == 0)
    def _():
        h_ref[0] = _rms(x0_ref[...], g_ref[...]).astype(BF16)

    h_ref[1 - slot] = _rms(xn_ref[...], g_ref[...]).astype(BF16)
    h = h_ref[slot]
    lane = lax.broadcasted_iota(jnp.int32, (PROJ_TM, HEAD_DIM), 1)
    first = lane < (ROT_DIM // 2)
    for j in range(w_ref.shape[1] // PROJ_TN):
        acc = jnp.dot(h, w_ref[:, j * PROJ_TN:(j + 1) * PROJ_TN], preferred_element_type=F32)
        plain = n_rope <= j < n_rope + n_plain
        for s in range(PROJ_TN // HEAD_DIM):
            sl = slice(j * PROJ_TN + s * HEAD_DIM, j * PROJ_TN + (s + 1) * HEAD_DIM)
            t = acc[:, s * HEAD_DIM:(s + 1) * HEAD_DIM]
            if not plain:
                t = _rms(t, gain_ref[:, sl])
            if j < n_rope:
                partner = jnp.where(first, pltpu.roll(t, HEAD_DIM - ROT_DIM // 2, axis=1),
                                    pltpu.roll(t, ROT_DIM // 2, axis=1))
                t = t * cos_ref[...] + partner * sin_ref[...]
            o_ref[:, sl] = t.astype(BF16)


def _mixproj(x, g, w, gain_cols, cos_t, sin_t, n_cols, n_rope, n_plain):
    t, d = x.shape
    n_row = t // PROJ_TM
    per_seq = cos_t.shape[0] // PROJ_TM
    kern = functools.partial(_mixproj_kernel, n_rope=n_rope, n_plain=n_plain)
    once = pl.Buffered(1)
    return pl.pallas_call(
        kern,
        grid=(n_row,),
        in_specs=[
            pl.BlockSpec((PROJ_TM, d), lambda i: (0, 0), pipeline_mode=once),
            pl.BlockSpec((PROJ_TM, d), lambda i: (jnp.minimum(i + 1, n_row - 1), 0)),
            pl.BlockSpec((1, d), lambda i: (0, 0)),
            pl.BlockSpec((d, n_cols), lambda i: (0, 0), pipeline_mode=once),
            pl.BlockSpec((1, n_cols), lambda i: (0, 0)),
            pl.BlockSpec((PROJ_TM, HEAD_DIM), lambda i: (i % per_seq, 0)),
            pl.BlockSpec((PROJ_TM, HEAD_DIM), lambda i: (i % per_seq, 0)),
        ],
        out_specs=pl.BlockSpec((PROJ_TM, n_cols), lambda i: (i, 0)),
        out_shape=jax.ShapeDtypeStruct((t, n_cols), BF16),
        scratch_shapes=[pltpu.VMEM((2, PROJ_TM, d), BF16)],
        compiler_params=_cparams(("arbitrary",)),
        name="mixproj",
    )(x, x, g, w, gain_cols, cos_t, sin_t)


def _banded_tiles(q, k, v, length):
    tiles = []
    for qb in range(length // ATT_TQ):
        i0 = qb * ATT_TQ
        lo = max(0, i0 - ATT_BACK)
        hi = min(length, i0 + ATT_TQ + ATT_BACK)
        tiles.append((i0, q[i0:i0 + ATT_TQ], k[lo:hi], v[lo:hi], lo - (i0 - ATT_BACK)))
    return tiles


def _banded_attention(tiles, tab_ref):
    scores = [lax.dot_general(q, k, (((1,), (1,)), ((), ())), preferred_element_type=F32)
              + tab_ref[:, c0:c0 + k.shape[0]] for _, q, k, _, c0 in tiles]
    maxes = [s.max(axis=-1, keepdims=True) for s in scores]
    probs = [jnp.exp(s - m) for s, m in zip(scores, maxes)]
    dens = [p.sum(axis=-1, keepdims=True) for p in probs]
    outs = [jnp.dot(p.astype(BF16), t[3], preferred_element_type=F32) / d for p, t, d in zip(probs, tiles, dens)]
    return [(o, m + jnp.log(d)) for o, m, d in zip(outs, maxes, dens)]


def _attn_kernel(*refs, seq, n_cast):
    qkv, tab_ref = refs[:9], refs[9]
    cast_in, o_ref, cast_out = refs[10:10 + n_cast], refs[10 + n_cast], refs[11 + n_cast:11 + 2 * n_cast]
    qf, kf, vf = refs[-7:-4]
    outs, lses = refs[-4:-2], refs[-2:]

    for src, dst in zip(cast_in, cast_out):
        dst[...] = src[...].astype(BF16)

    dilated = [(g, dil) for g, (_, dil) in enumerate(DSWA_GROUPS) if dil > 1]
    plain = [g for g, (_, dil) in enumerate(DSWA_GROUPS) if dil == 1]
    assert len(dilated) == len(outs) and len(plain) == 1

    for slot, (g, dil) in enumerate(dilated):
        length = seq // dil
        for src, dst in zip(qkv[3 * g:3 * g + 3], (qf, kf, vf)):
            dst[...] = src[...].astype(F32)
        tiles, rows = [], []
        for r in range(dil):
            q, k, v = (ref[pl.ds(r, length, stride=dil), :].astype(BF16) for ref in (qf, kf, vf))
            sub = _banded_tiles(q, k, v, length)
            tiles += sub
            rows += [pl.ds(t[0] * dil + r, ATT_TQ, stride=dil) for t in sub]
        for rw, (o, lse) in zip(rows, _banded_attention(tiles, tab_ref)):
            outs[slot][rw, :] = o
            lses[slot][rw, :] = jnp.broadcast_to(lse, (ATT_TQ, HEAD_DIM))

    g = plain[0]
    tiles = _banded_tiles(qkv[3 * g][...], qkv[3 * g + 1][...], qkv[3 * g + 2][...], seq)
    for (i0, *_), (o, lse) in zip(tiles, _banded_attention(tiles, tab_ref)):
        rw = slice(i0, i0 + ATT_TQ)
        parts = [(o, jnp.broadcast_to(lse, (ATT_TQ, HEAD_DIM)))] + [(a[rw, :], b[rw, :]) for a, b in zip(outs, lses)]
        m = functools.reduce(jnp.maximum, [l for _, l in parts])
        num = jnp.zeros((ATT_TQ, HEAD_DIM), F32)
        den = jnp.zeros((ATT_TQ, HEAD_DIM), F32)
        for a, l in parts:
            w = jnp.exp(l - m)
            num = num + w * a
            den = den + w
        o_ref[rw, :] = (num / den).astype(BF16)


def _attn_table():
    width = 2 * ATT_BACK + ATT_TQ
    row = jnp.arange(ATT_TQ, dtype=jnp.int32)[:, None]
    col = jnp.arange(width, dtype=jnp.int32)[None, :]
    half = DSWA_GROUPS[0][0] // (2 * DSWA_GROUPS[0][1])
    return jnp.where(jnp.abs(col - ATT_BACK - row) <= half, 0.0, NEG).astype(F32)


def _attn(p5, batch, seq, col_q, col_k, col_v, casts=()):
    halves = {win // (2 * dil) for win, dil in DSWA_GROUPS}
    assert len(halves) == 1 and halves.pop() <= ATT_BACK
    assert all(seq % dil == 0 and (seq // dil) % ATT_TQ == 0 for _, dil in DSWA_GROUPS)
    tab = _attn_table()
    hpg = DSWA_HEADS_PER_GROUP
    in_specs = []
    for g in range(len(DSWA_GROUPS)):
        for base in (col_q, col_k, col_v):
            in_specs.append(pl.BlockSpec((None, seq, HEAD_DIM),
                                         functools.partial(lambda b, h, c: (b, 0, c + h), c=base + g * hpg)))
    in_specs.append(pl.BlockSpec(tab.shape, lambda b, h: (0, 0)))
    cast_specs = _cast_specs(casts, batch * hpg, lambda b, h: b * hpg + h)
    out = pl.pallas_call(
        functools.partial(_attn_kernel, seq=seq, n_cast=len(casts)),
        grid=(batch, hpg),
        in_specs=in_specs + cast_specs,
        out_specs=[pl.BlockSpec((None, seq, HEAD_DIM), lambda b, h: (b, 0, h))] + cast_specs,
        out_shape=[jax.ShapeDtypeStruct((batch, seq, hpg * HEAD_DIM), BF16)]
                  + [jax.ShapeDtypeStruct(a.shape, BF16) for a in casts],
        scratch_shapes=[pltpu.VMEM((seq, HEAD_DIM), F32)] * 7,
        compiler_params=_cparams(("arbitrary", "arbitrary")),
        name="attn",
    )(*([p5] * 9), tab, *casts)
    return out[0], tuple(out[1:])


def _memattn_kernel(q_ref, mem_ref, g_ref, wkv_ref, gk_ref, o_ref, k_ref, v_ref):
    width = MEM_HEADS * HEAD_DIM

    @pl.when(pl.program_id(1) == 0)
    def _():
        mn = _rms(mem_ref[...], g_ref[...]).astype(BF16)
        kv = jnp.dot(mn, wkv_ref[...], preferred_element_type=F32)
        for h in range(MEM_HEADS):
            sl = slice(h * HEAD_DIM, (h + 1) * HEAD_DIM)
            k_ref[:, sl] = _rms(kv[:, sl], gk_ref[...]).astype(BF16)
        v_ref[...] = kv[:, width:].astype(BF16)

    for h in range(MEM_HEADS):
        sl = slice(h * HEAD_DIM, (h + 1) * HEAD_DIM)
        s = lax.dot_general(q_ref[:, sl], k_ref[:, sl], (((1,), (1,)), ((), ())), preferred_element_type=F32)
        p = jnp.exp(s - s.max(axis=-1, keepdims=True))
        den = p.sum(axis=-1, keepdims=True)
        o = jnp.dot(p.astype(BF16), v_ref[:, sl], preferred_element_type=F32)
        o_ref[:, sl] = (o / den).astype(BF16)


def _memattn(p5, mem, g_mem, w_kv, gk, col_block):
    batch, seq, _ = p5.shape
    n_mem, d = mem.shape[1:]
    width = MEM_HEADS * HEAD_DIM
    return pl.pallas_call(
        _memattn_kernel,
        grid=(batch, seq // MEM_TQ),
        in_specs=[
            pl.BlockSpec((None, MEM_TQ, width), lambda b, i: (b, i, col_block)),
            pl.BlockSpec((None, n_mem, d), lambda b, i: (b, 0, 0)),
            pl.BlockSpec((1, d), lambda b, i: (0, 0)),
            pl.BlockSpec((d, 2 * width), lambda b, i: (0, 0)),
            pl.BlockSpec((1, HEAD_DIM), lambda b, i: (0, 0)),
        ],
        out_specs=pl.BlockSpec((None, MEM_TQ, width), lambda b, i: (b, i, 0)),
        out_shape=jax.ShapeDtypeStruct((batch, seq, width), BF16),
        scratch_shapes=[pltpu.VMEM((n_mem, width), BF16), pltpu.VMEM((n_mem, width), BF16)],
        compiler_params=_cparams(("parallel", "arbitrary")),
        name="memattn",
    )(p5, mem, g_mem, w_kv, gk)


def _dft_kernel(f_ref, g_ref, cb, sb, cbt, sbt, cac, sac, *, seq):
    t = pl.program_id(0)
    n2 = 4 * seq
    theta = 2.0 * math.pi / n2

    def trig(m):
        ang = (m & (n2 - 1)).astype(F32) * theta
        return jnp.cos(ang), jnp.sin(ang)

    @pl.when(t == 0)
    def _():
        f_lo = lax.broadcasted_iota(jnp.int32, (DFT_TF, seq), 0)
        s = lax.broadcasted_iota(jnp.int32, (DFT_TF, seq), 1)
        cb[...], sb[...] = trig((2 * f_lo + 1) * s)
        s = lax.broadcasted_iota(jnp.int32, (seq, DFT_TF), 0)
        f_lo = lax.broadcasted_iota(jnp.int32, (seq, DFT_TF), 1)
        cbt[...], sbt[...] = trig((2 * f_lo + 1) * s)
        s = lax.broadcasted_iota(jnp.int32, (seq, LANES), 0)
        tile = lax.broadcasted_iota(jnp.int32, (seq, LANES), 1)
        cac[...], sac[...] = trig(2 * DFT_TF * tile * s)

    s = lax.broadcasted_iota(jnp.int32, (1, seq), 1)
    ca, sa = trig(2 * DFT_TF * t * s)
    f_ref[:DFT_TF, :] = (ca * cb[...] - sa * sb[...]).astype(BF16)
    f_ref[DFT_TF:, :] = (-(sa * cb[...] + ca * sb[...])).astype(BF16)
    pick = lax.broadcasted_iota(jnp.int32, (seq, LANES), 1) == t
    ca = jnp.sum(jnp.where(pick, cac[...], 0.0), axis=-1, keepdims=True)
    sa = jnp.sum(jnp.where(pick, sac[...], 0.0), axis=-1, keepdims=True)
    g_ref[:, :DFT_TF] = (ca * cbt[...] - sa * sbt[...]).astype(BF16)
    g_ref[:, DFT_TF:] = (-(sa * cbt[...] + ca * sbt[...])).astype(BF16)


def _dft_tables(seq):
    nf = seq // DFT_TF
    assert nf <= LANES
    return pl.pallas_call(
        functools.partial(_dft_kernel, seq=seq),
        grid=(nf,),
        out_specs=[pl.BlockSpec((2 * DFT_TF, seq), lambda t: (t, 0)),
                   pl.BlockSpec((seq, 2 * DFT_TF), lambda t: (0, t))],
        out_shape=[jax.ShapeDtypeStruct((2 * seq, seq), BF16), jax.ShapeDtypeStruct((seq, 2 * seq), BF16)],
        scratch_shapes=[pltpu.VMEM((DFT_TF, seq), F32)] * 2 + [pltpu.VMEM((seq, DFT_TF), F32)] * 2
                       + [pltpu.VMEM((seq, LANES), F32)] * 2,
        compiler_params=_cparams(("arbitrary",)),
        name="dft_tables",
    )()


def _filter_mlp_kernel(z_ref, w1_ref, b1_ref, w2_ref, b2_ref, w3_ref, b3_ref, w4_ref, fr_ref, t_ref, d_ref, o_ref):
    hp = lax.Precision.HIGHEST
    fr = fr_ref[...]
    hh = jnp.sin(fr * (jnp.dot(z_ref[...], w1_ref[...], precision=hp, preferred_element_type=F32) + b1_ref[...]))
    hh = jnp.sin(fr * (jnp.dot(hh, w2_ref[...], precision=hp, preferred_element_type=F32) + b2_ref[...]))
    hh = jnp.sin(fr * (jnp.dot(hh, w3_ref[...], precision=hp, preferred_element_type=F32) + b3_ref[...]))
    split = lambda a: (a.astype(BF16), (a - a.astype(BF16).astype(F32)).astype(BF16))
    (a_hi, a_lo), (w_hi, w_lo) = split(hh), split(w4_ref[...])
    h = (jnp.dot(a_hi, w_hi, preferred_element_type=F32) + jnp.dot(a_hi, w_lo, preferred_element_type=F32)
         + jnp.dot(a_lo, w_hi, preferred_element_type=F32))
    decay = jnp.exp(-t_ref[...] * d_ref[...])
    width = decay.shape[1]
    for c in range(h.shape[1] // width):
        o_ref[:, c * width:(c + 1) * width] = h[:, c * width:(c + 1) * width] * decay


def _filter_mlp(z, w1, b1, w2, b2, w3, b3, w4, freq, t_col, absd):
    seq = z.shape[0]
    tl = 512
    n_out = w4.shape[1]
    full = lambda a: pl.BlockSpec(a.shape, lambda i: (0, 0))
    return pl.pallas_call(
        _filter_mlp_kernel,
        grid=(seq // tl,),
        in_specs=[pl.BlockSpec((tl, z.shape[1]), lambda i: (i, 0)),
                  full(w1), full(b1), full(w2), full(b2), full(w3), full(b3), full(w4), full(freq),
                  pl.BlockSpec((tl, 1), lambda i: (i, 0)), full(absd)],
        out_specs=pl.BlockSpec((tl, n_out), lambda i: (i, 0)),
        out_shape=jax.ShapeDtypeStruct((seq, n_out), F32),
        compiler_params=_cparams(("parallel",)),
        name="hyena_filter_mlp",
    )(z, w1, b1, w2, b2, w3, b3, w4, freq, t_col, absd)


def _segment(delta):
    return 0 if delta == 0 else (2 * delta - 1 if delta > 0 else -2 * delta)


def _filter_dft_kernel(hf_ref, hb_ref, bias_ref, f_ref, *rest, blk, n_cast):
    cast_in, k_ref, cast_out, r_ref = rest[:n_cast], rest[n_cast], rest[n_cast + 1:2 * n_cast + 1], rest[-1]
    width = hf_ref.shape[1]

    for src, dst in zip(cast_in, cast_out):
        dst[...] = src[...].astype(BF16)

    @pl.when((pl.program_id(1) == 0) & (pl.program_id(2) == 0))
    def _():
        for c, ref in enumerate((hf_ref, hb_ref)):
            for part in range(HY_SPLIT):
                col = (HY_SPLIT * c + part) * width
                r_ref[:, col:col + width] = ref[part * blk:(part + 1) * blk, :].astype(BF16)

    @pl.when(pl.program_id(2) == 0)
    def _():
        r = r_ref[...]
        re = jnp.dot(f_ref[:DFT_TF, :], r, preferred_element_type=F32)
        im = jnp.dot(f_ref[DFT_TF:, :], r, preferred_element_type=F32)

        def transform(c, j):
            col = (HY_SPLIT * c + j) * width
            return re[:, col:col + width], im[:, col:col + width]

        row = lax.broadcasted_iota(jnp.int32, (DFT_TF, width), 0)
        sgn = jnp.where((row & 1) == 0, 1.0, -1.0)
        scale = 1.0 / blk
        (f0r, f0i), (b0r, b0i) = transform(0, 0), transform(1, 0)
        k_ref[0] = (f0r + b0r + bias_ref[...]) * scale
        k_ref[1] = (f0i - b0i) * scale
        for d in range(1, HY_SPLIT):
            for c, ref, conj in ((0, hf_ref, 1.0), (1, hb_ref, -1.0)):
                (ar, ai), (pr, pi) = transform(c, d), transform(c, d - 1)
                edge = ref[(d - 1) * blk:(d - 1) * blk + 1, :]
                n = 2 * _segment(d if c == 0 else -d)
                k_ref[n] = (ar - sgn * pi) * scale
                k_ref[n + 1] = (ai + sgn * (pr - edge)) * (conj * scale)


def _filter_dft(hfilt, bias, f_tab, width, casts=()):
    seq = hfilt.shape[0]
    blk = seq // HY_SPLIT
    nf = blk // DFT_TF
    n_spec = 2 * (2 * HY_SPLIT - 1)
    assert DFT_TF % 2 == 0
    pace = FILT_CAST_STEPS // (HY_ORDER * nf)
    cast_specs = _cast_specs(casts, HY_ORDER * nf * pace, lambda o, f, c: (o * nf + f) * pace + c)
    kern = functools.partial(_filter_dft_kernel, blk=blk, n_cast=len(casts))
    once = pl.Buffered(1)
    out = pl.pallas_call(
        kern,
        grid=(HY_ORDER, nf, pace),
        in_specs=[pl.BlockSpec((seq, width), lambda o, f, c: (0, 2 * o), pipeline_mode=once),
                  pl.BlockSpec((seq, width), lambda o, f, c: (0, 2 * o + 1), pipeline_mode=once),
                  pl.BlockSpec((None, 1, width), lambda o, f, c: (o, 0, 0)),
                  pl.BlockSpec((2 * DFT_TF, blk), lambda o, f, c: (f, 0))] + cast_specs,
        out_specs=[pl.BlockSpec((None, n_spec, DFT_TF, width), lambda o, f, c: (o, 0, f, 0))] + cast_specs,
        out_shape=[jax.ShapeDtypeStruct((HY_ORDER, n_spec, blk, width), F32)]
                  + [jax.ShapeDtypeStruct(a.shape, BF16) for a in casts],
        scratch_shapes=[pltpu.VMEM((blk, 2 * HY_SPLIT * width), BF16)],
        compiler_params=_cparams(("arbitrary", "arbitrary", "arbitrary")),
        name="hyena_filter_dft",
    )(hfilt, hfilt, bias, f_tab, *casts)
    return out[0], tuple(out[1:])


def _short_conv(u_ref, w_ref, b_ref, part, width):
    u = u_ref[...].astype(F32)
    seq = u.shape[0]
    sl = slice(part * width, (part + 1) * width)
    w0, w1, w2, b = w_ref[0:1, sl], w_ref[1:2, sl], w_ref[2:3, sl], b_ref[:, sl]
    prev = pltpu.roll(u, 1, axis=0)
    nxt = pltpu.roll(u, seq - 1, axis=0)
    out = prev * w0 + u * w1 + nxt * w2 + b
    sub = SUBLANES
    row = lax.broadcasted_iota(jnp.int32, (sub, width), 0)
    head = jnp.where(row == 0, u[:sub] * w1 + nxt[:sub] * w2 + b, out[:sub])
    tail = jnp.where(row == sub - 1, prev[-sub:] * w0 + u[-sub:] * w1 + b, out[-sub:])
    return jnp.concatenate([head, out[sub:-sub], tail], axis=0)


def _hyena_kernel(u0_ref, u1_ref, u2_ref, w_ref, b_ref, f_ref, g_ref, k_ref, o_ref, z_ref, acc_ref,
                  *, nf, width, blk):
    j = pl.program_id(1)
    cols = [slice(i * width, (i + 1) * width) for i in range(HY_SPLIT)]

    def put_blocks(v):
        for i, c in enumerate(cols):
            z_ref[:, c] = v[i * blk:(i + 1) * blk].astype(BF16)

    def conv_result():
        return jnp.concatenate([acc_ref[:, c] for c in cols], axis=0)

    @pl.when(j == 0)
    def _():
        put_blocks(_short_conv(u0_ref, w_ref, b_ref, 0, width))
        acc_ref[...] = jnp.zeros_like(acc_ref)

    uv = jnp.dot(f_ref[...], z_ref[...], preferred_element_type=F32)
    zr = [uv[:DFT_TF, c] for c in cols]
    zi = [uv[DFT_TF:, c] for c in cols]
    yr, yi = [], []
    for o in range(HY_SPLIT):
        acc_r = acc_i = None
        for i in range(HY_SPLIT):
            n = 2 * _segment(o - i)
            kr, ki = k_ref[n], k_ref[n + 1]
            tr = zr[i] * kr - zi[i] * ki
            ti = zr[i] * ki + zi[i] * kr
            acc_r, acc_i = (tr, ti) if acc_r is None else (acc_r + tr, acc_i + ti)
        yr.append(acc_r)
        yi.append(acc_i)
    y = jnp.concatenate([jnp.concatenate(yr, axis=1), jnp.concatenate(yi, axis=1)], axis=0)
    acc_ref[...] += jnp.dot(g_ref[...], y.astype(BF16), preferred_element_type=F32)

    @pl.when(j == nf - 1)
    def _():
        put_blocks(_short_conv(u1_ref, w_ref, b_ref, 1, width) * conv_result())
        acc_ref[...] = jnp.zeros_like(acc_ref)

    @pl.when(j == 2 * nf - 1)
    def _():
        o_ref[...] = (_short_conv(u2_ref, w_ref, b_ref, 2, width) * conv_result()).astype(BF16)


def _hyena(p5, conv_w, conv_b, f_tab, g_tab, spectra, col_block, width):
    batch, seq, _ = p5.shape
    blk = seq // HY_SPLIT
    nf = blk // DFT_TF
    n_spec = spectra.shape[1]
    assert HY_ORDER == 2 and n_spec == 2 * (2 * HY_SPLIT - 1)
    kern = functools.partial(_hyena_kernel, nf=nf, width=width, blk=blk)
    u_spec = lambda part: pl.BlockSpec((None, seq, width), lambda b, j: (b, 0, col_block + part))
    return pl.pallas_call(
        kern,
        grid=(batch, HY_ORDER * nf),
        in_specs=[u_spec(0), u_spec(1), u_spec(2),
                  pl.BlockSpec(conv_w.shape, lambda b, j: (0, 0)),
                  pl.BlockSpec(conv_b.shape, lambda b, j: (0, 0)),
                  pl.BlockSpec((2 * DFT_TF, blk), lambda b, j: (j % nf, 0)),
                  pl.BlockSpec((blk, 2 * DFT_TF), lambda b, j: (0, j % nf)),
                  pl.BlockSpec((None, n_spec, DFT_TF, width), lambda b, j: (j // nf, 0, j % nf, 0))],
        out_specs=pl.BlockSpec((None, seq, width), lambda b, j: (b, 0, 0)),
        out_shape=jax.ShapeDtypeStruct((batch, seq, width), BF16),
        scratch_shapes=[pltpu.VMEM((blk, HY_SPLIT * width), BF16), pltpu.VMEM((blk, HY_SPLIT * width), F32)],
        compiler_params=_cparams(("parallel", "arbitrary")),
        name="hyena_conv",
    )(p5, p5, p5, conv_w, conv_b, f_tab, g_tab, spectra)


def _merge_kernel(x_ref, g_ref, ya_ref, yb_ref, yc_ref, wga_ref, wgb_ref, wgc_ref, wa_ref, wb_ref, wc_ref,
                  wo_ref, o_ref, h_ref, m_ref, *, nj):
    s = pl.program_id(0)
    slot = s % 2

    @pl.when(s == 0)
    def _():
        m_ref[1] = jnp.zeros(m_ref.shape[1:], BF16)
        o_ref[...] = jnp.zeros_like(o_ref)

    @pl.when(s % nj == 0)
    def _():
        h_ref[...] = _rms(x_ref[...], g_ref[...]).astype(BF16)

    first = (s - 1) % nj == 0
    base = jnp.where(first, x_ref[...], o_ref[...])
    o_ref[...] = base + jnp.dot(m_ref[1 - slot], wo_ref[...], preferred_element_type=F32)

    h = h_ref[...]

    def branch(wg_ref, y_ref, w_ref):
        gate = jax.nn.sigmoid(jnp.dot(h, wg_ref[...], preferred_element_type=F32))
        return gate * jnp.dot(y_ref[...], w_ref[...], preferred_element_type=F32)

    merged = branch(wga_ref, ya_ref, wa_ref) + branch(wgb_ref, yb_ref, wb_ref) + branch(wgc_ref, yc_ref, wc_ref)
    m_ref[slot] = merged.astype(BF16)


def _merge(x, g, ya, yb, yc, w_gate, gate_col, w_a, w_b, w_c, w_o):
    t, d = x.shape
    nj = d // MERGE_TN
    g0 = gate_col // MERGE_TN
    n = (t // MERGE_TM) * nj
    cur = lambda s: jnp.minimum(s, n - 1)
    prev = lambda s: jnp.maximum(s - 1, 0)
    rows = lambda a: pl.BlockSpec((MERGE_TM, a.shape[1]), lambda s: (cur(s) // nj, 0))
    cols = lambda a, off: pl.BlockSpec((a.shape[0], MERGE_TN), lambda s: (0, cur(s) % nj + off))
    return pl.pallas_call(
        functools.partial(_merge_kernel, nj=nj),
        grid=(n + 1,),
        in_specs=[pl.BlockSpec((MERGE_TM, d), lambda s: (cur(s) // nj, 0)),
                  pl.BlockSpec((1, d), lambda s: (0, 0)), rows(ya), rows(yb), rows(yc),
                  cols(w_gate, g0), cols(w_gate, g0 + nj), cols(w_gate, g0 + 2 * nj),
                  cols(w_a, 0), cols(w_b, 0), cols(w_c, 0),
                  pl.BlockSpec((MERGE_TN, d), lambda s: (prev(s) % nj, 0))],
        out_specs=pl.BlockSpec((MERGE_TM, d), lambda s: (prev(s) // nj, 0)),
        out_shape=jax.ShapeDtypeStruct((t, d), F32),
        scratch_shapes=[pltpu.VMEM((MERGE_TM, d), BF16), pltpu.VMEM((2, MERGE_TM, MERGE_TN), BF16)],
        compiler_params=_cparams(("arbitrary",)),
        name="merge",
    )(x, g, ya, yb, yc, w_gate, w_gate, w_gate, w_a, w_b, w_c, w_o)


def _rope_tables(seq):
    half = ROT_DIM // 2
    inv = jnp.power(ROPE_THETA, -jnp.arange(0, ROT_DIM, 2, dtype=F32) / ROT_DIM)
    ang = jnp.arange(seq, dtype=F32)[:, None] * inv[None, :]
    cos, sin = jnp.cos(ang), jnp.sin(ang)
    ones = jnp.ones((seq, HEAD_DIM - ROT_DIM), F32)
    cos_t = jnp.concatenate([cos, cos, ones], axis=1)
    sin_t = jnp.concatenate([-sin, sin, 0.0 * ones], axis=1)
    assert cos_t.shape == (seq, HEAD_DIM) and half * 2 == ROT_DIM
    return cos_t, sin_t


def _hyena_positional_features(seq, emb):
    bands = (emb - 1) // 2
    t = jnp.linspace(0.0, 1.0, seq, dtype=F32)[:, None]
    w = 2.0 * math.pi * jnp.arange(seq, dtype=F32)[:, None] / seq
    f = jnp.linspace(1e-4, bands - 1, bands, dtype=F32)[None, :]
    return jnp.concatenate([t, jnp.cos(f * w), -jnp.sin(f * w)], axis=-1)


def _pad_to(a, shape):
    return jnp.pad(a, [(0, s - d) for s, d in zip(shape, a.shape)])


def kernel(x, mem, g_ff1, w_ff1_in, w_ff1_out, g_mix, w_in, a_gq, a_gk, hy_conv_w, hy_conv_b, hy_f_w1, hy_f_b1,
           hy_f_w2, hy_f_b2, hy_f_w3, hy_f_b3, hy_f_w4, hy_f_freq, hy_bias, g_mem, w_mem_kv, m_gq, m_gk,
           w_br_a, w_br_b, w_br_c, w_out, g_ff2, w_ff2_in, w_ff2_out, g_post):
    batch, seq, d = x.shape
    depth = g_ff1.shape[0]
    dswa_width = len(DSWA_GROUPS) * DSWA_HEADS_PER_GROUP * HEAD_DIM
    hy_width = hy_bias.shape[-1]
    mem_width = MEM_HEADS * HEAD_DIM
    n_cols = 3 * dswa_width + (HY_ORDER + 1) * hy_width + mem_width
    n_rope = 2 * dswa_width // PROJ_TN
    n_plain = (n_cols - mem_width) // PROJ_TN - n_rope
    assert 2 * dswa_width % PROJ_TN == 0 and mem_width == PROJ_TN and n_cols % PROJ_TN == 0
    assert 3 * dswa_width % hy_width == 0 and n_cols % MERGE_TN == 0 and seq & (seq - 1) == 0
    scale = 1.0 / math.sqrt(HEAD_DIM)

    cos_t, sin_t = _rope_tables(seq)
    f_tab, g_tab = _dft_tables(seq // HY_SPLIT)
    emb, hidden = hy_f_w1.shape[1:]
    z_feat = _pad_to(_hyena_positional_features(seq, emb), (seq, LANES))
    t_col = jnp.linspace(0.0, 1.0, seq, dtype=F32)[:, None]
    deltas = jnp.linspace(math.log(HY_TARGET) / HY_SLOW_DECAY, math.log(HY_TARGET) / HY_FAST_DECAY, hy_width, dtype=F32)
    absd = jnp.abs(deltas)[None, :]
    row = lambda v: v.reshape(1, -1)

    xt = x.reshape(batch * seq, d)
    for l in range(depth):
        pad_h = lambda a, shape: _pad_to(a.astype(F32), shape)
        hfilt = _filter_mlp(z_feat, pad_h(hy_f_w1[l], (LANES, LANES)), pad_h(row(hy_f_b1[l]), (1, LANES)),
                            pad_h(hy_f_w2[l], (LANES, LANES)), pad_h(row(hy_f_b2[l]), (1, LANES)),
                            pad_h(hy_f_w3[l], (LANES, LANES)), pad_h(row(hy_f_b3[l]), (1, LANES)),
                            pad_h(hy_f_w4[l], (LANES, hy_f_w4.shape[-1])), pad_h(row(hy_f_freq[l]), (1, LANES)),
                            t_col, absd)
        spectra, (w_ff1_in_b, w_ff1_out_b) = _filter_dft(
            hfilt, hy_bias[l].reshape(HY_ORDER, 1, hy_width), f_tab, hy_width, casts=(w_ff1_in[l], w_ff1_out[l]))

        xt, (w_ff2_in_b, w_ff2_out_b, w_in_b) = _ffn(
            xt, row(g_ff1[l]), w_ff1_in_b, w_ff1_out_b, row(g_post[l]), False,
            casts=(w_ff2_in[l], w_ff2_out[l], w_in[l]))

        heads = dswa_width // HEAD_DIM
        gain_cols = jnp.concatenate([jnp.tile(a_gq[l], heads) * scale, jnp.tile(a_gk[l], heads),
                                     jnp.ones((n_cols - 2 * dswa_width - mem_width,), F32),
                                     jnp.tile(m_gq[l], MEM_HEADS) * scale])[None, :]
        p5 = _mixproj(xt, row(g_mix[l]), w_in_b, gain_cols, cos_t, sin_t, n_cols, n_rope, n_plain)
        p5 = p5.reshape(batch, seq, n_cols)

        blk = dswa_width // HEAD_DIM
        y_a, (w_out_b, w_br_a_b, w_br_b_b, w_br_c_b, w_mem_kv_b) = _attn(
            p5, batch, seq, 0, blk, 2 * blk, casts=(w_out[l], w_br_a[l], w_br_b[l], w_br_c[l], w_mem_kv[l]))

        y_b = _hyena(p5, hy_conv_w[l], row(hy_conv_b[l]), f_tab, g_tab, spectra, 3 * dswa_width // hy_width, hy_width)

        y_c = _memattn(p5, mem, row(g_mem[l]), w_mem_kv_b, row(m_gk[l]), (n_cols - mem_width) // mem_width)

        t = batch * seq
        xt = _merge(xt, row(g_mix[l]), y_a.reshape(t, -1), y_b.reshape(t, -1), y_c.reshape(t, -1),
                    w_in_b, n_cols, w_br_a_b, w_br_b_b, w_br_c_b, w_out_b)

        xt, _ = _ffn(xt, row(g_ff2[l]), w_ff2_in_b, w_ff2_out_b, row(g_post[l]), True)
    return xt.reshape(batch, seq, d)
```

```python
import functools
import math

import jax
import jax.numpy as jnp
from jax import lax
from jax.experimental import pallas as pl
from jax.experimental.pallas import tpu as pltpu

F32 = jnp.float32
BF16 = jnp.bfloat16

HEAD_DIM = 128
ROPE_THETA = 500000.0
ROT_DIM = HEAD_DIM // 4
EPS = 1e-6
NEG = -1e30
DSWA_GROUPS = ((128, 1), (512, 4), (2048, 16))
DSWA_HEADS_PER_GROUP = 2
MEM_HEADS = 4
HY_ORDER = 2
HY_SHORT = 3
HY_FAST_DECAY = 0.3
HY_SLOW_DECAY = 1.5
HY_TARGET = 1e-2

LANES = 128
SUBLANES = 8
BF16_ROWS = 16
VMEM_LIMIT_BYTES = 60 * 1024 * 1024

FFN_TM = 1024
FFN_TF = 512
PROJ_TM = 512
PROJ_TN = 512
ATT_TQ = 128
ATT_BACK = 128
MEM_TQ = 1024
DFT_TF = 128
HY_SPLIT = 4
FILT_CAST_STEPS = 16
SHIFT_ROWS = 256
FILT_MLP_TL = 256
MERGE_TM = 1024
MERGE_TN = 256


def _cparams(sem):
    return pltpu.CompilerParams(dimension_semantics=sem, vmem_limit_bytes=VMEM_LIMIT_BYTES)


def _rms(x, g):
    return x * lax.rsqrt(jnp.mean(x * x, axis=-1, keepdims=True) + EPS) * g


def _ffn_kernel(x_ref, g_ref, wa_ref, wb_ref, wd_ref, gp_ref, *rest, nj, final_norm, n_cast):
    cast_in, o_ref, cast_out, h_ref = rest[:n_cast], rest[n_cast], rest[n_cast + 1:2 * n_cast + 1], rest[-1]
    j = pl.program_id(1)

    for src, dst in zip(cast_in, cast_out):
        dst[...] = src[...].astype(BF16)

    @pl.when(j == 0)
    def _():
        x = x_ref[...]
        h_ref[...] = _rms(x, g_ref[...]).astype(BF16)
        o_ref[...] = x

    h = h_ref[...]
    a = jnp.dot(h, wa_ref[...], preferred_element_type=F32)
    b = jnp.dot(h, wb_ref[...], preferred_element_type=F32)
    act = (0.5 * a * jax.nn.sigmoid(a) * b).astype(BF16)
    o_ref[...] += jnp.dot(act, wd_ref[...], preferred_element_type=F32)

    if final_norm:
        @pl.when(j == nj - 1)
        def _():
            o_ref[...] = _rms(o_ref[...], gp_ref[...])


def _cast_specs(casts, steps, flat):
    specs = []
    for a in casts:
        rows = BF16_ROWS
        while a.shape[0] % rows or a.shape[0] // rows > steps:
            rows += BF16_ROWS
        n = a.shape[0] // rows
        specs.append(pl.BlockSpec((rows, a.shape[1]), functools.partial(
            lambda *idx, n: (flat(*idx) * n // steps, 0), n=n)))
    return specs


def _ffn(x, g, w_in, w_out, g_post, final_norm, casts=()):
    t, d = x.shape
    d_ff = w_out.shape[0]
    nj = d_ff // FFN_TF
    cast_specs = _cast_specs(casts, (t // FFN_TM) * nj, lambda i, j: i * nj + j)
    kern = functools.partial(_ffn_kernel, nj=nj, final_norm=final_norm, n_cast=len(casts))
    out = pl.pallas_call(
        kern,
        grid=(t // FFN_TM, nj),
        in_specs=[
            pl.BlockSpec((FFN_TM, d), lambda i, j: (i, 0)),
            pl.BlockSpec((1, d), lambda i, j: (0, 0)),
            pl.BlockSpec((d, FFN_TF), lambda i, j: (0, j)),
            pl.BlockSpec((d, FFN_TF), lambda i, j: (0, j + nj)),
            pl.BlockSpec((FFN_TF, d), lambda i, j: (j, 0)),
            pl.BlockSpec((1, d), lambda i, j: (0, 0)),
        ] + cast_specs,
        out_specs=[pl.BlockSpec((FFN_TM, d), lambda i, j: (i, 0))] + cast_specs,
        out_shape=[jax.ShapeDtypeStruct((t, d), F32)] + [jax.ShapeDtypeStruct(a.shape, BF16) for a in casts],
        scratch_shapes=[pltpu.VMEM((FFN_TM, d), BF16)],
        compiler_params=_cparams(("arbitrary", "arbitrary")),
        name="ffn_final" if final_norm else "ffn",
    )(x, g, w_in, w_in, w_out, g_post, *casts)
    return out[0], tuple(out[1:])


def _mixproj_kernel(x0_ref, xn_ref, g_ref, w_ref, gain_ref, cos_ref, sin_ref, o_ref, h_ref, *, n_rope, n_plain):
    i = pl.program_id(0)
    slot = i % 2

    @pl.when(i == 0)
    def _():
        h_ref[0] = _rms(x0_ref[...], g_ref[...]).astype(BF16)

    h_ref[1 - slot] = _rms(xn_ref[...], g_ref[...]).astype(BF16)
    h = h_ref[slot]
    lane = lax.broadcasted_iota(jnp.int32, (PROJ_TM, HEAD_DIM), 1)
    first = lane < (ROT_DIM // 2)
    for j in range(w_ref.shape[1] // PROJ_TN):
        acc = jnp.dot(h, w_ref[:, j * PROJ_TN:(j + 1) * PROJ_TN], preferred_element_type=F32)
        plain = n_rope <= j < n_rope + n_plain
        for s in range(PROJ_TN // HEAD_DIM):
            sl = slice(j * PROJ_TN + s * HEAD_DIM, j * PROJ_TN + (s + 1) * HEAD_DIM)
            t = acc[:, s * HEAD_DIM:(s + 1) * HEAD_DIM]
            if not plain:
                t = _rms(t, gain_ref[:, sl])
            if j < n_rope:
                partner = jnp.where(first, pltpu.roll(t, HEAD_DIM - ROT_DIM // 2, axis=1),
                                    pltpu.roll(t, ROT_DIM // 2, axis=1))
                t = t * cos_ref[...] + partner * sin_ref[...]
            o_ref[:, sl] = t.astype(BF16)


def _mixproj(x, g, w, gain_cols, cos_t, sin_t, n_cols, n_rope, n_plain):
    t, d = x.shape
    n_row = t // PROJ_TM
    per_seq = cos_t.shape[0] // PROJ_TM
    kern = functools.partial(_mixproj_kernel, n_rope=n_rope, n_plain=n_plain)
    once = pl.Buffered(1)
    return pl.pallas_call(
        kern,
        grid=(n_row,),
        in_specs=[
            pl.BlockSpec((PROJ_TM, d), lambda i: (0, 0), pipeline_mode=once),
            pl.BlockSpec((PROJ_TM, d), lambda i: (jnp.minimum(i + 1, n_row - 1), 0)),
            pl.BlockSpec((1, d), lambda i: (0, 0)),
            pl.BlockSpec((d, n_cols), lambda i: (0, 0), pipeline_mode=once),
            pl.BlockSpec((1, n_cols), lambda i: (0, 0)),
            pl.BlockSpec((PROJ_TM, HEAD_DIM), lambda i: (i % per_seq, 0)),
            pl.BlockSpec((PROJ_TM, HEAD_DIM), lambda i: (i % per_seq, 0)),
        ],
        out_specs=pl.BlockSpec((PROJ_TM, n_cols), lambda i: (i, 0)),
        out_shape=jax.ShapeDtypeStruct((t, n_cols), BF16),
        scratch_shapes=[pltpu.VMEM((2, PROJ_TM, d), BF16)],
        compiler_params=_cparams(("arbitrary",)),
        name="mixproj",
    )(x, x, g, w, gain_cols, cos_t, sin_t)


def _banded_tiles(q, k, v, length):
    tiles = []
    for qb in range(length // ATT_TQ):
        i0 = qb * ATT_TQ
        lo = max(0, i0 - ATT_BACK)
        hi = min(length, i0 + ATT_TQ + ATT_BACK)
        tiles.append((i0, q[i0:i0 + ATT_TQ], k[lo:hi], v[lo:hi], lo - (i0 - ATT_BACK)))
    return tiles


def _banded_attention(tiles, tab_ref):
    scores = [lax.dot_general(q, k, (((1,), (1,)), ((), ())), preferred_element_type=F32)
              + tab_ref[:, c0:c0 + k.shape[0]] for _, q, k, _, c0 in tiles]
    maxes = [s.max(axis=-1, keepdims=True) for s in scores]
    probs = [jnp.exp(s - m) for s, m in zip(scores, maxes)]
    dens = [p.sum(axis=-1, keepdims=True) for p in probs]
    outs = [jnp.dot(p.astype(BF16), t[3], preferred_element_type=F32) / d for p, t, d in zip(probs, tiles, dens)]
    return [(o, m + jnp.log(d)) for o, m, d in zip(outs, maxes, dens)]


def _attn_kernel(*refs, seq, n_cast):
    qkv, tab_ref = refs[:9], refs[9]
    cast_in, o_ref, cast_out = refs[10:10 + n_cast], refs[10 + n_cast], refs[11 + n_cast:11 + 2 * n_cast]
    qf, kf, vf = refs[-7:-4]
    outs, lses = refs[-4:-2], refs[-2:]

    for src, dst in zip(cast_in, cast_out):
        dst[...] = src[...].astype(BF16)

    dilated = [(g, dil) for g, (_, dil) in enumerate(DSWA_GROUPS) if dil > 1]
    plain = [g for g, (_, dil) in enumerate(DSWA_GROUPS) if dil == 1]
    assert len(dilated) == len(outs) and len(plain) == 1

    for slot, (g, dil) in enumerate(dilated):
        length = seq // dil
        for src, dst in zip(qkv[3 * g:3 * g + 3], (qf, kf, vf)):
            dst[...] = src[...].astype(F32)
        tiles, rows = [], []
        for r in range(dil):
            q, k, v = (ref[pl.ds(r, length, stride=dil), :].astype(BF16) for ref in (qf, kf, vf))
            sub = _banded_tiles(q, k, v, length)
            tiles += sub
            rows += [pl.ds(t[0] * dil + r, ATT_TQ, stride=dil) for t in sub]
        for rw, (o, lse) in zip(rows, _banded_attention(tiles, tab_ref)):
            outs[slot][rw, :] = o
            lses[slot][rw, :] = jnp.broadcast_to(lse, (ATT_TQ, HEAD_DIM))

    g = plain[0]
    tiles = _banded_tiles(qkv[3 * g][...], qkv[3 * g + 1][...], qkv[3 * g + 2][...], seq)
    for (i0, *_), (o, lse) in zip(tiles, _banded_attention(tiles, tab_ref)):
        rw = slice(i0, i0 + ATT_TQ)
        parts = [(o, jnp.broadcast_to(lse, (ATT_TQ, HEAD_DIM)))] + [(a[rw, :], b[rw, :]) for a, b in zip(outs, lses)]
        m = functools.reduce(jnp.maximum, [l for _, l in parts])
        num = jnp.zeros((ATT_TQ, HEAD_DIM), F32)
        den = jnp.zeros((ATT_TQ, HEAD_DIM), F32)
        for a, l in parts:
            w = jnp.exp(l - m)
            num = num + w * a
            den = den + w
        o_ref[rw, :] = (num / den).astype(BF16)


def _attn_table():
    width = 2 * ATT_BACK + ATT_TQ
    row = jnp.arange(ATT_TQ, dtype=jnp.int32)[:, None]
    col = jnp.arange(width, dtype=jnp.int32)[None, :]
    half = DSWA_GROUPS[0][0] // (2 * DSWA_GROUPS[0][1])
    return jnp.where(jnp.abs(col - ATT_BACK - row) <= half, 0.0, NEG).astype(F32)


def _attn(p5, batch, seq, col_q, col_k, col_v, casts=()):
    halves = {win // (2 * dil) for win, dil in DSWA_GROUPS}
    assert len(halves) == 1 and halves.pop() <= ATT_BACK
    assert all(seq % dil == 0 and (seq // dil) % ATT_TQ == 0 for _, dil in DSWA_GROUPS)
    tab = _attn_table()
    hpg = DSWA_HEADS_PER_GROUP
    in_specs = []
    for g in range(len(DSWA_GROUPS)):
        for base in (col_q, col_k, col_v):
            in_specs.append(pl.BlockSpec((None, seq, HEAD_DIM),
                                         functools.partial(lambda b, h, c: (b, 0, c + h), c=base + g * hpg)))
    in_specs.append(pl.BlockSpec(tab.shape, lambda b, h: (0, 0)))
    cast_specs = _cast_specs(casts, batch * hpg, lambda b, h: b * hpg + h)
    out = pl.pallas_call(
        functools.partial(_attn_kernel, seq=seq, n_cast=len(casts)),
        grid=(batch, hpg),
        in_specs=in_specs + cast_specs,
        out_specs=[pl.BlockSpec((None, seq, HEAD_DIM), lambda b, h: (b, 0, h))] + cast_specs,
        out_shape=[jax.ShapeDtypeStruct((batch, seq, hpg * HEAD_DIM), BF16)]
                  + [jax.ShapeDtypeStruct(a.shape, BF16) for a in casts],
        scratch_shapes=[pltpu.VMEM((seq, HEAD_DIM), F32)] * 7,
        compiler_params=_cparams(("arbitrary", "arbitrary")),
        name="attn",
    )(*([p5] * 9), tab, *casts)
    return out[0], tuple(out[1:])


def _memattn_kernel(q_ref, mem_ref, g_ref, wkv_ref, gk_ref, o_ref, k_ref, v_ref):
    width = MEM_HEADS * HEAD_DIM

    @pl.when(pl.program_id(1) == 0)
    def _():
        mn = _rms(mem_ref[...], g_ref[...]).astype(BF16)
        kv = jnp.dot(mn, wkv_ref[...], preferred_element_type=F32)
        for h in range(MEM_HEADS):
            sl = slice(h * HEAD_DIM, (h + 1) * HEAD_DIM)
            k_ref[:, sl] = _rms(kv[:, sl], gk_ref[...]).astype(BF16)
        v_ref[...] = kv[:, width:].astype(BF16)

    for h in range(MEM_HEADS):
        sl = slice(h * HEAD_DIM, (h + 1) * HEAD_DIM)
        s = lax.dot_general(q_ref[:, sl], k_ref[:, sl], (((1,), (1,)), ((), ())), preferred_element_type=F32)
        p = jnp.exp(s - s.max(axis=-1, keepdims=True))
        den = p.sum(axis=-1, keepdims=True)
        o = jnp.dot(p.astype(BF16), v_ref[:, sl], preferred_element_type=F32)
        o_ref[:, sl] = (o / den).astype(BF16)


def _memattn(p5, mem, g_mem, w_kv, gk, col_block):
    batch, seq, _ = p5.shape
    n_mem, d = mem.shape[1:]
    width = MEM_HEADS * HEAD_DIM
    return pl.pallas_call(
        _memattn_kernel,
        grid=(batch, seq // MEM_TQ),
        in_specs=[
            pl.BlockSpec((None, MEM_TQ, width), lambda b, i: (b, i, col_block)),
            pl.BlockSpec((None, n_mem, d), lambda b, i: (b, 0, 0)),
            pl.BlockSpec((1, d), lambda b, i: (0, 0)),
            pl.BlockSpec((d, 2 * width), lambda b, i: (0, 0)),
            pl.BlockSpec((1, HEAD_DIM), lambda b, i: (0, 0)),
        ],
        out_specs=pl.BlockSpec((None, MEM_TQ, width), lambda b, i: (b, i, 0)),
        out_shape=jax.ShapeDtypeStruct((batch, seq, width), BF16),
        scratch_shapes=[pltpu.VMEM((n_mem, width), BF16), pltpu.VMEM((n_mem, width), BF16)],
        compiler_params=_cparams(("parallel", "arbitrary")),
        name="memattn",
    )(p5, mem, g_mem, w_kv, gk)


def _dft_kernel(f_ref, g_ref, cb, sb, cbt, sbt, cac, sac, *, seq):
    t = pl.program_id(0)
    n2 = 4 * seq
    theta = 2.0 * math.pi / n2

    def trig(m):
        ang = (m & (n2 - 1)).astype(F32) * theta
        return jnp.cos(ang), jnp.sin(ang)

    @pl.when(t == 0)
    def _():
        f_lo = lax.broadcasted_iota(jnp.int32, (DFT_TF, seq), 0)
        s = lax.broadcasted_iota(jnp.int32, (DFT_TF, seq), 1)
        cb[...], sb[...] = trig((2 * f_lo + 1) * s)
        s = lax.broadcasted_iota(jnp.int32, (seq, DFT_TF), 0)
        f_lo = lax.broadcasted_iota(jnp.int32, (seq, DFT_TF), 1)
        cbt[...], sbt[...] = trig((2 * f_lo + 1) * s)
        s = lax.broadcasted_iota(jnp.int32, (seq, LANES), 0)
        tile = lax.broadcasted_iota(jnp.int32, (seq, LANES), 1)
        cac[...], sac[...] = trig(2 * DFT_TF * tile * s)

    s = lax.broadcasted_iota(jnp.int32, (1, seq), 1)
    ca, sa = trig(2 * DFT_TF * t * s)
    f_ref[:DFT_TF, :] = (ca * cb[...] - sa * sb[...]).astype(BF16)
    f_ref[DFT_TF:, :] = (-(sa * cb[...] + ca * sb[...])).astype(BF16)
    pick = lax.broadcasted_iota(jnp.int32, (seq, LANES), 1) == t
    ca = jnp.sum(jnp.where(pick, cac[...], 0.0), axis=-1, keepdims=True)
    sa = jnp.sum(jnp.where(pick, sac[...], 0.0), axis=-1, keepdims=True)
    g_ref[:, :DFT_TF] = (ca * cbt[...] - sa * sbt[...]).astype(BF16)
    g_ref[:, DFT_TF:] = (-(sa * cbt[...] + ca * sbt[...])).astype(BF16)


def _dft_tables(seq):
    nf = seq // DFT_TF
    assert nf <= LANES
    return pl.pallas_call(
        functools.partial(_dft_kernel, seq=seq),
        grid=(nf,),
        out_specs=[pl.BlockSpec((2 * DFT_TF, seq), lambda t: (t, 0)),
                   pl.BlockSpec((seq, 2 * DFT_TF), lambda t: (0, t))],
        out_shape=[jax.ShapeDtypeStruct((2 * seq, seq), BF16), jax.ShapeDtypeStruct((seq, 2 * seq), BF16)],
        scratch_shapes=[pltpu.VMEM((DFT_TF, seq), F32)] * 2 + [pltpu.VMEM((seq, DFT_TF), F32)] * 2
                       + [pltpu.VMEM((seq, LANES), F32)] * 2,
        compiler_params=_cparams(("arbitrary",)),
        name="dft_tables",
    )()


def _filter_mlp_kernel(z_ref, w1_ref, b1_ref, w2_ref, b2_ref, w3_ref, b3_ref, w4_ref, fr_ref, t_ref, d_ref,
                       *rest, n_cast):
    cast_in, o_ref, cast_out = rest[:n_cast], rest[n_cast], rest[n_cast + 1:]
    for src, dst in zip(cast_in, cast_out):
        dst[...] = src[...].astype(BF16)

    hp = lax.Precision.HIGHEST
    fr = fr_ref[...]
    hh = jnp.sin(fr * (jnp.dot(z_ref[...], w1_ref[...], precision=hp, preferred_element_type=F32) + b1_ref[...]))
    hh = jnp.sin(fr * (jnp.dot(hh, w2_ref[...], precision=hp, preferred_element_type=F32) + b2_ref[...]))
    hh = jnp.sin(fr * (jnp.dot(hh, w3_ref[...], precision=hp, preferred_element_type=F32) + b3_ref[...]))
    split = lambda a: (a.astype(BF16), (a - a.astype(BF16).astype(F32)).astype(BF16))
    (a_hi, a_lo), (w_hi, w_lo) = split(hh), split(w4_ref[...])
    h = (jnp.dot(a_hi, w_hi, preferred_element_type=F32) + jnp.dot(a_hi, w_lo, preferred_element_type=F32)
         + jnp.dot(a_lo, w_hi, preferred_element_type=F32))
    decay = jnp.exp(-t_ref[...] * d_ref[...])
    width = decay.shape[1]
    for c in range(h.shape[1] // width):
        o_ref[:, c * width:(c + 1) * width] = h[:, c * width:(c + 1) * width] * decay


def _filter_mlp(z, w1, b1, w2, b2, w3, b3, w4, freq, t_col, absd, casts=()):
    seq = z.shape[0]
    tl = FILT_MLP_TL
    n_out = w4.shape[1]
    full = lambda a: pl.BlockSpec(a.shape, lambda i: (0, 0))
    cast_specs = _cast_specs(casts, seq // tl, lambda i: i)
    out = pl.pallas_call(
        functools.partial(_filter_mlp_kernel, n_cast=len(casts)),
        grid=(seq // tl,),
        in_specs=[pl.BlockSpec((tl, z.shape[1]), lambda i: (i, 0)),
                  full(w1), full(b1), full(w2), full(b2), full(w3), full(b3), full(w4), full(freq),
                  pl.BlockSpec((tl, 1), lambda i: (i, 0)), full(absd)] + cast_specs,
        out_specs=[pl.BlockSpec((tl, n_out), lambda i: (i, 0))] + cast_specs,
        out_shape=[jax.ShapeDtypeStruct((seq, n_out), F32)] + [jax.ShapeDtypeStruct(a.shape, BF16) for a in casts],
        compiler_params=_cparams(("arbitrary",)),
        name="hyena_filter_mlp",
    )(z, w1, b1, w2, b2, w3, b3, w4, freq, t_col, absd, *casts)
    return out[0], tuple(out[1:])


def _segment(delta):
    return 0 if delta == 0 else (2 * delta - 1 if delta > 0 else -2 * delta)


def _filter_dft_kernel(hf_ref, hb_ref, bias_ref, f_ref, *rest, blk, n_cast):
    cast_in, k_ref, cast_out, r_ref = rest[:n_cast], rest[n_cast], rest[n_cast + 1:2 * n_cast + 1], rest[-1]
    width = hf_ref.shape[1]

    for src, dst in zip(cast_in, cast_out):
        dst[...] = src[...].astype(BF16)

    @pl.when((pl.program_id(1) == 0) & (pl.program_id(2) == 0))
    def _():
        for c, ref in enumerate((hf_ref, hb_ref)):
            for part in range(HY_SPLIT):
                col = (HY_SPLIT * c + part) * width
                r_ref[:, col:col + width] = ref[part * blk:(part + 1) * blk, :].astype(BF16)

    @pl.when(pl.program_id(2) == 0)
    def _():
        r = r_ref[...]
        re = jnp.dot(f_ref[:DFT_TF, :], r, preferred_element_type=F32)
        im = jnp.dot(f_ref[DFT_TF:, :], r, preferred_element_type=F32)

        def transform(c, j):
            col = (HY_SPLIT * c + j) * width
            return re[:, col:col + width], im[:, col:col + width]

        row = lax.broadcasted_iota(jnp.int32, (DFT_TF, width), 0)
        sgn = jnp.where((row & 1) == 0, 1.0, -1.0)
        scale = 1.0 / blk
        (f0r, f0i), (b0r, b0i) = transform(0, 0), transform(1, 0)
        k_ref[0] = (f0r + b0r + bias_ref[...]) * scale
        k_ref[1] = (f0i - b0i) * scale
        for d in range(1, HY_SPLIT):
            for c, ref, conj in ((0, hf_ref, 1.0), (1, hb_ref, -1.0)):
                (ar, ai), (pr, pi) = transform(c, d), transform(c, d - 1)
                edge = ref[(d - 1) * blk:(d - 1) * blk + 1, :]
                n = 2 * _segment(d if c == 0 else -d)
                k_ref[n] = (ar - sgn * pi) * scale
                k_ref[n + 1] = (ai + sgn * (pr - edge)) * (conj * scale)


def _filter_dft(hfilt, bias, f_tab, width, casts=()):
    seq = hfilt.shape[0]
    blk = seq // HY_SPLIT
    nf = blk // DFT_TF
    n_spec = 2 * (2 * HY_SPLIT - 1)
    assert DFT_TF % 2 == 0
    pace = FILT_CAST_STEPS // (HY_ORDER * nf)
    cast_specs = _cast_specs(casts, HY_ORDER * nf * pace, lambda o, f, c: (o * nf + f) * pace + c)
    kern = functools.partial(_filter_dft_kernel, blk=blk, n_cast=len(casts))
    once = pl.Buffered(1)
    out = pl.pallas_call(
        kern,
        grid=(HY_ORDER, nf, pace),
        in_specs=[pl.BlockSpec((seq, width), lambda o, f, c: (0, 2 * o), pipeline_mode=once),
                  pl.BlockSpec((seq, width), lambda o, f, c: (0, 2 * o + 1), pipeline_mode=once),
                  pl.BlockSpec((None, 1, width), lambda o, f, c: (o, 0, 0)),
                  pl.BlockSpec((2 * DFT_TF, blk), lambda o, f, c: (f, 0))] + cast_specs,
        out_specs=[pl.BlockSpec((None, n_spec, DFT_TF, width), lambda o, f, c: (o, 0, f, 0))] + cast_specs,
        out_shape=[jax.ShapeDtypeStruct((HY_ORDER, n_spec, blk, width), F32)]
                  + [jax.ShapeDtypeStruct(a.shape, BF16) for a in casts],
        scratch_shapes=[pltpu.VMEM((blk, 2 * HY_SPLIT * width), BF16)],
        compiler_params=_cparams(("arbitrary", "arbitrary", "arbitrary")),
        name="hyena_filter_dft",
    )(hfilt, hfilt, bias, f_tab, *casts)
    return out[0], tuple(out[1:])


def _short_conv(u_ref, w_ref, b_ref, part, width, emit):
    seq = u_ref.shape[0]
    sl = slice(part * width, (part + 1) * width)
    w0, w1, w2, b = w_ref[0:1, sl], w_ref[1:2, sl], w_ref[2:3, sl], b_ref[:, sl]
    n = SHIFT_ROWS
    r = lax.broadcasted_iota(jnp.int32, (n, n), 0)
    c = lax.broadcasted_iota(jnp.int32, (n, n), 1)
    down = (c == r - 1).astype(BF16)
    up = (c == r + 1).astype(BF16)
    sub = SUBLANES
    row = lax.broadcasted_iota(jnp.int32, (sub, width), 0)
    for k in range(seq // n):
        ub = u_ref[k * n:(k + 1) * n, :]
        prev = jnp.dot(down, ub, preferred_element_type=F32)
        nxt = jnp.dot(up, ub, preferred_element_type=F32)
        if k > 0:
            edge = u_ref[k * n - BF16_ROWS:k * n, :].astype(F32)[BF16_ROWS - 1:BF16_ROWS]
            prev = jnp.concatenate([prev[:sub] + jnp.where(row == 0, edge, 0.0), prev[sub:]], axis=0)
        if k < seq // n - 1:
            edge = u_ref[(k + 1) * n:(k + 1) * n + BF16_ROWS, :].astype(F32)[0:1]
            nxt = jnp.concatenate([nxt[:-sub], nxt[-sub:] + jnp.where(row == sub - 1, edge, 0.0)], axis=0)
        emit(k * n, prev * w0 + ub.astype(F32) * w1 + nxt * w2 + b)


def _hyena_kernel(u0_ref, u1_ref, u2_ref, w_ref, b_ref, f_ref, g_ref, k_ref, o_ref, z_ref, acc_ref,
                  *, nf, width, blk):
    j = pl.program_id(1)
    cols = [slice(i * width, (i + 1) * width) for i in range(HY_SPLIT)]

    def place(row0, rows):
        return slice(row0 % blk, row0 % blk + rows), cols[row0 // blk]

    def first_input(row0, v):
        z_ref[place(row0, v.shape[0])] = v.astype(BF16)

    def next_input(row0, v):
        at = place(row0, v.shape[0])
        z_ref[at] = (v * acc_ref[at]).astype(BF16)

    def result(row0, v):
        o_ref[row0:row0 + v.shape[0], :] = (v * acc_ref[place(row0, v.shape[0])]).astype(BF16)

    @pl.when(j == 0)
    def _():
        _short_conv(u0_ref, w_ref, b_ref, 0, width, first_input)
        acc_ref[...] = jnp.zeros_like(acc_ref)

    uv = jnp.dot(f_ref[...], z_ref[...], preferred_element_type=F32)
    zr = [uv[:DFT_TF, c] for c in cols]
    zi = [uv[DFT_TF:, c] for c in cols]
    yr, yi = [], []
    for o in range(HY_SPLIT):
        acc_r = acc_i = None
        for i in range(HY_SPLIT):
            n = 2 * _segment(o - i)
            kr, ki = k_ref[n], k_ref[n + 1]
            tr = zr[i] * kr - zi[i] * ki
            ti = zr[i] * ki + zi[i] * kr
            acc_r, acc_i = (tr, ti) if acc_r is None else (acc_r + tr, acc_i + ti)
        yr.append(acc_r)
        yi.append(acc_i)
    y = jnp.concatenate([jnp.concatenate(yr, axis=1), jnp.concatenate(yi, axis=1)], axis=0)
    acc_ref[...] += jnp.dot(g_ref[...], y.astype(BF16), preferred_element_type=F32)

    @pl.when(j == nf - 1)
    def _():
        _short_conv(u1_ref, w_ref, b_ref, 1, width, next_input)
        acc_ref[...] = jnp.zeros_like(acc_ref)

    @pl.when(j == 2 * nf - 1)
    def _():
        _short_conv(u2_ref, w_ref, b_ref, 2, width, result)


def _hyena(p5, conv_w, conv_b, f_tab, g_tab, spectra, col_block, width):
    batch, seq, _ = p5.shape
    blk = seq // HY_SPLIT
    nf = blk // DFT_TF
    n_spec = spectra.shape[1]
    assert HY_ORDER == 2 and n_spec == 2 * (2 * HY_SPLIT - 1) and blk % SHIFT_ROWS == 0
    kern = functools.partial(_hyena_kernel, nf=nf, width=width, blk=blk)
    u_spec = lambda part: pl.BlockSpec((None, seq, width), lambda b, j: (b, 0, col_block + part))
    return pl.pallas_call(
        kern,
        grid=(batch, HY_ORDER * nf),
        in_specs=[u_spec(0), u_spec(1), u_spec(2),
                  pl.BlockSpec(conv_w.shape, lambda b, j: (0, 0)),
                  pl.BlockSpec(conv_b.shape, lambda b, j: (0, 0)),
                  pl.BlockSpec((2 * DFT_TF, blk), lambda b, j: (j % nf, 0)),
                  pl.BlockSpec((blk, 2 * DFT_TF), lambda b, j: (0, j % nf)),
                  pl.BlockSpec((None, n_spec, DFT_TF, width), lambda b, j: (j // nf, 0, j % nf, 0))],
        out_specs=pl.BlockSpec((None, seq, width), lambda b, j: (b, 0, 0)),
        out_shape=jax.ShapeDtypeStruct((batch, seq, width), BF16),
        scratch_shapes=[pltpu.VMEM((blk, HY_SPLIT * width), BF16), pltpu.VMEM((blk, HY_SPLIT * width), F32)],
        compiler_params=_cparams(("parallel", "arbitrary")),
        name="hyena_conv",
    )(p5, p5, p5, conv_w, conv_b, f_tab, g_tab, spectra)


def _merge_kernel(x_ref, g_ref, ya_ref, yb_ref, yc_ref, wga_ref, wgb_ref, wgc_ref, wa_ref, wb_ref, wc_ref,
                  wo_ref, o_ref, h_ref, m_ref, *, nj):
    s = pl.program_id(0)
    slot = s % 2

    @pl.when(s == 0)
    def _():
        m_ref[1] = jnp.zeros(m_ref.shape[1:], BF16)
        o_ref[...] = jnp.zeros_like(o_ref)

    @pl.when(s % nj == 0)
    def _():
        h_ref[...] = _rms(x_ref[...], g_ref[...]).astype(BF16)

    first = (s - 1) % nj == 0
    base = jnp.where(first, x_ref[...], o_ref[...])
    o_ref[...] = base + jnp.dot(m_ref[1 - slot], wo_ref[...], preferred_element_type=F32)

    h = h_ref[...]

    def branch(wg_ref, y_ref, w_ref):
        gate = jax.nn.sigmoid(jnp.dot(h, wg_ref[...], preferred_element_type=F32))
        return gate * jnp.dot(y_ref[...], w_ref[...], preferred_element_type=F32)

    merged = branch(wga_ref, ya_ref, wa_ref) + branch(wgb_ref, yb_ref, wb_ref) + branch(wgc_ref, yc_ref, wc_ref)
    m_ref[slot] = merged.astype(BF16)


def _merge(x, g, ya, yb, yc, w_gate, gate_col, w_a, w_b, w_c, w_o):
    t, d = x.shape
    nj = d // MERGE_TN
    g0 = gate_col // MERGE_TN
    n = (t // MERGE_TM) * nj
    cur = lambda s: jnp.minimum(s, n - 1)
    prev = lambda s: jnp.maximum(s - 1, 0)
    rows = lambda a: pl.BlockSpec((MERGE_TM, a.shape[1]), lambda s: (cur(s) // nj, 0))
    cols = lambda a, off: pl.BlockSpec((a.shape[0], MERGE_TN), lambda s: (0, cur(s) % nj + off))
    return pl.pallas_call(
        functools.partial(_merge_kernel, nj=nj),
        grid=(n + 1,),
        in_specs=[pl.BlockSpec((MERGE_TM, d), lambda s: (cur(s) // nj, 0)),
                  pl.BlockSpec((1, d), lambda s: (0, 0)), rows(ya), rows(yb), rows(yc),
                  cols(w_gate, g0), cols(w_gate, g0 + nj), cols(w_gate, g0 + 2 * nj),
                  cols(w_a, 0), cols(w_b, 0), cols(w_c, 0),
                  pl.BlockSpec((MERGE_TN, d), lambda s: (prev(s) % nj, 0))],
        out_specs=pl.BlockSpec((MERGE_TM, d), lambda s: (prev(s) // nj, 0)),
        out_shape=jax.ShapeDtypeStruct((t, d), F32),
        scratch_shapes=[pltpu.VMEM((MERGE_TM, d), BF16), pltpu.VMEM((2, MERGE_TM, MERGE_TN), BF16)],
        compiler_params=_cparams(("arbitrary",)),
        name="merge",
    )(x, g, ya, yb, yc, w_gate, w_gate, w_gate, w_a, w_b, w_c, w_o)


def _rope_tables(seq):
    half = ROT_DIM // 2
    inv = jnp.power(ROPE_THETA, -jnp.arange(0, ROT_DIM, 2, dtype=F32) / ROT_DIM)
    ang = jnp.arange(seq, dtype=F32)[:, None] * inv[None, :]
    cos, sin = jnp.cos(ang), jnp.sin(ang)
    ones = jnp.ones((seq, HEAD_DIM - ROT_DIM), F32)
    cos_t = jnp.concatenate([cos, cos, ones], axis=1)
    sin_t = jnp.concatenate([-sin, sin, 0.0 * ones], axis=1)
    assert cos_t.shape == (seq, HEAD_DIM) and half * 2 == ROT_DIM
    return cos_t, sin_t


def _hyena_positional_features(seq, emb):
    bands = (emb - 1) // 2
    t = jnp.linspace(0.0, 1.0, seq, dtype=F32)[:, None]
    w = 2.0 * math.pi * jnp.arange(seq, dtype=F32)[:, None] / seq
    f = jnp.linspace(1e-4, bands - 1, bands, dtype=F32)[None, :]
    return jnp.concatenate([t, jnp.cos(f * w), -jnp.sin(f * w)], axis=-1)


def _pad_to(a, shape):
    return jnp.pad(a, [(0, s - d) for s, d in zip(shape, a.shape)])


def kernel(x, mem, g_ff1, w_ff1_in, w_ff1_out, g_mix, w_in, a_gq, a_gk, hy_conv_w, hy_conv_b, hy_f_w1, hy_f_b1,
           hy_f_w2, hy_f_b2, hy_f_w3, hy_f_b3, hy_f_w4, hy_f_freq, hy_bias, g_mem, w_mem_kv, m_gq, m_gk,
           w_br_a, w_br_b, w_br_c, w_out, g_ff2, w_ff2_in, w_ff2_out, g_post):
    batch, seq, d = x.shape
    depth = g_ff1.shape[0]
    dswa_width = len(DSWA_GROUPS) * DSWA_HEADS_PER_GROUP * HEAD_DIM
    hy_width = hy_bias.shape[-1]
    mem_width = MEM_HEADS * HEAD_DIM
    n_cols = 3 * dswa_width + (HY_ORDER + 1) * hy_width + mem_width
    n_rope = 2 * dswa_width // PROJ_TN
    n_plain = (n_cols - mem_width) // PROJ_TN - n_rope
    assert 2 * dswa_width % PROJ_TN == 0 and mem_width == PROJ_TN and n_cols % PROJ_TN == 0
    assert 3 * dswa_width % hy_width == 0 and n_cols % MERGE_TN == 0 and seq & (seq - 1) == 0
    scale = 1.0 / math.sqrt(HEAD_DIM)

    cos_t, sin_t = _rope_tables(seq)
    f_tab, g_tab = _dft_tables(seq // HY_SPLIT)
    emb, hidden = hy_f_w1.shape[1:]
    z_feat = _pad_to(_hyena_positional_features(seq, emb), (seq, LANES))
    t_col = jnp.linspace(0.0, 1.0, seq, dtype=F32)[:, None]
    deltas = jnp.linspace(math.log(HY_TARGET) / HY_SLOW_DECAY, math.log(HY_TARGET) / HY_FAST_DECAY, hy_width, dtype=F32)
    absd = jnp.abs(deltas)[None, :]
    row = lambda v: v.reshape(1, -1)

    xt = x.reshape(batch * seq, d)
    for l in range(depth):
        pad_h = lambda a, shape: _pad_to(a.astype(F32), shape)
        hfilt, (w_ff1_out_b,) = _filter_mlp(
            z_feat, pad_h(hy_f_w1[l], (LANES, LANES)), pad_h(row(hy_f_b1[l]), (1, LANES)),
            pad_h(hy_f_w2[l], (LANES, LANES)), pad_h(row(hy_f_b2[l]), (1, LANES)),
            pad_h(hy_f_w3[l], (LANES, LANES)), pad_h(row(hy_f_b3[l]), (1, LANES)),
            pad_h(hy_f_w4[l], (LANES, hy_f_w4.shape[-1])), pad_h(row(hy_f_freq[l]), (1, LANES)),
            t_col, absd, casts=(w_ff1_out[l],))
        spectra, (w_ff1_in_b,) = _filter_dft(
            hfilt, hy_bias[l].reshape(HY_ORDER, 1, hy_width), f_tab, hy_width, casts=(w_ff1_in[l],))

        xt, (w_ff2_in_b, w_ff2_out_b, w_in_b) = _ffn(
            xt, row(g_ff1[l]), w_ff1_in_b, w_ff1_out_b, row(g_post[l]), False,
            casts=(w_ff2_in[l], w_ff2_out[l], w_in[l]))

        heads = dswa_width // HEAD_DIM
        gain_cols = jnp.concatenate([jnp.tile(a_gq[l], heads) * scale, jnp.tile(a_gk[l], heads),
                                     jnp.ones((n_cols - 2 * dswa_width - mem_width,), F32),
                                     jnp.tile(m_gq[l], MEM_HEADS) * scale])[None, :]
        p5 = _mixproj(xt, row(g_mix[l]), w_in_b, gain_cols, cos_t, sin_t, n_cols, n_rope, n_plain)
        p5 = p5.reshape(batch, seq, n_cols)

        blk = dswa_width // HEAD_DIM
        y_a, (w_out_b, w_br_a_b, w_br_b_b, w_br_c_b, w_mem_kv_b) = _attn(
            p5, batch, seq, 0, blk, 2 * blk, casts=(w_out[l], w_br_a[l], w_br_b[l], w_br_c[l], w_mem_kv[l]))

        y_b = _hyena(p5, hy_conv_w[l], row(hy_conv_b[l]), f_tab, g_tab, spectra, 3 * dswa_width // hy_width, hy_width)

        y_c = _memattn(p5, mem, row(g_mem[l]), w_mem_kv_b, row(m_gk[l]), (n_cols - mem_width) // mem_width)

        t = batch * seq
        xt = _merge(xt, row(g_mix[l]), y_a.reshape(t, -1), y_b.reshape(t, -1), y_c.reshape(t, -1),
                    w_in_b, n_cols, w_br_a_b, w_br_b_b, w_br_c_b, w_out_b)

        xt, _ = _ffn(xt, row(g_ff2[l]), w_ff2_in_b, w_ff2_out_b, row(g_post[l]), True)
    return xt.reshape(batch, seq, d)
```

```python
import functools
import math

import jax
import jax.numpy as jnp
from jax import lax
from jax.experimental import pallas as pl
from jax.experimental.pallas import tpu as pltpu

F32 = jnp.float32
BF16 = jnp.bfloat16

HEAD_DIM = 128
ROPE_THETA = 500000.0
ROT_DIM = HEAD_DIM // 4
EPS = 1e-6
NEG = -1e30
DSWA_GROUPS = ((128, 1), (512, 4), (2048, 16))
DSWA_HEADS_PER_GROUP = 2
MEM_HEADS = 4
HY_ORDER = 2
HY_SHORT = 3
HY_FAST_DECAY = 0.3
HY_SLOW_DECAY = 1.5
HY_TARGET = 1e-2

LANES = 128
SUBLANES = 8
BF16_ROWS = 16
VMEM_LIMIT_BYTES = 60 * 1024 * 1024

FFN_TM = 1024
FFN_TF = 512
PROJ_TM = 512
PROJ_TN = 512
ATT_TQ = 128
ATT_BACK = 128
MEM_TQ = 1024
DFT_TF = 128
HY_SPLIT = 4
FILT_CAST_STEPS = 16
SHIFT_ROWS = 256
FILT_MLP_TL = 256
MERGE_TM = 1024
MERGE_TN = 256


def _cparams(sem):
    return pltpu.CompilerParams(dimension_semantics=sem, vmem_limit_bytes=VMEM_LIMIT_BYTES)


def _rms(x, g):
    return x * lax.rsqrt(jnp.mean(x * x, axis=-1, keepdims=True) + EPS) * g


def _ffn_kernel(x_ref, g_ref, wa_ref, wb_ref, wd_ref, gp_ref, *rest, nj, final_norm, n_cast):
    cast_in, o_ref, cast_out, h_ref = rest[:n_cast], rest[n_cast], rest[n_cast + 1:2 * n_cast + 1], rest[-1]
    j = pl.program_id(1)

    for src, dst in zip(cast_in, cast_out):
        dst[...] = src[...].astype(BF16)

    @pl.when(j == 0)
    def _():
        x = x_ref[...]
        h_ref[...] = _rms(x, g_ref[...]).astype(BF16)
        o_ref[...] = x

    h = h_ref[...]
    a = jnp.dot(h, wa_ref[...], preferred_element_type=F32)
    b = jnp.dot(h, wb_ref[...], preferred_element_type=F32)
    act = (0.5 * a * jax.nn.sigmoid(a) * b).astype(BF16)
    o_ref[...] += jnp.dot(act, wd_ref[...], preferred_element_type=F32)

    if final_norm:
        @pl.when(j == nj - 1)
        def _():
            o_ref[...] = _rms(o_ref[...], gp_ref[...])


def _cast_specs(casts, steps, flat):
    specs = []
    for a in casts:
        rows = BF16_ROWS
        while a.shape[0] % rows or a.shape[0] // rows > steps:
            rows += BF16_ROWS
        n = a.shape[0] // rows
        specs.append(pl.BlockSpec((rows, a.shape[1]), functools.partial(
            lambda *idx, n: (flat(*idx) * n // steps, 0), n=n)))
    return specs


def _ffn(x, g, w_in, w_out, g_post, final_norm, casts=()):
    t, d = x.shape
    d_ff = w_out.shape[0]
    nj = d_ff // FFN_TF
    cast_specs = _cast_specs(casts, (t // FFN_TM) * nj, lambda i, j: i * nj + j)
    kern = functools.partial(_ffn_kernel, nj=nj, final_norm=final_norm, n_cast=len(casts))
    out = pl.pallas_call(
        kern,
        grid=(t // FFN_TM, nj),
        in_specs=[
            pl.BlockSpec((FFN_TM, d), lambda i, j: (i, 0)),
            pl.BlockSpec((1, d), lambda i, j: (0, 0)),
            pl.BlockSpec((d, FFN_TF), lambda i, j: (0, j)),
            pl.BlockSpec((d, FFN_TF), lambda i, j: (0, j + nj)),
            pl.BlockSpec((FFN_TF, d), lambda i, j: (j, 0)),
            pl.BlockSpec((1, d), lambda i, j: (0, 0)),
        ] + cast_specs,
        out_specs=[pl.BlockSpec((FFN_TM, d), lambda i, j: (i, 0))] + cast_specs,
        out_shape=[jax.ShapeDtypeStruct((t, d), F32)] + [jax.ShapeDtypeStruct(a.shape, BF16) for a in casts],
        scratch_shapes=[pltpu.VMEM((FFN_TM, d), BF16)],
        compiler_params=_cparams(("arbitrary", "arbitrary")),
        name="ffn_final" if final_norm else "ffn",
    )(x, g, w_in, w_in, w_out, g_post, *casts)
    return out[0], tuple(out[1:])


def _mixproj_kernel(x0_ref, xn_ref, g_ref, w_ref, gain_ref, cos_ref, sin_ref, o_ref, h_ref, *, n_rope, n_plain):
    i = pl.program_id(0)
    slot = i % 2

    @pl.when(i == 0)
    def _():
        h_ref[0] = _rms(x0_ref[...], g_ref[...]).astype(BF16)

    h_ref[1 - slot] = _rms(xn_ref[...], g_ref[...]).astype(BF16)
    h = h_ref[slot]
    lane = lax.broadcasted_iota(jnp.int32, (PROJ_TM, HEAD_DIM), 1)
    first = lane < (ROT_DIM // 2)
    for j in range(w_ref.shape[1] // PROJ_TN):
        acc = jnp.dot(h, w_ref[:, j * PROJ_TN:(j + 1) * PROJ_TN], preferred_element_type=F32)
        plain = n_rope <= j < n_rope + n_plain
        for s in range(PROJ_TN // HEAD_DIM):
            sl = slice(j * PROJ_TN + s * HEAD_DIM, j * PROJ_TN + (s + 1) * HEAD_DIM)
            t = acc[:, s * HEAD_DIM:(s + 1) * HEAD_DIM]
            if not plain:
                t = _rms(t, gain_ref[:, sl])
            if j < n_rope:
                partner = jnp.where(first, pltpu.roll(t, HEAD_DIM - ROT_DIM // 2, axis=1),
                                    pltpu.roll(t, ROT_DIM // 2, axis=1))
                t = t * cos_ref[...] + partner * sin_ref[...]
            o_ref[:, sl] = t.astype(BF16)


def _mixproj(x, g, w, gain_cols, cos_t, sin_t, n_cols, n_rope, n_plain):
    t, d = x.shape
    n_row = t // PROJ_TM
    per_seq = cos_t.shape[0] // PROJ_TM
    kern = functools.partial(_mixproj_kernel, n_rope=n_rope, n_plain=n_plain)
    once = pl.Buffered(1)
    return pl.pallas_call(
        kern,
        grid=(n_row,),
        in_specs=[
            pl.BlockSpec((PROJ_TM, d), lambda i: (0, 0), pipeline_mode=once),
            pl.BlockSpec((PROJ_TM, d), lambda i: (jnp.minimum(i + 1, n_row - 1), 0)),
            pl.BlockSpec((1, d), lambda i: (0, 0)),
            pl.BlockSpec((d, n_cols), lambda i: (0, 0), pipeline_mode=once),
            pl.BlockSpec((1, n_cols), lambda i: (0, 0)),
            pl.BlockSpec((PROJ_TM, HEAD_DIM), lambda i: (i % per_seq, 0)),
            pl.BlockSpec((PROJ_TM, HEAD_DIM), lambda i: (i % per_seq, 0)),
        ],
        out_specs=pl.BlockSpec((PROJ_TM, n_cols), lambda i: (i, 0)),
        out_shape=jax.ShapeDtypeStruct((t, n_cols), BF16),
        scratch_shapes=[pltpu.VMEM((2, PROJ_TM, d), BF16)],
        compiler_params=_cparams(("arbitrary",)),
        name="mixproj",
    )(x, x, g, w, gain_cols, cos_t, sin_t)


def _banded_tiles(q, k, v, length):
    tiles = []
    for qb in range(length // ATT_TQ):
        i0 = qb * ATT_TQ
        lo = max(0, i0 - ATT_BACK)
        hi = min(length, i0 + ATT_TQ + ATT_BACK)
        tiles.append((i0, q[i0:i0 + ATT_TQ], k[lo:hi], v[lo:hi], lo - (i0 - ATT_BACK)))
    return tiles


def _banded_attention(tiles, tab_ref):
    scores = [lax.dot_general(q, k, (((1,), (1,)), ((), ())), preferred_element_type=F32)
              + tab_ref[:, c0:c0 + k.shape[0]] for _, q, k, _, c0 in tiles]
    maxes = [s.max(axis=-1, keepdims=True) for s in scores]
    probs = [jnp.exp(s - m) for s, m in zip(scores, maxes)]
    dens = [p.sum(axis=-1, keepdims=True) for p in probs]
    outs = [jnp.dot(p.astype(BF16), t[3], preferred_element_type=F32) / d for p, t, d in zip(probs, tiles, dens)]
    return [(o, m + jnp.log(d)) for o, m, d in zip(outs, maxes, dens)]


def _attn_kernel(*refs, seq, n_cast):
    qkv, tab_ref = refs[:9], refs[9]
    cast_in, o_ref, cast_out = refs[10:10 + n_cast], refs[10 + n_cast], refs[11 + n_cast:11 + 2 * n_cast]
    qf, kf, vf = refs[-7:-4]
    outs, lses = refs[-4:-2], refs[-2:]

    for src, dst in zip(cast_in, cast_out):
        dst[...] = src[...].astype(BF16)

    dilated = [(g, dil) for g, (_, dil) in enumerate(DSWA_GROUPS) if dil > 1]
    plain = [g for g, (_, dil) in enumerate(DSWA_GROUPS) if dil == 1]
    assert len(dilated) == len(outs) and len(plain) == 1

    for slot, (g, dil) in enumerate(dilated):
        length = seq // dil
        for src, dst in zip(qkv[3 * g:3 * g + 3], (qf, kf, vf)):
            dst[...] = src[...].astype(F32)
        tiles, rows = [], []
        for r in range(dil):
            q, k, v = (ref[pl.ds(r, length, stride=dil), :].astype(BF16) for ref in (qf, kf, vf))
            sub = _banded_tiles(q, k, v, length)
            tiles += sub
            rows += [pl.ds(t[0] * dil + r, ATT_TQ, stride=dil) for t in sub]
        for rw, (o, lse) in zip(rows, _banded_attention(tiles, tab_ref)):
            outs[slot][rw, :] = o
            lses[slot][rw, :] = jnp.broadcast_to(lse, (ATT_TQ, HEAD_DIM))

    g = plain[0]
    tiles = _banded_tiles(qkv[3 * g][...], qkv[3 * g + 1][...], qkv[3 * g + 2][...], seq)
    for (i0, *_), (o, lse) in zip(tiles, _banded_attention(tiles, tab_ref)):
        rw = slice(i0, i0 + ATT_TQ)
        parts = [(o, jnp.broadcast_to(lse, (ATT_TQ, HEAD_DIM)))] + [(a[rw, :], b[rw, :]) for a, b in zip(outs, lses)]
        m = functools.reduce(jnp.maximum, [l for _, l in parts])
        num = jnp.zeros((ATT_TQ, HEAD_DIM), F32)
        den = jnp.zeros((ATT_TQ, HEAD_DIM), F32)
        for a, l in parts:
            w = jnp.exp(l - m)
            num = num + w * a
            den = den + w
        o_ref[rw, :] = (num / den).astype(BF16)


def _attn_table():
    width = 2 * ATT_BACK + ATT_TQ
    row = jnp.arange(ATT_TQ, dtype=jnp.int32)[:, None]
    col = jnp.arange(width, dtype=jnp.int32)[None, :]
    half = DSWA_GROUPS[0][0] // (2 * DSWA_GROUPS[0][1])
    return jnp.where(jnp.abs(col - ATT_BACK - row) <= half, 0.0, NEG).astype(F32)


def _attn(p5, batch, seq, col_q, col_k, col_v, casts=()):
    halves = {win // (2 * dil) for win, dil in DSWA_GROUPS}
    assert len(halves) == 1 and halves.pop() <= ATT_BACK
    assert all(seq % dil == 0 and (seq // dil) % ATT_TQ == 0 for _, dil in DSWA_GROUPS)
    tab = _attn_table()
    hpg = DSWA_HEADS_PER_GROUP
    in_specs = []
    for g in range(len(DSWA_GROUPS)):
        for base in (col_q, col_k, col_v):
            in_specs.append(pl.BlockSpec((None, seq, HEAD_DIM),
                                         functools.partial(lambda b, h, c: (b, 0, c + h), c=base + g * hpg)))
    in_specs.append(pl.BlockSpec(tab.shape, lambda b, h: (0, 0)))
    cast_specs = _cast_specs(casts, batch * hpg, lambda b, h: b * hpg + h)
    out = pl.pallas_call(
        functools.partial(_attn_kernel, seq=seq, n_cast=len(casts)),
        grid=(batch, hpg),
        in_specs=in_specs + cast_specs,
        out_specs=[pl.BlockSpec((None, seq, HEAD_DIM), lambda b, h: (b, 0, h))] + cast_specs,
        out_shape=[jax.ShapeDtypeStruct((batch, seq, hpg * HEAD_DIM), BF16)]
                  + [jax.ShapeDtypeStruct(a.shape, BF16) for a in casts],
        scratch_shapes=[pltpu.VMEM((seq, HEAD_DIM), F32)] * 7,
        compiler_params=_cparams(("arbitrary", "arbitrary")),
        name="attn",
    )(*([p5] * 9), tab, *casts)
    return out[0], tuple(out[1:])


def _memattn_kernel(q_ref, mem_ref, g_ref, wkv_ref, gk_ref, o_ref, k_ref, v_ref):
    width = MEM_HEADS * HEAD_DIM

    @pl.when(pl.program_id(1) == 0)
    def _():
        mn = _rms(mem_ref[...], g_ref[...]).astype(BF16)
        kv = jnp.dot(mn, wkv_ref[...], preferred_element_type=F32)
        for h in range(MEM_HEADS):
            sl = slice(h * HEAD_DIM, (h + 1) * HEAD_DIM)
            k_ref[:, sl] = _rms(kv[:, sl], gk_ref[...]).astype(BF16)
        v_ref[...] = kv[:, width:].astype(BF16)

    for h in range(MEM_HEADS):
        sl = slice(h * HEAD_DIM, (h + 1) * HEAD_DIM)
        s = lax.dot_general(q_ref[:, sl], k_ref[:, sl], (((1,), (1,)), ((), ())), preferred_element_type=F32)
        p = jnp.exp(s - s.max(axis=-1, keepdims=True))
        den = p.sum(axis=-1, keepdims=True)
        o = jnp.dot(p.astype(BF16), v_ref[:, sl], preferred_element_type=F32)
        o_ref[:, sl] = (o / den).astype(BF16)


def _memattn(p5, mem, g_mem, w_kv, gk, col_block):
    batch, seq, _ = p5.shape
    n_mem, d = mem.shape[1:]
    width = MEM_HEADS * HEAD_DIM
    return pl.pallas_call(
        _memattn_kernel,
        grid=(batch, seq // MEM_TQ),
        in_specs=[
            pl.BlockSpec((None, MEM_TQ, width), lambda b, i: (b, i, col_block)),
            pl.BlockSpec((None, n_mem, d), lambda b, i: (b, 0, 0)),
            pl.BlockSpec((1, d), lambda b, i: (0, 0)),
            pl.BlockSpec((d, 2 * width), lambda b, i: (0, 0)),
            pl.BlockSpec((1, HEAD_DIM), lambda b, i: (0, 0)),
        ],
        out_specs=pl.BlockSpec((None, MEM_TQ, width), lambda b, i: (b, i, 0)),
        out_shape=jax.ShapeDtypeStruct((batch, seq, width), BF16),
        scratch_shapes=[pltpu.VMEM((n_mem, width), BF16), pltpu.VMEM((n_mem, width), BF16)],
        compiler_params=_cparams(("parallel", "arbitrary")),
        name="memattn",
    )(p5, mem, g_mem, w_kv, gk)


def _dft_kernel(f_ref, g_ref, cb, sb, cbt, sbt, cac, sac, *, seq):
    t = pl.program_id(0)
    n2 = 4 * seq
    theta = 2.0 * math.pi / n2

    def trig(m):
        ang = (m & (n2 - 1)).astype(F32) * theta
        return jnp.cos(ang), jnp.sin(ang)

    @pl.when(t == 0)
    def _():
        f_lo = lax.broadcasted_iota(jnp.int32, (DFT_TF, seq), 0)
        s = lax.broadcasted_iota(jnp.int32, (DFT_TF, seq), 1)
        cb[...], sb[...] = trig((2 * f_lo + 1) * s)
        s = lax.broadcasted_iota(jnp.int32, (seq, DFT_TF), 0)
        f_lo = lax.broadcasted_iota(jnp.int32, (seq, DFT_TF), 1)
        cbt[...], sbt[...] = trig((2 * f_lo + 1) * s)
        s = lax.broadcasted_iota(jnp.int32, (seq, LANES), 0)
        tile = lax.broadcasted_iota(jnp.int32, (seq, LANES), 1)
        cac[...], sac[...] = trig(2 * DFT_TF * tile * s)

    s = lax.broadcasted_iota(jnp.int32, (1, seq), 1)
    ca, sa = trig(2 * DFT_TF * t * s)
    f_ref[:DFT_TF, :] = (ca * cb[...] - sa * sb[...]).astype(BF16)
    f_ref[DFT_TF:, :] = (-(sa * cb[...] + ca * sb[...])).astype(BF16)
    pick = lax.broadcasted_iota(jnp.int32, (seq, LANES), 1) == t
    ca = jnp.sum(jnp.where(pick, cac[...], 0.0), axis=-1, keepdims=True)
    sa = jnp.sum(jnp.where(pick, sac[...], 0.0), axis=-1, keepdims=True)
    g_ref[:, :DFT_TF] = (ca * cbt[...] - sa * sbt[...]).astype(BF16)
    g_ref[:, DFT_TF:] = (-(sa * cbt[...] + ca * sbt[...])).astype(BF16)


def _dft_tables(seq):
    nf = seq // DFT_TF
    assert nf <= LANES
    return pl.pallas_call(
        functools.partial(_dft_kernel, seq=seq),
        grid=(nf,),
        out_specs=[pl.BlockSpec((2 * DFT_TF, seq), lambda t: (t, 0)),
                   pl.BlockSpec((seq, 2 * DFT_TF), lambda t: (0, t))],
        out_shape=[jax.ShapeDtypeStruct((2 * seq, seq), BF16), jax.ShapeDtypeStruct((seq, 2 * seq), BF16)],
        scratch_shapes=[pltpu.VMEM((DFT_TF, seq), F32)] * 2 + [pltpu.VMEM((seq, DFT_TF), F32)] * 2
                       + [pltpu.VMEM((seq, LANES), F32)] * 2,
        compiler_params=_cparams(("arbitrary",)),
        name="dft_tables",
    )()


def _filter_mlp_kernel(z_ref, w1_ref, b1_ref, w2_ref, b2_ref, w3_ref, b3_ref, w4_ref, fr_ref, t_ref, d_ref,
                       *rest, n_cast):
    cast_in, o_ref, cast_out = rest[:n_cast], rest[n_cast], rest[n_cast + 1:]
    for src, dst in zip(cast_in, cast_out):
        dst[...] = src[...].astype(BF16)

    hp = lax.Precision.HIGHEST
    fr = fr_ref[...]
    hh = jnp.sin(fr * (jnp.dot(z_ref[...], w1_ref[...], precision=hp, preferred_element_type=F32) + b1_ref[...]))
    hh = jnp.sin(fr * (jnp.dot(hh, w2_ref[...], precision=hp, preferred_element_type=F32) + b2_ref[...]))
    hh = jnp.sin(fr * (jnp.dot(hh, w3_ref[...], precision=hp, preferred_element_type=F32) + b3_ref[...]))
    split = lambda a: (a.astype(BF16), (a - a.astype(BF16).astype(F32)).astype(BF16))
    (a_hi, a_lo), (w_hi, w_lo) = split(hh), split(w4_ref[...])
    h = (jnp.dot(a_hi, w_hi, preferred_element_type=F32) + jnp.dot(a_hi, w_lo, preferred_element_type=F32)
         + jnp.dot(a_lo, w_hi, preferred_element_type=F32))
    decay = jnp.exp(-t_ref[...] * d_ref[...])
    width = decay.shape[1]
    for c in range(h.shape[1] // width):
        o_ref[:, c * width:(c + 1) * width] = h[:, c * width:(c + 1) * width] * decay


def _filter_mlp(z, w1, b1, w2, b2, w3, b3, w4, freq, t_col, absd, casts=()):
    seq = z.shape[0]
    tl = FILT_MLP_TL
    n_out = w4.shape[1]
    full = lambda a: pl.BlockSpec(a.shape, lambda i: (0, 0))
    cast_specs = _cast_specs(casts, seq // tl, lambda i: i)
    out = pl.pallas_call(
        functools.partial(_filter_mlp_kernel, n_cast=len(casts)),
        grid=(seq // tl,),
        in_specs=[pl.BlockSpec((tl, z.shape[1]), lambda i: (i, 0)),
                  full(w1), full(b1), full(w2), full(b2), full(w3), full(b3), full(w4), full(freq),
                  pl.BlockSpec((tl, 1), lambda i: (i, 0)), full(absd)] + cast_specs,
        out_specs=[pl.BlockSpec((tl, n_out), lambda i: (i, 0))] + cast_specs,
        out_shape=[jax.ShapeDtypeStruct((seq, n_out), F32)] + [jax.ShapeDtypeStruct(a.shape, BF16) for a in casts],
        compiler_params=_cparams(("arbitrary",)),
        name="hyena_filter_mlp",
    )(z, w1, b1, w2, b2, w3, b3, w4, freq, t_col, absd, *casts)
    return out[0], tuple(out[1:])


_TOEPLITZ_BLOCKS = {2: ((0, -1, 1),), 4: ((0, -1, 1), (-2, -3, -1), (2, 1, 3))}


def _csub(x, y):
    return x[0] - y[0], x[1] - y[1]


def _cadd(x, y):
    return x[0] + y[0], x[1] + y[1]


def _cmul(k, x):
    return k[0] * x[0] - k[1] * x[1], k[0] * x[1] + k[1] * x[0]


def _toeplitz2(coef, u, v):
    p = _cmul(coef[0], _cadd(u, v))
    return _cadd(p, _cmul(coef[1], v)), _cadd(p, _cmul(coef[2], u))


def _filter_dft_kernel(hf_ref, hb_ref, bias_ref, f_ref, *rest, blk, n_cast):
    cast_in, k_ref, cast_out, r_ref = rest[:n_cast], rest[n_cast], rest[n_cast + 1:2 * n_cast + 1], rest[-1]
    width = hf_ref.shape[1]

    for src, dst in zip(cast_in, cast_out):
        dst[...] = src[...].astype(BF16)

    @pl.when((pl.program_id(1) == 0) & (pl.program_id(2) == 0))
    def _():
        for c, ref in enumerate((hf_ref, hb_ref)):
            for part in range(HY_SPLIT):
                col = (HY_SPLIT * c + part) * width
                r_ref[:, col:col + width] = ref[part * blk:(part + 1) * blk, :].astype(BF16)

    @pl.when(pl.program_id(2) == 0)
    def _():
        r = r_ref[...]
        re = jnp.dot(f_ref[:DFT_TF, :], r, preferred_element_type=F32)
        im = jnp.dot(f_ref[DFT_TF:, :], r, preferred_element_type=F32)

        def transform(c, j):
            col = (HY_SPLIT * c + j) * width
            return re[:, col:col + width], im[:, col:col + width]

        row = lax.broadcasted_iota(jnp.int32, (DFT_TF, width), 0)
        sgn = jnp.where((row & 1) == 0, 1.0, -1.0)
        scale = 1.0 / blk
        (f0r, f0i), (b0r, b0i) = transform(0, 0), transform(1, 0)
        seg = {0: ((f0r + b0r + bias_ref[...]) * scale, (f0i - b0i) * scale)}
        for d in range(1, HY_SPLIT):
            for c, ref, conj in ((0, hf_ref, 1.0), (1, hb_ref, -1.0)):
                (ar, ai), (pr, pi) = transform(c, d), transform(c, d - 1)
                edge = ref[(d - 1) * blk:(d - 1) * blk + 1, :]
                seg[d if c == 0 else -d] = ((ar - sgn * pi) * scale, (ai + sgn * (pr - edge)) * (conj * scale))

        blocks = _TOEPLITZ_BLOCKS[HY_SPLIT]
        first = [seg[d] for d in blocks[0]]
        n = 0
        for g, offsets in enumerate(blocks):
            a, b, c = [seg[d] for d in offsets] if g == 0 else [_csub(seg[d], x) for d, x in zip(offsets, first)]
            for cr, ci in (a, _csub(b, a), _csub(c, a)):
                k_ref[n] = cr
                k_ref[n + 1] = ci
                n += 2


def _filter_dft(hfilt, bias, f_tab, width, casts=()):
    seq = hfilt.shape[0]
    blk = seq // HY_SPLIT
    nf = blk // DFT_TF
    n_spec = 2 * 3 * len(_TOEPLITZ_BLOCKS[HY_SPLIT])
    assert DFT_TF % 2 == 0
    pace = FILT_CAST_STEPS // (HY_ORDER * nf)
    cast_specs = _cast_specs(casts, HY_ORDER * nf * pace, lambda o, f, c: (o * nf + f) * pace + c)
    kern = functools.partial(_filter_dft_kernel, blk=blk, n_cast=len(casts))
    once = pl.Buffered(1)
    out = pl.pallas_call(
        kern,
        grid=(HY_ORDER, nf, pace),
        in_specs=[pl.BlockSpec((seq, width), lambda o, f, c: (0, 2 * o), pipeline_mode=once),
                  pl.BlockSpec((seq, width), lambda o, f, c: (0, 2 * o + 1), pipeline_mode=once),
                  pl.BlockSpec((None, 1, width), lambda o, f, c: (o, 0, 0)),
                  pl.BlockSpec((2 * DFT_TF, blk), lambda o, f, c: (f, 0))] + cast_specs,
        out_specs=[pl.BlockSpec((None, n_spec, DFT_TF, width), lambda o, f, c: (o, 0, f, 0))] + cast_specs,
        out_shape=[jax.ShapeDtypeStruct((HY_ORDER, n_spec, blk, width), F32)]
                  + [jax.ShapeDtypeStruct(a.shape, BF16) for a in casts],
        scratch_shapes=[pltpu.VMEM((blk, 2 * HY_SPLIT * width), BF16)],
        compiler_params=_cparams(("arbitrary", "arbitrary", "arbitrary")),
        name="hyena_filter_dft",
    )(hfilt, hfilt, bias, f_tab, *casts)
    return out[0], tuple(out[1:])


def _short_conv(u_ref, w_ref, b_ref, part, width, emit):
    seq = u_ref.shape[0]
    sl = slice(part * width, (part + 1) * width)
    w0, w1, w2, b = w_ref[0:1, sl], w_ref[1:2, sl], w_ref[2:3, sl], b_ref[:, sl]
    n = SHIFT_ROWS
    r = lax.broadcasted_iota(jnp.int32, (n, n), 0)
    c = lax.broadcasted_iota(jnp.int32, (n, n), 1)
    down = (c == r - 1).astype(BF16)
    up = (c == r + 1).astype(BF16)
    sub = SUBLANES
    row = lax.broadcasted_iota(jnp.int32, (sub, width), 0)
    for k in range(seq // n):
        ub = u_ref[k * n:(k + 1) * n, :]
        prev = jnp.dot(down, ub, preferred_element_type=F32)
        nxt = jnp.dot(up, ub, preferred_element_type=F32)
        if k > 0:
            edge = u_ref[k * n - BF16_ROWS:k * n, :].astype(F32)[BF16_ROWS - 1:BF16_ROWS]
            prev = jnp.concatenate([prev[:sub] + jnp.where(row == 0, edge, 0.0), prev[sub:]], axis=0)
        if k < seq // n - 1:
            edge = u_ref[(k + 1) * n:(k + 1) * n + BF16_ROWS, :].astype(F32)[0:1]
            nxt = jnp.concatenate([nxt[:-sub], nxt[-sub:] + jnp.where(row == sub - 1, edge, 0.0)], axis=0)
        emit(k * n, prev * w0 + ub.astype(F32) * w1 + nxt * w2 + b)


def _hyena_kernel(u0_ref, u1_ref, u2_ref, w_ref, b_ref, f_ref, g_ref, k_ref, o_ref, z_ref, acc_ref,
                  *, nf, width, blk):
    j = pl.program_id(1)
    cols = [slice(i * width, (i + 1) * width) for i in range(HY_SPLIT)]

    def place(row0, rows):
        return slice(row0 % blk, row0 % blk + rows), cols[row0 // blk]

    def first_input(row0, v):
        z_ref[place(row0, v.shape[0])] = v.astype(BF16)

    def next_input(row0, v):
        at = place(row0, v.shape[0])
        z_ref[at] = (v * acc_ref[at]).astype(BF16)

    def result(row0, v):
        o_ref[row0:row0 + v.shape[0], :] = (v * acc_ref[place(row0, v.shape[0])]).astype(BF16)

    @pl.when(j == 0)
    def _():
        _short_conv(u0_ref, w_ref, b_ref, 0, width, first_input)
        acc_ref[...] = jnp.zeros_like(acc_ref)

    uv = jnp.dot(f_ref[...], z_ref[...], preferred_element_type=F32)
    z = [(uv[:DFT_TF, c], uv[DFT_TF:, c]) for c in cols]
    coef = [[(k_ref[2 * (3 * g + t)], k_ref[2 * (3 * g + t) + 1]) for t in range(3)]
            for g in range(len(_TOEPLITZ_BLOCKS[HY_SPLIT]))]
    if HY_SPLIT == 2:
        ys = _toeplitz2(coef[0], z[0], z[1])
    else:
        p1 = _toeplitz2(coef[0], _cadd(z[0], z[2]), _cadd(z[1], z[3]))
        p2 = _toeplitz2(coef[1], z[2], z[3])
        p3 = _toeplitz2(coef[2], z[0], z[1])
        ys = (_cadd(p1[0], p2[0]), _cadd(p1[1], p2[1]), _cadd(p1[0], p3[0]), _cadd(p1[1], p3[1]))
    y = jnp.concatenate([jnp.concatenate([r for r, _ in ys], axis=1),
                         jnp.concatenate([i for _, i in ys], axis=1)], axis=0)
    acc_ref[...] += jnp.dot(g_ref[...], y.astype(BF16), preferred_element_type=F32)

    @pl.when(j == nf - 1)
    def _():
        _short_conv(u1_ref, w_ref, b_ref, 1, width, next_input)
        acc_ref[...] = jnp.zeros_like(acc_ref)

    @pl.when(j == 2 * nf - 1)
    def _():
        _short_conv(u2_ref, w_ref, b_ref, 2, width, result)


def _hyena(p5, conv_w, conv_b, f_tab, g_tab, spectra, col_block, width):
    batch, seq, _ = p5.shape
    blk = seq // HY_SPLIT
    nf = blk // DFT_TF
    n_spec = spectra.shape[1]
    assert HY_ORDER == 2 and n_spec == 6 * len(_TOEPLITZ_BLOCKS[HY_SPLIT]) and blk % SHIFT_ROWS == 0
    kern = functools.partial(_hyena_kernel, nf=nf, width=width, blk=blk)
    u_spec = lambda part: pl.BlockSpec((None, seq, width), lambda b, j: (b, 0, col_block + part))
    return pl.pallas_call(
        kern,
        grid=(batch, HY_ORDER * nf),
        in_specs=[u_spec(0), u_spec(1), u_spec(2),
                  pl.BlockSpec(conv_w.shape, lambda b, j: (0, 0)),
                  pl.BlockSpec(conv_b.shape, lambda b, j: (0, 0)),
                  pl.BlockSpec((2 * DFT_TF, blk), lambda b, j: (j % nf, 0)),
                  pl.BlockSpec((blk, 2 * DFT_TF), lambda b, j: (0, j % nf)),
                  pl.BlockSpec((None, n_spec, DFT_TF, width), lambda b, j: (j // nf, 0, j % nf, 0))],
        out_specs=pl.BlockSpec((None, seq, width), lambda b, j: (b, 0, 0), pipeline_mode=pl.Buffered(1)),
        out_shape=jax.ShapeDtypeStruct((batch, seq, width), BF16),
        scratch_shapes=[pltpu.VMEM((blk, HY_SPLIT * width), BF16), pltpu.VMEM((blk, HY_SPLIT * width), F32)],
        compiler_params=_cparams(("parallel", "arbitrary")),
        name="hyena_conv",
    )(p5, p5, p5, conv_w, conv_b, f_tab, g_tab, spectra)


def _merge_kernel(x_ref, g_ref, ya_ref, yb_ref, yc_ref, wga_ref, wgb_ref, wgc_ref, wa_ref, wb_ref, wc_ref,
                  wo_ref, o_ref, h_ref, m_ref, *, nj):
    s = pl.program_id(0)
    slot = s % 2

    @pl.when(s == 0)
    def _():
        m_ref[1] = jnp.zeros(m_ref.shape[1:], BF16)
        o_ref[...] = jnp.zeros_like(o_ref)

    @pl.when(s % nj == 0)
    def _():
        h_ref[...] = _rms(x_ref[...], g_ref[...]).astype(BF16)

    first = (s - 1) % nj == 0
    base = jnp.where(first, x_ref[...], o_ref[...])
    o_ref[...] = base + jnp.dot(m_ref[1 - slot], wo_ref[...], preferred_element_type=F32)

    h = h_ref[...]

    def branch(wg_ref, y_ref, w_ref):
        gate = jax.nn.sigmoid(jnp.dot(h, wg_ref[...], preferred_element_type=F32))
        return gate * jnp.dot(y_ref[...], w_ref[...], preferred_element_type=F32)

    merged = branch(wga_ref, ya_ref, wa_ref) + branch(wgb_ref, yb_ref, wb_ref) + branch(wgc_ref, yc_ref, wc_ref)
    m_ref[slot] = merged.astype(BF16)


def _merge(x, g, ya, yb, yc, w_gate, gate_col, w_a, w_b, w_c, w_o):
    t, d = x.shape
    nj = d // MERGE_TN
    g0 = gate_col // MERGE_TN
    n = (t // MERGE_TM) * nj
    cur = lambda s: jnp.minimum(s, n - 1)
    prev = lambda s: jnp.maximum(s - 1, 0)
    rows = lambda a: pl.BlockSpec((MERGE_TM, a.shape[1]), lambda s: (cur(s) // nj, 0))
    cols = lambda a, off: pl.BlockSpec((a.shape[0], MERGE_TN), lambda s: (0, cur(s) % nj + off))
    return pl.pallas_call(
        functools.partial(_merge_kernel, nj=nj),
        grid=(n + 1,),
        in_specs=[pl.BlockSpec((MERGE_TM, d), lambda s: (cur(s) // nj, 0)),
                  pl.BlockSpec((1, d), lambda s: (0, 0)), rows(ya), rows(yb), rows(yc),
                  cols(w_gate, g0), cols(w_gate, g0 + nj), cols(w_gate, g0 + 2 * nj),
                  cols(w_a, 0), cols(w_b, 0), cols(w_c, 0),
                  pl.BlockSpec((MERGE_TN, d), lambda s: (prev(s) % nj, 0))],
        out_specs=pl.BlockSpec((MERGE_TM, d), lambda s: (prev(s) // nj, 0)),
        out_shape=jax.ShapeDtypeStruct((t, d), F32),
        scratch_shapes=[pltpu.VMEM((MERGE_TM, d), BF16), pltpu.VMEM((2, MERGE_TM, MERGE_TN), BF16)],
        compiler_params=_cparams(("arbitrary",)),
        name="merge",
    )(x, g, ya, yb, yc, w_gate, w_gate, w_gate, w_a, w_b, w_c, w_o)


def _rope_tables(seq):
    half = ROT_DIM // 2
    inv = jnp.power(ROPE_THETA, -jnp.arange(0, ROT_DIM, 2, dtype=F32) / ROT_DIM)
    ang = jnp.arange(seq, dtype=F32)[:, None] * inv[None, :]
    cos, sin = jnp.cos(ang), jnp.sin(ang)
    ones = jnp.ones((seq, HEAD_DIM - ROT_DIM), F32)
    cos_t = jnp.concatenate([cos, cos, ones], axis=1)
    sin_t = jnp.concatenate([-sin, sin, 0.0 * ones], axis=1)
    assert cos_t.shape == (seq, HEAD_DIM) and half * 2 == ROT_DIM
    return cos_t, sin_t


def _hyena_positional_features(seq, emb):
    bands = (emb - 1) // 2
    t = jnp.linspace(0.0, 1.0, seq, dtype=F32)[:, None]
    w = 2.0 * math.pi * jnp.arange(seq, dtype=F32)[:, None] / seq
    f = jnp.linspace(1e-4, bands - 1, bands, dtype=F32)[None, :]
    return jnp.concatenate([t, jnp.cos(f * w), -jnp.sin(f * w)], axis=-1)


def _pad_to(a, shape):
    return jnp.pad(a, [(0, s - d) for s, d in zip(shape, a.shape)])


def kernel(x, mem, g_ff1, w_ff1_in, w_ff1_out, g_mix, w_in, a_gq, a_gk, hy_conv_w, hy_conv_b, hy_f_w1, hy_f_b1,
           hy_f_w2, hy_f_b2, hy_f_w3, hy_f_b3, hy_f_w4, hy_f_freq, hy_bias, g_mem, w_mem_kv, m_gq, m_gk,
           w_br_a, w_br_b, w_br_c, w_out, g_ff2, w_ff2_in, w_ff2_out, g_post):
    batch, seq, d = x.shape
    depth = g_ff1.shape[0]
    dswa_width = len(DSWA_GROUPS) * DSWA_HEADS_PER_GROUP * HEAD_DIM
    hy_width = hy_bias.shape[-1]
    mem_width = MEM_HEADS * HEAD_DIM
    n_cols = 3 * dswa_width + (HY_ORDER + 1) * hy_width + mem_width
    n_rope = 2 * dswa_width // PROJ_TN
    n_plain = (n_cols - mem_width) // PROJ_TN - n_rope
    assert 2 * dswa_width % PROJ_TN == 0 and mem_width == PROJ_TN and n_cols % PROJ_TN == 0
    assert 3 * dswa_width % hy_width == 0 and n_cols % MERGE_TN == 0 and seq & (seq - 1) == 0
    scale = 1.0 / math.sqrt(HEAD_DIM)

    cos_t, sin_t = _rope_tables(seq)
    f_tab, g_tab = _dft_tables(seq // HY_SPLIT)
    emb, hidden = hy_f_w1.shape[1:]
    z_feat = _pad_to(_hyena_positional_features(seq, emb), (seq, LANES))
    t_col = jnp.linspace(0.0, 1.0, seq, dtype=F32)[:, None]
    deltas = jnp.linspace(math.log(HY_TARGET) / HY_SLOW_DECAY, math.log(HY_TARGET) / HY_FAST_DECAY, hy_width, dtype=F32)
    absd = jnp.abs(deltas)[None, :]
    row = lambda v: v.reshape(1, -1)

    xt = x.reshape(batch * seq, d)
    for l in range(depth):
        pad_h = lambda a, shape: _pad_to(a.astype(F32), shape)
        hfilt, (w_ff1_out_b,) = _filter_mlp(
            z_feat, pad_h(hy_f_w1[l], (LANES, LANES)), pad_h(row(hy_f_b1[l]), (1, LANES)),
            pad_h(hy_f_w2[l], (LANES, LANES)), pad_h(row(hy_f_b2[l]), (1, LANES)),
            pad_h(hy_f_w3[l], (LANES, LANES)), pad_h(row(hy_f_b3[l]), (1, LANES)),
            pad_h(hy_f_w4[l], (LANES, hy_f_w4.shape[-1])), pad_h(row(hy_f_freq[l]), (1, LANES)),
            t_col, absd, casts=(w_ff1_out[l],))
        spectra, (w_ff1_in_b,) = _filter_dft(
            hfilt, hy_bias[l].reshape(HY_ORDER, 1, hy_width), f_tab, hy_width, casts=(w_ff1_in[l],))

        xt, (w_ff2_in_b, w_ff2_out_b, w_in_b) = _ffn(
            xt, row(g_ff1[l]), w_ff1_in_b, w_ff1_out_b, row(g_post[l]), False,
            casts=(w_ff2_in[l], w_ff2_out[l], w_in[l]))

        heads = dswa_width // HEAD_DIM
        gain_cols = jnp.concatenate([jnp.tile(a_gq[l], heads) * scale, jnp.tile(a_gk[l], heads),
                                     jnp.ones((n_cols - 2 * dswa_width - mem_width,), F32),
                                     jnp.tile(m_gq[l], MEM_HEADS) * scale])[None, :]
        p5 = _mixproj(xt, row(g_mix[l]), w_in_b, gain_cols, cos_t, sin_t, n_cols, n_rope, n_plain)
        p5 = p5.reshape(batch, seq, n_cols)

        blk = dswa_width // HEAD_DIM
        y_a, (w_out_b, w_br_a_b, w_br_b_b, w_br_c_b, w_mem_kv_b) = _attn(
            p5, batch, seq, 0, blk, 2 * blk, casts=(w_out[l], w_br_a[l], w_br_b[l], w_br_c[l], w_mem_kv[l]))

        y_b = _hyena(p5, hy_conv_w[l], row(hy_conv_b[l]), f_tab, g_tab, spectra, 3 * dswa_width // hy_width, hy_width)

        y_c = _memattn(p5, mem, row(g_mem[l]), w_mem_kv_b, row(m_gk[l]), (n_cols - mem_width) // mem_width)

        t = batch * seq
        xt = _merge(xt, row(g_mix[l]), y_a.reshape(t, -1), y_b.reshape(t, -1), y_c.reshape(t, -1),
                    w_in_b, n_cols, w_br_a_b, w_br_b_b, w_br_c_b, w_out_b)

        xt, _ = _ffn(xt, row(g_ff2[l]), w_ff2_in_b, w_ff2_out_b, row(g_post[l]), True)
    return xt.reshape(batch, seq, d)
```

```python
import functools
import math

import jax
import jax.numpy as jnp
from jax import lax
from jax.experimental import pallas as pl
from jax.experimental.pallas import tpu as pltpu

F32 = jnp.float32
BF16 = jnp.bfloat16

HEAD_DIM = 128
ROPE_THETA = 500000.0
ROT_DIM = HEAD_DIM // 4
EPS = 1e-6
NEG = -1e30
DSWA_GROUPS = ((128, 1), (512, 4), (2048, 16))
DSWA_HEADS_PER_GROUP = 2
MEM_HEADS = 4
HY_ORDER = 2
HY_SHORT = 3
HY_FAST_DECAY = 0.3
HY_SLOW_DECAY = 1.5
HY_TARGET = 1e-2

LANES = 128
SUBLANES = 8
BF16_ROWS = 16
VMEM_LIMIT_BYTES = 60 * 1024 * 1024

FFN_TM = 1024
FFN_TF = 512
PROJ_TM = 512
PROJ_TN = 512
ATT_TQ = 128
ATT_BACK = 128
MEM_TQ = 1024
DFT_TF = 128
HY_SPLIT = 4
HY_CHUNK = 256
FILT_CAST_STEPS = 16
SHIFT_ROWS = 256
FILT_MLP_TL = 256
MERGE_TM = 1024
MERGE_TN = 256


def _cparams(sem):
    return pltpu.CompilerParams(dimension_semantics=sem, vmem_limit_bytes=VMEM_LIMIT_BYTES)


def _rms(x, g):
    return x * lax.rsqrt(jnp.mean(x * x, axis=-1, keepdims=True) + EPS) * g


def _ffn_kernel(x_ref, g_ref, wa_ref, wb_ref, wd_ref, gp_ref, *rest, nj, final_norm, n_cast):
    cast_in, o_ref, cast_out, h_ref = rest[:n_cast], rest[n_cast], rest[n_cast + 1:2 * n_cast + 1], rest[-1]
    j = pl.program_id(1)

    for src, dst in zip(cast_in, cast_out):
        dst[...] = src[...].astype(BF16)

    @pl.when(j == 0)
    def _():
        x = x_ref[...]
        h_ref[...] = _rms(x, g_ref[...]).astype(BF16)
        o_ref[...] = x

    h = h_ref[...]
    a = jnp.dot(h, wa_ref[...], preferred_element_type=F32)
    b = jnp.dot(h, wb_ref[...], preferred_element_type=F32)
    act = (0.5 * a * jax.nn.sigmoid(a) * b).astype(BF16)
    o_ref[...] += jnp.dot(act, wd_ref[...], preferred_element_type=F32)

    if final_norm:
        @pl.when(j == nj - 1)
        def _():
            o_ref[...] = _rms(o_ref[...], gp_ref[...])


def _cast_specs(casts, steps, flat):
    specs = []
    for a in casts:
        rows = BF16_ROWS
        while a.shape[0] % rows or a.shape[0] // rows > steps:
            rows += BF16_ROWS
        n = a.shape[0] // rows
        specs.append(pl.BlockSpec((rows, a.shape[1]), functools.partial(
            lambda *idx, n: (flat(*idx) * n // steps, 0), n=n)))
    return specs


def _ffn(x, g, w_in, w_out, g_post, final_norm, casts=()):
    t, d = x.shape
    d_ff = w_out.shape[0]
    nj = d_ff // FFN_TF
    cast_specs = _cast_specs(casts, (t // FFN_TM) * nj, lambda i, j: i * nj + j)
    kern = functools.partial(_ffn_kernel, nj=nj, final_norm=final_norm, n_cast=len(casts))
    out = pl.pallas_call(
        kern,
        grid=(t // FFN_TM, nj),
        in_specs=[
            pl.BlockSpec((FFN_TM, d), lambda i, j: (i, 0)),
            pl.BlockSpec((1, d), lambda i, j: (0, 0)),
            pl.BlockSpec((d, FFN_TF), lambda i, j: (0, j)),
            pl.BlockSpec((d, FFN_TF), lambda i, j: (0, j + nj)),
            pl.BlockSpec((FFN_TF, d), lambda i, j: (j, 0)),
            pl.BlockSpec((1, d), lambda i, j: (0, 0)),
        ] + cast_specs,
        out_specs=[pl.BlockSpec((FFN_TM, d), lambda i, j: (i, 0))] + cast_specs,
        out_shape=[jax.ShapeDtypeStruct((t, d), F32)] + [jax.ShapeDtypeStruct(a.shape, BF16) for a in casts],
        scratch_shapes=[pltpu.VMEM((FFN_TM, d), BF16)],
        compiler_params=_cparams(("arbitrary", "arbitrary")),
        name="ffn_final" if final_norm else "ffn",
    )(x, g, w_in, w_in, w_out, g_post, *casts)
    return out[0], tuple(out[1:])


def _mixproj_kernel(x0_ref, xn_ref, g_ref, w_ref, gain_ref, cos_ref, sin_ref, o_ref, h_ref, *, n_rope, n_plain):
    i = pl.program_id(0)
    slot = i % 2

    @pl.when(i == 0)
    def _():
        h_ref[0] = _rms(x0_ref[...], g_ref[...]).astype(BF16)

    h_ref[1 - slot] = _rms(xn_ref[...], g_ref[...]).astype(BF16)
    h = h_ref[slot]
    lane = lax.broadcasted_iota(jnp.int32, (PROJ_TM, HEAD_DIM), 1)
    first = lane < (ROT_DIM // 2)
    n_tiles = w_ref.shape[1] // PROJ_TN
    order = [j for j in range(n_tiles) if not n_rope <= j < n_rope + n_plain] + list(range(n_rope, n_rope + n_plain))
    for j in order:
        acc = jnp.dot(h, w_ref[:, j * PROJ_TN:(j + 1) * PROJ_TN], preferred_element_type=F32)
        plain = n_rope <= j < n_rope + n_plain
        for s in range(PROJ_TN // HEAD_DIM):
            sl = slice(j * PROJ_TN + s * HEAD_DIM, j * PROJ_TN + (s + 1) * HEAD_DIM)
            t = acc[:, s * HEAD_DIM:(s + 1) * HEAD_DIM]
            if not plain:
                t = _rms(t, gain_ref[:, sl])
            if j < n_rope:
                partner = jnp.where(first, pltpu.roll(t, HEAD_DIM - ROT_DIM // 2, axis=1),
                                    pltpu.roll(t, ROT_DIM // 2, axis=1))
                t = t * cos_ref[...] + partner * sin_ref[...]
            o_ref[:, sl] = t.astype(BF16)


def _mixproj(x, g, w, gain_cols, cos_t, sin_t, n_cols, n_rope, n_plain):
    t, d = x.shape
    n_row = t // PROJ_TM
    per_seq = cos_t.shape[0] // PROJ_TM
    kern = functools.partial(_mixproj_kernel, n_rope=n_rope, n_plain=n_plain)
    once = pl.Buffered(1)
    return pl.pallas_call(
        kern,
        grid=(n_row,),
        in_specs=[
            pl.BlockSpec((PROJ_TM, d), lambda i: (0, 0), pipeline_mode=once),
            pl.BlockSpec((PROJ_TM, d), lambda i: (jnp.minimum(i + 1, n_row - 1), 0)),
            pl.BlockSpec((1, d), lambda i: (0, 0)),
            pl.BlockSpec((d, n_cols), lambda i: (0, 0), pipeline_mode=once),
            pl.BlockSpec((1, n_cols), lambda i: (0, 0)),
            pl.BlockSpec((PROJ_TM, HEAD_DIM), lambda i: (i % per_seq, 0)),
            pl.BlockSpec((PROJ_TM, HEAD_DIM), lambda i: (i % per_seq, 0)),
        ],
        out_specs=pl.BlockSpec((PROJ_TM, n_cols), lambda i: (i, 0)),
        out_shape=jax.ShapeDtypeStruct((t, n_cols), BF16),
        scratch_shapes=[pltpu.VMEM((2, PROJ_TM, d), BF16)],
        compiler_params=_cparams(("arbitrary",)),
        name="mixproj",
    )(x, x, g, w, gain_cols, cos_t, sin_t)


def _banded_tiles(q, k, v, length):
    tiles = []
    for qb in range(length // ATT_TQ):
        i0 = qb * ATT_TQ
        lo = max(0, i0 - ATT_BACK)
        hi = min(length, i0 + ATT_TQ + ATT_BACK)
        tiles.append((i0, q[i0:i0 + ATT_TQ], k[lo:hi], v[lo:hi], lo - (i0 - ATT_BACK)))
    return tiles


def _banded_attention(tiles, tab_ref):
    scores = [lax.dot_general(q, k, (((1,), (1,)), ((), ())), preferred_element_type=F32)
              + tab_ref[:, c0:c0 + k.shape[0]] for _, q, k, _, c0 in tiles]
    maxes = [s.max(axis=-1, keepdims=True) for s in scores]
    probs = [jnp.exp(s - m) for s, m in zip(scores, maxes)]
    dens = [p.sum(axis=-1, keepdims=True) for p in probs]
    outs = [jnp.dot(p.astype(BF16), t[3], preferred_element_type=F32) / d for p, t, d in zip(probs, tiles, dens)]
    return [(o, m + jnp.log(d)) for o, m, d in zip(outs, maxes, dens)]


def _attn_kernel(*refs, seq, n_cast):
    qkv, tab_ref = refs[:9], refs[9]
    cast_in, o_ref, cast_out = refs[10:10 + n_cast], refs[10 + n_cast], refs[11 + n_cast:11 + 2 * n_cast]
    qf, kf, vf = refs[-7:-4]
    outs, lses = refs[-4:-2], refs[-2:]

    for src, dst in zip(cast_in, cast_out):
        dst[...] = src[...].astype(BF16)

    dilated = [(g, dil) for g, (_, dil) in enumerate(DSWA_GROUPS) if dil > 1]
    plain = [g for g, (_, dil) in enumerate(DSWA_GROUPS) if dil == 1]
    assert len(dilated) == len(outs) and len(plain) == 1

    for slot, (g, dil) in enumerate(dilated):
        length = seq // dil
        for src, dst in zip(qkv[3 * g:3 * g + 3], (qf, kf, vf)):
            dst[...] = src[...].astype(F32)
        tiles, rows = [], []
        for r in range(dil):
            q, k, v = (ref[pl.ds(r, length, stride=dil), :].astype(BF16) for ref in (qf, kf, vf))
            sub = _banded_tiles(q, k, v, length)
            tiles += sub
            rows += [pl.ds(t[0] * dil + r, ATT_TQ, stride=dil) for t in sub]
        for rw, (o, lse) in zip(rows, _banded_attention(tiles, tab_ref)):
            outs[slot][rw, :] = o
            lses[slot][rw, :] = jnp.broadcast_to(lse, (ATT_TQ, HEAD_DIM))

    g = plain[0]
    tiles = _banded_tiles(qkv[3 * g][...], qkv[3 * g + 1][...], qkv[3 * g + 2][...], seq)
    for (i0, *_), (o, lse) in zip(tiles, _banded_attention(tiles, tab_ref)):
        rw = slice(i0, i0 + ATT_TQ)
        parts = [(o, jnp.broadcast_to(lse, (ATT_TQ, HEAD_DIM)))] + [(a[rw, :], b[rw, :]) for a, b in zip(outs, lses)]
        m = functools.reduce(jnp.maximum, [l for _, l in parts])
        num = jnp.zeros((ATT_TQ, HEAD_DIM), F32)
        den = jnp.zeros((ATT_TQ, HEAD_DIM), F32)
        for a, l in parts:
            w = jnp.exp(l - m)
            num = num + w * a
            den = den + w
        o_ref[rw, :] = (num / den).astype(BF16)


def _attn_table():
    width = 2 * ATT_BACK + ATT_TQ
    row = jnp.arange(ATT_TQ, dtype=jnp.int32)[:, None]
    col = jnp.arange(width, dtype=jnp.int32)[None, :]
    half = DSWA_GROUPS[0][0] // (2 * DSWA_GROUPS[0][1])
    return jnp.where(jnp.abs(col - ATT_BACK - row) <= half, 0.0, NEG).astype(F32)


def _attn(p5, batch, seq, col_q, col_k, col_v, casts=()):
    halves = {win // (2 * dil) for win, dil in DSWA_GROUPS}
    assert len(halves) == 1 and halves.pop() <= ATT_BACK
    assert all(seq % dil == 0 and (seq // dil) % ATT_TQ == 0 for _, dil in DSWA_GROUPS)
    tab = _attn_table()
    hpg = DSWA_HEADS_PER_GROUP
    in_specs = []
    for g in range(len(DSWA_GROUPS)):
        for base in (col_q, col_k, col_v):
            in_specs.append(pl.BlockSpec((None, seq, HEAD_DIM),
                                         functools.partial(lambda b, h, c: (b, 0, c + h), c=base + g * hpg)))
    in_specs.append(pl.BlockSpec(tab.shape, lambda b, h: (0, 0)))
    cast_specs = _cast_specs(casts, batch * hpg, lambda b, h: b * hpg + h)
    out = pl.pallas_call(
        functools.partial(_attn_kernel, seq=seq, n_cast=len(casts)),
        grid=(batch, hpg),
        in_specs=in_specs + cast_specs,
        out_specs=[pl.BlockSpec((None, seq, HEAD_DIM), lambda b, h: (b, 0, h))] + cast_specs,
        out_shape=[jax.ShapeDtypeStruct((batch, seq, hpg * HEAD_DIM), BF16)]
                  + [jax.ShapeDtypeStruct(a.shape, BF16) for a in casts],
        scratch_shapes=[pltpu.VMEM((seq, HEAD_DIM), F32)] * 7,
        compiler_params=_cparams(("arbitrary", "arbitrary")),
        name="attn",
    )(*([p5] * 9), tab, *casts)
    return out[0], tuple(out[1:])


def _memattn_kernel(q_ref, mem_ref, g_ref, wkv_ref, gk_ref, o_ref, k_ref, v_ref):
    width = MEM_HEADS * HEAD_DIM

    @pl.when(pl.program_id(1) == 0)
    def _():
        mn = _rms(mem_ref[...], g_ref[...]).astype(BF16)
        kv = jnp.dot(mn, wkv_ref[...], preferred_element_type=F32)
        for h in range(MEM_HEADS):
            sl = slice(h * HEAD_DIM, (h + 1) * HEAD_DIM)
            k_ref[:, sl] = _rms(kv[:, sl], gk_ref[...]).astype(BF16)
        v_ref[...] = kv[:, width:].astype(BF16)

    for h in range(MEM_HEADS):
        sl = slice(h * HEAD_DIM, (h + 1) * HEAD_DIM)
        s = lax.dot_general(q_ref[:, sl], k_ref[:, sl], (((1,), (1,)), ((), ())), preferred_element_type=F32)
        p = jnp.exp(s - s.max(axis=-1, keepdims=True))
        den = p.sum(axis=-1, keepdims=True)
        o = jnp.dot(p.astype(BF16), v_ref[:, sl], preferred_element_type=F32)
        o_ref[:, sl] = (o / den).astype(BF16)


def _memattn(p5, mem, g_mem, w_kv, gk, col_block):
    batch, seq, _ = p5.shape
    n_mem, d = mem.shape[1:]
    width = MEM_HEADS * HEAD_DIM
    return pl.pallas_call(
        _memattn_kernel,
        grid=(batch, seq // MEM_TQ),
        in_specs=[
            pl.BlockSpec((None, MEM_TQ, width), lambda b, i: (b, i, col_block)),
            pl.BlockSpec((None, n_mem, d), lambda b, i: (b, 0, 0)),
            pl.BlockSpec((1, d), lambda b, i: (0, 0)),
            pl.BlockSpec((d, 2 * width), lambda b, i: (0, 0)),
            pl.BlockSpec((1, HEAD_DIM), lambda b, i: (0, 0)),
        ],
        out_specs=pl.BlockSpec((None, MEM_TQ, width), lambda b, i: (b, i, 0)),
        out_shape=jax.ShapeDtypeStruct((batch, seq, width), BF16),
        scratch_shapes=[pltpu.VMEM((n_mem, width), BF16), pltpu.VMEM((n_mem, width), BF16)],
        compiler_params=_cparams(("parallel", "arbitrary")),
        name="memattn",
    )(p5, mem, g_mem, w_kv, gk)


def _dft_kernel(f_ref, g_ref, cb, sb, cbt, sbt, cac, sac, *, seq):
    t = pl.program_id(0)
    n2 = 4 * seq
    theta = 2.0 * math.pi / n2

    def trig(m):
        ang = (m & (n2 - 1)).astype(F32) * theta
        return jnp.cos(ang), jnp.sin(ang)

    @pl.when(t == 0)
    def _():
        f_lo = lax.broadcasted_iota(jnp.int32, (DFT_TF, seq), 0)
        s = lax.broadcasted_iota(jnp.int32, (DFT_TF, seq), 1)
        cb[...], sb[...] = trig((2 * f_lo + 1) * s)
        s = lax.broadcasted_iota(jnp.int32, (seq, DFT_TF), 0)
        f_lo = lax.broadcasted_iota(jnp.int32, (seq, DFT_TF), 1)
        cbt[...], sbt[...] = trig((2 * f_lo + 1) * s)
        s = lax.broadcasted_iota(jnp.int32, (seq, LANES), 0)
        tile = lax.broadcasted_iota(jnp.int32, (seq, LANES), 1)
        cac[...], sac[...] = trig(2 * DFT_TF * tile * s)

    s = lax.broadcasted_iota(jnp.int32, (1, seq), 1)
    ca, sa = trig(2 * DFT_TF * t * s)
    f_ref[:DFT_TF, :] = (ca * cb[...] - sa * sb[...]).astype(BF16)
    f_ref[DFT_TF:, :] = (-(sa * cb[...] + ca * sb[...])).astype(BF16)
    pick = lax.broadcasted_iota(jnp.int32, (seq, LANES), 1) == t
    ca = jnp.sum(jnp.where(pick, cac[...], 0.0), axis=-1, keepdims=True)
    sa = jnp.sum(jnp.where(pick, sac[...], 0.0), axis=-1, keepdims=True)
    g_ref[:, :DFT_TF] = (ca * cbt[...] - sa * sbt[...]).astype(BF16)
    g_ref[:, DFT_TF:] = (-(sa * cbt[...] + ca * sbt[...])).astype(BF16)


def _dft_tables(seq):
    nf = seq // DFT_TF
    assert nf <= LANES
    return pl.pallas_call(
        functools.partial(_dft_kernel, seq=seq),
        grid=(nf,),
        out_specs=[pl.BlockSpec((2 * DFT_TF, seq), lambda t: (t, 0)),
                   pl.BlockSpec((seq, 2 * DFT_TF), lambda t: (0, t))],
        out_shape=[jax.ShapeDtypeStruct((2 * seq, seq), BF16), jax.ShapeDtypeStruct((seq, 2 * seq), BF16)],
        scratch_shapes=[pltpu.VMEM((DFT_TF, seq), F32)] * 2 + [pltpu.VMEM((seq, DFT_TF), F32)] * 2
                       + [pltpu.VMEM((seq, LANES), F32)] * 2,
        compiler_params=_cparams(("arbitrary",)),
        name="dft_tables",
    )()


def _filter_mlp_kernel(z_ref, w1_ref, b1_ref, w2_ref, b2_ref, w3_ref, b3_ref, w4_ref, fr_ref, t_ref, d_ref,
                       *rest, n_cast):
    cast_in, o_ref, cast_out = rest[:n_cast], rest[n_cast], rest[n_cast + 1:]
    for src, dst in zip(cast_in, cast_out):
        dst[...] = src[...].astype(BF16)

    hp = lax.Precision.HIGHEST
    fr = fr_ref[...]
    hh = jnp.sin(fr * (jnp.dot(z_ref[...], w1_ref[...], precision=hp, preferred_element_type=F32) + b1_ref[...]))
    hh = jnp.sin(fr * (jnp.dot(hh, w2_ref[...], precision=hp, preferred_element_type=F32) + b2_ref[...]))
    hh = jnp.sin(fr * (jnp.dot(hh, w3_ref[...], precision=hp, preferred_element_type=F32) + b3_ref[...]))
    split = lambda a: (a.astype(BF16), (a - a.astype(BF16).astype(F32)).astype(BF16))
    (a_hi, a_lo), (w_hi, w_lo) = split(hh), split(w4_ref[...])
    h = (jnp.dot(a_hi, w_hi, preferred_element_type=F32) + jnp.dot(a_hi, w_lo, preferred_element_type=F32)
         + jnp.dot(a_lo, w_hi, preferred_element_type=F32))
    decay = jnp.exp(-t_ref[...] * d_ref[...])
    width = decay.shape[1]
    for c in range(h.shape[1] // width):
        o_ref[:, c * width:(c + 1) * width] = h[:, c * width:(c + 1) * width] * decay


def _filter_mlp(z, w1, b1, w2, b2, w3, b3, w4, freq, t_col, absd, casts=()):
    seq = z.shape[0]
    tl = FILT_MLP_TL
    n_out = w4.shape[1]
    full = lambda a: pl.BlockSpec(a.shape, lambda i: (0, 0))
    cast_specs = _cast_specs(casts, seq // tl, lambda i: i)
    out = pl.pallas_call(
        functools.partial(_filter_mlp_kernel, n_cast=len(casts)),
        grid=(seq // tl,),
        in_specs=[pl.BlockSpec((tl, z.shape[1]), lambda i: (i, 0)),
                  full(w1), full(b1), full(w2), full(b2), full(w3), full(b3), full(w4), full(freq),
                  pl.BlockSpec((tl, 1), lambda i: (i, 0)), full(absd)] + cast_specs,
        out_specs=[pl.BlockSpec((tl, n_out), lambda i: (i, 0))] + cast_specs,
        out_shape=[jax.ShapeDtypeStruct((seq, n_out), F32)] + [jax.ShapeDtypeStruct(a.shape, BF16) for a in casts],
        compiler_params=_cparams(("arbitrary",)),
        name="hyena_filter_mlp",
    )(z, w1, b1, w2, b2, w3, b3, w4, freq, t_col, absd, *casts)
    return out[0], tuple(out[1:])


_TOEPLITZ_BLOCKS = {2: ((0, -1, 1),), 4: ((0, -1, 1), (-2, -3, -1), (2, 1, 3))}


def _csub(x, y):
    return x[0] - y[0], x[1] - y[1]


def _cadd(x, y):
    return x[0] + y[0], x[1] + y[1]


def _cmul(k, x):
    return k[0] * x[0] - k[1] * x[1], k[0] * x[1] + k[1] * x[0]


def _toeplitz2(coef, u, v):
    p = _cmul(coef[0], _cadd(u, v))
    return _cadd(p, _cmul(coef[1], v)), _cadd(p, _cmul(coef[2], u))


def _filter_dft_kernel(hf_ref, hb_ref, bias_ref, f_ref, *rest, blk, n_cast):
    cast_in, k_ref, cast_out, r_ref = rest[:n_cast], rest[n_cast], rest[n_cast + 1:2 * n_cast + 1], rest[-1]
    width = hf_ref.shape[1]

    for src, dst in zip(cast_in, cast_out):
        dst[...] = src[...].astype(BF16)

    @pl.when((pl.program_id(1) == 0) & (pl.program_id(2) == 0))
    def _():
        for c, ref in enumerate((hf_ref, hb_ref)):
            for part in range(HY_SPLIT):
                col = (HY_SPLIT * c + part) * width
                r_ref[:, col:col + width] = ref[part * blk:(part + 1) * blk, :].astype(BF16)

    @pl.when(pl.program_id(2) == 0)
    def _():
        r = r_ref[...]
        re = jnp.dot(f_ref[:DFT_TF, :], r, preferred_element_type=F32)
        im = jnp.dot(f_ref[DFT_TF:, :], r, preferred_element_type=F32)

        def transform(c, j):
            col = (HY_SPLIT * c + j) * width
            return re[:, col:col + width], im[:, col:col + width]

        row = lax.broadcasted_iota(jnp.int32, (DFT_TF, width), 0)
        sgn = jnp.where((row & 1) == 0, 1.0, -1.0)
        scale = 1.0 / blk
        (f0r, f0i), (b0r, b0i) = transform(0, 0), transform(1, 0)
        seg = {0: ((f0r + b0r + bias_ref[...]) * scale, (f0i - b0i) * scale)}
        for d in range(1, HY_SPLIT):
            for c, ref, conj in ((0, hf_ref, 1.0), (1, hb_ref, -1.0)):
                (ar, ai), (pr, pi) = transform(c, d), transform(c, d - 1)
                edge = ref[(d - 1) * blk:(d - 1) * blk + 1, :]
                seg[d if c == 0 else -d] = ((ar - sgn * pi) * scale, (ai + sgn * (pr - edge)) * (conj * scale))

        blocks = _TOEPLITZ_BLOCKS[HY_SPLIT]
        first = [seg[d] for d in blocks[0]]
        n = 0
        for g, offsets in enumerate(blocks):
            a, b, c = [seg[d] for d in offsets] if g == 0 else [_csub(seg[d], x) for d, x in zip(offsets, first)]
            for cr, ci in (a, _csub(b, a), _csub(c, a)):
                k_ref[n] = cr
                k_ref[n + 1] = ci
                n += 2


def _filter_dft(hfilt, bias, f_tab, width, casts=()):
    seq = hfilt.shape[0]
    blk = seq // HY_SPLIT
    nf = blk // DFT_TF
    n_spec = 2 * 3 * len(_TOEPLITZ_BLOCKS[HY_SPLIT])
    assert DFT_TF % 2 == 0
    pace = FILT_CAST_STEPS // (HY_ORDER * nf)
    cast_specs = _cast_specs(casts, HY_ORDER * nf * pace, lambda o, f, c: (o * nf + f) * pace + c)
    kern = functools.partial(_filter_dft_kernel, blk=blk, n_cast=len(casts))
    once = pl.Buffered(1)
    out = pl.pallas_call(
        kern,
        grid=(HY_ORDER, nf, pace),
        in_specs=[pl.BlockSpec((seq, width), lambda o, f, c: (0, 2 * o), pipeline_mode=once),
                  pl.BlockSpec((seq, width), lambda o, f, c: (0, 2 * o + 1), pipeline_mode=once),
                  pl.BlockSpec((None, 1, width), lambda o, f, c: (o, 0, 0)),
                  pl.BlockSpec((2 * DFT_TF, blk), lambda o, f, c: (f, 0))] + cast_specs,
        out_specs=[pl.BlockSpec((None, n_spec, DFT_TF, width), lambda o, f, c: (o, 0, f, 0))] + cast_specs,
        out_shape=[jax.ShapeDtypeStruct((HY_ORDER, n_spec, blk, width), F32)]
                  + [jax.ShapeDtypeStruct(a.shape, BF16) for a in casts],
        scratch_shapes=[pltpu.VMEM((blk, 2 * HY_SPLIT * width), BF16)],
        compiler_params=_cparams(("arbitrary", "arbitrary", "arbitrary")),
        name="hyena_filter_dft",
    )(hfilt, hfilt, bias, f_tab, *casts)
    return out[0], tuple(out[1:])


def _short_conv(u_ref, w_ref, b_ref, part, width, emit):
    seq = u_ref.shape[0]
    sl = slice(part * width, (part + 1) * width)
    w0, w1, w2, b = w_ref[0:1, sl], w_ref[1:2, sl], w_ref[2:3, sl], b_ref[:, sl]
    n = SHIFT_ROWS
    r = lax.broadcasted_iota(jnp.int32, (n, n), 0)
    c = lax.broadcasted_iota(jnp.int32, (n, n), 1)
    down = (c == r - 1).astype(BF16)
    up = (c == r + 1).astype(BF16)
    sub = SUBLANES
    row = lax.broadcasted_iota(jnp.int32, (sub, width), 0)
    for k in range(seq // n):
        ub = u_ref[k * n:(k + 1) * n, :]
        prev = jnp.dot(down, ub, preferred_element_type=F32)
        nxt = jnp.dot(up, ub, preferred_element_type=F32)
        if k > 0:
            edge = u_ref[k * n - BF16_ROWS:k * n, :].astype(F32)[BF16_ROWS - 1:BF16_ROWS]
            prev = jnp.concatenate([prev[:sub] + jnp.where(row == 0, edge, 0.0), prev[sub:]], axis=0)
        if k < seq // n - 1:
            edge = u_ref[(k + 1) * n:(k + 1) * n + BF16_ROWS, :].astype(F32)[0:1]
            nxt = jnp.concatenate([nxt[:-sub], nxt[-sub:] + jnp.where(row == sub - 1, edge, 0.0)], axis=0)
        emit(k * n, prev * w0 + ub.astype(F32) * w1 + nxt * w2 + b)


def _hyena_kernel(u0_ref, u1_ref, u2_ref, w_ref, b_ref, f_ref, g_ref, k_ref, o_ref, z_ref, acc_ref,
                  *, nf, width, blk):
    j = pl.program_id(1)
    cw = HY_CHUNK
    gw = HY_SPLIT * cw
    groups = width // cw

    def cols(g, i):
        return slice(g * gw + i * cw, g * gw + (i + 1) * cw)

    def place(row0, rows):
        return slice(row0 % blk, row0 % blk + rows), row0 // blk

    def first_input(row0, v):
        rows, i = place(row0, v.shape[0])
        for g in range(groups):
            z_ref[rows, cols(g, i)] = v[:, g * cw:(g + 1) * cw].astype(BF16)

    def next_input(row0, v):
        rows, i = place(row0, v.shape[0])
        for g in range(groups):
            z_ref[rows, cols(g, i)] = (v[:, g * cw:(g + 1) * cw] * acc_ref[rows, cols(g, i)]).astype(BF16)

    def result(row0, v):
        rows, i = place(row0, v.shape[0])
        for g in range(groups):
            o_ref[row0:row0 + v.shape[0], g * cw:(g + 1) * cw] = (
                v[:, g * cw:(g + 1) * cw] * acc_ref[rows, cols(g, i)]).astype(BF16)

    @pl.when(j == 0)
    def _():
        _short_conv(u0_ref, w_ref, b_ref, 0, width, first_input)
        acc_ref[...] = jnp.zeros_like(acc_ref)

    spans = [slice(g * gw, (g + 1) * gw) for g in range(groups)]
    uvs = [jnp.dot(f_ref[...], z_ref[:, span], preferred_element_type=F32) for span in spans]
    ys_all = []
    for g, uv in enumerate(uvs):
        ch = slice(g * cw, (g + 1) * cw)
        z = [(uv[:DFT_TF, i * cw:(i + 1) * cw], uv[DFT_TF:, i * cw:(i + 1) * cw]) for i in range(HY_SPLIT)]
        coef = [[(k_ref[2 * (3 * b + t), :, ch], k_ref[2 * (3 * b + t) + 1, :, ch]) for t in range(3)]
                for b in range(len(_TOEPLITZ_BLOCKS[HY_SPLIT]))]
        if HY_SPLIT == 2:
            ys = _toeplitz2(coef[0], z[0], z[1])
        else:
            p1 = _toeplitz2(coef[0], _cadd(z[0], z[2]), _cadd(z[1], z[3]))
            p2 = _toeplitz2(coef[1], z[2], z[3])
            p3 = _toeplitz2(coef[2], z[0], z[1])
            ys = (_cadd(p1[0], p2[0]), _cadd(p1[1], p2[1]), _cadd(p1[0], p3[0]), _cadd(p1[1], p3[1]))
        ys_all.append(jnp.concatenate([jnp.concatenate([r for r, _ in ys], axis=1),
                                       jnp.concatenate([i for _, i in ys], axis=1)], axis=0).astype(BF16))
    for span, y in zip(spans, ys_all):
        acc_ref[:, span] += jnp.dot(g_ref[...], y, preferred_element_type=F32)

    @pl.when(j == nf - 1)
    def _():
        _short_conv(u1_ref, w_ref, b_ref, 1, width, next_input)
        acc_ref[...] = jnp.zeros_like(acc_ref)

    @pl.when(j == 2 * nf - 1)
    def _():
        _short_conv(u2_ref, w_ref, b_ref, 2, width, result)


def _hyena(p5, conv_w, conv_b, f_tab, g_tab, spectra, col_block, width):
    batch, seq, _ = p5.shape
    blk = seq // HY_SPLIT
    nf = blk // DFT_TF
    n_spec = spectra.shape[1]
    assert HY_ORDER == 2 and n_spec == 6 * len(_TOEPLITZ_BLOCKS[HY_SPLIT]) and blk % SHIFT_ROWS == 0
    kern = functools.partial(_hyena_kernel, nf=nf, width=width, blk=blk)
    u_spec = lambda part: pl.BlockSpec((None, seq, width), lambda b, j: (b, 0, col_block + part))
    return pl.pallas_call(
        kern,
        grid=(batch, HY_ORDER * nf),
        in_specs=[u_spec(0), u_spec(1), u_spec(2),
                  pl.BlockSpec(conv_w.shape, lambda b, j: (0, 0)),
                  pl.BlockSpec(conv_b.shape, lambda b, j: (0, 0)),
                  pl.BlockSpec((2 * DFT_TF, blk), lambda b, j: (j % nf, 0)),
                  pl.BlockSpec((blk, 2 * DFT_TF), lambda b, j: (0, j % nf)),
                  pl.BlockSpec((None, n_spec, DFT_TF, width), lambda b, j: (j // nf, 0, j % nf, 0))],
        out_specs=pl.BlockSpec((None, seq, width), lambda b, j: (b, 0, 0), pipeline_mode=pl.Buffered(1)),
        out_shape=jax.ShapeDtypeStruct((batch, seq, width), BF16),
        scratch_shapes=[pltpu.VMEM((blk, HY_SPLIT * width), BF16), pltpu.VMEM((blk, HY_SPLIT * width), F32)],
        compiler_params=_cparams(("parallel", "arbitrary")),
        name="hyena_conv",
    )(p5, p5, p5, conv_w, conv_b, f_tab, g_tab, spectra)


def _merge_kernel(x_ref, g_ref, ya_ref, yb_ref, yc_ref, wga_ref, wgb_ref, wgc_ref, wa_ref, wb_ref, wc_ref,
                  wo_ref, o_ref, h_ref, m_ref, *, nj):
    s = pl.program_id(0)
    slot = s % 2

    @pl.when(s == 0)
    def _():
        m_ref[1] = jnp.zeros(m_ref.shape[1:], BF16)
        o_ref[...] = jnp.zeros_like(o_ref)

    @pl.when(s % nj == 0)
    def _():
        h_ref[...] = _rms(x_ref[...], g_ref[...]).astype(BF16)

    first = (s - 1) % nj == 0
    base = jnp.where(first, x_ref[...], o_ref[...])
    o_ref[...] = base + jnp.dot(m_ref[1 - slot], wo_ref[...], preferred_element_type=F32)

    h = h_ref[...]

    def branch(wg_ref, y_ref, w_ref):
        gate = jax.nn.sigmoid(jnp.dot(h, wg_ref[...], preferred_element_type=F32))
        return gate * jnp.dot(y_ref[...], w_ref[...], preferred_element_type=F32)

    merged = branch(wga_ref, ya_ref, wa_ref) + branch(wgb_ref, yb_ref, wb_ref) + branch(wgc_ref, yc_ref, wc_ref)
    m_ref[slot] = merged.astype(BF16)


def _merge(x, g, ya, yb, yc, w_gate, gate_col, w_a, w_b, w_c, w_o):
    t, d = x.shape
    nj = d // MERGE_TN
    g0 = gate_col // MERGE_TN
    n = (t // MERGE_TM) * nj
    cur = lambda s: jnp.minimum(s, n - 1)
    prev = lambda s: jnp.maximum(s - 1, 0)
    rows = lambda a: pl.BlockSpec((MERGE_TM, a.shape[1]), lambda s: (cur(s) // nj, 0))
    cols = lambda a, off: pl.BlockSpec((a.shape[0], MERGE_TN), lambda s: (0, cur(s) % nj + off))
    return pl.pallas_call(
        functools.partial(_merge_kernel, nj=nj),
        grid=(n + 1,),
        in_specs=[pl.BlockSpec((MERGE_TM, d), lambda s: (cur(s) // nj, 0)),
                  pl.BlockSpec((1, d), lambda s: (0, 0)), rows(ya), rows(yb), rows(yc),
                  cols(w_gate, g0), cols(w_gate, g0 + nj), cols(w_gate, g0 + 2 * nj),
                  cols(w_a, 0), cols(w_b, 0), cols(w_c, 0),
                  pl.BlockSpec((MERGE_TN, d), lambda s: (prev(s) % nj, 0))],
        out_specs=pl.BlockSpec((MERGE_TM, d), lambda s: (prev(s) // nj, 0)),
        out_shape=jax.ShapeDtypeStruct((t, d), F32),
        scratch_shapes=[pltpu.VMEM((MERGE_TM, d), BF16), pltpu.VMEM((2, MERGE_TM, MERGE_TN), BF16)],
        compiler_params=_cparams(("arbitrary",)),
        name="merge",
    )(x, g, ya, yb, yc, w_gate, w_gate, w_gate, w_a, w_b, w_c, w_o)


def _rope_tables(seq):
    half = ROT_DIM // 2
    inv = jnp.power(ROPE_THETA, -jnp.arange(0, ROT_DIM, 2, dtype=F32) / ROT_DIM)
    ang = jnp.arange(seq, dtype=F32)[:, None] * inv[None, :]
    cos, sin = jnp.cos(ang), jnp.sin(ang)
    ones = jnp.ones((seq, HEAD_DIM - ROT_DIM), F32)
    cos_t = jnp.concatenate([cos, cos, ones], axis=1)
    sin_t = jnp.concatenate([-sin, sin, 0.0 * ones], axis=1)
    assert cos_t.shape == (seq, HEAD_DIM) and half * 2 == ROT_DIM
    return cos_t, sin_t


def _hyena_positional_features(seq, emb):
    bands = (emb - 1) // 2
    t = jnp.linspace(0.0, 1.0, seq, dtype=F32)[:, None]
    w = 2.0 * math.pi * jnp.arange(seq, dtype=F32)[:, None] / seq
    f = jnp.linspace(1e-4, bands - 1, bands, dtype=F32)[None, :]
    return jnp.concatenate([t, jnp.cos(f * w), -jnp.sin(f * w)], axis=-1)


def _pad_to(a, shape):
    return jnp.pad(a, [(0, s - d) for s, d in zip(shape, a.shape)])


def kernel(x, mem, g_ff1, w_ff1_in, w_ff1_out, g_mix, w_in, a_gq, a_gk, hy_conv_w, hy_conv_b, hy_f_w1, hy_f_b1,
           hy_f_w2, hy_f_b2, hy_f_w3, hy_f_b3, hy_f_w4, hy_f_freq, hy_bias, g_mem, w_mem_kv, m_gq, m_gk,
           w_br_a, w_br_b, w_br_c, w_out, g_ff2, w_ff2_in, w_ff2_out, g_post):
    batch, seq, d = x.shape
    depth = g_ff1.shape[0]
    dswa_width = len(DSWA_GROUPS) * DSWA_HEADS_PER_GROUP * HEAD_DIM
    hy_width = hy_bias.shape[-1]
    mem_width = MEM_HEADS * HEAD_DIM
    n_cols = 3 * dswa_width + (HY_ORDER + 1) * hy_width + mem_width
    n_rope = 2 * dswa_width // PROJ_TN
    n_plain = (n_cols - mem_width) // PROJ_TN - n_rope
    assert 2 * dswa_width % PROJ_TN == 0 and mem_width == PROJ_TN and n_cols % PROJ_TN == 0
    assert 3 * dswa_width % hy_width == 0 and n_cols % MERGE_TN == 0 and seq & (seq - 1) == 0
    scale = 1.0 / math.sqrt(HEAD_DIM)

    cos_t, sin_t = _rope_tables(seq)
    f_tab, g_tab = _dft_tables(seq // HY_SPLIT)
    emb, hidden = hy_f_w1.shape[1:]
    z_feat = _pad_to(_hyena_positional_features(seq, emb), (seq, LANES))
    t_col = jnp.linspace(0.0, 1.0, seq, dtype=F32)[:, None]
    deltas = jnp.linspace(math.log(HY_TARGET) / HY_SLOW_DECAY, math.log(HY_TARGET) / HY_FAST_DECAY, hy_width, dtype=F32)
    absd = jnp.abs(deltas)[None, :]
    row = lambda v: v.reshape(1, -1)

    xt = x.reshape(batch * seq, d)
    for l in range(depth):
        pad_h = lambda a, shape: _pad_to(a.astype(F32), shape)
        hfilt, (w_ff1_out_b,) = _filter_mlp(
            z_feat, pad_h(hy_f_w1[l], (LANES, LANES)), pad_h(row(hy_f_b1[l]), (1, LANES)),
            pad_h(hy_f_w2[l], (LANES, LANES)), pad_h(row(hy_f_b2[l]), (1, LANES)),
            pad_h(hy_f_w3[l], (LANES, LANES)), pad_h(row(hy_f_b3[l]), (1, LANES)),
            pad_h(hy_f_w4[l], (LANES, hy_f_w4.shape[-1])), pad_h(row(hy_f_freq[l]), (1, LANES)),
            t_col, absd, casts=(w_ff1_out[l],))
        spectra, (w_ff1_in_b,) = _filter_dft(
            hfilt, hy_bias[l].reshape(HY_ORDER, 1, hy_width), f_tab, hy_width, casts=(w_ff1_in[l],))

        xt, (w_ff2_in_b, w_ff2_out_b, w_in_b) = _ffn(
            xt, row(g_ff1[l]), w_ff1_in_b, w_ff1_out_b, row(g_post[l]), False,
            casts=(w_ff2_in[l], w_ff2_out[l], w_in[l]))

        heads = dswa_width // HEAD_DIM
        gain_cols = jnp.concatenate([jnp.tile(a_gq[l], heads) * scale, jnp.tile(a_gk[l], heads),
                                     jnp.ones((n_cols - 2 * dswa_width - mem_width,), F32),
                                     jnp.tile(m_gq[l], MEM_HEADS) * scale])[None, :]
        p5 = _mixproj(xt, row(g_mix[l]), w_in_b, gain_cols, cos_t, sin_t, n_cols, n_rope, n_plain)
        p5 = p5.reshape(batch, seq, n_cols)

        blk = dswa_width // HEAD_DIM
        y_a, (w_out_b, w_br_a_b, w_br_b_b, w_br_c_b, w_mem_kv_b) = _attn(
            p5, batch, seq, 0, blk, 2 * blk, casts=(w_out[l], w_br_a[l], w_br_b[l], w_br_c[l], w_mem_kv[l]))

        y_b = _hyena(p5, hy_conv_w[l], row(hy_conv_b[l]), f_tab, g_tab, spectra, 3 * dswa_width // hy_width, hy_width)

        y_c = _memattn(p5, mem, row(g_mem[l]), w_mem_kv_b, row(m_gk[l]), (n_cols - mem_width) // mem_width)

        t = batch * seq
        xt = _merge(xt, row(g_mix[l]), y_a.reshape(t, -1), y_b.reshape(t, -1), y_c.reshape(t, -1),
                    w_in_b, n_cols, w_br_a_b, w_br_b_b, w_br_c_b, w_out_b)

        xt, _ = _ffn(xt, row(g_ff2[l]), w_ff2_in_b, w_ff2_out_b, row(g_post[l]), True)
    return xt.reshape(batch, seq, d)
```

```python
import functools
import math

import jax
import jax.numpy as jnp
from jax import lax
from jax.experimental import pallas as pl
from jax.experimental.pallas import tpu as pltpu

F32 = jnp.float32
BF16 = jnp.bfloat16

HEAD_DIM = 128
ROPE_THETA = 500000.0
ROT_DIM = HEAD_DIM // 4
EPS = 1e-6
NEG = -1e30
DSWA_GROUPS = ((128, 1), (512, 4), (2048, 16))
DSWA_HEADS_PER_GROUP = 2
MEM_HEADS = 4
HY_ORDER = 2
HY_SHORT = 3
HY_FAST_DECAY = 0.3
HY_SLOW_DECAY = 1.5
HY_TARGET = 1e-2

LANES = 128
SUBLANES = 8
BF16_ROWS = 16
VMEM_LIMIT_BYTES = 60 * 1024 * 1024

FFN_TM = 1024
FFN_TF = 512
PROJ_TM = 512
PROJ_TN = 512
ATT_TQ = 128
ATT_BACK = 128
MEM_TQ = 1024
DFT_TF = 128
HY_SPLIT = 4
HY_CHUNK = 256
FILT_CAST_STEPS = 16
SHIFT_ROWS = 256
FILT_MLP_TL = 256
MERGE_TM = 1024
MERGE_TN = 256


def _cparams(sem):
    return pltpu.CompilerParams(dimension_semantics=sem, vmem_limit_bytes=VMEM_LIMIT_BYTES)


def _rms(x, g):
    return x * lax.rsqrt(jnp.mean(x * x, axis=-1, keepdims=True) + EPS) * g


def _ffn_kernel(x_ref, g_ref, wa_ref, wb_ref, wd_ref, gp_ref, *rest, nj, final_norm, n_cast):
    cast_in, o_ref, cast_out, h_ref = rest[:n_cast], rest[n_cast], rest[n_cast + 1:2 * n_cast + 1], rest[-1]
    j = pl.program_id(1)

    for src, dst in zip(cast_in, cast_out):
        dst[...] = src[...].astype(BF16)

    @pl.when(j == 0)
    def _():
        x = x_ref[...]
        h_ref[...] = _rms(x, g_ref[...]).astype(BF16)
        o_ref[...] = x

    h = h_ref[...]
    a = jnp.dot(h, wa_ref[...], preferred_element_type=F32)
    b = jnp.dot(h, wb_ref[...], preferred_element_type=F32)
    act = (0.5 * a * jax.nn.sigmoid(a) * b).astype(BF16)
    o_ref[...] += jnp.dot(act, wd_ref[...], preferred_element_type=F32)

    if final_norm:
        @pl.when(j == nj - 1)
        def _():
            o_ref[...] = _rms(o_ref[...], gp_ref[...])


def _cast_specs(casts, steps, flat):
    specs = []
    for a in casts:
        rows = BF16_ROWS
        while a.shape[0] % rows or a.shape[0] // rows > steps:
            rows += BF16_ROWS
        n = a.shape[0] // rows
        specs.append(pl.BlockSpec((rows, a.shape[1]), functools.partial(
            lambda *idx, n: (flat(*idx) * n // steps, 0), n=n)))
    return specs


def _ffn(x, g, w_in, w_out, g_post, final_norm, casts=()):
    t, d = x.shape
    d_ff = w_out.shape[0]
    nj = d_ff // FFN_TF
    cast_specs = _cast_specs(casts, (t // FFN_TM) * nj, lambda i, j: i * nj + j)
    kern = functools.partial(_ffn_kernel, nj=nj, final_norm=final_norm, n_cast=len(casts))
    out = pl.pallas_call(
        kern,
        grid=(t // FFN_TM, nj),
        in_specs=[
            pl.BlockSpec((FFN_TM, d), lambda i, j: (i, 0)),
            pl.BlockSpec((1, d), lambda i, j: (0, 0)),
            pl.BlockSpec((d, FFN_TF), lambda i, j: (0, j)),
            pl.BlockSpec((d, FFN_TF), lambda i, j: (0, j + nj)),
            pl.BlockSpec((FFN_TF, d), lambda i, j: (j, 0)),
            pl.BlockSpec((1, d), lambda i, j: (0, 0)),
        ] + cast_specs,
        out_specs=[pl.BlockSpec((FFN_TM, d), lambda i, j: (i, 0))] + cast_specs,
        out_shape=[jax.ShapeDtypeStruct((t, d), F32)] + [jax.ShapeDtypeStruct(a.shape, BF16) for a in casts],
        scratch_shapes=[pltpu.VMEM((FFN_TM, d), BF16)],
        compiler_params=_cparams(("arbitrary", "arbitrary")),
        name="ffn_final" if final_norm else "ffn",
    )(x, g, w_in, w_in, w_out, g_post, *casts)
    return out[0], tuple(out[1:])


def _mixproj_kernel(x0_ref, xn_ref, g_ref, w_ref, gain_ref, cos_ref, sin_ref, o_ref, h_ref, *, n_rope, n_plain):
    i = pl.program_id(0)
    slot = i % 2

    @pl.when(i == 0)
    def _():
        h_ref[0] = _rms(x0_ref[...], g_ref[...]).astype(BF16)

    h_ref[1 - slot] = _rms(xn_ref[...], g_ref[...]).astype(BF16)
    h = h_ref[slot]
    lane = lax.broadcasted_iota(jnp.int32, (PROJ_TM, HEAD_DIM), 1)
    first = lane < (ROT_DIM // 2)
    n_tiles = w_ref.shape[1] // PROJ_TN
    order = [j for j in range(n_tiles) if not n_rope <= j < n_rope + n_plain] + list(range(n_rope, n_rope + n_plain))
    for j in order:
        acc = jnp.dot(h, w_ref[:, j * PROJ_TN:(j + 1) * PROJ_TN], preferred_element_type=F32)
        plain = n_rope <= j < n_rope + n_plain
        for s in range(PROJ_TN // HEAD_DIM):
            sl = slice(j * PROJ_TN + s * HEAD_DIM, j * PROJ_TN + (s + 1) * HEAD_DIM)
            t = acc[:, s * HEAD_DIM:(s + 1) * HEAD_DIM]
            if not plain:
                t = _rms(t, gain_ref[:, sl])
            if j < n_rope:
                partner = jnp.where(first, pltpu.roll(t, HEAD_DIM - ROT_DIM // 2, axis=1),
                                    pltpu.roll(t, ROT_DIM // 2, axis=1))
                t = t * cos_ref[...] + partner * sin_ref[...]
            o_ref[:, sl] = t.astype(BF16)


def _mixproj(x, g, w, gain_cols, cos_t, sin_t, n_cols, n_rope, n_plain):
    t, d = x.shape
    n_row = t // PROJ_TM
    per_seq = cos_t.shape[0] // PROJ_TM
    kern = functools.partial(_mixproj_kernel, n_rope=n_rope, n_plain=n_plain)
    once = pl.Buffered(1)
    return pl.pallas_call(
        kern,
        grid=(n_row,),
        in_specs=[
            pl.BlockSpec((PROJ_TM, d), lambda i: (0, 0), pipeline_mode=once),
            pl.BlockSpec((PROJ_TM, d), lambda i: (jnp.minimum(i + 1, n_row - 1), 0)),
            pl.BlockSpec((1, d), lambda i: (0, 0)),
            pl.BlockSpec((d, n_cols), lambda i: (0, 0), pipeline_mode=once),
            pl.BlockSpec((1, n_cols), lambda i: (0, 0)),
            pl.BlockSpec((PROJ_TM, HEAD_DIM), lambda i: (i % per_seq, 0)),
            pl.BlockSpec((PROJ_TM, HEAD_DIM), lambda i: (i % per_seq, 0)),
        ],
        out_specs=pl.BlockSpec((PROJ_TM, n_cols), lambda i: (i, 0)),
        out_shape=jax.ShapeDtypeStruct((t, n_cols), BF16),
        scratch_shapes=[pltpu.VMEM((2, PROJ_TM, d), BF16)],
        compiler_params=_cparams(("arbitrary",)),
        name="mixproj",
    )(x, x, g, w, gain_cols, cos_t, sin_t)


def _banded_tiles(q, k, v, length):
    tiles = []
    for qb in range(length // ATT_TQ):
        i0 = qb * ATT_TQ
        lo = max(0, i0 - ATT_BACK)
        hi = min(length, i0 + ATT_TQ + ATT_BACK)
        tiles.append((i0, q[i0:i0 + ATT_TQ], k[lo:hi], v[lo:hi], lo - (i0 - ATT_BACK)))
    return tiles


def _banded_attention(tiles, tab_ref):
    scores = [lax.dot_general(q, k, (((1,), (1,)), ((), ())), preferred_element_type=F32)
              + tab_ref[:, c0:c0 + k.shape[0]] for _, q, k, _, c0 in tiles]
    maxes = [s.max(axis=-1, keepdims=True) for s in scores]
    probs = [jnp.exp(s - m) for s, m in zip(scores, maxes)]
    dens = [p.sum(axis=-1, keepdims=True) for p in probs]
    outs = [jnp.dot(p.astype(BF16), t[3], preferred_element_type=F32) / d for p, t, d in zip(probs, tiles, dens)]
    return [(o, m + jnp.log(d)) for o, m, d in zip(outs, maxes, dens)]


def _attn_kernel(*refs, seq, n_cast):
    qkv, tab_ref = refs[:9], refs[9]
    cast_in, o_ref, cast_out = refs[10:10 + n_cast], refs[10 + n_cast], refs[11 + n_cast:11 + 2 * n_cast]
    qf, kf, vf = refs[-7:-4]
    outs, lses = refs[-4:-2], refs[-2:]

    for src, dst in zip(cast_in, cast_out):
        dst[...] = src[...].astype(BF16)

    dilated = [(g, dil) for g, (_, dil) in enumerate(DSWA_GROUPS) if dil > 1]
    plain = [g for g, (_, dil) in enumerate(DSWA_GROUPS) if dil == 1]
    assert len(dilated) == len(outs) and len(plain) == 1

    for slot, (g, dil) in enumerate(dilated):
        length = seq // dil
        for src, dst in zip(qkv[3 * g:3 * g + 3], (qf, kf, vf)):
            dst[...] = src[...].astype(F32)
        tiles, rows = [], []
        for r in range(dil):
            q, k, v = (ref[pl.ds(r, length, stride=dil), :].astype(BF16) for ref in (qf, kf, vf))
            sub = _banded_tiles(q, k, v, length)
            tiles += sub
            rows += [pl.ds(t[0] * dil + r, ATT_TQ, stride=dil) for t in sub]
        for rw, (o, lse) in zip(rows, _banded_attention(tiles, tab_ref)):
            outs[slot][rw, :] = o
            lses[slot][rw, :] = jnp.broadcast_to(lse, (ATT_TQ, HEAD_DIM))

    g = plain[0]
    tiles = _banded_tiles(qkv[3 * g][...], qkv[3 * g + 1][...], qkv[3 * g + 2][...], seq)
    for (i0, *_), (o, lse) in zip(tiles, _banded_attention(tiles, tab_ref)):
        rw = slice(i0, i0 + ATT_TQ)
        parts = [(o, jnp.broadcast_to(lse, (ATT_TQ, HEAD_DIM)))] + [(a[rw, :], b[rw, :]) for a, b in zip(outs, lses)]
        m = functools.reduce(jnp.maximum, [l for _, l in parts])
        num = jnp.zeros((ATT_TQ, HEAD_DIM), F32)
        den = jnp.zeros((ATT_TQ, HEAD_DIM), F32)
        for a, l in parts:
            w = jnp.exp(l - m)
            num = num + w * a
            den = den + w
        o_ref[rw, :] = (num / den).astype(BF16)


def _attn_table():
    width = 2 * ATT_BACK + ATT_TQ
    row = jnp.arange(ATT_TQ, dtype=jnp.int32)[:, None]
    col = jnp.arange(width, dtype=jnp.int32)[None, :]
    half = DSWA_GROUPS[0][0] // (2 * DSWA_GROUPS[0][1])
    return jnp.where(jnp.abs(col - ATT_BACK - row) <= half, 0.0, NEG).astype(F32)


def _attn(p5, batch, seq, col_q, col_k, col_v, casts=()):
    halves = {win // (2 * dil) for win, dil in DSWA_GROUPS}
    assert len(halves) == 1 and halves.pop() <= ATT_BACK
    assert all(seq % dil == 0 and (seq // dil) % ATT_TQ == 0 for _, dil in DSWA_GROUPS)
    tab = _attn_table()
    hpg = DSWA_HEADS_PER_GROUP
    in_specs = []
    for g in range(len(DSWA_GROUPS)):
        for base in (col_q, col_k, col_v):
            in_specs.append(pl.BlockSpec((None, seq, HEAD_DIM),
                                         functools.partial(lambda b, h, c: (b, 0, c + h), c=base + g * hpg)))
    in_specs.append(pl.BlockSpec(tab.shape, lambda b, h: (0, 0)))
    cast_specs = _cast_specs(casts, batch * hpg, lambda b, h: b * hpg + h)
    out = pl.pallas_call(
        functools.partial(_attn_kernel, seq=seq, n_cast=len(casts)),
        grid=(batch, hpg),
        in_specs=in_specs + cast_specs,
        out_specs=[pl.BlockSpec((None, seq, HEAD_DIM), lambda b, h: (b, 0, h))] + cast_specs,
        out_shape=[jax.ShapeDtypeStruct((batch, seq, hpg * HEAD_DIM), BF16)]
                  + [jax.ShapeDtypeStruct(a.shape, BF16) for a in casts],
        scratch_shapes=[pltpu.VMEM((seq, HEAD_DIM), F32)] * 7,
        compiler_params=_cparams(("arbitrary", "arbitrary")),
        name="attn",
    )(*([p5] * 9), tab, *casts)
    return out[0], tuple(out[1:])


def _memattn_kernel(q_ref, mem_ref, g_ref, wkv_ref, gk_ref, o_ref, k_ref, v_ref):
    width = MEM_HEADS * HEAD_DIM

    @pl.when(pl.program_id(1) == 0)
    def _():
        mn = _rms(mem_ref[...], g_ref[...]).astype(BF16)
        kv = jnp.dot(mn, wkv_ref[...], preferred_element_type=F32)
        for h in range(MEM_HEADS):
            sl = slice(h * HEAD_DIM, (h + 1) * HEAD_DIM)
            k_ref[:, sl] = _rms(kv[:, sl], gk_ref[...]).astype(BF16)
        v_ref[...] = kv[:, width:].astype(BF16)

    for h in range(MEM_HEADS):
        sl = slice(h * HEAD_DIM, (h + 1) * HEAD_DIM)
        s = lax.dot_general(q_ref[:, sl], k_ref[:, sl], (((1,), (1,)), ((), ())), preferred_element_type=F32)
        p = jnp.exp(s - s.max(axis=-1, keepdims=True))
        den = p.sum(axis=-1, keepdims=True)
        o = jnp.dot(p.astype(BF16), v_ref[:, sl], preferred_element_type=F32)
        o_ref[:, sl] = (o / den).astype(BF16)


def _memattn(p5, mem, g_mem, w_kv, gk, col_block):
    batch, seq, _ = p5.shape
    n_mem, d = mem.shape[1:]
    width = MEM_HEADS * HEAD_DIM
    return pl.pallas_call(
        _memattn_kernel,
        grid=(batch, seq // MEM_TQ),
        in_specs=[
            pl.BlockSpec((None, MEM_TQ, width), lambda b, i: (b, i, col_block)),
            pl.BlockSpec((None, n_mem, d), lambda b, i: (b, 0, 0)),
            pl.BlockSpec((1, d), lambda b, i: (0, 0)),
            pl.BlockSpec((d, 2 * width), lambda b, i: (0, 0)),
            pl.BlockSpec((1, HEAD_DIM), lambda b, i: (0, 0)),
        ],
        out_specs=pl.BlockSpec((None, MEM_TQ, width), lambda b, i: (b, i, 0)),
        out_shape=jax.ShapeDtypeStruct((batch, seq, width), BF16),
        scratch_shapes=[pltpu.VMEM((n_mem, width), BF16), pltpu.VMEM((n_mem, width), BF16)],
        compiler_params=_cparams(("parallel", "arbitrary")),
        name="memattn",
    )(p5, mem, g_mem, w_kv, gk)


def _dft_kernel(f_ref, g_ref, cb, sb, cbt, sbt, cac, sac, *, seq):
    t = pl.program_id(0)
    n2 = 4 * seq
    theta = 2.0 * math.pi / n2

    def trig(m):
        ang = (m & (n2 - 1)).astype(F32) * theta
        return jnp.cos(ang), jnp.sin(ang)

    @pl.when(t == 0)
    def _():
        f_lo = lax.broadcasted_iota(jnp.int32, (DFT_TF, seq), 0)
        s = lax.broadcasted_iota(jnp.int32, (DFT_TF, seq), 1)
        cb[...], sb[...] = trig((2 * f_lo + 1) * s)
        s = lax.broadcasted_iota(jnp.int32, (seq, DFT_TF), 0)
        f_lo = lax.broadcasted_iota(jnp.int32, (seq, DFT_TF), 1)
        cbt[...], sbt[...] = trig((2 * f_lo + 1) * s)
        s = lax.broadcasted_iota(jnp.int32, (seq, LANES), 0)
        tile = lax.broadcasted_iota(jnp.int32, (seq, LANES), 1)
        cac[...], sac[...] = trig(2 * DFT_TF * tile * s)

    s = lax.broadcasted_iota(jnp.int32, (1, seq), 1)
    ca, sa = trig(2 * DFT_TF * t * s)
    f_ref[:DFT_TF, :] = (ca * cb[...] - sa * sb[...]).astype(BF16)
    f_ref[DFT_TF:, :] = (-(sa * cb[...] + ca * sb[...])).astype(BF16)
    pick = lax.broadcasted_iota(jnp.int32, (seq, LANES), 1) == t
    ca = jnp.sum(jnp.where(pick, cac[...], 0.0), axis=-1, keepdims=True)
    sa = jnp.sum(jnp.where(pick, sac[...], 0.0), axis=-1, keepdims=True)
    g_ref[:, :DFT_TF] = (ca * cbt[...] - sa * sbt[...]).astype(BF16)
    g_ref[:, DFT_TF:] = (-(sa * cbt[...] + ca * sbt[...])).astype(BF16)


def _dft_tables(seq):
    nf = seq // DFT_TF
    assert nf <= LANES
    return pl.pallas_call(
        functools.partial(_dft_kernel, seq=seq),
        grid=(nf,),
        out_specs=[pl.BlockSpec((2 * DFT_TF, seq), lambda t: (t, 0)),
                   pl.BlockSpec((seq, 2 * DFT_TF), lambda t: (0, t))],
        out_shape=[jax.ShapeDtypeStruct((2 * seq, seq), BF16), jax.ShapeDtypeStruct((seq, 2 * seq), BF16)],
        scratch_shapes=[pltpu.VMEM((DFT_TF, seq), F32)] * 2 + [pltpu.VMEM((seq, DFT_TF), F32)] * 2
                       + [pltpu.VMEM((seq, LANES), F32)] * 2,
        compiler_params=_cparams(("arbitrary",)),
        name="dft_tables",
    )()


def _filter_mlp_kernel(z_ref, w_ref, v_ref, w4_ref, t_ref, d_ref, *rest, n_cast):
    cast_in, o_ref, cast_out = rest[:n_cast], rest[n_cast], rest[n_cast + 1:-2]
    w_hi, w_lo = rest[-2:]
    for src, dst in zip(cast_in, cast_out):
        dst[...] = src[...].astype(BF16)

    split = lambda a: (a.astype(BF16), (a - a.astype(BF16).astype(F32)).astype(BF16))

    @pl.when(pl.program_id(0) == 0)
    def _():
        hidden = w4_ref.shape[0]
        for ref, part in zip((w_hi, w_lo), split(w4_ref[...])):
            ref[:hidden, :] = part
            ref[hidden:, :] = jnp.zeros((ref.shape[0] - hidden, ref.shape[1]), BF16)

    hp = lax.Precision.HIGHEST
    fr = v_ref[3:4, :]
    hh = z_ref[...]
    for n in range(3):
        hh = jnp.sin(fr * (jnp.dot(hh, w_ref[n], precision=hp, preferred_element_type=F32) + v_ref[n:n + 1, :]))
    a_hi, a_lo = split(hh)
    h = (jnp.dot(a_hi, w_hi[...], preferred_element_type=F32) + jnp.dot(a_hi, w_lo[...], preferred_element_type=F32)
         + jnp.dot(a_lo, w_hi[...], preferred_element_type=F32))
    decay = jnp.exp(-t_ref[...] * d_ref[...])
    width = decay.shape[1]
    for c in range(h.shape[1] // width):
        o_ref[:, c * width:(c + 1) * width] = h[:, c * width:(c + 1) * width] * decay


def _filter_mlp(z, w123, vecs, w4, t_col, absd, casts=()):
    seq = z.shape[0]
    tl = FILT_MLP_TL
    n_out = w4.shape[1]
    assert w4.shape[0] % BF16_ROWS == 0 and w4.shape[0] <= LANES
    full = lambda a: pl.BlockSpec(a.shape, lambda i: (0,) * a.ndim)
    cast_specs = _cast_specs(casts, seq // tl, lambda i: i)
    out = pl.pallas_call(
        functools.partial(_filter_mlp_kernel, n_cast=len(casts)),
        grid=(seq // tl,),
        in_specs=[pl.BlockSpec((tl, z.shape[1]), lambda i: (i, 0)), full(w123), full(vecs), full(w4),
                  pl.BlockSpec((tl, 1), lambda i: (i, 0)), full(absd)] + cast_specs,
        out_specs=[pl.BlockSpec((tl, n_out), lambda i: (i, 0))] + cast_specs,
        out_shape=[jax.ShapeDtypeStruct((seq, n_out), F32)] + [jax.ShapeDtypeStruct(a.shape, BF16) for a in casts],
        scratch_shapes=[pltpu.VMEM((LANES, n_out), BF16)] * 2,
        compiler_params=_cparams(("arbitrary",)),
        name="hyena_filter_mlp",
    )(z, w123, vecs, w4, t_col, absd, *casts)
    return out[0], tuple(out[1:])


_TOEPLITZ_BLOCKS = {2: ((0, -1, 1),), 4: ((0, -1, 1), (-2, -3, -1), (2, 1, 3))}


def _csub(x, y):
    return x[0] - y[0], x[1] - y[1]


def _cadd(x, y):
    return x[0] + y[0], x[1] + y[1]


def _cmul(k, x):
    return k[0] * x[0] - k[1] * x[1], k[0] * x[1] + k[1] * x[0]


def _toeplitz2(coef, u, v):
    p = _cmul(coef[0], _cadd(u, v))
    return _cadd(p, _cmul(coef[1], v)), _cadd(p, _cmul(coef[2], u))


def _filter_dft_kernel(hf_ref, hb_ref, bias_ref, f_ref, *rest, blk, n_cast):
    cast_in, k_ref, cast_out, r_ref = rest[:n_cast], rest[n_cast], rest[n_cast + 1:2 * n_cast + 1], rest[-1]
    width = hf_ref.shape[1]

    for src, dst in zip(cast_in, cast_out):
        dst[...] = src[...].astype(BF16)

    @pl.when((pl.program_id(1) == 0) & (pl.program_id(2) == 0))
    def _():
        for c, ref in enumerate((hf_ref, hb_ref)):
            for part in range(HY_SPLIT):
                col = (HY_SPLIT * c + part) * width
                r_ref[:, col:col + width] = ref[part * blk:(part + 1) * blk, :].astype(BF16)

    @pl.when(pl.program_id(2) == 0)
    def _():
        r = r_ref[...]
        re = jnp.dot(f_ref[:DFT_TF, :], r, preferred_element_type=F32)
        im = jnp.dot(f_ref[DFT_TF:, :], r, preferred_element_type=F32)

        def transform(c, j):
            col = (HY_SPLIT * c + j) * width
            return re[:, col:col + width], im[:, col:col + width]

        row = lax.broadcasted_iota(jnp.int32, (DFT_TF, width), 0)
        sgn = jnp.where((row & 1) == 0, 1.0, -1.0)
        scale = 1.0 / blk
        (f0r, f0i), (b0r, b0i) = transform(0, 0), transform(1, 0)
        seg = {0: ((f0r + b0r + bias_ref[...]) * scale, (f0i - b0i) * scale)}
        for d in range(1, HY_SPLIT):
            for c, ref, conj in ((0, hf_ref, 1.0), (1, hb_ref, -1.0)):
                (ar, ai), (pr, pi) = transform(c, d), transform(c, d - 1)
                edge = ref[(d - 1) * blk:(d - 1) * blk + 1, :]
                seg[d if c == 0 else -d] = ((ar - sgn * pi) * scale, (ai + sgn * (pr - edge)) * (conj * scale))

        blocks = _TOEPLITZ_BLOCKS[HY_SPLIT]
        first = [seg[d] for d in blocks[0]]
        n = 0
        for g, offsets in enumerate(blocks):
            a, b, c = [seg[d] for d in offsets] if g == 0 else [_csub(seg[d], x) for d, x in zip(offsets, first)]
            for cr, ci in (a, _csub(b, a), _csub(c, a)):
                k_ref[n] = cr
                k_ref[n + 1] = ci
                n += 2


def _filter_dft(hfilt, bias, f_tab, width, casts=()):
    seq = hfilt.shape[0]
    blk = seq // HY_SPLIT
    nf = blk // DFT_TF
    n_spec = 2 * 3 * len(_TOEPLITZ_BLOCKS[HY_SPLIT])
    assert DFT_TF % 2 == 0
    pace = FILT_CAST_STEPS // (HY_ORDER * nf)
    cast_specs = _cast_specs(casts, HY_ORDER * nf * pace, lambda o, f, c: (o * nf + f) * pace + c)
    kern = functools.partial(_filter_dft_kernel, blk=blk, n_cast=len(casts))
    once = pl.Buffered(1)
    out = pl.pallas_call(
        kern,
        grid=(HY_ORDER, nf, pace),
        in_specs=[pl.BlockSpec((seq, width), lambda o, f, c: (0, 2 * o), pipeline_mode=once),
                  pl.BlockSpec((seq, width), lambda o, f, c: (0, 2 * o + 1), pipeline_mode=once),
                  pl.BlockSpec((None, 1, width), lambda o, f, c: (o, 0, 0)),
                  pl.BlockSpec((2 * DFT_TF, blk), lambda o, f, c: (f, 0))] + cast_specs,
        out_specs=[pl.BlockSpec((None, n_spec, DFT_TF, width), lambda o, f, c: (o, 0, f, 0))] + cast_specs,
        out_shape=[jax.ShapeDtypeStruct((HY_ORDER, n_spec, blk, width), F32)]
                  + [jax.ShapeDtypeStruct(a.shape, BF16) for a in casts],
        scratch_shapes=[pltpu.VMEM((blk, 2 * HY_SPLIT * width), BF16)],
        compiler_params=_cparams(("arbitrary", "arbitrary", "arbitrary")),
        name="hyena_filter_dft",
    )(hfilt, hfilt, bias, f_tab, *casts)
    return out[0], tuple(out[1:])


def _short_conv(u_ref, w_ref, b_ref, part, width, emit):
    seq = u_ref.shape[0]
    sl = slice(part * width, (part + 1) * width)
    w0, w1, w2, b = w_ref[0:1, sl], w_ref[1:2, sl], w_ref[2:3, sl], b_ref[:, sl]
    n = SHIFT_ROWS
    r = lax.broadcasted_iota(jnp.int32, (n, n), 0)
    c = lax.broadcasted_iota(jnp.int32, (n, n), 1)
    down = (c == r - 1).astype(BF16)
    up = (c == r + 1).astype(BF16)
    sub = SUBLANES
    row = lax.broadcasted_iota(jnp.int32, (sub, width), 0)
    for k in range(seq // n):
        ub = u_ref[k * n:(k + 1) * n, :]
        prev = jnp.dot(down, ub, preferred_element_type=F32)
        nxt = jnp.dot(up, ub, preferred_element_type=F32)
        if k > 0:
            edge = u_ref[k * n - BF16_ROWS:k * n, :].astype(F32)[BF16_ROWS - 1:BF16_ROWS]
            prev = jnp.concatenate([prev[:sub] + jnp.where(row == 0, edge, 0.0), prev[sub:]], axis=0)
        if k < seq // n - 1:
            edge = u_ref[(k + 1) * n:(k + 1) * n + BF16_ROWS, :].astype(F32)[0:1]
            nxt = jnp.concatenate([nxt[:-sub], nxt[-sub:] + jnp.where(row == sub - 1, edge, 0.0)], axis=0)
        emit(k * n, prev * w0 + ub.astype(F32) * w1 + nxt * w2 + b)


def _hyena_kernel(u0_ref, u1_ref, u2_ref, w_ref, b_ref, f_ref, g_ref, k_ref, o_ref, z_ref, acc_ref,
                  *, nf, width, blk):
    j = pl.program_id(1)
    cw = HY_CHUNK
    gw = HY_SPLIT * cw
    groups = width // cw

    def cols(g, i):
        return slice(g * gw + i * cw, g * gw + (i + 1) * cw)

    def place(row0, rows):
        return slice(row0 % blk, row0 % blk + rows), row0 // blk

    def first_input(row0, v):
        rows, i = place(row0, v.shape[0])
        for g in range(groups):
            z_ref[rows, cols(g, i)] = v[:, g * cw:(g + 1) * cw].astype(BF16)

    def next_input(row0, v):
        rows, i = place(row0, v.shape[0])
        for g in range(groups):
            z_ref[rows, cols(g, i)] = (v[:, g * cw:(g + 1) * cw] * acc_ref[rows, cols(g, i)]).astype(BF16)

    def result(row0, v):
        rows, i = place(row0, v.shape[0])
        for g in range(groups):
            o_ref[row0:row0 + v.shape[0], g * cw:(g + 1) * cw] = (
                v[:, g * cw:(g + 1) * cw] * acc_ref[rows, cols(g, i)]).astype(BF16)

    @pl.when(j == 0)
    def _():
        _short_conv(u0_ref, w_ref, b_ref, 0, width, first_input)
        acc_ref[...] = jnp.zeros_like(acc_ref)

    spans = [slice(g * gw, (g + 1) * gw) for g in range(groups)]
    uvs = [jnp.dot(f_ref[...], z_ref[:, span], preferred_element_type=F32) for span in spans]
    ys_all = []
    for g, uv in enumerate(uvs):
        ch = slice(g * cw, (g + 1) * cw)
        z = [(uv[:DFT_TF, i * cw:(i + 1) * cw], uv[DFT_TF:, i * cw:(i + 1) * cw]) for i in range(HY_SPLIT)]
        coef = [[(k_ref[2 * (3 * b + t), :, ch], k_ref[2 * (3 * b + t) + 1, :, ch]) for t in range(3)]
                for b in range(len(_TOEPLITZ_BLOCKS[HY_SPLIT]))]
        if HY_SPLIT == 2:
            ys = _toeplitz2(coef[0], z[0], z[1])
        else:
            p1 = _toeplitz2(coef[0], _cadd(z[0], z[2]), _cadd(z[1], z[3]))
            p2 = _toeplitz2(coef[1], z[2], z[3])
            p3 = _toeplitz2(coef[2], z[0], z[1])
            ys = (_cadd(p1[0], p2[0]), _cadd(p1[1], p2[1]), _cadd(p1[0], p3[0]), _cadd(p1[1], p3[1]))
        ys_all.append(jnp.concatenate([jnp.concatenate([r for r, _ in ys], axis=1),
                                       jnp.concatenate([i for _, i in ys], axis=1)], axis=0).astype(BF16))
    for span, y in zip(spans, ys_all):
        acc_ref[:, span] += jnp.dot(g_ref[...], y, preferred_element_type=F32)

    @pl.when(j == nf - 1)
    def _():
        _short_conv(u1_ref, w_ref, b_ref, 1, width, next_input)
        acc_ref[...] = jnp.zeros_like(acc_ref)

    @pl.when(j == 2 * nf - 1)
    def _():
        _short_conv(u2_ref, w_ref, b_ref, 2, width, result)


def _hyena(p5, conv_w, conv_b, f_tab, g_tab, spectra, col_block, width):
    batch, seq, _ = p5.shape
    blk = seq // HY_SPLIT
    nf = blk // DFT_TF
    n_spec = spectra.shape[1]
    assert HY_ORDER == 2 and n_spec == 6 * len(_TOEPLITZ_BLOCKS[HY_SPLIT]) and blk % SHIFT_ROWS == 0
    assert conv_w.shape[0] == HY_SHORT == 3 and width % HY_CHUNK == 0
    kern = functools.partial(_hyena_kernel, nf=nf, width=width, blk=blk)
    u_spec = lambda part: pl.BlockSpec((None, seq, width), lambda b, j: (b, 0, col_block + part))
    return pl.pallas_call(
        kern,
        grid=(batch, HY_ORDER * nf),
        in_specs=[u_spec(0), u_spec(1), u_spec(2),
                  pl.BlockSpec(conv_w.shape, lambda b, j: (0, 0)),
                  pl.BlockSpec(conv_b.shape, lambda b, j: (0, 0)),
                  pl.BlockSpec((2 * DFT_TF, blk), lambda b, j: (j % nf, 0)),
                  pl.BlockSpec((blk, 2 * DFT_TF), lambda b, j: (0, j % nf)),
                  pl.BlockSpec((None, n_spec, DFT_TF, width), lambda b, j: (j // nf, 0, j % nf, 0))],
        out_specs=pl.BlockSpec((None, seq, width), lambda b, j: (b, 0, 0), pipeline_mode=pl.Buffered(1)),
        out_shape=jax.ShapeDtypeStruct((batch, seq, width), BF16),
        scratch_shapes=[pltpu.VMEM((blk, HY_SPLIT * width), BF16), pltpu.VMEM((blk, HY_SPLIT * width), F32)],
        compiler_params=_cparams(("parallel", "arbitrary")),
        name="hyena_conv",
    )(p5, p5, p5, conv_w, conv_b, f_tab, g_tab, spectra)


def _merge_kernel(x_ref, g_ref, ya_ref, yb_ref, yc_ref, wga_ref, wgb_ref, wgc_ref, wa_ref, wb_ref, wc_ref,
                  wo_ref, o_ref, h_ref, m_ref, *, nj):
    s = pl.program_id(0)
    slot = s % 2

    @pl.when(s == 0)
    def _():
        m_ref[1] = jnp.zeros(m_ref.shape[1:], BF16)
        o_ref[...] = jnp.zeros_like(o_ref)

    @pl.when(s % nj == 0)
    def _():
        h_ref[...] = _rms(x_ref[...], g_ref[...]).astype(BF16)

    first = (s - 1) % nj == 0
    base = jnp.where(first, x_ref[...], o_ref[...])
    o_ref[...] = base + jnp.dot(m_ref[1 - slot], wo_ref[...], preferred_element_type=F32)

    h = h_ref[...]

    def branch(wg_ref, y_ref, w_ref):
        gate = jax.nn.sigmoid(jnp.dot(h, wg_ref[...], preferred_element_type=F32))
        return gate * jnp.dot(y_ref[...], w_ref[...], preferred_element_type=F32)

    merged = branch(wga_ref, ya_ref, wa_ref) + branch(wgb_ref, yb_ref, wb_ref) + branch(wgc_ref, yc_ref, wc_ref)
    m_ref[slot] = merged.astype(BF16)


def _merge(x, g, ya, yb, yc, w_gate, gate_col, w_a, w_b, w_c, w_o):
    t, d = x.shape
    nj = d // MERGE_TN
    g0 = gate_col // MERGE_TN
    n = (t // MERGE_TM) * nj
    cur = lambda s: jnp.minimum(s, n - 1)
    prev = lambda s: jnp.maximum(s - 1, 0)
    rows = lambda a: pl.BlockSpec((MERGE_TM, a.shape[1]), lambda s: (cur(s) // nj, 0))
    cols = lambda a, off: pl.BlockSpec((a.shape[0], MERGE_TN), lambda s: (0, cur(s) % nj + off))
    return pl.pallas_call(
        functools.partial(_merge_kernel, nj=nj),
        grid=(n + 1,),
        in_specs=[pl.BlockSpec((MERGE_TM, d), lambda s: (cur(s) // nj, 0)),
                  pl.BlockSpec((1, d), lambda s: (0, 0)), rows(ya), rows(yb), rows(yc),
                  cols(w_gate, g0), cols(w_gate, g0 + nj), cols(w_gate, g0 + 2 * nj),
                  cols(w_a, 0), cols(w_b, 0), cols(w_c, 0),
                  pl.BlockSpec((MERGE_TN, d), lambda s: (prev(s) % nj, 0))],
        out_specs=pl.BlockSpec((MERGE_TM, d), lambda s: (prev(s) // nj, 0)),
        out_shape=jax.ShapeDtypeStruct((t, d), F32),
        scratch_shapes=[pltpu.VMEM((MERGE_TM, d), BF16), pltpu.VMEM((2, MERGE_TM, MERGE_TN), BF16)],
        compiler_params=_cparams(("arbitrary",)),
        name="merge",
    )(x, g, ya, yb, yc, w_gate, w_gate, w_gate, w_a, w_b, w_c, w_o)


def _rope_tables(seq):
    half = ROT_DIM // 2
    inv = jnp.power(ROPE_THETA, -jnp.arange(0, ROT_DIM, 2, dtype=F32) / ROT_DIM)
    ang = jnp.arange(seq, dtype=F32)[:, None] * inv[None, :]
    cos, sin = jnp.cos(ang), jnp.sin(ang)
    ones = jnp.ones((seq, HEAD_DIM - ROT_DIM), F32)
    cos_t = jnp.concatenate([cos, cos, ones], axis=1)
    sin_t = jnp.concatenate([-sin, sin, 0.0 * ones], axis=1)
    assert cos_t.shape == (seq, HEAD_DIM) and half * 2 == ROT_DIM
    return cos_t, sin_t


def _hyena_positional_features(seq, emb):
    bands = (emb - 1) // 2
    t = jnp.linspace(0.0, 1.0, seq, dtype=F32)[:, None]
    w = 2.0 * math.pi * jnp.arange(seq, dtype=F32)[:, None] / seq
    f = jnp.linspace(1e-4, bands - 1, bands, dtype=F32)[None, :]
    return jnp.concatenate([t, jnp.cos(f * w), -jnp.sin(f * w)], axis=-1)


def _pad_to(a, shape):
    return jnp.pad(a, [(0, s - d) for s, d in zip(shape, a.shape)])


def kernel(x, mem, g_ff1, w_ff1_in, w_ff1_out, g_mix, w_in, a_gq, a_gk, hy_conv_w, hy_conv_b, hy_f_w1, hy_f_b1,
           hy_f_w2, hy_f_b2, hy_f_w3, hy_f_b3, hy_f_w4, hy_f_freq, hy_bias, g_mem, w_mem_kv, m_gq, m_gk,
           w_br_a, w_br_b, w_br_c, w_out, g_ff2, w_ff2_in, w_ff2_out, g_post):
    batch, seq, d = x.shape
    depth = g_ff1.shape[0]
    dswa_width = len(DSWA_GROUPS) * DSWA_HEADS_PER_GROUP * HEAD_DIM
    hy_width = hy_bias.shape[-1]
    mem_width = MEM_HEADS * HEAD_DIM
    n_cols = 3 * dswa_width + (HY_ORDER + 1) * hy_width + mem_width
    n_rope = 2 * dswa_width // PROJ_TN
    n_plain = (n_cols - mem_width) // PROJ_TN - n_rope
    assert 2 * dswa_width % PROJ_TN == 0 and mem_width == PROJ_TN and n_cols % PROJ_TN == 0
    assert 3 * dswa_width % hy_width == 0 and n_cols % MERGE_TN == 0 and seq & (seq - 1) == 0
    scale = 1.0 / math.sqrt(HEAD_DIM)

    cos_t, sin_t = _rope_tables(seq)
    f_tab, g_tab = _dft_tables(seq // HY_SPLIT)
    emb, hidden = hy_f_w1.shape[1:]
    z_feat = _pad_to(_hyena_positional_features(seq, emb), (seq, LANES))
    t_col = jnp.linspace(0.0, 1.0, seq, dtype=F32)[:, None]
    deltas = jnp.linspace(math.log(HY_TARGET) / HY_SLOW_DECAY, math.log(HY_TARGET) / HY_FAST_DECAY, hy_width, dtype=F32)
    absd = jnp.abs(deltas)[None, :]
    row = lambda v: v.reshape(1, -1)

    xt = x.reshape(batch * seq, d)
    for l in range(depth):
        w123 = jnp.stack([_pad_to(w.astype(F32), (LANES, LANES)) for w in (hy_f_w1[l], hy_f_w2[l], hy_f_w3[l])])
        vecs = _pad_to(jnp.stack([hy_f_b1[l], hy_f_b2[l], hy_f_b3[l], hy_f_freq[l]]).astype(F32), (SUBLANES, LANES))
        hfilt, (w_ff1_out_b,) = _filter_mlp(z_feat, w123, vecs, hy_f_w4[l], t_col, absd, casts=(w_ff1_out[l],))
        spectra, (w_ff1_in_b,) = _filter_dft(
            hfilt, hy_bias[l].reshape(HY_ORDER, 1, hy_width), f_tab, hy_width, casts=(w_ff1_in[l],))

        xt, (w_ff2_in_b, w_ff2_out_b, w_in_b) = _ffn(
            xt, row(g_ff1[l]), w_ff1_in_b, w_ff1_out_b, row(g_post[l]), False,
            casts=(w_ff2_in[l], w_ff2_out[l], w_in[l]))

        heads = dswa_width // HEAD_DIM
        gain_cols = jnp.concatenate([jnp.tile(a_gq[l], heads) * scale, jnp.tile(a_gk[l], heads),
                                     jnp.ones((n_cols - 2 * dswa_width - mem_width,), F32),
                                     jnp.tile(m_gq[l], MEM_HEADS) * scale])[None, :]
        p5 = _mixproj(xt, row(g_mix[l]), w_in_b, gain_cols, cos_t, sin_t, n_cols, n_rope, n_plain)
        p5 = p5.reshape(batch, seq, n_cols)

        blk = dswa_width // HEAD_DIM
        y_a, (w_out_b, w_br_a_b, w_br_b_b, w_br_c_b, w_mem_kv_b) = _attn(
            p5, batch, seq, 0, blk, 2 * blk, casts=(w_out[l], w_br_a[l], w_br_b[l], w_br_c[l], w_mem_kv[l]))

        y_b = _hyena(p5, hy_conv_w[l], row(hy_conv_b[l]), f_tab, g_tab, spectra, 3 * dswa_width // hy_width, hy_width)

        y_c = _memattn(p5, mem, row(g_mem[l]), w_mem_kv_b, row(m_gk[l]), (n_cols - mem_width) // mem_width)

        t = batch * seq
        xt = _merge(xt, row(g_mix[l]), y_a.reshape(t, -1), y_b.reshape(t, -1), y_c.reshape(t, -1),
                    w_in_b, n_cols, w_br_a_b, w_br_b_b, w_br_c_b, w_out_b)

        xt, _ = _ffn(xt, row(g_ff2[l]), w_ff2_in_b, w_ff2_out_b, row(g_post[l]), True)
    return xt.reshape(batch, seq, d)
```

```python
import functools
import math

import jax
import jax.numpy as jnp
from jax import lax
from jax.experimental import pallas as pl
from jax.experimental.pallas import tpu as pltpu

F32 = jnp.float32
BF16 = jnp.bfloat16

HEAD_DIM = 128
ROPE_THETA = 500000.0
ROT_DIM = HEAD_DIM // 4
EPS = 1e-6
NEG = -1e30
DSWA_GROUPS = ((128, 1), (512, 4), (2048, 16))
DSWA_HEADS_PER_GROUP = 2
MEM_HEADS = 4
HY_ORDER = 2
HY_SHORT = 3
HY_FAST_DECAY = 0.3
HY_SLOW_DECAY = 1.5
HY_TARGET = 1e-2

LANES = 128
SUBLANES = 8
BF16_ROWS = 16
VMEM_LIMIT_BYTES = 60 * 1024 * 1024

FFN_TM = 1024
FFN_TF = 512
PROJ_TM = 512
PROJ_TN = 512
ATT_TQ = 128
ATT_BACK = 128
MEM_TQ = 1024
DFT_TF = 128
HY_SPLIT = 4
HY_CHUNK = 256
FILT_CAST_STEPS = 16
SHIFT_ROWS = 256
FILT_MLP_TL = 256
MERGE_TM = 1024
MERGE_TN = 256


def _cparams(sem):
    return pltpu.CompilerParams(dimension_semantics=sem, vmem_limit_bytes=VMEM_LIMIT_BYTES)


def _rms(x, g):
    return x * lax.rsqrt(jnp.mean(x * x, axis=-1, keepdims=True) + EPS) * g


def _ffn_kernel(x_ref, g_ref, wa_ref, wb_ref, wd_ref, gp_ref, *rest, nj, final_norm, n_cast):
    cast_in, o_ref, cast_out, h_ref = rest[:n_cast], rest[n_cast], rest[n_cast + 1:2 * n_cast + 1], rest[-1]
    j = pl.program_id(1)

    for src, dst in zip(cast_in, cast_out):
        dst[...] = src[...].astype(BF16)

    @pl.when(j == 0)
    def _():
        x = x_ref[...]
        h_ref[...] = _rms(x, g_ref[...]).astype(BF16)
        o_ref[...] = x

    h = h_ref[...]
    a = jnp.dot(h, wa_ref[...], preferred_element_type=F32)
    b = jnp.dot(h, wb_ref[...], preferred_element_type=F32)
    act = (0.5 * a * jax.nn.sigmoid(a) * b).astype(BF16)
    o_ref[...] += jnp.dot(act, wd_ref[...], preferred_element_type=F32)

    if final_norm:
        @pl.when(j == nj - 1)
        def _():
            o_ref[...] = _rms(o_ref[...], gp_ref[...])


def _cast_specs(casts, steps, flat):
    specs = []
    for a in casts:
        rows = BF16_ROWS
        while a.shape[0] % rows or a.shape[0] // rows > steps:
            rows += BF16_ROWS
        n = a.shape[0] // rows
        specs.append(pl.BlockSpec((rows, a.shape[1]), functools.partial(
            lambda *idx, n: (flat(*idx) * n // steps, 0), n=n)))
    return specs


def _ffn(x, g, w_in, w_out, g_post, final_norm, casts=()):
    t, d = x.shape
    d_ff = w_out.shape[0]
    nj = d_ff // FFN_TF
    cast_specs = _cast_specs(casts, (t // FFN_TM) * nj, lambda i, j: i * nj + j)
    kern = functools.partial(_ffn_kernel, nj=nj, final_norm=final_norm, n_cast=len(casts))
    out = pl.pallas_call(
        kern,
        grid=(t // FFN_TM, nj),
        in_specs=[
            pl.BlockSpec((FFN_TM, d), lambda i, j: (i, 0)),
            pl.BlockSpec((1, d), lambda i, j: (0, 0)),
            pl.BlockSpec((d, FFN_TF), lambda i, j: (0, j)),
            pl.BlockSpec((d, FFN_TF), lambda i, j: (0, j + nj)),
            pl.BlockSpec((FFN_TF, d), lambda i, j: (j, 0)),
            pl.BlockSpec((1, d), lambda i, j: (0, 0)),
        ] + cast_specs,
        out_specs=[pl.BlockSpec((FFN_TM, d), lambda i, j: (i, 0))] + cast_specs,
        out_shape=[jax.ShapeDtypeStruct((t, d), F32)] + [jax.ShapeDtypeStruct(a.shape, BF16) for a in casts],
        scratch_shapes=[pltpu.VMEM((FFN_TM, d), BF16)],
        compiler_params=_cparams(("arbitrary", "arbitrary")),
        name="ffn_final" if final_norm else "ffn",
    )(x, g, w_in, w_in, w_out, g_post, *casts)
    return out[0], tuple(out[1:])


def _mixproj_kernel(x0_ref, xn_ref, g_ref, w_ref, gain_ref, cos_ref, sin_ref, o_ref, h_ref, *, n_rope, n_plain):
    i = pl.program_id(0)
    slot = i % 2

    @pl.when(i == 0)
    def _():
        h_ref[0] = _rms(x0_ref[...], g_ref[...]).astype(BF16)

    h_ref[1 - slot] = _rms(xn_ref[...], g_ref[...]).astype(BF16)
    h = h_ref[slot]
    lane = lax.broadcasted_iota(jnp.int32, (PROJ_TM, HEAD_DIM), 1)
    first = lane < (ROT_DIM // 2)
    n_tiles = w_ref.shape[1] // PROJ_TN
    order = [j for j in range(n_tiles) if not n_rope <= j < n_rope + n_plain] + list(range(n_rope, n_rope + n_plain))
    for j in order:
        acc = jnp.dot(h, w_ref[:, j * PROJ_TN:(j + 1) * PROJ_TN], preferred_element_type=F32)
        plain = n_rope <= j < n_rope + n_plain
        for s in range(PROJ_TN // HEAD_DIM):
            sl = slice(j * PROJ_TN + s * HEAD_DIM, j * PROJ_TN + (s + 1) * HEAD_DIM)
            t = acc[:, s * HEAD_DIM:(s + 1) * HEAD_DIM]
            if not plain:
                t = _rms(t, gain_ref[:, sl])
            if j < n_rope:
                partner = jnp.where(first, pltpu.roll(t, HEAD_DIM - ROT_DIM // 2, axis=1),
                                    pltpu.roll(t, ROT_DIM // 2, axis=1))
                t = t * cos_ref[...] + partner * sin_ref[...]
            o_ref[:, sl] = t.astype(BF16)


def _mixproj(x, g, w, gain_cols, cos_t, sin_t, n_cols, n_rope, n_plain):
    t, d = x.shape
    n_row = t // PROJ_TM
    per_seq = cos_t.shape[0] // PROJ_TM
    kern = functools.partial(_mixproj_kernel, n_rope=n_rope, n_plain=n_plain)
    once = pl.Buffered(1)
    return pl.pallas_call(
        kern,
        grid=(n_row,),
        in_specs=[
            pl.BlockSpec((PROJ_TM, d), lambda i: (0, 0), pipeline_mode=once),
            pl.BlockSpec((PROJ_TM, d), lambda i: (jnp.minimum(i + 1, n_row - 1), 0)),
            pl.BlockSpec((1, d), lambda i: (0, 0)),
            pl.BlockSpec((d, n_cols), lambda i: (0, 0), pipeline_mode=once),
            pl.BlockSpec((1, n_cols), lambda i: (0, 0)),
            pl.BlockSpec((PROJ_TM, HEAD_DIM), lambda i: (i % per_seq, 0)),
            pl.BlockSpec((PROJ_TM, HEAD_DIM), lambda i: (i % per_seq, 0)),
        ],
        out_specs=pl.BlockSpec((PROJ_TM, n_cols), lambda i: (i, 0)),
        out_shape=jax.ShapeDtypeStruct((t, n_cols), BF16),
        scratch_shapes=[pltpu.VMEM((2, PROJ_TM, d), BF16)],
        compiler_params=_cparams(("arbitrary",)),
        name="mixproj",
    )(x, x, g, w, gain_cols, cos_t, sin_t)


def _banded_tiles(q, k, v, length):
    tiles = []
    for qb in range(length // ATT_TQ):
        i0 = qb * ATT_TQ
        lo = max(0, i0 - ATT_BACK)
        hi = min(length, i0 + ATT_TQ + ATT_BACK)
        tiles.append((i0, q[i0:i0 + ATT_TQ], k[lo:hi], v[lo:hi], lo - (i0 - ATT_BACK)))
    return tiles


def _banded_attention(tiles, tab_ref):
    scores = [lax.dot_general(q, k, (((1,), (1,)), ((), ())), preferred_element_type=F32)
              + tab_ref[:, c0:c0 + k.shape[0]] for _, q, k, _, c0 in tiles]
    maxes = [s.max(axis=-1, keepdims=True) for s in scores]
    probs = [jnp.exp(s - m) for s, m in zip(scores, maxes)]
    dens = [p.sum(axis=-1, keepdims=True) for p in probs]
    outs = [jnp.dot(p.astype(BF16), t[3], preferred_element_type=F32) / d for p, t, d in zip(probs, tiles, dens)]
    return [(o, m + jnp.log(d)) for o, m, d in zip(outs, maxes, dens)]


def _attn_kernel(*refs, seq, n_cast):
    qkv, tab_ref = refs[:3], refs[3]
    cast_in, o_ref, cast_out = refs[4:4 + n_cast], refs[4 + n_cast], refs[5 + n_cast:5 + 2 * n_cast]
    qf, kf, vf = refs[-7:-4]
    outs, lses = refs[-4:-2], refs[-2:]

    for src, dst in zip(cast_in, cast_out):
        dst[...] = src[...].astype(BF16)

    dilated = [(g, dil) for g, (_, dil) in enumerate(DSWA_GROUPS) if dil > 1]
    plain = [g for g, (_, dil) in enumerate(DSWA_GROUPS) if dil == 1]
    assert len(dilated) == len(outs) and len(plain) == 1

    for h in range(DSWA_HEADS_PER_GROUP):
        head = lambda g: slice((g * DSWA_HEADS_PER_GROUP + h) * HEAD_DIM, (g * DSWA_HEADS_PER_GROUP + h + 1) * HEAD_DIM)
        for slot, (g, dil) in enumerate(dilated):
            length = seq // dil
            for src, dst in zip(qkv, (qf, kf, vf)):
                dst[...] = src[:, head(g)].astype(F32)
            tiles, rows = [], []
            for r in range(dil):
                q, k, v = (ref[pl.ds(r, length, stride=dil), :].astype(BF16) for ref in (qf, kf, vf))
                sub = _banded_tiles(q, k, v, length)
                tiles += sub
                rows += [pl.ds(t[0] * dil + r, ATT_TQ, stride=dil) for t in sub]
            for rw, (o, lse) in zip(rows, _banded_attention(tiles, tab_ref)):
                outs[slot][rw, :] = o
                lses[slot][rw, :] = jnp.broadcast_to(lse, (ATT_TQ, HEAD_DIM))

        cs = head(plain[0])
        tiles = _banded_tiles(qkv[0][:, cs], qkv[1][:, cs], qkv[2][:, cs], seq)
        for (i0, *_), (o, lse) in zip(tiles, _banded_attention(tiles, tab_ref)):
            rw = slice(i0, i0 + ATT_TQ)
            parts = [(o, jnp.broadcast_to(lse, (ATT_TQ, HEAD_DIM)))] + [(a[rw, :], b[rw, :]) for a, b in zip(outs, lses)]
            m = functools.reduce(jnp.maximum, [l for _, l in parts])
            num = jnp.zeros((ATT_TQ, HEAD_DIM), F32)
            den = jnp.zeros((ATT_TQ, HEAD_DIM), F32)
            for a, l in parts:
                w = jnp.exp(l - m)
                num = num + w * a
                den = den + w
            o_ref[rw, h * HEAD_DIM:(h + 1) * HEAD_DIM] = (num / den).astype(BF16)


def _attn_table():
    width = 2 * ATT_BACK + ATT_TQ
    row = jnp.arange(ATT_TQ, dtype=jnp.int32)[:, None]
    col = jnp.arange(width, dtype=jnp.int32)[None, :]
    half = DSWA_GROUPS[0][0] // (2 * DSWA_GROUPS[0][1])
    return jnp.where(jnp.abs(col - ATT_BACK - row) <= half, 0.0, NEG).astype(F32)


def _attn(p5, batch, seq, casts=()):
    halves = {win // (2 * dil) for win, dil in DSWA_GROUPS}
    assert len(halves) == 1 and halves.pop() <= ATT_BACK
    assert all(seq % dil == 0 and (seq // dil) % ATT_TQ == 0 for _, dil in DSWA_GROUPS)
    tab = _attn_table()
    hpg = DSWA_HEADS_PER_GROUP
    width = len(DSWA_GROUPS) * hpg * HEAD_DIM
    in_specs = [pl.BlockSpec((None, seq, width), functools.partial(lambda b, c: (b, 0, c), c=c)) for c in range(3)]
    in_specs.append(pl.BlockSpec(tab.shape, lambda b: (0, 0)))
    cast_specs = _cast_specs(casts, batch, lambda b: b)
    out = pl.pallas_call(
        functools.partial(_attn_kernel, seq=seq, n_cast=len(casts)),
        grid=(batch,),
        in_specs=in_specs + cast_specs,
        out_specs=[pl.BlockSpec((None, seq, hpg * HEAD_DIM), lambda b: (b, 0, 0))] + cast_specs,
        out_shape=[jax.ShapeDtypeStruct((batch, seq, hpg * HEAD_DIM), BF16)]
                  + [jax.ShapeDtypeStruct(a.shape, BF16) for a in casts],
        scratch_shapes=[pltpu.VMEM((seq, HEAD_DIM), F32)] * 7,
        compiler_params=_cparams(("arbitrary",)),
        name="attn",
    )(p5, p5, p5, tab, *casts)
    return out[0], tuple(out[1:])


def _memattn_kernel(q_ref, mem_ref, g_ref, wkv_ref, gk_ref, o_ref, k_ref, v_ref):
    width = MEM_HEADS * HEAD_DIM

    @pl.when(pl.program_id(1) == 0)
    def _():
        mn = _rms(mem_ref[...], g_ref[...]).astype(BF16)
        kv = jnp.dot(mn, wkv_ref[...], preferred_element_type=F32)
        for h in range(MEM_HEADS):
            sl = slice(h * HEAD_DIM, (h + 1) * HEAD_DIM)
            k_ref[:, sl] = _rms(kv[:, sl], gk_ref[...]).astype(BF16)
        v_ref[...] = kv[:, width:].astype(BF16)

    for h in range(MEM_HEADS):
        sl = slice(h * HEAD_DIM, (h + 1) * HEAD_DIM)
        s = lax.dot_general(q_ref[:, sl], k_ref[:, sl], (((1,), (1,)), ((), ())), preferred_element_type=F32)
        p = jnp.exp(s - s.max(axis=-1, keepdims=True))
        den = p.sum(axis=-1, keepdims=True)
        o = jnp.dot(p.astype(BF16), v_ref[:, sl], preferred_element_type=F32)
        o_ref[:, sl] = (o / den).astype(BF16)


def _memattn(p5, mem, g_mem, w_kv, gk, col_block):
    batch, seq, _ = p5.shape
    n_mem, d = mem.shape[1:]
    width = MEM_HEADS * HEAD_DIM
    return pl.pallas_call(
        _memattn_kernel,
        grid=(batch, seq // MEM_TQ),
        in_specs=[
            pl.BlockSpec((None, MEM_TQ, width), lambda b, i: (b, i, col_block)),
            pl.BlockSpec((None, n_mem, d), lambda b, i: (b, 0, 0)),
            pl.BlockSpec((1, d), lambda b, i: (0, 0)),
            pl.BlockSpec((d, 2 * width), lambda b, i: (0, 0)),
            pl.BlockSpec((1, HEAD_DIM), lambda b, i: (0, 0)),
        ],
        out_specs=pl.BlockSpec((None, MEM_TQ, width), lambda b, i: (b, i, 0)),
        out_shape=jax.ShapeDtypeStruct((batch, seq, width), BF16),
        scratch_shapes=[pltpu.VMEM((n_mem, width), BF16), pltpu.VMEM((n_mem, width), BF16)],
        compiler_params=_cparams(("parallel", "arbitrary")),
        name="memattn",
    )(p5, mem, g_mem, w_kv, gk)


def _dft_kernel(f_ref, g_ref, cb, sb, cbt, sbt, cac, sac, *, seq):
    t = pl.program_id(0)
    n2 = 4 * seq
    theta = 2.0 * math.pi / n2

    def trig(m):
        ang = (m & (n2 - 1)).astype(F32) * theta
        return jnp.cos(ang), jnp.sin(ang)

    @pl.when(t == 0)
    def _():
        f_lo = lax.broadcasted_iota(jnp.int32, (DFT_TF, seq), 0)
        s = lax.broadcasted_iota(jnp.int32, (DFT_TF, seq), 1)
        cb[...], sb[...] = trig((2 * f_lo + 1) * s)
        s = lax.broadcasted_iota(jnp.int32, (seq, DFT_TF), 0)
        f_lo = lax.broadcasted_iota(jnp.int32, (seq, DFT_TF), 1)
        cbt[...], sbt[...] = trig((2 * f_lo + 1) * s)
        s = lax.broadcasted_iota(jnp.int32, (seq, LANES), 0)
        tile = lax.broadcasted_iota(jnp.int32, (seq, LANES), 1)
        cac[...], sac[...] = trig(2 * DFT_TF * tile * s)

    s = lax.broadcasted_iota(jnp.int32, (1, seq), 1)
    ca, sa = trig(2 * DFT_TF * t * s)
    f_ref[:DFT_TF, :] = (ca * cb[...] - sa * sb[...]).astype(BF16)
    f_ref[DFT_TF:, :] = (-(sa * cb[...] + ca * sb[...])).astype(BF16)
    pick = lax.broadcasted_iota(jnp.int32, (seq, LANES), 1) == t
    ca = jnp.sum(jnp.where(pick, cac[...], 0.0), axis=-1, keepdims=True)
    sa = jnp.sum(jnp.where(pick, sac[...], 0.0), axis=-1, keepdims=True)
    g_ref[:, :DFT_TF] = (ca * cbt[...] - sa * sbt[...]).astype(BF16)
    g_ref[:, DFT_TF:] = (-(sa * cbt[...] + ca * sbt[...])).astype(BF16)


def _dft_tables(seq):
    nf = seq // DFT_TF
    assert nf <= LANES
    return pl.pallas_call(
        functools.partial(_dft_kernel, seq=seq),
        grid=(nf,),
        out_specs=[pl.BlockSpec((2 * DFT_TF, seq), lambda t: (t, 0)),
                   pl.BlockSpec((seq, 2 * DFT_TF), lambda t: (0, t))],
        out_shape=[jax.ShapeDtypeStruct((2 * seq, seq), BF16), jax.ShapeDtypeStruct((seq, 2 * seq), BF16)],
        scratch_shapes=[pltpu.VMEM((DFT_TF, seq), F32)] * 2 + [pltpu.VMEM((seq, DFT_TF), F32)] * 2
                       + [pltpu.VMEM((seq, LANES), F32)] * 2,
        compiler_params=_cparams(("arbitrary",)),
        name="dft_tables",
    )()


def _filter_mlp_kernel(z_ref, w_ref, v_ref, w4_ref, t_ref, d_ref, *rest, n_cast):
    cast_in, o_ref, cast_out = rest[:n_cast], rest[n_cast], rest[n_cast + 1:-2]
    w_hi, w_lo = rest[-2:]
    for src, dst in zip(cast_in, cast_out):
        dst[...] = src[...].astype(BF16)

    split = lambda a: (a.astype(BF16), (a - a.astype(BF16).astype(F32)).astype(BF16))

    @pl.when(pl.program_id(0) == 0)
    def _():
        hidden = w4_ref.shape[0]
        for ref, part in zip((w_hi, w_lo), split(w4_ref[...])):
            ref[:hidden, :] = part
            ref[hidden:, :] = jnp.zeros((ref.shape[0] - hidden, ref.shape[1]), BF16)

    hp = lax.Precision.HIGHEST
    fr = v_ref[3:4, :]
    hh = z_ref[...]
    for n in range(3):
        hh = jnp.sin(fr * (jnp.dot(hh, w_ref[n], precision=hp, preferred_element_type=F32) + v_ref[n:n + 1, :]))
    a_hi, a_lo = split(hh)
    h = (jnp.dot(a_hi, w_hi[...], preferred_element_type=F32) + jnp.dot(a_hi, w_lo[...], preferred_element_type=F32)
         + jnp.dot(a_lo, w_hi[...], preferred_element_type=F32))
    decay = jnp.exp(-t_ref[...] * d_ref[...])
    width = decay.shape[1]
    for c in range(h.shape[1] // width):
        o_ref[:, c * width:(c + 1) * width] = h[:, c * width:(c + 1) * width] * decay


def _filter_mlp(z, w123, vecs, w4, t_col, absd, casts=()):
    seq = z.shape[0]
    tl = FILT_MLP_TL
    n_out = w4.shape[1]
    assert w4.shape[0] % BF16_ROWS == 0 and w4.shape[0] <= LANES
    full = lambda a: pl.BlockSpec(a.shape, lambda i: (0,) * a.ndim)
    cast_specs = _cast_specs(casts, seq // tl, lambda i: i)
    out = pl.pallas_call(
        functools.partial(_filter_mlp_kernel, n_cast=len(casts)),
        grid=(seq // tl,),
        in_specs=[pl.BlockSpec((tl, z.shape[1]), lambda i: (i, 0)), full(w123), full(vecs), full(w4),
                  pl.BlockSpec((tl, 1), lambda i: (i, 0)), full(absd)] + cast_specs,
        out_specs=[pl.BlockSpec((tl, n_out), lambda i: (i, 0))] + cast_specs,
        out_shape=[jax.ShapeDtypeStruct((seq, n_out), F32)] + [jax.ShapeDtypeStruct(a.shape, BF16) for a in casts],
        scratch_shapes=[pltpu.VMEM((LANES, n_out), BF16)] * 2,
        compiler_params=_cparams(("arbitrary",)),
        name="hyena_filter_mlp",
    )(z, w123, vecs, w4, t_col, absd, *casts)
    return out[0], tuple(out[1:])


_TOEPLITZ_BLOCKS = {2: ((0, -1, 1),), 4: ((0, -1, 1), (-2, -3, -1), (2, 1, 3))}


def _csub(x, y):
    return x[0] - y[0], x[1] - y[1]


def _cadd(x, y):
    return x[0] + y[0], x[1] + y[1]


def _cmul(k, x):
    return k[0] * x[0] - k[1] * x[1], k[0] * x[1] + k[1] * x[0]


def _toeplitz2(coef, u, v):
    p = _cmul(coef[0], _cadd(u, v))
    return _cadd(p, _cmul(coef[1], v)), _cadd(p, _cmul(coef[2], u))


def _filter_dft_kernel(hf_ref, hb_ref, bias_ref, f_ref, *rest, blk, n_cast):
    cast_in, k_ref, cast_out, r_ref = rest[:n_cast], rest[n_cast], rest[n_cast + 1:2 * n_cast + 1], rest[-1]
    width = hf_ref.shape[1]

    for src, dst in zip(cast_in, cast_out):
        dst[...] = src[...].astype(BF16)

    @pl.when((pl.program_id(1) == 0) & (pl.program_id(2) == 0))
    def _():
        for c, ref in enumerate((hf_ref, hb_ref)):
            for part in range(HY_SPLIT):
                col = (HY_SPLIT * c + part) * width
                r_ref[:, col:col + width] = ref[part * blk:(part + 1) * blk, :].astype(BF16)

    @pl.when(pl.program_id(2) == 0)
    def _():
        r = r_ref[...]
        re = jnp.dot(f_ref[:DFT_TF, :], r, preferred_element_type=F32)
        im = jnp.dot(f_ref[DFT_TF:, :], r, preferred_element_type=F32)

        def transform(c, j):
            col = (HY_SPLIT * c + j) * width
            return re[:, col:col + width], im[:, col:col + width]

        row = lax.broadcasted_iota(jnp.int32, (DFT_TF, width), 0)
        sgn = jnp.where((row & 1) == 0, 1.0, -1.0)
        scale = 1.0 / blk
        (f0r, f0i), (b0r, b0i) = transform(0, 0), transform(1, 0)
        seg = {0: ((f0r + b0r + bias_ref[...]) * scale, (f0i - b0i) * scale)}
        for d in range(1, HY_SPLIT):
            for c, ref, conj in ((0, hf_ref, 1.0), (1, hb_ref, -1.0)):
                (ar, ai), (pr, pi) = transform(c, d), transform(c, d - 1)
                edge = ref[(d - 1) * blk:(d - 1) * blk + 1, :]
                seg[d if c == 0 else -d] = ((ar - sgn * pi) * scale, (ai + sgn * (pr - edge)) * (conj * scale))

        blocks = _TOEPLITZ_BLOCKS[HY_SPLIT]
        first = [seg[d] for d in blocks[0]]
        n = 0
        for g, offsets in enumerate(blocks):
            a, b, c = [seg[d] for d in offsets] if g == 0 else [_csub(seg[d], x) for d, x in zip(offsets, first)]
            for cr, ci in (a, _csub(b, a), _csub(c, a)):
                k_ref[n] = cr
                k_ref[n + 1] = ci
                n += 2


def _filter_dft(hfilt, bias, f_tab, width, casts=()):
    seq = hfilt.shape[0]
    blk = seq // HY_SPLIT
    nf = blk // DFT_TF
    n_spec = 2 * 3 * len(_TOEPLITZ_BLOCKS[HY_SPLIT])
    assert DFT_TF % 2 == 0
    pace = FILT_CAST_STEPS // (HY_ORDER * nf)
    cast_specs = _cast_specs(casts, HY_ORDER * nf * pace, lambda o, f, c: (o * nf + f) * pace + c)
    kern = functools.partial(_filter_dft_kernel, blk=blk, n_cast=len(casts))
    once = pl.Buffered(1)
    out = pl.pallas_call(
        kern,
        grid=(HY_ORDER, nf, pace),
        in_specs=[pl.BlockSpec((seq, width), lambda o, f, c: (0, 2 * o), pipeline_mode=once),
                  pl.BlockSpec((seq, width), lambda o, f, c: (0, 2 * o + 1), pipeline_mode=once),
                  pl.BlockSpec((None, 1, width), lambda o, f, c: (o, 0, 0)),
                  pl.BlockSpec((2 * DFT_TF, blk), lambda o, f, c: (f, 0))] + cast_specs,
        out_specs=[pl.BlockSpec((None, n_spec, DFT_TF, width), lambda o, f, c: (o, 0, f, 0))] + cast_specs,
        out_shape=[jax.ShapeDtypeStruct((HY_ORDER, n_spec, blk, width), F32)]
                  + [jax.ShapeDtypeStruct(a.shape, BF16) for a in casts],
        scratch_shapes=[pltpu.VMEM((blk, 2 * HY_SPLIT * width), BF16)],
        compiler_params=_cparams(("arbitrary", "arbitrary", "arbitrary")),
        name="hyena_filter_dft",
    )(hfilt, hfilt, bias, f_tab, *casts)
    return out[0], tuple(out[1:])


def _short_conv(u_ref, w_ref, b_ref, part, width, emit):
    seq = u_ref.shape[0]
    sl = slice(part * width, (part + 1) * width)
    w0, w1, w2, b = w_ref[0:1, sl], w_ref[1:2, sl], w_ref[2:3, sl], b_ref[:, sl]
    n = SHIFT_ROWS
    r = lax.broadcasted_iota(jnp.int32, (n, n), 0)
    c = lax.broadcasted_iota(jnp.int32, (n, n), 1)
    down = (c == r - 1).astype(BF16)
    up = (c == r + 1).astype(BF16)
    sub = SUBLANES
    row = lax.broadcasted_iota(jnp.int32, (sub, width), 0)
    for k in range(seq // n):
        ub = u_ref[k * n:(k + 1) * n, :]
        prev = jnp.dot(down, ub, preferred_element_type=F32)
        nxt = jnp.dot(up, ub, preferred_element_type=F32)
        if k > 0:
            edge = u_ref[k * n - BF16_ROWS:k * n, :].astype(F32)[BF16_ROWS - 1:BF16_ROWS]
            prev = jnp.concatenate([prev[:sub] + jnp.where(row == 0, edge, 0.0), prev[sub:]], axis=0)
        if k < seq // n - 1:
            edge = u_ref[(k + 1) * n:(k + 1) * n + BF16_ROWS, :].astype(F32)[0:1]
            nxt = jnp.concatenate([nxt[:-sub], nxt[-sub:] + jnp.where(row == sub - 1, edge, 0.0)], axis=0)
        emit(k * n, prev * w0 + ub.astype(F32) * w1 + nxt * w2 + b)


def _hyena_kernel(u0_ref, u1_ref, u2_ref, w_ref, b_ref, f_ref, g_ref, k_ref, o_ref, z_ref, acc_ref,
                  *, nf, width, blk):
    j = pl.program_id(1)
    cw = HY_CHUNK
    gw = HY_SPLIT * cw
    groups = width // cw

    def cols(g, i):
        return slice(g * gw + i * cw, g * gw + (i + 1) * cw)

    def place(row0, rows):
        return slice(row0 % blk, row0 % blk + rows), row0 // blk

    def first_input(row0, v):
        rows, i = place(row0, v.shape[0])
        for g in range(groups):
            z_ref[rows, cols(g, i)] = v[:, g * cw:(g + 1) * cw].astype(BF16)

    def next_input(row0, v):
        rows, i = place(row0, v.shape[0])
        for g in range(groups):
            z_ref[rows, cols(g, i)] = (v[:, g * cw:(g + 1) * cw] * acc_ref[rows, cols(g, i)]).astype(BF16)

    def result(row0, v):
        rows, i = place(row0, v.shape[0])
        for g in range(groups):
            o_ref[row0:row0 + v.shape[0], g * cw:(g + 1) * cw] = (
                v[:, g * cw:(g + 1) * cw] * acc_ref[rows, cols(g, i)]).astype(BF16)

    @pl.when(j == 0)
    def _():
        _short_conv(u0_ref, w_ref, b_ref, 0, width, first_input)
        acc_ref[...] = jnp.zeros_like(acc_ref)

    spans = [slice(g * gw, (g + 1) * gw) for g in range(groups)]
    uvs = [jnp.dot(f_ref[...], z_ref[:, span], preferred_element_type=F32) for span in spans]
    ys_all = []
    for g, uv in enumerate(uvs):
        ch = slice(g * cw, (g + 1) * cw)
        z = [(uv[:DFT_TF, i * cw:(i + 1) * cw], uv[DFT_TF:, i * cw:(i + 1) * cw]) for i in range(HY_SPLIT)]
        coef = [[(k_ref[2 * (3 * b + t), :, ch], k_ref[2 * (3 * b + t) + 1, :, ch]) for t in range(3)]
                for b in range(len(_TOEPLITZ_BLOCKS[HY_SPLIT]))]
        if HY_SPLIT == 2:
            ys = _toeplitz2(coef[0], z[0], z[1])
        else:
            p1 = _toeplitz2(coef[0], _cadd(z[0], z[2]), _cadd(z[1], z[3]))
            p2 = _toeplitz2(coef[1], z[2], z[3])
            p3 = _toeplitz2(coef[2], z[0], z[1])
            ys = (_cadd(p1[0], p2[0]), _cadd(p1[1], p2[1]), _cadd(p1[0], p3[0]), _cadd(p1[1], p3[1]))
        ys_all.append(jnp.concatenate([jnp.concatenate([r for r, _ in ys], axis=1),
                                       jnp.concatenate([i for _, i in ys], axis=1)], axis=0).astype(BF16))
    for span, y in zip(spans, ys_all):
        acc_ref[:, span] += jnp.dot(g_ref[...], y, preferred_element_type=F32)

    @pl.when(j == nf - 1)
    def _():
        _short_conv(u1_ref, w_ref, b_ref, 1, width, next_input)
        acc_ref[...] = jnp.zeros_like(acc_ref)

    @pl.when(j == 2 * nf - 1)
    def _():
        _short_conv(u2_ref, w_ref, b_ref, 2, width, result)


def _hyena(p5, conv_w, conv_b, f_tab, g_tab, spectra, col_block, width):
    batch, seq, _ = p5.shape
    blk = seq // HY_SPLIT
    nf = blk // DFT_TF
    n_spec = spectra.shape[1]
    assert HY_ORDER == 2 and n_spec == 6 * len(_TOEPLITZ_BLOCKS[HY_SPLIT]) and blk % SHIFT_ROWS == 0
    assert conv_w.shape[0] == HY_SHORT == 3 and width % HY_CHUNK == 0
    kern = functools.partial(_hyena_kernel, nf=nf, width=width, blk=blk)
    u_spec = lambda part: pl.BlockSpec((None, seq, width), lambda b, j: (b, 0, col_block + part))
    return pl.pallas_call(
        kern,
        grid=(batch, HY_ORDER * nf),
        in_specs=[u_spec(0), u_spec(1), u_spec(2),
                  pl.BlockSpec(conv_w.shape, lambda b, j: (0, 0)),
                  pl.BlockSpec(conv_b.shape, lambda b, j: (0, 0)),
                  pl.BlockSpec((2 * DFT_TF, blk), lambda b, j: (j % nf, 0)),
                  pl.BlockSpec((blk, 2 * DFT_TF), lambda b, j: (0, j % nf)),
                  pl.BlockSpec((None, n_spec, DFT_TF, width), lambda b, j: (j // nf, 0, j % nf, 0))],
        out_specs=pl.BlockSpec((None, seq, width), lambda b, j: (b, 0, 0), pipeline_mode=pl.Buffered(1)),
        out_shape=jax.ShapeDtypeStruct((batch, seq, width), BF16),
        scratch_shapes=[pltpu.VMEM((blk, HY_SPLIT * width), BF16), pltpu.VMEM((blk, HY_SPLIT * width), F32)],
        compiler_params=_cparams(("parallel", "arbitrary")),
        name="hyena_conv",
    )(p5, p5, p5, conv_w, conv_b, f_tab, g_tab, spectra)


def _merge_kernel(x_ref, g_ref, ya_ref, yb_ref, yc_ref, wga_ref, wgb_ref, wgc_ref, wa_ref, wb_ref, wc_ref,
                  wo_ref, o_ref, h_ref, m_ref, *, nj):
    s = pl.program_id(0)
    slot = s % 2

    @pl.when(s == 0)
    def _():
        m_ref[1] = jnp.zeros(m_ref.shape[1:], BF16)
        o_ref[...] = jnp.zeros_like(o_ref)

    @pl.when(s % nj == 0)
    def _():
        h_ref[...] = _rms(x_ref[...], g_ref[...]).astype(BF16)

    first = (s - 1) % nj == 0
    base = jnp.where(first, x_ref[...], o_ref[...])
    o_ref[...] = base + jnp.dot(m_ref[1 - slot], wo_ref[...], preferred_element_type=F32)

    h = h_ref[...]

    def branch(wg_ref, y_ref, w_ref):
        gate = jax.nn.sigmoid(jnp.dot(h, wg_ref[...], preferred_element_type=F32))
        return gate * jnp.dot(y_ref[...], w_ref[...], preferred_element_type=F32)

    merged = branch(wga_ref, ya_ref, wa_ref) + branch(wgb_ref, yb_ref, wb_ref) + branch(wgc_ref, yc_ref, wc_ref)
    m_ref[slot] = merged.astype(BF16)


def _merge(x, g, ya, yb, yc, w_gate, gate_col, w_a, w_b, w_c, w_o):
    t, d = x.shape
    nj = d // MERGE_TN
    g0 = gate_col // MERGE_TN
    n = (t // MERGE_TM) * nj
    cur = lambda s: jnp.minimum(s, n - 1)
    prev = lambda s: jnp.maximum(s - 1, 0)
    rows = lambda a: pl.BlockSpec((MERGE_TM, a.shape[1]), lambda s: (cur(s) // nj, 0))
    cols = lambda a, off: pl.BlockSpec((a.shape[0], MERGE_TN), lambda s: (0, cur(s) % nj + off))
    return pl.pallas_call(
        functools.partial(_merge_kernel, nj=nj),
        grid=(n + 1,),
        in_specs=[pl.BlockSpec((MERGE_TM, d), lambda s: (cur(s) // nj, 0)),
                  pl.BlockSpec((1, d), lambda s: (0, 0)), rows(ya), rows(yb), rows(yc),
                  cols(w_gate, g0), cols(w_gate, g0 + nj), cols(w_gate, g0 + 2 * nj),
                  cols(w_a, 0), cols(w_b, 0), cols(w_c, 0),
                  pl.BlockSpec((MERGE_TN, d), lambda s: (prev(s) % nj, 0))],
        out_specs=pl.BlockSpec((MERGE_TM, d), lambda s: (prev(s) // nj, 0)),
        out_shape=jax.ShapeDtypeStruct((t, d), F32),
        scratch_shapes=[pltpu.VMEM((MERGE_TM, d), BF16), pltpu.VMEM((2, MERGE_TM, MERGE_TN), BF16)],
        compiler_params=_cparams(("arbitrary",)),
        name="merge",
    )(x, g, ya, yb, yc, w_gate, w_gate, w_gate, w_a, w_b, w_c, w_o)


def _rope_tables(seq):
    half = ROT_DIM // 2
    inv = jnp.power(ROPE_THETA, -jnp.arange(0, ROT_DIM, 2, dtype=F32) / ROT_DIM)
    ang = jnp.arange(seq, dtype=F32)[:, None] * inv[None, :]
    cos, sin = jnp.cos(ang), jnp.sin(ang)
    ones = jnp.ones((seq, HEAD_DIM - ROT_DIM), F32)
    cos_t = jnp.concatenate([cos, cos, ones], axis=1)
    sin_t = jnp.concatenate([-sin, sin, 0.0 * ones], axis=1)
    assert cos_t.shape == (seq, HEAD_DIM) and half * 2 == ROT_DIM
    return cos_t, sin_t


def _hyena_positional_features(seq, emb):
    bands = (emb - 1) // 2
    t = jnp.linspace(0.0, 1.0, seq, dtype=F32)[:, None]
    w = 2.0 * math.pi * jnp.arange(seq, dtype=F32)[:, None] / seq
    f = jnp.linspace(1e-4, bands - 1, bands, dtype=F32)[None, :]
    return jnp.concatenate([t, jnp.cos(f * w), -jnp.sin(f * w)], axis=-1)


def _pad_to(a, shape):
    return jnp.pad(a, [(0, s - d) for s, d in zip(shape, a.shape)])


def kernel(x, mem, g_ff1, w_ff1_in, w_ff1_out, g_mix, w_in, a_gq, a_gk, hy_conv_w, hy_conv_b, hy_f_w1, hy_f_b1,
           hy_f_w2, hy_f_b2, hy_f_w3, hy_f_b3, hy_f_w4, hy_f_freq, hy_bias, g_mem, w_mem_kv, m_gq, m_gk,
           w_br_a, w_br_b, w_br_c, w_out, g_ff2, w_ff2_in, w_ff2_out, g_post):
    batch, seq, d = x.shape
    depth = g_ff1.shape[0]
    dswa_width = len(DSWA_GROUPS) * DSWA_HEADS_PER_GROUP * HEAD_DIM
    hy_width = hy_bias.shape[-1]
    mem_width = MEM_HEADS * HEAD_DIM
    n_cols = 3 * dswa_width + (HY_ORDER + 1) * hy_width + mem_width
    n_rope = 2 * dswa_width // PROJ_TN
    n_plain = (n_cols - mem_width) // PROJ_TN - n_rope
    assert 2 * dswa_width % PROJ_TN == 0 and mem_width == PROJ_TN and n_cols % PROJ_TN == 0
    assert 3 * dswa_width % hy_width == 0 and n_cols % MERGE_TN == 0 and seq & (seq - 1) == 0
    scale = 1.0 / math.sqrt(HEAD_DIM)

    cos_t, sin_t = _rope_tables(seq)
    f_tab, g_tab = _dft_tables(seq // HY_SPLIT)
    emb, hidden = hy_f_w1.shape[1:]
    z_feat = _pad_to(_hyena_positional_features(seq, emb), (seq, LANES))
    t_col = jnp.linspace(0.0, 1.0, seq, dtype=F32)[:, None]
    deltas = jnp.linspace(math.log(HY_TARGET) / HY_SLOW_DECAY, math.log(HY_TARGET) / HY_FAST_DECAY, hy_width, dtype=F32)
    absd = jnp.abs(deltas)[None, :]
    row = lambda v: v.reshape(1, -1)

    xt = x.reshape(batch * seq, d)
    for l in range(depth):
        w123 = jnp.stack([_pad_to(w.astype(F32), (LANES, LANES)) for w in (hy_f_w1[l], hy_f_w2[l], hy_f_w3[l])])
        vecs = _pad_to(jnp.stack([hy_f_b1[l], hy_f_b2[l], hy_f_b3[l], hy_f_freq[l]]).astype(F32), (SUBLANES, LANES))
        hfilt, (w_ff1_out_b,) = _filter_mlp(z_feat, w123, vecs, hy_f_w4[l], t_col, absd, casts=(w_ff1_out[l],))
        spectra, (w_ff1_in_b,) = _filter_dft(
            hfilt, hy_bias[l].reshape(HY_ORDER, 1, hy_width), f_tab, hy_width, casts=(w_ff1_in[l],))

        xt, (w_ff2_in_b, w_ff2_out_b, w_in_b) = _ffn(
            xt, row(g_ff1[l]), w_ff1_in_b, w_ff1_out_b, row(g_post[l]), False,
            casts=(w_ff2_in[l], w_ff2_out[l], w_in[l]))

        heads = dswa_width // HEAD_DIM
        gain_cols = jnp.concatenate([jnp.tile(a_gq[l], heads) * scale, jnp.tile(a_gk[l], heads),
                                     jnp.ones((n_cols - 2 * dswa_width - mem_width,), F32),
                                     jnp.tile(m_gq[l], MEM_HEADS) * scale])[None, :]
        p5 = _mixproj(xt, row(g_mix[l]), w_in_b, gain_cols, cos_t, sin_t, n_cols, n_rope, n_plain)
        p5 = p5.reshape(batch, seq, n_cols)

        y_a, (w_out_b, w_br_a_b, w_br_b_b, w_br_c_b, w_mem_kv_b) = _attn(
            p5, batch, seq, casts=(w_out[l], w_br_a[l], w_br_b[l], w_br_c[l], w_mem_kv[l]))

        y_b = _hyena(p5, hy_conv_w[l], row(hy_conv_b[l]), f_tab, g_tab, spectra, 3 * dswa_width // hy_width, hy_width)

        y_c = _memattn(p5, mem, row(g_mem[l]), w_mem_kv_b, row(m_gk[l]), (n_cols - mem_width) // mem_width)

        t = batch * seq
        xt = _merge(xt, row(g_mix[l]), y_a.reshape(t, -1), y_b.reshape(t, -1), y_c.reshape(t, -1),
                    w_in_b, n_cols, w_br_a_b, w_br_b_b, w_br_c_b, w_out_b)

        xt, _ = _ffn(xt, row(g_ff2[l]), w_ff2_in_b, w_ff2_out_b, row(g_post[l]), True)
    return xt.reshape(batch, seq, d)
```

```python
import functools
import math

import jax
import jax.numpy as jnp
from jax import lax
from jax.experimental import pallas as pl
from jax.experimental.pallas import tpu as pltpu

F32 = jnp.float32
BF16 = jnp.bfloat16

HEAD_DIM = 128
ROPE_THETA = 500000.0
ROT_DIM = HEAD_DIM // 4
EPS = 1e-6
NEG = -1e30
DSWA_GROUPS = ((128, 1), (512, 4), (2048, 16))
DSWA_HEADS_PER_GROUP = 2
MEM_HEADS = 4
HY_ORDER = 2
HY_SHORT = 3
HY_FAST_DECAY = 0.3
HY_SLOW_DECAY = 1.5
HY_TARGET = 1e-2

LANES = 128
SUBLANES = 8
BF16_ROWS = 16
VMEM_LIMIT_BYTES = 60 * 1024 * 1024

FFN_TM = 1024
FFN_TF = 512
PROJ_TM = 512
PROJ_TN = 512
ATT_TQ = 128
ATT_BACK = 128
MEM_TQ = 1024
DFT_TF = 128
HY_SPLIT = 4
HY_CHUNK = 256
FILT_CAST_STEPS = 16
SHIFT_ROWS = 256
FILT_MLP_TL = 256
MERGE_TM = 1024
MERGE_TN = 256


def _cparams(sem):
    return pltpu.CompilerParams(dimension_semantics=sem, vmem_limit_bytes=VMEM_LIMIT_BYTES)


def _rms(x, g):
    return x * lax.rsqrt(jnp.mean(x * x, axis=-1, keepdims=True) + EPS) * g


def _ffn_kernel(x_ref, g_ref, wa_ref, wb_ref, wd_ref, gp_ref, *rest, nj, final_norm, n_cast):
    cast_in, o_ref, cast_out, h_ref = rest[:n_cast], rest[n_cast], rest[n_cast + 1:2 * n_cast + 1], rest[-1]
    j = pl.program_id(1)

    for src, dst in zip(cast_in, cast_out):
        dst[...] = src[...].astype(BF16)

    @pl.when(j == 0)
    def _():
        x = x_ref[...]
        h_ref[...] = _rms(x, g_ref[...]).astype(BF16)
        o_ref[...] = x

    h = h_ref[...]
    a = jnp.dot(h, wa_ref[...], preferred_element_type=F32)
    b = jnp.dot(h, wb_ref[...], preferred_element_type=F32)
    act = (0.5 * a * jax.nn.sigmoid(a) * b).astype(BF16)
    o_ref[...] += jnp.dot(act, wd_ref[...], preferred_element_type=F32)

    if final_norm:
        @pl.when(j == nj - 1)
        def _():
            o_ref[...] = _rms(o_ref[...], gp_ref[...])


def _cast_specs(casts, steps, flat):
    specs = []
    for a in casts:
        rows = BF16_ROWS
        while a.shape[0] % rows or a.shape[0] // rows > steps:
            rows += BF16_ROWS
        n = a.shape[0] // rows
        specs.append(pl.BlockSpec((rows, a.shape[1]), functools.partial(
            lambda *idx, n: (flat(*idx) * n // steps, 0), n=n)))
    return specs


def _ffn(x, g, w_in, w_out, g_post, final_norm, casts=()):
    t, d = x.shape
    d_ff = w_out.shape[0]
    nj = d_ff // FFN_TF
    cast_specs = _cast_specs(casts, (t // FFN_TM) * nj, lambda i, j: i * nj + j)
    kern = functools.partial(_ffn_kernel, nj=nj, final_norm=final_norm, n_cast=len(casts))
    out = pl.pallas_call(
        kern,
        grid=(t // FFN_TM, nj),
        in_specs=[
            pl.BlockSpec((FFN_TM, d), lambda i, j: (i, 0)),
            pl.BlockSpec((1, d), lambda i, j: (0, 0)),
            pl.BlockSpec((d, FFN_TF), lambda i, j: (0, j)),
            pl.BlockSpec((d, FFN_TF), lambda i, j: (0, j + nj)),
            pl.BlockSpec((FFN_TF, d), lambda i, j: (j, 0)),
            pl.BlockSpec((1, d), lambda i, j: (0, 0)),
        ] + cast_specs,
        out_specs=[pl.BlockSpec((FFN_TM, d), lambda i, j: (i, 0))] + cast_specs,
        out_shape=[jax.ShapeDtypeStruct((t, d), F32)] + [jax.ShapeDtypeStruct(a.shape, BF16) for a in casts],
        scratch_shapes=[pltpu.VMEM((FFN_TM, d), BF16)],
        compiler_params=_cparams(("arbitrary", "arbitrary")),
        name="ffn_final" if final_norm else "ffn",
    )(x, g, w_in, w_in, w_out, g_post, *casts)
    return out[0], tuple(out[1:])


def _mixproj_kernel(x0_ref, xn_ref, g_ref, w_ref, gain_ref, cos_ref, sin_ref, o_ref, h_ref, *, n_rope, n_plain):
    i = pl.program_id(0)
    slot = i % 2

    @pl.when(i == 0)
    def _():
        h_ref[0] = _rms(x0_ref[...], g_ref[...]).astype(BF16)

    h_ref[1 - slot] = _rms(xn_ref[...], g_ref[...]).astype(BF16)
    h = h_ref[slot]
    lane = lax.broadcasted_iota(jnp.int32, (PROJ_TM, HEAD_DIM), 1)
    first = lane < (ROT_DIM // 2)
    n_tiles = w_ref.shape[1] // PROJ_TN
    order = [j for j in range(n_tiles) if not n_rope <= j < n_rope + n_plain] + list(range(n_rope, n_rope + n_plain))
    for j in order:
        acc = jnp.dot(h, w_ref[:, j * PROJ_TN:(j + 1) * PROJ_TN], preferred_element_type=F32)
        plain = n_rope <= j < n_rope + n_plain
        for s in range(PROJ_TN // HEAD_DIM):
            sl = slice(j * PROJ_TN + s * HEAD_DIM, j * PROJ_TN + (s + 1) * HEAD_DIM)
            t = acc[:, s * HEAD_DIM:(s + 1) * HEAD_DIM]
            if not plain:
                t = _rms(t, gain_ref[:, sl])
            if j < n_rope:
                partner = jnp.where(first, pltpu.roll(t, HEAD_DIM - ROT_DIM // 2, axis=1),
                                    pltpu.roll(t, ROT_DIM // 2, axis=1))
                t = t * cos_ref[...] + partner * sin_ref[...]
            o_ref[:, sl] = t.astype(BF16)


def _mixproj(x, g, w, gain_cols, cos_t, sin_t, n_cols, n_rope, n_plain):
    t, d = x.shape
    n_row = t // PROJ_TM
    per_seq = cos_t.shape[0] // PROJ_TM
    kern = functools.partial(_mixproj_kernel, n_rope=n_rope, n_plain=n_plain)
    once = pl.Buffered(1)
    return pl.pallas_call(
        kern,
        grid=(n_row,),
        in_specs=[
            pl.BlockSpec((PROJ_TM, d), lambda i: (0, 0), pipeline_mode=once),
            pl.BlockSpec((PROJ_TM, d), lambda i: (jnp.minimum(i + 1, n_row - 1), 0)),
            pl.BlockSpec((1, d), lambda i: (0, 0)),
            pl.BlockSpec((d, n_cols), lambda i: (0, 0), pipeline_mode=once),
            pl.BlockSpec((1, n_cols), lambda i: (0, 0)),
            pl.BlockSpec((PROJ_TM, HEAD_DIM), lambda i: (i % per_seq, 0)),
            pl.BlockSpec((PROJ_TM, HEAD_DIM), lambda i: (i % per_seq, 0)),
        ],
        out_specs=pl.BlockSpec((PROJ_TM, n_cols), lambda i: (i, 0)),
        out_shape=jax.ShapeDtypeStruct((t, n_cols), BF16),
        scratch_shapes=[pltpu.VMEM((2, PROJ_TM, d), BF16)],
        compiler_params=_cparams(("arbitrary",)),
        name="mixproj",
    )(x, x, g, w, gain_cols, cos_t, sin_t)


def _banded_tiles(q, k, v, length):
    tiles = []
    for qb in range(length // ATT_TQ):
        i0 = qb * ATT_TQ
        lo = max(0, i0 - ATT_BACK)
        hi = min(length, i0 + ATT_TQ + ATT_BACK)
        tiles.append((i0, q[i0:i0 + ATT_TQ], k[lo:hi], v[lo:hi], lo - (i0 - ATT_BACK)))
    return tiles


def _banded_attention(tiles, tab_ref):
    scores = [lax.dot_general(q, k, (((1,), (1,)), ((), ())), preferred_element_type=F32)
              + tab_ref[:, c0:c0 + k.shape[0]] for _, q, k, _, c0 in tiles]
    maxes = [s.max(axis=-1, keepdims=True) for s in scores]
    probs = [jnp.exp(s - m) for s, m in zip(scores, maxes)]
    dens = [p.sum(axis=-1, keepdims=True) for p in probs]
    outs = [jnp.dot(p.astype(BF16), t[3], preferred_element_type=F32) / d for p, t, d in zip(probs, tiles, dens)]
    return [(o, m + jnp.log(d)) for o, m, d in zip(outs, maxes, dens)]


def _attn_kernel(*refs, seq, n_cast):
    qkv, tab_ref = refs[:9], refs[9]
    cast_in, o_ref, cast_out = refs[10:10 + n_cast], refs[10 + n_cast], refs[11 + n_cast:11 + 2 * n_cast]
    qf, kf, vf = refs[-7:-4]
    outs, lses = refs[-4:-2], refs[-2:]

    for src, dst in zip(cast_in, cast_out):
        dst[...] = src[...].astype(BF16)

    dilated = [(g, dil) for g, (_, dil) in enumerate(DSWA_GROUPS) if dil > 1]
    plain = [g for g, (_, dil) in enumerate(DSWA_GROUPS) if dil == 1]
    assert len(dilated) == len(outs) and len(plain) == 1

    for slot, (g, dil) in enumerate(dilated):
        length = seq // dil
        for src, dst in zip(qkv[3 * g:3 * g + 3], (qf, kf, vf)):
            dst[...] = src[...].astype(F32)
        tiles, rows = [], []
        for r in range(dil):
            q, k, v = (ref[pl.ds(r, length, stride=dil), :].astype(BF16) for ref in (qf, kf, vf))
            sub = _banded_tiles(q, k, v, length)
            tiles += sub
            rows += [pl.ds(t[0] * dil + r, ATT_TQ, stride=dil) for t in sub]
        for rw, (o, lse) in zip(rows, _banded_attention(tiles, tab_ref)):
            outs[slot][rw, :] = o
            lses[slot][rw, :] = jnp.broadcast_to(lse, (ATT_TQ, HEAD_DIM))

    g = plain[0]
    tiles = _banded_tiles(qkv[3 * g][...], qkv[3 * g + 1][...], qkv[3 * g + 2][...], seq)
    for (i0, *_), (o, lse) in zip(tiles, _banded_attention(tiles, tab_ref)):
        rw = slice(i0, i0 + ATT_TQ)
        parts = [(o, jnp.broadcast_to(lse, (ATT_TQ, HEAD_DIM)))] + [(a[rw, :], b[rw, :]) for a, b in zip(outs, lses)]
        m = functools.reduce(jnp.maximum, [l for _, l in parts])
        num = jnp.zeros((ATT_TQ, HEAD_DIM), F32)
        den = jnp.zeros((ATT_TQ, HEAD_DIM), F32)
        for a, l in parts:
            w = jnp.exp(l - m)
            num = num + w * a
            den = den + w
        o_ref[rw, :] = (num / den).astype(BF16)


def _attn_table():
    width = 2 * ATT_BACK + ATT_TQ
    row = jnp.arange(ATT_TQ, dtype=jnp.int32)[:, None]
    col = jnp.arange(width, dtype=jnp.int32)[None, :]
    half = DSWA_GROUPS[0][0] // (2 * DSWA_GROUPS[0][1])
    return jnp.where(jnp.abs(col - ATT_BACK - row) <= half, 0.0, NEG).astype(F32)


def _attn(p5, batch, seq, col_q, col_k, col_v, casts=()):
    halves = {win // (2 * dil) for win, dil in DSWA_GROUPS}
    assert len(halves) == 1 and halves.pop() <= ATT_BACK
    assert all(seq % dil == 0 and (seq // dil) % ATT_TQ == 0 for _, dil in DSWA_GROUPS)
    tab = _attn_table()
    hpg = DSWA_HEADS_PER_GROUP
    in_specs = []
    for g in range(len(DSWA_GROUPS)):
        for base in (col_q, col_k, col_v):
            in_specs.append(pl.BlockSpec((None, seq, HEAD_DIM),
                                         functools.partial(lambda b, h, c: (b, 0, c + h), c=base + g * hpg)))
    in_specs.append(pl.BlockSpec(tab.shape, lambda b, h: (0, 0)))
    cast_specs = _cast_specs(casts, batch * hpg, lambda b, h: b * hpg + h)
    out = pl.pallas_call(
        functools.partial(_attn_kernel, seq=seq, n_cast=len(casts)),
        grid=(batch, hpg),
        in_specs=in_specs + cast_specs,
        out_specs=[pl.BlockSpec((None, seq, HEAD_DIM), lambda b, h: (b, 0, h))] + cast_specs,
        out_shape=[jax.ShapeDtypeStruct((batch, seq, hpg * HEAD_DIM), BF16)]
                  + [jax.ShapeDtypeStruct(a.shape, BF16) for a in casts],
        scratch_shapes=[pltpu.VMEM((seq, HEAD_DIM), F32)] * 7,
        compiler_params=_cparams(("arbitrary", "arbitrary")),
        name="attn",
    )(*([p5] * 9), tab, *casts)
    return out[0], tuple(out[1:])


def _memattn_kernel(q_ref, mem_ref, g_ref, wkv_ref, gk_ref, o_ref, k_ref, v_ref):
    width = MEM_HEADS * HEAD_DIM

    @pl.when(pl.program_id(1) == 0)
    def _():
        mn = _rms(mem_ref[...], g_ref[...]).astype(BF16)
        kv = jnp.dot(mn, wkv_ref[...], preferred_element_type=F32)
        for h in range(MEM_HEADS):
            sl = slice(h * HEAD_DIM, (h + 1) * HEAD_DIM)
            k_ref[:, sl] = _rms(kv[:, sl], gk_ref[...]).astype(BF16)
        v_ref[...] = kv[:, width:].astype(BF16)

    for h in range(MEM_HEADS):
        sl = slice(h * HEAD_DIM, (h + 1) * HEAD_DIM)
        s = lax.dot_general(q_ref[:, sl], k_ref[:, sl], (((1,), (1,)), ((), ())), preferred_element_type=F32)
        p = jnp.exp(s - s.max(axis=-1, keepdims=True))
        den = p.sum(axis=-1, keepdims=True)
        o = jnp.dot(p.astype(BF16), v_ref[:, sl], preferred_element_type=F32)
        o_ref[:, sl] = (o / den).astype(BF16)


def _memattn(p5, mem, g_mem, w_kv, gk, col_block):
    batch, seq, _ = p5.shape
    n_mem, d = mem.shape[1:]
    width = MEM_HEADS * HEAD_DIM
    return pl.pallas_call(
        _memattn_kernel,
        grid=(batch, seq // MEM_TQ),
        in_specs=[
            pl.BlockSpec((None, MEM_TQ, width), lambda b, i: (b, i, col_block)),
            pl.BlockSpec((None, n_mem, d), lambda b, i: (b, 0, 0)),
            pl.BlockSpec((1, d), lambda b, i: (0, 0)),
            pl.BlockSpec((d, 2 * width), lambda b, i: (0, 0)),
            pl.BlockSpec((1, HEAD_DIM), lambda b, i: (0, 0)),
        ],
        out_specs=pl.BlockSpec((None, MEM_TQ, width), lambda b, i: (b, i, 0)),
        out_shape=jax.ShapeDtypeStruct((batch, seq, width), BF16),
        scratch_shapes=[pltpu.VMEM((n_mem, width), BF16), pltpu.VMEM((n_mem, width), BF16)],
        compiler_params=_cparams(("parallel", "arbitrary")),
        name="memattn",
    )(p5, mem, g_mem, w_kv, gk)


def _dft_kernel(f_ref, g_ref, cb, sb, cbt, sbt, cac, sac, *, seq):
    t = pl.program_id(0)
    n2 = 4 * seq
    theta = 2.0 * math.pi / n2

    def trig(m):
        ang = (m & (n2 - 1)).astype(F32) * theta
        return jnp.cos(ang), jnp.sin(ang)

    @pl.when(t == 0)
    def _():
        f_lo = lax.broadcasted_iota(jnp.int32, (DFT_TF, seq), 0)
        s = lax.broadcasted_iota(jnp.int32, (DFT_TF, seq), 1)
        cb[...], sb[...] = trig((2 * f_lo + 1) * s)
        s = lax.broadcasted_iota(jnp.int32, (seq, DFT_TF), 0)
        f_lo = lax.broadcasted_iota(jnp.int32, (seq, DFT_TF), 1)
        cbt[...], sbt[...] = trig((2 * f_lo + 1) * s)
        s = lax.broadcasted_iota(jnp.int32, (seq, LANES), 0)
        tile = lax.broadcasted_iota(jnp.int32, (seq, LANES), 1)
        cac[...], sac[...] = trig(2 * DFT_TF * tile * s)

    s = lax.broadcasted_iota(jnp.int32, (1, seq), 1)
    ca, sa = trig(2 * DFT_TF * t * s)
    f_ref[:DFT_TF, :] = (ca * cb[...] - sa * sb[...]).astype(BF16)
    f_ref[DFT_TF:, :] = (-(sa * cb[...] + ca * sb[...])).astype(BF16)
    pick = lax.broadcasted_iota(jnp.int32, (seq, LANES), 1) == t
    ca = jnp.sum(jnp.where(pick, cac[...], 0.0), axis=-1, keepdims=True)
    sa = jnp.sum(jnp.where(pick, sac[...], 0.0), axis=-1, keepdims=True)
    g_ref[:, :DFT_TF] = (ca * cbt[...] - sa * sbt[...]).astype(BF16)
    g_ref[:, DFT_TF:] = (-(sa * cbt[...] + ca * sbt[...])).astype(BF16)


def _dft_tables(seq):
    nf = seq // DFT_TF
    assert nf <= LANES
    return pl.pallas_call(
        functools.partial(_dft_kernel, seq=seq),
        grid=(nf,),
        out_specs=[pl.BlockSpec((2 * DFT_TF, seq), lambda t: (t, 0)),
                   pl.BlockSpec((seq, 2 * DFT_TF), lambda t: (0, t))],
        out_shape=[jax.ShapeDtypeStruct((2 * seq, seq), BF16), jax.ShapeDtypeStruct((seq, 2 * seq), BF16)],
        scratch_shapes=[pltpu.VMEM((DFT_TF, seq), F32)] * 2 + [pltpu.VMEM((seq, DFT_TF), F32)] * 2
                       + [pltpu.VMEM((seq, LANES), F32)] * 2,
        compiler_params=_cparams(("arbitrary",)),
        name="dft_tables",
    )()


def _filter_mlp_kernel(z_ref, w_ref, v_ref, w4_ref, t_ref, d_ref, *rest, n_cast):
    cast_in, o_ref, cast_out = rest[:n_cast], rest[n_cast], rest[n_cast + 1:-2]
    w_hi, w_lo = rest[-2:]
    for src, dst in zip(cast_in, cast_out):
        dst[...] = src[...].astype(BF16)

    split = lambda a: (a.astype(BF16), (a - a.astype(BF16).astype(F32)).astype(BF16))

    @pl.when(pl.program_id(0) == 0)
    def _():
        hidden = w4_ref.shape[0]
        for ref, part in zip((w_hi, w_lo), split(w4_ref[...])):
            ref[:hidden, :] = part
            ref[hidden:, :] = jnp.zeros((ref.shape[0] - hidden, ref.shape[1]), BF16)

    hp = lax.Precision.HIGHEST
    fr = v_ref[3:4, :]
    hh = z_ref[...]
    for n in range(3):
        hh = jnp.sin(fr * (jnp.dot(hh, w_ref[n], precision=hp, preferred_element_type=F32) + v_ref[n:n + 1, :]))
    a_hi, a_lo = split(hh)
    h = (jnp.dot(a_hi, w_hi[...], preferred_element_type=F32) + jnp.dot(a_hi, w_lo[...], preferred_element_type=F32)
         + jnp.dot(a_lo, w_hi[...], preferred_element_type=F32))
    decay = jnp.exp(-t_ref[...] * d_ref[...])
    width = decay.shape[1]
    for c in range(h.shape[1] // width):
        o_ref[:, c * width:(c + 1) * width] = h[:, c * width:(c + 1) * width] * decay


def _filter_mlp(z, w123, vecs, w4, t_col, absd, casts=()):
    seq = z.shape[0]
    tl = FILT_MLP_TL
    n_out = w4.shape[1]
    assert w4.shape[0] % BF16_ROWS == 0 and w4.shape[0] <= LANES
    full = lambda a: pl.BlockSpec(a.shape, lambda i: (0,) * a.ndim)
    cast_specs = _cast_specs(casts, seq // tl, lambda i: i)
    out = pl.pallas_call(
        functools.partial(_filter_mlp_kernel, n_cast=len(casts)),
        grid=(seq // tl,),
        in_specs=[pl.BlockSpec((tl, z.shape[1]), lambda i: (i, 0)), full(w123), full(vecs), full(w4),
                  pl.BlockSpec((tl, 1), lambda i: (i, 0)), full(absd)] + cast_specs,
        out_specs=[pl.BlockSpec((tl, n_out), lambda i: (i, 0))] + cast_specs,
        out_shape=[jax.ShapeDtypeStruct((seq, n_out), F32)] + [jax.ShapeDtypeStruct(a.shape, BF16) for a in casts],
        scratch_shapes=[pltpu.VMEM((LANES, n_out), BF16)] * 2,
        compiler_params=_cparams(("arbitrary",)),
        name="hyena_filter_mlp",
    )(z, w123, vecs, w4, t_col, absd, *casts)
    return out[0], tuple(out[1:])


_TOEPLITZ_BLOCKS = {2: ((0, -1, 1),), 4: ((0, -1, 1), (-2, -3, -1), (2, 1, 3))}


def _csub(x, y):
    return x[0] - y[0], x[1] - y[1]


def _cadd(x, y):
    return x[0] + y[0], x[1] + y[1]


def _cmul(k, x):
    return k[0] * x[0] - k[1] * x[1], k[0] * x[1] + k[1] * x[0]


def _toeplitz2(coef, u, v):
    p = _cmul(coef[0], _cadd(u, v))
    return _cadd(p, _cmul(coef[1], v)), _cadd(p, _cmul(coef[2], u))


def _filter_dft_kernel(hf_ref, hb_ref, bias_ref, f_ref, *rest, blk, n_cast):
    cast_in, k_ref, cast_out, r_ref = rest[:n_cast], rest[n_cast], rest[n_cast + 1:2 * n_cast + 1], rest[-1]
    width = hf_ref.shape[1]

    for src, dst in zip(cast_in, cast_out):
        dst[...] = src[...].astype(BF16)

    @pl.when((pl.program_id(1) == 0) & (pl.program_id(2) == 0))
    def _():
        for c, ref in enumerate((hf_ref, hb_ref)):
            for part in range(HY_SPLIT):
                col = (HY_SPLIT * c + part) * width
                r_ref[:, col:col + width] = ref[part * blk:(part + 1) * blk, :].astype(BF16)

    @pl.when(pl.program_id(2) == 0)
    def _():
        r = r_ref[...]
        re = jnp.dot(f_ref[:DFT_TF, :], r, preferred_element_type=F32)
        im = jnp.dot(f_ref[DFT_TF:, :], r, preferred_element_type=F32)

        def transform(c, j):
            col = (HY_SPLIT * c + j) * width
            return re[:, col:col + width], im[:, col:col + width]

        row = lax.broadcasted_iota(jnp.int32, (DFT_TF, width), 0)
        sgn = jnp.where((row & 1) == 0, 1.0, -1.0)
        scale = 1.0 / blk
        (f0r, f0i), (b0r, b0i) = transform(0, 0), transform(1, 0)
        seg = {0: ((f0r + b0r + bias_ref[...]) * scale, (f0i - b0i) * scale)}
        for d in range(1, HY_SPLIT):
            for c, ref, conj in ((0, hf_ref, 1.0), (1, hb_ref, -1.0)):
                (ar, ai), (pr, pi) = transform(c, d), transform(c, d - 1)
                edge = ref[(d - 1) * blk:(d - 1) * blk + 1, :]
                seg[d if c == 0 else -d] = ((ar - sgn * pi) * scale, (ai + sgn * (pr - edge)) * (conj * scale))

        blocks = _TOEPLITZ_BLOCKS[HY_SPLIT]
        first = [seg[d] for d in blocks[0]]
        n = 0
        for g, offsets in enumerate(blocks):
            a, b, c = [seg[d] for d in offsets] if g == 0 else [_csub(seg[d], x) for d, x in zip(offsets, first)]
            for cr, ci in (a, _csub(b, a), _csub(c, a)):
                k_ref[n] = cr
                k_ref[n + 1] = ci
                n += 2


def _filter_dft(hfilt, bias, f_tab, width, casts=()):
    seq = hfilt.shape[0]
    blk = seq // HY_SPLIT
    nf = blk // DFT_TF
    n_spec = 2 * 3 * len(_TOEPLITZ_BLOCKS[HY_SPLIT])
    assert DFT_TF % 2 == 0
    pace = FILT_CAST_STEPS // (HY_ORDER * nf)
    cast_specs = _cast_specs(casts, HY_ORDER * nf * pace, lambda o, f, c: (o * nf + f) * pace + c)
    kern = functools.partial(_filter_dft_kernel, blk=blk, n_cast=len(casts))
    once = pl.Buffered(1)
    out = pl.pallas_call(
        kern,
        grid=(HY_ORDER, nf, pace),
        in_specs=[pl.BlockSpec((seq, width), lambda o, f, c: (0, 2 * o), pipeline_mode=once),
                  pl.BlockSpec((seq, width), lambda o, f, c: (0, 2 * o + 1), pipeline_mode=once),
                  pl.BlockSpec((None, 1, width), lambda o, f, c: (o, 0, 0)),
                  pl.BlockSpec((2 * DFT_TF, blk), lambda o, f, c: (f, 0))] + cast_specs,
        out_specs=[pl.BlockSpec((None, n_spec, DFT_TF, width), lambda o, f, c: (o, 0, f, 0))] + cast_specs,
        out_shape=[jax.ShapeDtypeStruct((HY_ORDER, n_spec, blk, width), F32)]
                  + [jax.ShapeDtypeStruct(a.shape, BF16) for a in casts],
        scratch_shapes=[pltpu.VMEM((blk, 2 * HY_SPLIT * width), BF16)],
        compiler_params=_cparams(("arbitrary", "arbitrary", "arbitrary")),
        name="hyena_filter_dft",
    )(hfilt, hfilt, bias, f_tab, *casts)
    return out[0], tuple(out[1:])


def _short_conv(u_ref, w_ref, b_ref, part, width, emit):
    seq = u_ref.shape[0]
    sl = slice(part * width, (part + 1) * width)
    w0, w1, w2, b = w_ref[0:1, sl], w_ref[1:2, sl], w_ref[2:3, sl], b_ref[:, sl]
    n = SHIFT_ROWS
    r = lax.broadcasted_iota(jnp.int32, (n, n), 0)
    c = lax.broadcasted_iota(jnp.int32, (n, n), 1)
    down = (c == r - 1).astype(BF16)
    up = (c == r + 1).astype(BF16)
    sub = SUBLANES
    row = lax.broadcasted_iota(jnp.int32, (sub, width), 0)
    for k in range(seq // n):
        ub = u_ref[k * n:(k + 1) * n, :]
        prev = jnp.dot(down, ub, preferred_element_type=F32)
        nxt = jnp.dot(up, ub, preferred_element_type=F32)
        if k > 0:
            edge = u_ref[k * n - BF16_ROWS:k * n, :].astype(F32)[BF16_ROWS - 1:BF16_ROWS]
            prev = jnp.concatenate([prev[:sub] + jnp.where(row == 0, edge, 0.0), prev[sub:]], axis=0)
        if k < seq // n - 1:
            edge = u_ref[(k + 1) * n:(k + 1) * n + BF16_ROWS, :].astype(F32)[0:1]
            nxt = jnp.concatenate([nxt[:-sub], nxt[-sub:] + jnp.where(row == sub - 1, edge, 0.0)], axis=0)
        emit(k * n, prev * w0 + ub.astype(F32) * w1 + nxt * w2 + b)


def _hyena_kernel(u0_ref, u1_ref, u2_ref, w_ref, b_ref, f_ref, g_ref, k_ref, o_ref, z_ref, acc_ref,
                  *, nf, width, blk):
    j = pl.program_id(1)
    cw = HY_CHUNK
    gw = HY_SPLIT * cw
    groups = width // cw

    def cols(g, i):
        return slice(g * gw + i * cw, g * gw + (i + 1) * cw)

    def place(row0, rows):
        return slice(row0 % blk, row0 % blk + rows), row0 // blk

    def first_input(row0, v):
        rows, i = place(row0, v.shape[0])
        for g in range(groups):
            z_ref[rows, cols(g, i)] = v[:, g * cw:(g + 1) * cw].astype(BF16)

    def next_input(row0, v):
        rows, i = place(row0, v.shape[0])
        for g in range(groups):
            z_ref[rows, cols(g, i)] = (v[:, g * cw:(g + 1) * cw] * acc_ref[rows, cols(g, i)]).astype(BF16)

    def result(row0, v):
        rows, i = place(row0, v.shape[0])
        for g in range(groups):
            o_ref[row0:row0 + v.shape[0], g * cw:(g + 1) * cw] = (
                v[:, g * cw:(g + 1) * cw] * acc_ref[rows, cols(g, i)]).astype(BF16)

    @pl.when(j == 0)
    def _():
        _short_conv(u0_ref, w_ref, b_ref, 0, width, first_input)
        acc_ref[...] = jnp.zeros_like(acc_ref)

    spans = [slice(g * gw, (g + 1) * gw) for g in range(groups)]
    uvs = [jnp.dot(f_ref[...], z_ref[:, span], preferred_element_type=F32) for span in spans]
    ys_all = []
    for g, uv in enumerate(uvs):
        ch = slice(g * cw, (g + 1) * cw)
        z = [(uv[:DFT_TF, i * cw:(i + 1) * cw], uv[DFT_TF:, i * cw:(i + 1) * cw]) for i in range(HY_SPLIT)]
        coef = [[(k_ref[2 * (3 * b + t), :, ch], k_ref[2 * (3 * b + t) + 1, :, ch]) for t in range(3)]
                for b in range(len(_TOEPLITZ_BLOCKS[HY_SPLIT]))]
        if HY_SPLIT == 2:
            ys = _toeplitz2(coef[0], z[0], z[1])
        else:
            p1 = _toeplitz2(coef[0], _cadd(z[0], z[2]), _cadd(z[1], z[3]))
            p2 = _toeplitz2(coef[1], z[2], z[3])
            p3 = _toeplitz2(coef[2], z[0], z[1])
            ys = (_cadd(p1[0], p2[0]), _cadd(p1[1], p2[1]), _cadd(p1[0], p3[0]), _cadd(p1[1], p3[1]))
        ys_all.append(jnp.concatenate([jnp.concatenate([r for r, _ in ys], axis=1),
                                       jnp.concatenate([i for _, i in ys], axis=1)], axis=0).astype(BF16))
    for span, y in zip(spans, ys_all):
        acc_ref[:, span] += jnp.dot(g_ref[...], y, preferred_element_type=F32)

    @pl.when(j == nf - 1)
    def _():
        _short_conv(u1_ref, w_ref, b_ref, 1, width, next_input)
        acc_ref[...] = jnp.zeros_like(acc_ref)

    @pl.when(j == 2 * nf - 1)
    def _():
        _short_conv(u2_ref, w_ref, b_ref, 2, width, result)


def _hyena(p5, conv_w, conv_b, f_tab, g_tab, spectra, col_block, width):
    batch, seq, _ = p5.shape
    blk = seq // HY_SPLIT
    nf = blk // DFT_TF
    n_spec = spectra.shape[1]
    assert HY_ORDER == 2 and n_spec == 6 * len(_TOEPLITZ_BLOCKS[HY_SPLIT]) and blk % SHIFT_ROWS == 0
    assert conv_w.shape[0] == HY_SHORT == 3 and width % HY_CHUNK == 0
    kern = functools.partial(_hyena_kernel, nf=nf, width=width, blk=blk)
    u_spec = lambda part: pl.BlockSpec((None, seq, width), lambda b, j: (b, 0, col_block + part))
    return pl.pallas_call(
        kern,
        grid=(batch, HY_ORDER * nf),
        in_specs=[u_spec(0), u_spec(1), u_spec(2),
                  pl.BlockSpec(conv_w.shape, lambda b, j: (0, 0)),
                  pl.BlockSpec(conv_b.shape, lambda b, j: (0, 0)),
                  pl.BlockSpec((2 * DFT_TF, blk), lambda b, j: (j % nf, 0)),
                  pl.BlockSpec((blk, 2 * DFT_TF), lambda b, j: (0, j % nf)),
                  pl.BlockSpec((None, n_spec, DFT_TF, width), lambda b, j: (j // nf, 0, j % nf, 0))],
        out_specs=pl.BlockSpec((None, seq, width), lambda b, j: (b, 0, 0)),
        out_shape=jax.ShapeDtypeStruct((batch, seq, width), BF16),
        scratch_shapes=[pltpu.VMEM((blk, HY_SPLIT * width), BF16), pltpu.VMEM((blk, HY_SPLIT * width), F32)],
        compiler_params=_cparams(("parallel", "arbitrary")),
        name="hyena_conv",
    )(p5, p5, p5, conv_w, conv_b, f_tab, g_tab, spectra)


def _merge_kernel(x_ref, g_ref, ya_ref, yb_ref, yc_ref, wga_ref, wgb_ref, wgc_ref, wa_ref, wb_ref, wc_ref,
                  wo_ref, o_ref, h_ref, m_ref, *, nj):
    s = pl.program_id(0)
    slot = s % 2

    @pl.when(s == 0)
    def _():
        m_ref[1] = jnp.zeros(m_ref.shape[1:], BF16)
        o_ref[...] = jnp.zeros_like(o_ref)

    @pl.when(s % nj == 0)
    def _():
        h_ref[...] = _rms(x_ref[...], g_ref[...]).astype(BF16)

    first = (s - 1) % nj == 0
    base = jnp.where(first, x_ref[...], o_ref[...])
    o_ref[...] = base + jnp.dot(m_ref[1 - slot], wo_ref[...], preferred_element_type=F32)

    h = h_ref[...]

    def branch(wg_ref, y_ref, w_ref):
        gate = jax.nn.sigmoid(jnp.dot(h, wg_ref[...], preferred_element_type=F32))
        return gate * jnp.dot(y_ref[...], w_ref[...], preferred_element_type=F32)

    merged = branch(wga_ref, ya_ref, wa_ref) + branch(wgb_ref, yb_ref, wb_ref) + branch(wgc_ref, yc_ref, wc_ref)
    m_ref[slot] = merged.astype(BF16)


def _merge(x, g, ya, yb, yc, w_gate, gate_col, w_a, w_b, w_c, w_o):
    t, d = x.shape
    nj = d // MERGE_TN
    g0 = gate_col // MERGE_TN
    n = (t // MERGE_TM) * nj
    cur = lambda s: jnp.minimum(s, n - 1)
    prev = lambda s: jnp.maximum(s - 1, 0)
    rows = lambda a: pl.BlockSpec((MERGE_TM, a.shape[1]), lambda s: (cur(s) // nj, 0))
    cols = lambda a, off: pl.BlockSpec((a.shape[0], MERGE_TN), lambda s: (0, cur(s) % nj + off))
    return pl.pallas_call(
        functools.partial(_merge_kernel, nj=nj),
        grid=(n + 1,),
        in_specs=[pl.BlockSpec((MERGE_TM, d), lambda s: (cur(s) // nj, 0)),
                  pl.BlockSpec((1, d), lambda s: (0, 0)), rows(ya), rows(yb), rows(yc),
                  cols(w_gate, g0), cols(w_gate, g0 + nj), cols(w_gate, g0 + 2 * nj),
                  cols(w_a, 0), cols(w_b, 0), cols(w_c, 0),
                  pl.BlockSpec((MERGE_TN, d), lambda s: (prev(s) % nj, 0))],
        out_specs=pl.BlockSpec((MERGE_TM, d), lambda s: (prev(s) // nj, 0)),
        out_shape=jax.ShapeDtypeStruct((t, d), F32),
        scratch_shapes=[pltpu.VMEM((MERGE_TM, d), BF16), pltpu.VMEM((2, MERGE_TM, MERGE_TN), BF16)],
        compiler_params=_cparams(("arbitrary",)),
        name="merge",
    )(x, g, ya, yb, yc, w_gate, w_gate, w_gate, w_a, w_b, w_c, w_o)


def _rope_tables(seq):
    half = ROT_DIM // 2
    inv = jnp.power(ROPE_THETA, -jnp.arange(0, ROT_DIM, 2, dtype=F32) / ROT_DIM)
    ang = jnp.arange(seq, dtype=F32)[:, None] * inv[None, :]
    cos, sin = jnp.cos(ang), jnp.sin(ang)
    ones = jnp.ones((seq, HEAD_DIM - ROT_DIM), F32)
    cos_t = jnp.concatenate([cos, cos, ones], axis=1)
    sin_t = jnp.concatenate([-sin, sin, 0.0 * ones], axis=1)
    assert cos_t.shape == (seq, HEAD_DIM) and half * 2 == ROT_DIM
    return cos_t, sin_t


def _hyena_positional_features(seq, emb):
    bands = (emb - 1) // 2
    t = jnp.linspace(0.0, 1.0, seq, dtype=F32)[:, None]
    w = 2.0 * math.pi * jnp.arange(seq, dtype=F32)[:, None] / seq
    f = jnp.linspace(1e-4, bands - 1, bands, dtype=F32)[None, :]
    return jnp.concatenate([t, jnp.cos(f * w), -jnp.sin(f * w)], axis=-1)


def _pad_to(a, shape):
    return jnp.pad(a, [(0, s - d) for s, d in zip(shape, a.shape)])


def kernel(x, mem, g_ff1, w_ff1_in, w_ff1_out, g_mix, w_in, a_gq, a_gk, hy_conv_w, hy_conv_b, hy_f_w1, hy_f_b1,
           hy_f_w2, hy_f_b2, hy_f_w3, hy_f_b3, hy_f_w4, hy_f_freq, hy_bias, g_mem, w_mem_kv, m_gq, m_gk,
           w_br_a, w_br_b, w_br_c, w_out, g_ff2, w_ff2_in, w_ff2_out, g_post):
    batch, seq, d = x.shape
    depth = g_ff1.shape[0]
    dswa_width = len(DSWA_GROUPS) * DSWA_HEADS_PER_GROUP * HEAD_DIM
    hy_width = hy_bias.shape[-1]
    mem_width = MEM_HEADS * HEAD_DIM
    n_cols = 3 * dswa_width + (HY_ORDER + 1) * hy_width + mem_width
    n_rope = 2 * dswa_width // PROJ_TN
    n_plain = (n_cols - mem_width) // PROJ_TN - n_rope
    assert 2 * dswa_width % PROJ_TN == 0 and mem_width == PROJ_TN and n_cols % PROJ_TN == 0
    assert 3 * dswa_width % hy_width == 0 and n_cols % MERGE_TN == 0 and seq & (seq - 1) == 0
    scale = 1.0 / math.sqrt(HEAD_DIM)

    cos_t, sin_t = _rope_tables(seq)
    f_tab, g_tab = _dft_tables(seq // HY_SPLIT)
    emb, hidden = hy_f_w1.shape[1:]
    z_feat = _pad_to(_hyena_positional_features(seq, emb), (seq, LANES))
    t_col = jnp.linspace(0.0, 1.0, seq, dtype=F32)[:, None]
    deltas = jnp.linspace(math.log(HY_TARGET) / HY_SLOW_DECAY, math.log(HY_TARGET) / HY_FAST_DECAY, hy_width, dtype=F32)
    absd = jnp.abs(deltas)[None, :]
    row = lambda v: v.reshape(1, -1)

    xt = x.reshape(batch * seq, d)
    for l in range(depth):
        w123 = jnp.stack([_pad_to(w.astype(F32), (LANES, LANES)) for w in (hy_f_w1[l], hy_f_w2[l], hy_f_w3[l])])
        vecs = _pad_to(jnp.stack([hy_f_b1[l], hy_f_b2[l], hy_f_b3[l], hy_f_freq[l]]).astype(F32), (SUBLANES, LANES))
        hfilt, (w_ff1_out_b,) = _filter_mlp(z_feat, w123, vecs, hy_f_w4[l], t_col, absd, casts=(w_ff1_out[l],))
        spectra, (w_ff1_in_b,) = _filter_dft(
            hfilt, hy_bias[l].reshape(HY_ORDER, 1, hy_width), f_tab, hy_width, casts=(w_ff1_in[l],))

        xt, (w_ff2_in_b, w_ff2_out_b, w_in_b) = _ffn(
            xt, row(g_ff1[l]), w_ff1_in_b, w_ff1_out_b, row(g_post[l]), False,
            casts=(w_ff2_in[l], w_ff2_out[l], w_in[l]))

        heads = dswa_width // HEAD_DIM
        gain_cols = jnp.concatenate([jnp.tile(a_gq[l], heads) * scale, jnp.tile(a_gk[l], heads),
                                     jnp.ones((n_cols - 2 * dswa_width - mem_width,), F32),
                                     jnp.tile(m_gq[l], MEM_HEADS) * scale])[None, :]
        p5 = _mixproj(xt, row(g_mix[l]), w_in_b, gain_cols, cos_t, sin_t, n_cols, n_rope, n_plain)
        p5 = p5.reshape(batch, seq, n_cols)

        blk = dswa_width // HEAD_DIM
        y_a, (w_out_b, w_br_a_b, w_br_b_b, w_br_c_b, w_mem_kv_b) = _attn(
            p5, batch, seq, 0, blk, 2 * blk, casts=(w_out[l], w_br_a[l], w_br_b[l], w_br_c[l], w_mem_kv[l]))

        y_b = _hyena(p5, hy_conv_w[l], row(hy_conv_b[l]), f_tab, g_tab, spectra, 3 * dswa_width // hy_width, hy_width)

        y_c = _memattn(p5, mem, row(g_mem[l]), w_mem_kv_b, row(m_gk[l]), (n_cols - mem_width) // mem_width)

        t = batch * seq
        xt = _merge(xt, row(g_mix[l]), y_a.reshape(t, -1), y_b.reshape(t, -1), y_c.reshape(t, -1),
                    w_in_b, n_cols, w_br_a_b, w_br_b_b, w_br_c_b, w_out_b)

        xt, _ = _ffn(xt, row(g_ff2[l]), w_ff2_in_b, w_ff2_out_b, row(g_post[l]), True)
    return xt.reshape(batch, seq, d)
```

```python
import functools
import math

import jax
import jax.numpy as jnp
from jax import lax
from jax.experimental import pallas as pl
from jax.experimental.pallas import tpu as pltpu

F32 = jnp.float32
BF16 = jnp.bfloat16

HEAD_DIM = 128
ROPE_THETA = 500000.0
ROT_DIM = HEAD_DIM // 4
EPS = 1e-6
NEG = -1e30
DSWA_GROUPS = ((128, 1), (512, 4), (2048, 16))
DSWA_HEADS_PER_GROUP = 2
MEM_HEADS = 4
HY_ORDER = 2
HY_SHORT = 3
HY_FAST_DECAY = 0.3
HY_SLOW_DECAY = 1.5
HY_TARGET = 1e-2

LANES = 128
SUBLANES = 8
BF16_ROWS = 16
VMEM_LIMIT_BYTES = 60 * 1024 * 1024

FFN_TM = 1024
FFN_TF = 512
PROJ_TM = 512
PROJ_TN = 512
ATT_TQ = 128
ATT_BACK = 128
MEM_TQ = 1024
DFT_TF = 128
HY_SPLIT = 4
HY_CHUNK = 128
FILT_CAST_STEPS = 16
SHIFT_ROWS = 256
FILT_MLP_TL = 256
MERGE_TM = 1024
MERGE_TN = 256


def _cparams(sem):
    return pltpu.CompilerParams(dimension_semantics=sem, vmem_limit_bytes=VMEM_LIMIT_BYTES)


def _rms(x, g):
    return x * lax.rsqrt(jnp.mean(x * x, axis=-1, keepdims=True) + EPS) * g


def _ffn_kernel(x_ref, g_ref, wa_ref, wb_ref, wd_ref, gp_ref, *rest, nj, final_norm, n_cast):
    cast_in, o_ref, cast_out, h_ref = rest[:n_cast], rest[n_cast], rest[n_cast + 1:2 * n_cast + 1], rest[-1]
    j = pl.program_id(1)

    for src, dst in zip(cast_in, cast_out):
        dst[...] = src[...].astype(BF16)

    @pl.when(j == 0)
    def _():
        x = x_ref[...]
        h_ref[...] = _rms(x, g_ref[...]).astype(BF16)
        o_ref[...] = x

    h = h_ref[...]
    a = jnp.dot(h, wa_ref[...], preferred_element_type=F32)
    b = jnp.dot(h, wb_ref[...], preferred_element_type=F32)
    act = (0.5 * a * jax.nn.sigmoid(a) * b).astype(BF16)
    o_ref[...] += jnp.dot(act, wd_ref[...], preferred_element_type=F32)

    if final_norm:
        @pl.when(j == nj - 1)
        def _():
            o_ref[...] = _rms(o_ref[...], gp_ref[...])


def _cast_specs(casts, steps, flat):
    specs = []
    for a in casts:
        rows = BF16_ROWS
        while a.shape[0] % rows or a.shape[0] // rows > steps:
            rows += BF16_ROWS
        n = a.shape[0] // rows
        specs.append(pl.BlockSpec((rows, a.shape[1]), functools.partial(
            lambda *idx, n: (flat(*idx) * n // steps, 0), n=n)))
    return specs


def _ffn(x, g, w_in, w_out, g_post, final_norm, casts=()):
    t, d = x.shape
    d_ff = w_out.shape[0]
    nj = d_ff // FFN_TF
    cast_specs = _cast_specs(casts, (t // FFN_TM) * nj, lambda i, j: i * nj + j)
    kern = functools.partial(_ffn_kernel, nj=nj, final_norm=final_norm, n_cast=len(casts))
    out = pl.pallas_call(
        kern,
        grid=(t // FFN_TM, nj),
        in_specs=[
            pl.BlockSpec((FFN_TM, d), lambda i, j: (i, 0)),
            pl.BlockSpec((1, d), lambda i, j: (0, 0)),
            pl.BlockSpec((d, FFN_TF), lambda i, j: (0, j)),
            pl.BlockSpec((d, FFN_TF), lambda i, j: (0, j + nj)),
            pl.BlockSpec((FFN_TF, d), lambda i, j: (j, 0)),
            pl.BlockSpec((1, d), lambda i, j: (0, 0)),
        ] + cast_specs,
        out_specs=[pl.BlockSpec((FFN_TM, d), lambda i, j: (i, 0))] + cast_specs,
        out_shape=[jax.ShapeDtypeStruct((t, d), F32)] + [jax.ShapeDtypeStruct(a.shape, BF16) for a in casts],
        scratch_shapes=[pltpu.VMEM((FFN_TM, d), BF16)],
        compiler_params=_cparams(("arbitrary", "arbitrary")),
        name="ffn_final" if final_norm else "ffn",
    )(x, g, w_in, w_in, w_out, g_post, *casts)
    return out[0], tuple(out[1:])


def _mixproj_kernel(x0_ref, xn_ref, g_ref, w_ref, gain_ref, cos_ref, sin_ref, o_ref, h_ref, *, n_rope, n_plain):
    i = pl.program_id(0)
    slot = i % 2

    @pl.when(i == 0)
    def _():
        h_ref[0] = _rms(x0_ref[...], g_ref[...]).astype(BF16)

    h_ref[1 - slot] = _rms(xn_ref[...], g_ref[...]).astype(BF16)
    h = h_ref[slot]
    lane = lax.broadcasted_iota(jnp.int32, (PROJ_TM, HEAD_DIM), 1)
    first = lane < (ROT_DIM // 2)
    n_tiles = w_ref.shape[1] // PROJ_TN
    order = [j for j in range(n_tiles) if not n_rope <= j < n_rope + n_plain] + list(range(n_rope, n_rope + n_plain))
    for j in order:
        acc = jnp.dot(h, w_ref[:, j * PROJ_TN:(j + 1) * PROJ_TN], preferred_element_type=F32)
        plain = n_rope <= j < n_rope + n_plain
        for s in range(PROJ_TN // HEAD_DIM):
            sl = slice(j * PROJ_TN + s * HEAD_DIM, j * PROJ_TN + (s + 1) * HEAD_DIM)
            t = acc[:, s * HEAD_DIM:(s + 1) * HEAD_DIM]
            if not plain:
                t = _rms(t, gain_ref[:, sl])
            if j < n_rope:
                partner = jnp.where(first, pltpu.roll(t, HEAD_DIM - ROT_DIM // 2, axis=1),
                                    pltpu.roll(t, ROT_DIM // 2, axis=1))
                t = t * cos_ref[...] + partner * sin_ref[...]
            o_ref[:, sl] = t.astype(BF16)


def _mixproj(x, g, w, gain_cols, cos_t, sin_t, n_cols, n_rope, n_plain):
    t, d = x.shape
    n_row = t // PROJ_TM
    per_seq = cos_t.shape[0] // PROJ_TM
    kern = functools.partial(_mixproj_kernel, n_rope=n_rope, n_plain=n_plain)
    once = pl.Buffered(1)
    return pl.pallas_call(
        kern,
        grid=(n_row,),
        in_specs=[
            pl.BlockSpec((PROJ_TM, d), lambda i: (0, 0), pipeline_mode=once),
            pl.BlockSpec((PROJ_TM, d), lambda i: (jnp.minimum(i + 1, n_row - 1), 0)),
            pl.BlockSpec((1, d), lambda i: (0, 0)),
            pl.BlockSpec((d, n_cols), lambda i: (0, 0), pipeline_mode=once),
            pl.BlockSpec((1, n_cols), lambda i: (0, 0)),
            pl.BlockSpec((PROJ_TM, HEAD_DIM), lambda i: (i % per_seq, 0)),
            pl.BlockSpec((PROJ_TM, HEAD_DIM), lambda i: (i % per_seq, 0)),
        ],
        out_specs=pl.BlockSpec((PROJ_TM, n_cols), lambda i: (i, 0)),
        out_shape=jax.ShapeDtypeStruct((t, n_cols), BF16),
        scratch_shapes=[pltpu.VMEM((2, PROJ_TM, d), BF16)],
        compiler_params=_cparams(("arbitrary",)),
        name="mixproj",
    )(x, x, g, w, gain_cols, cos_t, sin_t)


def _banded_tiles(q, k, v, length):
    tiles = []
    for qb in range(length // ATT_TQ):
        i0 = qb * ATT_TQ
        lo = max(0, i0 - ATT_BACK)
        hi = min(length, i0 + ATT_TQ + ATT_BACK)
        tiles.append((i0, q[i0:i0 + ATT_TQ], k[lo:hi], v[lo:hi], lo - (i0 - ATT_BACK)))
    return tiles


def _banded_attention(tiles, tab_ref):
    scores = [lax.dot_general(q, k, (((1,), (1,)), ((), ())), preferred_element_type=F32)
              + tab_ref[:, c0:c0 + k.shape[0]] for _, q, k, _, c0 in tiles]
    maxes = [s.max(axis=-1, keepdims=True) for s in scores]
    probs = [jnp.exp(s - m) for s, m in zip(scores, maxes)]
    dens = [p.sum(axis=-1, keepdims=True) for p in probs]
    outs = [jnp.dot(p.astype(BF16), t[3], preferred_element_type=F32) / d for p, t, d in zip(probs, tiles, dens)]
    return [(o, m + jnp.log(d)) for o, m, d in zip(outs, maxes, dens)]


def _attn_kernel(*refs, seq, n_cast):
    qkv, tab_ref = refs[:9], refs[9]
    cast_in, o_ref, cast_out = refs[10:10 + n_cast], refs[10 + n_cast], refs[11 + n_cast:11 + 2 * n_cast]
    qf, kf, vf = refs[-7:-4]
    outs, lses = refs[-4:-2], refs[-2:]

    for src, dst in zip(cast_in, cast_out):
        dst[...] = src[...].astype(BF16)

    dilated = [(g, dil) for g, (_, dil) in enumerate(DSWA_GROUPS) if dil > 1]
    plain = [g for g, (_, dil) in enumerate(DSWA_GROUPS) if dil == 1]
    assert len(dilated) == len(outs) and len(plain) == 1

    for slot, (g, dil) in enumerate(dilated):
        length = seq // dil
        for src, dst in zip(qkv[3 * g:3 * g + 3], (qf, kf, vf)):
            dst[...] = src[...].astype(F32)
        tiles, rows = [], []
        for r in range(dil):
            q, k, v = (ref[pl.ds(r, length, stride=dil), :].astype(BF16) for ref in (qf, kf, vf))
            sub = _banded_tiles(q, k, v, length)
            tiles += sub
            rows += [pl.ds(t[0] * dil + r, ATT_TQ, stride=dil) for t in sub]
        for rw, (o, lse) in zip(rows, _banded_attention(tiles, tab_ref)):
            outs[slot][rw, :] = o
            lses[slot][rw, :] = jnp.broadcast_to(lse, (ATT_TQ, HEAD_DIM))

    g = plain[0]
    tiles = _banded_tiles(qkv[3 * g][...], qkv[3 * g + 1][...], qkv[3 * g + 2][...], seq)
    for (i0, *_), (o, lse) in zip(tiles, _banded_attention(tiles, tab_ref)):
        rw = slice(i0, i0 + ATT_TQ)
        parts = [(o, jnp.broadcast_to(lse, (ATT_TQ, HEAD_DIM)))] + [(a[rw, :], b[rw, :]) for a, b in zip(outs, lses)]
        m = functools.reduce(jnp.maximum, [l for _, l in parts])
        num = jnp.zeros((ATT_TQ, HEAD_DIM), F32)
        den = jnp.zeros((ATT_TQ, HEAD_DIM), F32)
        for a, l in parts:
            w = jnp.exp(l - m)
            num = num + w * a
            den = den + w
        o_ref[rw, :] = (num / den).astype(BF16)


def _attn_table():
    width = 2 * ATT_BACK + ATT_TQ
    row = jnp.arange(ATT_TQ, dtype=jnp.int32)[:, None]
    col = jnp.arange(width, dtype=jnp.int32)[None, :]
    half = DSWA_GROUPS[0][0] // (2 * DSWA_GROUPS[0][1])
    return jnp.where(jnp.abs(col - ATT_BACK - row) <= half, 0.0, NEG).astype(F32)


def _attn(p5, batch, seq, col_q, col_k, col_v, casts=()):
    halves = {win // (2 * dil) for win, dil in DSWA_GROUPS}
    assert len(halves) == 1 and halves.pop() <= ATT_BACK
    assert all(seq % dil == 0 and (seq // dil) % ATT_TQ == 0 for _, dil in DSWA_GROUPS)
    tab = _attn_table()
    hpg = DSWA_HEADS_PER_GROUP
    in_specs = []
    for g in range(len(DSWA_GROUPS)):
        for base in (col_q, col_k, col_v):
            in_specs.append(pl.BlockSpec((None, seq, HEAD_DIM),
                                         functools.partial(lambda b, h, c: (b, 0, c + h), c=base + g * hpg)))
    in_specs.append(pl.BlockSpec(tab.shape, lambda b, h: (0, 0)))
    cast_specs = _cast_specs(casts, batch * hpg, lambda b, h: b * hpg + h)
    out = pl.pallas_call(
        functools.partial(_attn_kernel, seq=seq, n_cast=len(casts)),
        grid=(batch, hpg),
        in_specs=in_specs + cast_specs,
        out_specs=[pl.BlockSpec((None, seq, HEAD_DIM), lambda b, h: (b, 0, h))] + cast_specs,
        out_shape=[jax.ShapeDtypeStruct((batch, seq, hpg * HEAD_DIM), BF16)]
                  + [jax.ShapeDtypeStruct(a.shape, BF16) for a in casts],
        scratch_shapes=[pltpu.VMEM((seq, HEAD_DIM), F32)] * 7,
        compiler_params=_cparams(("arbitrary", "arbitrary")),
        name="attn",
    )(*([p5] * 9), tab, *casts)
    return out[0], tuple(out[1:])


def _memattn_kernel(q_ref, mem_ref, g_ref, wkv_ref, gk_ref, o_ref, k_ref, v_ref):
    width = MEM_HEADS * HEAD_DIM

    @pl.when(pl.program_id(1) == 0)
    def _():
        mn = _rms(mem_ref[...], g_ref[...]).astype(BF16)
        kv = jnp.dot(mn, wkv_ref[...], preferred_element_type=F32)
        for h in range(MEM_HEADS):
            sl = slice(h * HEAD_DIM, (h + 1) * HEAD_DIM)
            k_ref[:, sl] = _rms(kv[:, sl], gk_ref[...]).astype(BF16)
        v_ref[...] = kv[:, width:].astype(BF16)

    for h in range(MEM_HEADS):
        sl = slice(h * HEAD_DIM, (h + 1) * HEAD_DIM)
        s = lax.dot_general(q_ref[:, sl], k_ref[:, sl], (((1,), (1,)), ((), ())), preferred_element_type=F32)
        p = jnp.exp(s - s.max(axis=-1, keepdims=True))
        den = p.sum(axis=-1, keepdims=True)
        o = jnp.dot(p.astype(BF16), v_ref[:, sl], preferred_element_type=F32)
        o_ref[:, sl] = (o / den).astype(BF16)


def _memattn(p5, mem, g_mem, w_kv, gk, col_block):
    batch, seq, _ = p5.shape
    n_mem, d = mem.shape[1:]
    width = MEM_HEADS * HEAD_DIM
    return pl.pallas_call(
        _memattn_kernel,
        grid=(batch, seq // MEM_TQ),
        in_specs=[
            pl.BlockSpec((None, MEM_TQ, width), lambda b, i: (b, i, col_block)),
            pl.BlockSpec((None, n_mem, d), lambda b, i: (b, 0, 0)),
            pl.BlockSpec((1, d), lambda b, i: (0, 0)),
            pl.BlockSpec((d, 2 * width), lambda b, i: (0, 0)),
            pl.BlockSpec((1, HEAD_DIM), lambda b, i: (0, 0)),
        ],
        out_specs=pl.BlockSpec((None, MEM_TQ, width), lambda b, i: (b, i, 0)),
        out_shape=jax.ShapeDtypeStruct((batch, seq, width), BF16),
        scratch_shapes=[pltpu.VMEM((n_mem, width), BF16), pltpu.VMEM((n_mem, width), BF16)],
        compiler_params=_cparams(("parallel", "arbitrary")),
        name="memattn",
    )(p5, mem, g_mem, w_kv, gk)


def _dft_kernel(f_ref, g_ref, cb, sb, cbt, sbt, cac, sac, *, seq):
    t = pl.program_id(0)
    n2 = 4 * seq
    theta = 2.0 * math.pi / n2

    def trig(m):
        ang = (m & (n2 - 1)).astype(F32) * theta
        return jnp.cos(ang), jnp.sin(ang)

    @pl.when(t == 0)
    def _():
        f_lo = lax.broadcasted_iota(jnp.int32, (DFT_TF, seq), 0)
        s = lax.broadcasted_iota(jnp.int32, (DFT_TF, seq), 1)
        cb[...], sb[...] = trig((2 * f_lo + 1) * s)
        s = lax.broadcasted_iota(jnp.int32, (seq, DFT_TF), 0)
        f_lo = lax.broadcasted_iota(jnp.int32, (seq, DFT_TF), 1)
        cbt[...], sbt[...] = trig((2 * f_lo + 1) * s)
        s = lax.broadcasted_iota(jnp.int32, (seq, LANES), 0)
        tile = lax.broadcasted_iota(jnp.int32, (seq, LANES), 1)
        cac[...], sac[...] = trig(2 * DFT_TF * tile * s)

    s = lax.broadcasted_iota(jnp.int32, (1, seq), 1)
    ca, sa = trig(2 * DFT_TF * t * s)
    f_ref[:DFT_TF, :] = (ca * cb[...] - sa * sb[...]).astype(BF16)
    f_ref[DFT_TF:, :] = (-(sa * cb[...] + ca * sb[...])).astype(BF16)
    pick = lax.broadcasted_iota(jnp.int32, (seq, LANES), 1) == t
    ca = jnp.sum(jnp.where(pick, cac[...], 0.0), axis=-1, keepdims=True)
    sa = jnp.sum(jnp.where(pick, sac[...], 0.0), axis=-1, keepdims=True)
    g_ref[:, :DFT_TF] = (ca * cbt[...] - sa * sbt[...]).astype(BF16)
    g_ref[:, DFT_TF:] = (-(sa * cbt[...] + ca * sbt[...])).astype(BF16)


def _dft_tables(seq):
    nf = seq // DFT_TF
    assert nf <= LANES
    return pl.pallas_call(
        functools.partial(_dft_kernel, seq=seq),
        grid=(nf,),
        out_specs=[pl.BlockSpec((2 * DFT_TF, seq), lambda t: (t, 0)),
                   pl.BlockSpec((seq, 2 * DFT_TF), lambda t: (0, t))],
        out_shape=[jax.ShapeDtypeStruct((2 * seq, seq), BF16), jax.ShapeDtypeStruct((seq, 2 * seq), BF16)],
        scratch_shapes=[pltpu.VMEM((DFT_TF, seq), F32)] * 2 + [pltpu.VMEM((seq, DFT_TF), F32)] * 2
                       + [pltpu.VMEM((seq, LANES), F32)] * 2,
        compiler_params=_cparams(("arbitrary",)),
        name="dft_tables",
    )()


def _filter_mlp_kernel(z_ref, w_ref, v_ref, w4_ref, t_ref, d_ref, *rest, n_cast):
    cast_in, o_ref, cast_out = rest[:n_cast], rest[n_cast], rest[n_cast + 1:-2]
    w_hi, w_lo = rest[-2:]
    for src, dst in zip(cast_in, cast_out):
        dst[...] = src[...].astype(BF16)

    split = lambda a: (a.astype(BF16), (a - a.astype(BF16).astype(F32)).astype(BF16))

    @pl.when(pl.program_id(0) == 0)
    def _():
        hidden = w4_ref.shape[0]
        for ref, part in zip((w_hi, w_lo), split(w4_ref[...])):
            ref[:hidden, :] = part
            ref[hidden:, :] = jnp.zeros((ref.shape[0] - hidden, ref.shape[1]), BF16)

    hp = lax.Precision.HIGHEST
    fr = v_ref[3:4, :]
    hh = z_ref[...]
    for n in range(3):
        hh = jnp.sin(fr * (jnp.dot(hh, w_ref[n], precision=hp, preferred_element_type=F32) + v_ref[n:n + 1, :]))
    a_hi, a_lo = split(hh)
    h = (jnp.dot(a_hi, w_hi[...], preferred_element_type=F32) + jnp.dot(a_hi, w_lo[...], preferred_element_type=F32)
         + jnp.dot(a_lo, w_hi[...], preferred_element_type=F32))
    decay = jnp.exp(-t_ref[...] * d_ref[...])
    width = decay.shape[1]
    for c in range(h.shape[1] // width):
        o_ref[:, c * width:(c + 1) * width] = h[:, c * width:(c + 1) * width] * decay


def _filter_mlp(z, w123, vecs, w4, t_col, absd, casts=()):
    seq = z.shape[0]
    tl = FILT_MLP_TL
    n_out = w4.shape[1]
    assert w4.shape[0] % BF16_ROWS == 0 and w4.shape[0] <= LANES
    full = lambda a: pl.BlockSpec(a.shape, lambda i: (0,) * a.ndim)
    cast_specs = _cast_specs(casts, seq // tl, lambda i: i)
    out = pl.pallas_call(
        functools.partial(_filter_mlp_kernel, n_cast=len(casts)),
        grid=(seq // tl,),
        in_specs=[pl.BlockSpec((tl, z.shape[1]), lambda i: (i, 0)), full(w123), full(vecs), full(w4),
                  pl.BlockSpec((tl, 1), lambda i: (i, 0)), full(absd)] + cast_specs,
        out_specs=[pl.BlockSpec((tl, n_out), lambda i: (i, 0))] + cast_specs,
        out_shape=[jax.ShapeDtypeStruct((seq, n_out), F32)] + [jax.ShapeDtypeStruct(a.shape, BF16) for a in casts],
        scratch_shapes=[pltpu.VMEM((LANES, n_out), BF16)] * 2,
        compiler_params=_cparams(("arbitrary",)),
        name="hyena_filter_mlp",
    )(z, w123, vecs, w4, t_col, absd, *casts)
    return out[0], tuple(out[1:])


_TOEPLITZ_BLOCKS = {2: ((0, -1, 1),), 4: ((0, -1, 1), (-2, -3, -1), (2, 1, 3))}


def _csub(x, y):
    return x[0] - y[0], x[1] - y[1]


def _cadd(x, y):
    return x[0] + y[0], x[1] + y[1]


def _cmul(k, x):
    return k[0] * x[0] - k[1] * x[1], k[0] * x[1] + k[1] * x[0]


def _toeplitz2(coef, u, v):
    p = _cmul(coef[0], _cadd(u, v))
    return _cadd(p, _cmul(coef[1], v)), _cadd(p, _cmul(coef[2], u))


def _filter_dft_kernel(hf_ref, hb_ref, bias_ref, f_ref, *rest, blk, n_cast):
    cast_in, k_ref, cast_out, r_ref = rest[:n_cast], rest[n_cast], rest[n_cast + 1:2 * n_cast + 1], rest[-1]
    width = hf_ref.shape[1]

    for src, dst in zip(cast_in, cast_out):
        dst[...] = src[...].astype(BF16)

    @pl.when((pl.program_id(1) == 0) & (pl.program_id(2) == 0))
    def _():
        for c, ref in enumerate((hf_ref, hb_ref)):
            for part in range(HY_SPLIT):
                col = (HY_SPLIT * c + part) * width
                r_ref[:, col:col + width] = ref[part * blk:(part + 1) * blk, :].astype(BF16)

    @pl.when(pl.program_id(2) == 0)
    def _():
        r = r_ref[...]
        re = jnp.dot(f_ref[:DFT_TF, :], r, preferred_element_type=F32)
        im = jnp.dot(f_ref[DFT_TF:, :], r, preferred_element_type=F32)

        def transform(c, j):
            col = (HY_SPLIT * c + j) * width
            return re[:, col:col + width], im[:, col:col + width]

        row = lax.broadcasted_iota(jnp.int32, (DFT_TF, width), 0)
        sgn = jnp.where((row & 1) == 0, 1.0, -1.0)
        scale = 1.0 / blk
        (f0r, f0i), (b0r, b0i) = transform(0, 0), transform(1, 0)
        seg = {0: ((f0r + b0r + bias_ref[...]) * scale, (f0i - b0i) * scale)}
        for d in range(1, HY_SPLIT):
            for c, ref, conj in ((0, hf_ref, 1.0), (1, hb_ref, -1.0)):
                (ar, ai), (pr, pi) = transform(c, d), transform(c, d - 1)
                edge = ref[(d - 1) * blk:(d - 1) * blk + 1, :]
                seg[d if c == 0 else -d] = ((ar - sgn * pi) * scale, (ai + sgn * (pr - edge)) * (conj * scale))

        blocks = _TOEPLITZ_BLOCKS[HY_SPLIT]
        first = [seg[d] for d in blocks[0]]
        n = 0
        for g, offsets in enumerate(blocks):
            a, b, c = [seg[d] for d in offsets] if g == 0 else [_csub(seg[d], x) for d, x in zip(offsets, first)]
            for cr, ci in (a, _csub(b, a), _csub(c, a)):
                k_ref[n] = cr
                k_ref[n + 1] = ci
                n += 2


def _filter_dft(hfilt, bias, f_tab, width, casts=()):
    seq = hfilt.shape[0]
    blk = seq // HY_SPLIT
    nf = blk // DFT_TF
    n_spec = 2 * 3 * len(_TOEPLITZ_BLOCKS[HY_SPLIT])
    assert DFT_TF % 2 == 0
    pace = FILT_CAST_STEPS // (HY_ORDER * nf)
    cast_specs = _cast_specs(casts, HY_ORDER * nf * pace, lambda o, f, c: (o * nf + f) * pace + c)
    kern = functools.partial(_filter_dft_kernel, blk=blk, n_cast=len(casts))
    once = pl.Buffered(1)
    out = pl.pallas_call(
        kern,
        grid=(HY_ORDER, nf, pace),
        in_specs=[pl.BlockSpec((seq, width), lambda o, f, c: (0, 2 * o), pipeline_mode=once),
                  pl.BlockSpec((seq, width), lambda o, f, c: (0, 2 * o + 1), pipeline_mode=once),
                  pl.BlockSpec((None, 1, width), lambda o, f, c: (o, 0, 0)),
                  pl.BlockSpec((2 * DFT_TF, blk), lambda o, f, c: (f, 0))] + cast_specs,
        out_specs=[pl.BlockSpec((None, n_spec, DFT_TF, width), lambda o, f, c: (o, 0, f, 0))] + cast_specs,
        out_shape=[jax.ShapeDtypeStruct((HY_ORDER, n_spec, blk, width), F32)]
                  + [jax.ShapeDtypeStruct(a.shape, BF16) for a in casts],
        scratch_shapes=[pltpu.VMEM((blk, 2 * HY_SPLIT * width), BF16)],
        compiler_params=_cparams(("arbitrary", "arbitrary", "arbitrary")),
        name="hyena_filter_dft",
    )(hfilt, hfilt, bias, f_tab, *casts)
    return out[0], tuple(out[1:])


def _short_conv(u_ref, w_ref, b_ref, part, width, emit):
    seq = u_ref.shape[0]
    sl = slice(part * width, (part + 1) * width)
    w0, w1, w2, b = w_ref[0:1, sl], w_ref[1:2, sl], w_ref[2:3, sl], b_ref[:, sl]
    n = SHIFT_ROWS
    r = lax.broadcasted_iota(jnp.int32, (n, n), 0)
    c = lax.broadcasted_iota(jnp.int32, (n, n), 1)
    down = (c == r - 1).astype(BF16)
    up = (c == r + 1).astype(BF16)
    sub = SUBLANES
    row = lax.broadcasted_iota(jnp.int32, (sub, width), 0)
    for k in range(seq // n):
        ub = u_ref[k * n:(k + 1) * n, :]
        prev = jnp.dot(down, ub, preferred_element_type=F32)
        nxt = jnp.dot(up, ub, preferred_element_type=F32)
        if k > 0:
            edge = u_ref[k * n - BF16_ROWS:k * n, :].astype(F32)[BF16_ROWS - 1:BF16_ROWS]
            prev = jnp.concatenate([prev[:sub] + jnp.where(row == 0, edge, 0.0), prev[sub:]], axis=0)
        if k < seq // n - 1:
            edge = u_ref[(k + 1) * n:(k + 1) * n + BF16_ROWS, :].astype(F32)[0:1]
            nxt = jnp.concatenate([nxt[:-sub], nxt[-sub:] + jnp.where(row == sub - 1, edge, 0.0)], axis=0)
        emit(k * n, prev * w0 + ub.astype(F32) * w1 + nxt * w2 + b)


def _hyena_kernel(u0_ref, u1_ref, u2_ref, w_ref, b_ref, f_ref, g_ref, k_ref, o_ref, z_ref, acc_ref,
                  *, nf, width, blk):
    j = pl.program_id(1)
    cw = HY_CHUNK
    gw = HY_SPLIT * cw
    groups = width // cw

    def cols(g, i):
        return slice(g * gw + i * cw, g * gw + (i + 1) * cw)

    def place(row0, rows):
        return slice(row0 % blk, row0 % blk + rows), row0 // blk

    def first_input(row0, v):
        rows, i = place(row0, v.shape[0])
        for g in range(groups):
            z_ref[rows, cols(g, i)] = v[:, g * cw:(g + 1) * cw].astype(BF16)

    def next_input(row0, v):
        rows, i = place(row0, v.shape[0])
        for g in range(groups):
            z_ref[rows, cols(g, i)] = (v[:, g * cw:(g + 1) * cw] * acc_ref[rows, cols(g, i)]).astype(BF16)

    def result(row0, v):
        rows, i = place(row0, v.shape[0])
        for g in range(groups):
            o_ref[row0:row0 + v.shape[0], g * cw:(g + 1) * cw] = (
                v[:, g * cw:(g + 1) * cw] * acc_ref[rows, cols(g, i)]).astype(BF16)

    @pl.when(j == 0)
    def _():
        _short_conv(u0_ref, w_ref, b_ref, 0, width, first_input)
        acc_ref[...] = jnp.zeros_like(acc_ref)

    spans = [slice(g * gw, (g + 1) * gw) for g in range(groups)]
    uvs = [jnp.dot(f_ref[...], z_ref[:, span], preferred_element_type=F32) for span in spans]
    ys_all = []
    for g, uv in enumerate(uvs):
        ch = slice(g * cw, (g + 1) * cw)
        z = [(uv[:DFT_TF, i * cw:(i + 1) * cw], uv[DFT_TF:, i * cw:(i + 1) * cw]) for i in range(HY_SPLIT)]
        coef = [[(k_ref[2 * (3 * b + t), :, ch], k_ref[2 * (3 * b + t) + 1, :, ch]) for t in range(3)]
                for b in range(len(_TOEPLITZ_BLOCKS[HY_SPLIT]))]
        if HY_SPLIT == 2:
            ys = _toeplitz2(coef[0], z[0], z[1])
        else:
            p1 = _toeplitz2(coef[0], _cadd(z[0], z[2]), _cadd(z[1], z[3]))
            p2 = _toeplitz2(coef[1], z[2], z[3])
            p3 = _toeplitz2(coef[2], z[0], z[1])
            ys = (_cadd(p1[0], p2[0]), _cadd(p1[1], p2[1]), _cadd(p1[0], p3[0]), _cadd(p1[1], p3[1]))
        ys_all.append(jnp.concatenate([jnp.concatenate([r for r, _ in ys], axis=1),
                                       jnp.concatenate([i for _, i in ys], axis=1)], axis=0).astype(BF16))
    for span, y in zip(spans, ys_all):
        acc_ref[:, span] += jnp.dot(g_ref[...], y, preferred_element_type=F32)

    @pl.when(j == nf - 1)
    def _():
        _short_conv(u1_ref, w_ref, b_ref, 1, width, next_input)
        acc_ref[...] = jnp.zeros_like(acc_ref)

    @pl.when(j == 2 * nf - 1)
    def _():
        _short_conv(u2_ref, w_ref, b_ref, 2, width, result)


def _hyena(p5, conv_w, conv_b, f_tab, g_tab, spectra, col_block, width):
    batch, seq, _ = p5.shape
    blk = seq // HY_SPLIT
    nf = blk // DFT_TF
    n_spec = spectra.shape[1]
    assert HY_ORDER == 2 and n_spec == 6 * len(_TOEPLITZ_BLOCKS[HY_SPLIT]) and blk % SHIFT_ROWS == 0
    assert conv_w.shape[0] == HY_SHORT == 3 and width % HY_CHUNK == 0
    kern = functools.partial(_hyena_kernel, nf=nf, width=width, blk=blk)
    u_spec = lambda part: pl.BlockSpec((None, seq, width), lambda b, j: (b, 0, col_block + part))
    return pl.pallas_call(
        kern,
        grid=(batch, HY_ORDER * nf),
        in_specs=[u_spec(0), u_spec(1), u_spec(2),
                  pl.BlockSpec(conv_w.shape, lambda b, j: (0, 0)),
                  pl.BlockSpec(conv_b.shape, lambda b, j: (0, 0)),
                  pl.BlockSpec((2 * DFT_TF, blk), lambda b, j: (j % nf, 0)),
                  pl.BlockSpec((blk, 2 * DFT_TF), lambda b, j: (0, j % nf)),
                  pl.BlockSpec((None, n_spec, DFT_TF, width), lambda b, j: (j // nf, 0, j % nf, 0))],
        out_specs=pl.BlockSpec((None, seq, width), lambda b, j: (b, 0, 0)),
        out_shape=jax.ShapeDtypeStruct((batch, seq, width), BF16),
        scratch_shapes=[pltpu.VMEM((blk, HY_SPLIT * width), BF16), pltpu.VMEM((blk, HY_SPLIT * width), F32)],
        compiler_params=_cparams(("parallel", "arbitrary")),
        name="hyena_conv",
    )(p5, p5, p5, conv_w, conv_b, f_tab, g_tab, spectra)


def _merge_kernel(x_ref, g_ref, ya_ref, yb_ref, yc_ref, wga_ref, wgb_ref, wgc_ref, wa_ref, wb_ref, wc_ref,
                  wo_ref, o_ref, h_ref, m_ref, *, nj):
    s = pl.program_id(0)
    slot = s % 2

    @pl.when(s == 0)
    def _():
        m_ref[1] = jnp.zeros(m_ref.shape[1:], BF16)
        o_ref[...] = jnp.zeros_like(o_ref)

    @pl.when(s % nj == 0)
    def _():
        h_ref[...] = _rms(x_ref[...], g_ref[...]).astype(BF16)

    first = (s - 1) % nj == 0
    base = jnp.where(first, x_ref[...], o_ref[...])
    o_ref[...] = base + jnp.dot(m_ref[1 - slot], wo_ref[...], preferred_element_type=F32)

    h = h_ref[...]

    def branch(wg_ref, y_ref, w_ref):
        gate = jax.nn.sigmoid(jnp.dot(h, wg_ref[...], preferred_element_type=F32))
        return gate * jnp.dot(y_ref[...], w_ref[...], preferred_element_type=F32)

    merged = branch(wga_ref, ya_ref, wa_ref) + branch(wgb_ref, yb_ref, wb_ref) + branch(wgc_ref, yc_ref, wc_ref)
    m_ref[slot] = merged.astype(BF16)


def _merge(x, g, ya, yb, yc, w_gate, gate_col, w_a, w_b, w_c, w_o):
    t, d = x.shape
    nj = d // MERGE_TN
    g0 = gate_col // MERGE_TN
    n = (t // MERGE_TM) * nj
    cur = lambda s: jnp.minimum(s, n - 1)
    prev = lambda s: jnp.maximum(s - 1, 0)
    rows = lambda a: pl.BlockSpec((MERGE_TM, a.shape[1]), lambda s: (cur(s) // nj, 0))
    cols = lambda a, off: pl.BlockSpec((a.shape[0], MERGE_TN), lambda s: (0, cur(s) % nj + off))
    return pl.pallas_call(
        functools.partial(_merge_kernel, nj=nj),
        grid=(n + 1,),
        in_specs=[pl.BlockSpec((MERGE_TM, d), lambda s: (cur(s) // nj, 0)),
                  pl.BlockSpec((1, d), lambda s: (0, 0)), rows(ya), rows(yb), rows(yc),
                  cols(w_gate, g0), cols(w_gate, g0 + nj), cols(w_gate, g0 + 2 * nj),
                  cols(w_a, 0), cols(w_b, 0), cols(w_c, 0),
                  pl.BlockSpec((MERGE_TN, d), lambda s: (prev(s) % nj, 0))],
        out_specs=pl.BlockSpec((MERGE_TM, d), lambda s: (prev(s) // nj, 0)),
        out_shape=jax.ShapeDtypeStruct((t, d), F32),
        scratch_shapes=[pltpu.VMEM((MERGE_TM, d), BF16), pltpu.VMEM((2, MERGE_TM, MERGE_TN), BF16)],
        compiler_params=_cparams(("arbitrary",)),
        name="merge",
    )(x, g, ya, yb, yc, w_gate, w_gate, w_gate, w_a, w_b, w_c, w_o)


def _rope_tables(seq):
    half = ROT_DIM // 2
    inv = jnp.power(ROPE_THETA, -jnp.arange(0, ROT_DIM, 2, dtype=F32) / ROT_DIM)
    ang = jnp.arange(seq, dtype=F32)[:, None] * inv[None, :]
    cos, sin = jnp.cos(ang), jnp.sin(ang)
    ones = jnp.ones((seq, HEAD_DIM - ROT_DIM), F32)
    cos_t = jnp.concatenate([cos, cos, ones], axis=1)
    sin_t = jnp.concatenate([-sin, sin, 0.0 * ones], axis=1)
    assert cos_t.shape == (seq, HEAD_DIM) and half * 2 == ROT_DIM
    return cos_t, sin_t


def _hyena_positional_features(seq, emb):
    bands = (emb - 1) // 2
    t = jnp.linspace(0.0, 1.0, seq, dtype=F32)[:, None]
    w = 2.0 * math.pi * jnp.arange(seq, dtype=F32)[:, None] / seq
    f = jnp.linspace(1e-4, bands - 1, bands, dtype=F32)[None, :]
    return jnp.concatenate([t, jnp.cos(f * w), -jnp.sin(f * w)], axis=-1)


def _pad_to(a, shape):
    return jnp.pad(a, [(0, s - d) for s, d in zip(shape, a.shape)])


def kernel(x, mem, g_ff1, w_ff1_in, w_ff1_out, g_mix, w_in, a_gq, a_gk, hy_conv_w, hy_conv_b, hy_f_w1, hy_f_b1,
           hy_f_w2, hy_f_b2, hy_f_w3, hy_f_b3, hy_f_w4, hy_f_freq, hy_bias, g_mem, w_mem_kv, m_gq, m_gk,
           w_br_a, w_br_b, w_br_c, w_out, g_ff2, w_ff2_in, w_ff2_out, g_post):
    batch, seq, d = x.shape
    depth = g_ff1.shape[0]
    dswa_width = len(DSWA_GROUPS) * DSWA_HEADS_PER_GROUP * HEAD_DIM
    hy_width = hy_bias.shape[-1]
    mem_width = MEM_HEADS * HEAD_DIM
    n_cols = 3 * dswa_width + (HY_ORDER + 1) * hy_width + mem_width
    n_rope = 2 * dswa_width // PROJ_TN
    n_plain = (n_cols - mem_width) // PROJ_TN - n_rope
    assert 2 * dswa_width % PROJ_TN == 0 and mem_width == PROJ_TN and n_cols % PROJ_TN == 0
    assert 3 * dswa_width % hy_width == 0 and n_cols % MERGE_TN == 0 and seq & (seq - 1) == 0
    scale = 1.0 / math.sqrt(HEAD_DIM)

    cos_t, sin_t = _rope_tables(seq)
    f_tab, g_tab = _dft_tables(seq // HY_SPLIT)
    emb, hidden = hy_f_w1.shape[1:]
    z_feat = _pad_to(_hyena_positional_features(seq, emb), (seq, LANES))
    t_col = jnp.linspace(0.0, 1.0, seq, dtype=F32)[:, None]
    deltas = jnp.linspace(math.log(HY_TARGET) / HY_SLOW_DECAY, math.log(HY_TARGET) / HY_FAST_DECAY, hy_width, dtype=F32)
    absd = jnp.abs(deltas)[None, :]
    row = lambda v: v.reshape(1, -1)

    xt = x.reshape(batch * seq, d)
    for l in range(depth):
        w123 = jnp.stack([_pad_to(w.astype(F32), (LANES, LANES)) for w in (hy_f_w1[l], hy_f_w2[l], hy_f_w3[l])])
        vecs = _pad_to(jnp.stack([hy_f_b1[l], hy_f_b2[l], hy_f_b3[l], hy_f_freq[l]]).astype(F32), (SUBLANES, LANES))
        hfilt, (w_ff1_out_b,) = _filter_mlp(z_feat, w123, vecs, hy_f_w4[l], t_col, absd, casts=(w_ff1_out[l],))
        spectra, (w_ff1_in_b,) = _filter_dft(
            hfilt, hy_bias[l].reshape(HY_ORDER, 1, hy_width), f_tab, hy_width, casts=(w_ff1_in[l],))

        xt, (w_ff2_in_b, w_ff2_out_b, w_in_b) = _ffn(
            xt, row(g_ff1[l]), w_ff1_in_b, w_ff1_out_b, row(g_post[l]), False,
            casts=(w_ff2_in[l], w_ff2_out[l], w_in[l]))

        heads = dswa_width // HEAD_DIM
        gain_cols = jnp.concatenate([jnp.tile(a_gq[l], heads) * scale, jnp.tile(a_gk[l], heads),
                                     jnp.ones((n_cols - 2 * dswa_width - mem_width,), F32),
                                     jnp.tile(m_gq[l], MEM_HEADS) * scale])[None, :]
        p5 = _mixproj(xt, row(g_mix[l]), w_in_b, gain_cols, cos_t, sin_t, n_cols, n_rope, n_plain)
        p5 = p5.reshape(batch, seq, n_cols)

        blk = dswa_width // HEAD_DIM
        y_a, (w_out_b, w_br_a_b, w_br_b_b, w_br_c_b, w_mem_kv_b) = _attn(
            p5, batch, seq, 0, blk, 2 * blk, casts=(w_out[l], w_br_a[l], w_br_b[l], w_br_c[l], w_mem_kv[l]))

        y_b = _hyena(p5, hy_conv_w[l], row(hy_conv_b[l]), f_tab, g_tab, spectra, 3 * dswa_width // hy_width, hy_width)

        y_c = _memattn(p5, mem, row(g_mem[l]), w_mem_kv_b, row(m_gk[l]), (n_cols - mem_width) // mem_width)

        t = batch * seq
        xt = _merge(xt, row(g_mix[l]), y_a.reshape(t, -1), y_b.reshape(t, -1), y_c.reshape(t, -1),
                    w_in_b, n_cols, w_br_a_b, w_br_b_b, w_br_c_b, w_out_b)

        xt, _ = _ffn(xt, row(g_ff2[l]), w_ff2_in_b, w_ff2_out_b, row(g_post[l]), True)
    return xt.reshape(batch, seq, d)
```

```python
import functools
import math

import jax
import jax.numpy as jnp
from jax import lax
from jax.experimental import pallas as pl
from jax.experimental.pallas import tpu as pltpu

F32 = jnp.float32
BF16 = jnp.bfloat16

HEAD_DIM = 128
ROPE_THETA = 500000.0
ROT_DIM = HEAD_DIM // 4
EPS = 1e-6
NEG = -1e30
DSWA_GROUPS = ((128, 1), (512, 4), (2048, 16))
DSWA_HEADS_PER_GROUP = 2
MEM_HEADS = 4
HY_ORDER = 2
HY_SHORT = 3
HY_FAST_DECAY = 0.3
HY_SLOW_DECAY = 1.5
HY_TARGET = 1e-2

LANES = 128
SUBLANES = 8
BF16_ROWS = 16
VMEM_LIMIT_BYTES = 60 * 1024 * 1024

FFN_TM = 1024
FFN_TF = 512
FFN_SUB = 256
PROJ_TM = 512
PROJ_TN = 512
ATT_TQ = 128
ATT_BACK = 128
MEM_TQ = 1024
DFT_TF = 128
HY_SPLIT = 4
HY_CHUNK = 128
FILT_CAST_STEPS = 16
SHIFT_ROWS = 256
FILT_MLP_TL = 256
MERGE_TM = 1024
MERGE_TN = 256


def _cparams(sem):
    return pltpu.CompilerParams(dimension_semantics=sem, vmem_limit_bytes=VMEM_LIMIT_BYTES)


def _rms(x, g):
    return x * lax.rsqrt(jnp.mean(x * x, axis=-1, keepdims=True) + EPS) * g


def _ffn_kernel(x_ref, g_ref, wa_ref, wb_ref, wd_ref, gp_ref, *rest, nj, final_norm, n_cast):
    cast_in, o_ref, cast_out, h_ref = rest[:n_cast], rest[n_cast], rest[n_cast + 1:2 * n_cast + 1], rest[-1]
    j = pl.program_id(1)

    for src, dst in zip(cast_in, cast_out):
        dst[...] = src[...].astype(BF16)

    @pl.when(j == 0)
    def _():
        x = x_ref[...]
        h_ref[...] = _rms(x, g_ref[...]).astype(BF16)
        o_ref[...] = x

    h = h_ref[...]
    sub = FFN_SUB if n_cast == 0 else FFN_TF
    pieces = [slice(c, c + sub) for c in range(0, FFN_TF, sub)]
    ups = [(jnp.dot(h, wa_ref[:, p], preferred_element_type=F32), jnp.dot(h, wb_ref[:, p], preferred_element_type=F32))
           for p in pieces]
    acts = [(0.5 * a * jax.nn.sigmoid(a) * b).astype(BF16) for a, b in ups]
    downs = [jnp.dot(act, wd_ref[p, :], preferred_element_type=F32) for act, p in zip(acts, pieces)]
    o_ref[...] += functools.reduce(lambda x, y: x + y, downs)

    if final_norm:
        @pl.when(j == nj - 1)
        def _():
            o_ref[...] = _rms(o_ref[...], gp_ref[...])


def _cast_specs(casts, steps, flat):
    specs = []
    for a in casts:
        rows = BF16_ROWS
        while a.shape[0] % rows or a.shape[0] // rows > steps:
            rows += BF16_ROWS
        n = a.shape[0] // rows
        specs.append(pl.BlockSpec((rows, a.shape[1]), functools.partial(
            lambda *idx, n: (flat(*idx) * n // steps, 0), n=n)))
    return specs


def _ffn(x, g, w_in, w_out, g_post, final_norm, casts=()):
    t, d = x.shape
    d_ff = w_out.shape[0]
    nj = d_ff // FFN_TF
    cast_specs = _cast_specs(casts, (t // FFN_TM) * nj, lambda i, j: i * nj + j)
    kern = functools.partial(_ffn_kernel, nj=nj, final_norm=final_norm, n_cast=len(casts))
    out = pl.pallas_call(
        kern,
        grid=(t // FFN_TM, nj),
        in_specs=[
            pl.BlockSpec((FFN_TM, d), lambda i, j: (i, 0)),
            pl.BlockSpec((1, d), lambda i, j: (0, 0)),
            pl.BlockSpec((d, FFN_TF), lambda i, j: (0, j)),
            pl.BlockSpec((d, FFN_TF), lambda i, j: (0, j + nj)),
            pl.BlockSpec((FFN_TF, d), lambda i, j: (j, 0)),
            pl.BlockSpec((1, d), lambda i, j: (0, 0)),
        ] + cast_specs,
        out_specs=[pl.BlockSpec((FFN_TM, d), lambda i, j: (i, 0))] + cast_specs,
        out_shape=[jax.ShapeDtypeStruct((t, d), F32)] + [jax.ShapeDtypeStruct(a.shape, BF16) for a in casts],
        scratch_shapes=[pltpu.VMEM((FFN_TM, d), BF16)],
        compiler_params=_cparams(("arbitrary", "arbitrary")),
        name="ffn_final" if final_norm else "ffn",
    )(x, g, w_in, w_in, w_out, g_post, *casts)
    return out[0], tuple(out[1:])


def _mixproj_kernel(x0_ref, xn_ref, g_ref, w_ref, gain_ref, cos_ref, sin_ref, o_ref, h_ref, *, n_rope, n_plain):
    i = pl.program_id(0)
    slot = i % 2

    @pl.when(i == 0)
    def _():
        h_ref[0] = _rms(x0_ref[...], g_ref[...]).astype(BF16)

    h_ref[1 - slot] = _rms(xn_ref[...], g_ref[...]).astype(BF16)
    h = h_ref[slot]
    lane = lax.broadcasted_iota(jnp.int32, (PROJ_TM, HEAD_DIM), 1)
    first = lane < (ROT_DIM // 2)
    n_tiles = w_ref.shape[1] // PROJ_TN
    order = [j for j in range(n_tiles) if not n_rope <= j < n_rope + n_plain] + list(range(n_rope, n_rope + n_plain))
    for j in order:
        acc = jnp.dot(h, w_ref[:, j * PROJ_TN:(j + 1) * PROJ_TN], preferred_element_type=F32)
        plain = n_rope <= j < n_rope + n_plain
        for s in range(PROJ_TN // HEAD_DIM):
            sl = slice(j * PROJ_TN + s * HEAD_DIM, j * PROJ_TN + (s + 1) * HEAD_DIM)
            t = acc[:, s * HEAD_DIM:(s + 1) * HEAD_DIM]
            if not plain:
                t = _rms(t, gain_ref[:, sl])
            if j < n_rope:
                partner = jnp.where(first, pltpu.roll(t, HEAD_DIM - ROT_DIM // 2, axis=1),
                                    pltpu.roll(t, ROT_DIM // 2, axis=1))
                t = t * cos_ref[...] + partner * sin_ref[...]
            o_ref[:, sl] = t.astype(BF16)


def _mixproj(x, g, w, gain_cols, cos_t, sin_t, n_cols, n_rope, n_plain):
    t, d = x.shape
    n_row = t // PROJ_TM
    per_seq = cos_t.shape[0] // PROJ_TM
    kern = functools.partial(_mixproj_kernel, n_rope=n_rope, n_plain=n_plain)
    once = pl.Buffered(1)
    return pl.pallas_call(
        kern,
        grid=(n_row,),
        in_specs=[
            pl.BlockSpec((PROJ_TM, d), lambda i: (0, 0), pipeline_mode=once),
            pl.BlockSpec((PROJ_TM, d), lambda i: (jnp.minimum(i + 1, n_row - 1), 0)),
            pl.BlockSpec((1, d), lambda i: (0, 0)),
            pl.BlockSpec((d, n_cols), lambda i: (0, 0), pipeline_mode=once),
            pl.BlockSpec((1, n_cols), lambda i: (0, 0)),
            pl.BlockSpec((PROJ_TM, HEAD_DIM), lambda i: (i % per_seq, 0)),
            pl.BlockSpec((PROJ_TM, HEAD_DIM), lambda i: (i % per_seq, 0)),
        ],
        out_specs=pl.BlockSpec((PROJ_TM, n_cols), lambda i: (i, 0)),
        out_shape=jax.ShapeDtypeStruct((t, n_cols), BF16),
        scratch_shapes=[pltpu.VMEM((2, PROJ_TM, d), BF16)],
        compiler_params=_cparams(("arbitrary",)),
        name="mixproj",
    )(x, x, g, w, gain_cols, cos_t, sin_t)


def _banded_tiles(q, k, v, length):
    tiles = []
    for qb in range(length // ATT_TQ):
        i0 = qb * ATT_TQ
        lo = max(0, i0 - ATT_BACK)
        hi = min(length, i0 + ATT_TQ + ATT_BACK)
        tiles.append((i0, q[i0:i0 + ATT_TQ], k[lo:hi], v[lo:hi], lo - (i0 - ATT_BACK)))
    return tiles


def _banded_attention(tiles, tab_ref):
    scores = [lax.dot_general(q, k, (((1,), (1,)), ((), ())), preferred_element_type=F32)
              + tab_ref[:, c0:c0 + k.shape[0]] for _, q, k, _, c0 in tiles]
    maxes = [s.max(axis=-1, keepdims=True) for s in scores]
    probs = [jnp.exp(s - m) for s, m in zip(scores, maxes)]
    dens = [p.sum(axis=-1, keepdims=True) for p in probs]
    outs = [jnp.dot(p.astype(BF16), t[3], preferred_element_type=F32) / d for p, t, d in zip(probs, tiles, dens)]
    return [(o, m + jnp.log(d)) for o, m, d in zip(outs, maxes, dens)]


def _attn_kernel(*refs, seq, n_cast):
    qkv, tab_ref = refs[:9], refs[9]
    cast_in, o_ref, cast_out = refs[10:10 + n_cast], refs[10 + n_cast], refs[11 + n_cast:11 + 2 * n_cast]
    qf, kf, vf = refs[-7:-4]
    outs, lses = refs[-4:-2], refs[-2:]

    for src, dst in zip(cast_in, cast_out):
        dst[...] = src[...].astype(BF16)

    dilated = [(g, dil) for g, (_, dil) in enumerate(DSWA_GROUPS) if dil > 1]
    plain = [g for g, (_, dil) in enumerate(DSWA_GROUPS) if dil == 1]
    assert len(dilated) == len(outs) and len(plain) == 1

    for slot, (g, dil) in enumerate(dilated):
        length = seq // dil
        for src, dst in zip(qkv[3 * g:3 * g + 3], (qf, kf, vf)):
            dst[...] = src[...].astype(F32)
        tiles, rows = [], []
        for r in range(dil):
            q, k, v = (ref[pl.ds(r, length, stride=dil), :].astype(BF16) for ref in (qf, kf, vf))
            sub = _banded_tiles(q, k, v, length)
            tiles += sub
            rows += [pl.ds(t[0] * dil + r, ATT_TQ, stride=dil) for t in sub]
        for rw, (o, lse) in zip(rows, _banded_attention(tiles, tab_ref)):
            outs[slot][rw, :] = o
            lses[slot][rw, :] = jnp.broadcast_to(lse, (ATT_TQ, HEAD_DIM))

    g = plain[0]
    tiles = _banded_tiles(qkv[3 * g][...], qkv[3 * g + 1][...], qkv[3 * g + 2][...], seq)
    for (i0, *_), (o, lse) in zip(tiles, _banded_attention(tiles, tab_ref)):
        rw = slice(i0, i0 + ATT_TQ)
        parts = [(o, jnp.broadcast_to(lse, (ATT_TQ, HEAD_DIM)))] + [(a[rw, :], b[rw, :]) for a, b in zip(outs, lses)]
        m = functools.reduce(jnp.maximum, [l for _, l in parts])
        num = jnp.zeros((ATT_TQ, HEAD_DIM), F32)
        den = jnp.zeros((ATT_TQ, HEAD_DIM), F32)
        for a, l in parts:
            w = jnp.exp(l - m)
            num = num + w * a
            den = den + w
        o_ref[rw, :] = (num / den).astype(BF16)


def _attn_table():
    width = 2 * ATT_BACK + ATT_TQ
    row = jnp.arange(ATT_TQ, dtype=jnp.int32)[:, None]
    col = jnp.arange(width, dtype=jnp.int32)[None, :]
    half = DSWA_GROUPS[0][0] // (2 * DSWA_GROUPS[0][1])
    return jnp.where(jnp.abs(col - ATT_BACK - row) <= half, 0.0, NEG).astype(F32)


def _attn(p5, batch, seq, col_q, col_k, col_v, casts=()):
    halves = {win // (2 * dil) for win, dil in DSWA_GROUPS}
    assert len(halves) == 1 and halves.pop() <= ATT_BACK
    assert all(seq % dil == 0 and (seq // dil) % ATT_TQ == 0 for _, dil in DSWA_GROUPS)
    tab = _attn_table()
    hpg = DSWA_HEADS_PER_GROUP
    in_specs = []
    for g in range(len(DSWA_GROUPS)):
        for base in (col_q, col_k, col_v):
            in_specs.append(pl.BlockSpec((None, seq, HEAD_DIM),
                                         functools.partial(lambda b, h, c: (b, 0, c + h), c=base + g * hpg)))
    in_specs.append(pl.BlockSpec(tab.shape, lambda b, h: (0, 0)))
    cast_specs = _cast_specs(casts, batch * hpg, lambda b, h: b * hpg + h)
    out = pl.pallas_call(
        functools.partial(_attn_kernel, seq=seq, n_cast=len(casts)),
        grid=(batch, hpg),
        in_specs=in_specs + cast_specs,
        out_specs=[pl.BlockSpec((None, seq, HEAD_DIM), lambda b, h: (b, 0, h))] + cast_specs,
        out_shape=[jax.ShapeDtypeStruct((batch, seq, hpg * HEAD_DIM), BF16)]
                  + [jax.ShapeDtypeStruct(a.shape, BF16) for a in casts],
        scratch_shapes=[pltpu.VMEM((seq, HEAD_DIM), F32)] * 7,
        compiler_params=_cparams(("arbitrary", "arbitrary")),
        name="attn",
    )(*([p5] * 9), tab, *casts)
    return out[0], tuple(out[1:])


def _memattn_kernel(q_ref, mem_ref, g_ref, wkv_ref, gk_ref, o_ref, k_ref, v_ref):
    width = MEM_HEADS * HEAD_DIM

    @pl.when(pl.program_id(1) == 0)
    def _():
        mn = _rms(mem_ref[...], g_ref[...]).astype(BF16)
        kv = jnp.dot(mn, wkv_ref[...], preferred_element_type=F32)
        for h in range(MEM_HEADS):
            sl = slice(h * HEAD_DIM, (h + 1) * HEAD_DIM)
            k_ref[:, sl] = _rms(kv[:, sl], gk_ref[...]).astype(BF16)
        v_ref[...] = kv[:, width:].astype(BF16)

    for h in range(MEM_HEADS):
        sl = slice(h * HEAD_DIM, (h + 1) * HEAD_DIM)
        s = lax.dot_general(q_ref[:, sl], k_ref[:, sl], (((1,), (1,)), ((), ())), preferred_element_type=F32)
        p = jnp.exp(s - s.max(axis=-1, keepdims=True))
        den = p.sum(axis=-1, keepdims=True)
        o = jnp.dot(p.astype(BF16), v_ref[:, sl], preferred_element_type=F32)
        o_ref[:, sl] = (o / den).astype(BF16)


def _memattn(p5, mem, g_mem, w_kv, gk, col_block):
    batch, seq, _ = p5.shape
    n_mem, d = mem.shape[1:]
    width = MEM_HEADS * HEAD_DIM
    return pl.pallas_call(
        _memattn_kernel,
        grid=(batch, seq // MEM_TQ),
        in_specs=[
            pl.BlockSpec((None, MEM_TQ, width), lambda b, i: (b, i, col_block)),
            pl.BlockSpec((None, n_mem, d), lambda b, i: (b, 0, 0)),
            pl.BlockSpec((1, d), lambda b, i: (0, 0)),
            pl.BlockSpec((d, 2 * width), lambda b, i: (0, 0)),
            pl.BlockSpec((1, HEAD_DIM), lambda b, i: (0, 0)),
        ],
        out_specs=pl.BlockSpec((None, MEM_TQ, width), lambda b, i: (b, i, 0)),
        out_shape=jax.ShapeDtypeStruct((batch, seq, width), BF16),
        scratch_shapes=[pltpu.VMEM((n_mem, width), BF16), pltpu.VMEM((n_mem, width), BF16)],
        compiler_params=_cparams(("parallel", "arbitrary")),
        name="memattn",
    )(p5, mem, g_mem, w_kv, gk)


def _dft_kernel(f_ref, g_ref, cb, sb, cbt, sbt, cac, sac, *, seq):
    t = pl.program_id(0)
    n2 = 4 * seq
    theta = 2.0 * math.pi / n2

    def trig(m):
        ang = (m & (n2 - 1)).astype(F32) * theta
        return jnp.cos(ang), jnp.sin(ang)

    @pl.when(t == 0)
    def _():
        f_lo = lax.broadcasted_iota(jnp.int32, (DFT_TF, seq), 0)
        s = lax.broadcasted_iota(jnp.int32, (DFT_TF, seq), 1)
        cb[...], sb[...] = trig((2 * f_lo + 1) * s)
        s = lax.broadcasted_iota(jnp.int32, (seq, DFT_TF), 0)
        f_lo = lax.broadcasted_iota(jnp.int32, (seq, DFT_TF), 1)
        cbt[...], sbt[...] = trig((2 * f_lo + 1) * s)
        s = lax.broadcasted_iota(jnp.int32, (seq, LANES), 0)
        tile = lax.broadcasted_iota(jnp.int32, (seq, LANES), 1)
        cac[...], sac[...] = trig(2 * DFT_TF * tile * s)

    s = lax.broadcasted_iota(jnp.int32, (1, seq), 1)
    ca, sa = trig(2 * DFT_TF * t * s)
    f_ref[:DFT_TF, :] = (ca * cb[...] - sa * sb[...]).astype(BF16)
    f_ref[DFT_TF:, :] = (-(sa * cb[...] + ca * sb[...])).astype(BF16)
    pick = lax.broadcasted_iota(jnp.int32, (seq, LANES), 1) == t
    ca = jnp.sum(jnp.where(pick, cac[...], 0.0), axis=-1, keepdims=True)
    sa = jnp.sum(jnp.where(pick, sac[...], 0.0), axis=-1, keepdims=True)
    g_ref[:, :DFT_TF] = (ca * cbt[...] - sa * sbt[...]).astype(BF16)
    g_ref[:, DFT_TF:] = (-(sa * cbt[...] + ca * sbt[...])).astype(BF16)


def _dft_tables(seq):
    nf = seq // DFT_TF
    assert nf <= LANES
    return pl.pallas_call(
        functools.partial(_dft_kernel, seq=seq),
        grid=(nf,),
        out_specs=[pl.BlockSpec((2 * DFT_TF, seq), lambda t: (t, 0)),
                   pl.BlockSpec((seq, 2 * DFT_TF), lambda t: (0, t))],
        out_shape=[jax.ShapeDtypeStruct((2 * seq, seq), BF16), jax.ShapeDtypeStruct((seq, 2 * seq), BF16)],
        scratch_shapes=[pltpu.VMEM((DFT_TF, seq), F32)] * 2 + [pltpu.VMEM((seq, DFT_TF), F32)] * 2
                       + [pltpu.VMEM((seq, LANES), F32)] * 2,
        compiler_params=_cparams(("arbitrary",)),
        name="dft_tables",
    )()


def _filter_mlp_kernel(z_ref, w_ref, v_ref, w4_ref, t_ref, d_ref, *rest, n_cast):
    cast_in, o_ref, cast_out = rest[:n_cast], rest[n_cast], rest[n_cast + 1:-2]
    w_hi, w_lo = rest[-2:]
    for src, dst in zip(cast_in, cast_out):
        dst[...] = src[...].astype(BF16)

    split = lambda a: (a.astype(BF16), (a - a.astype(BF16).astype(F32)).astype(BF16))

    @pl.when(pl.program_id(0) == 0)
    def _():
        hidden = w4_ref.shape[0]
        for ref, part in zip((w_hi, w_lo), split(w4_ref[...])):
            ref[:hidden, :] = part
            ref[hidden:, :] = jnp.zeros((ref.shape[0] - hidden, ref.shape[1]), BF16)

    hp = lax.Precision.HIGHEST
    fr = v_ref[3:4, :]
    hh = z_ref[...]
    for n in range(3):
        hh = jnp.sin(fr * (jnp.dot(hh, w_ref[n], precision=hp, preferred_element_type=F32) + v_ref[n:n + 1, :]))
    a_hi, a_lo = split(hh)
    h = (jnp.dot(a_hi, w_hi[...], preferred_element_type=F32) + jnp.dot(a_hi, w_lo[...], preferred_element_type=F32)
         + jnp.dot(a_lo, w_hi[...], preferred_element_type=F32))
    decay = jnp.exp(-t_ref[...] * d_ref[...])
    width = decay.shape[1]
    for c in range(h.shape[1] // width):
        o_ref[:, c * width:(c + 1) * width] = h[:, c * width:(c + 1) * width] * decay


def _filter_mlp(z, w123, vecs, w4, t_col, absd, casts=()):
    seq = z.shape[0]
    tl = FILT_MLP_TL
    n_out = w4.shape[1]
    assert w4.shape[0] % BF16_ROWS == 0 and w4.shape[0] <= LANES
    full = lambda a: pl.BlockSpec(a.shape, lambda i: (0,) * a.ndim)
    cast_specs = _cast_specs(casts, seq // tl, lambda i: i)
    out = pl.pallas_call(
        functools.partial(_filter_mlp_kernel, n_cast=len(casts)),
        grid=(seq // tl,),
        in_specs=[pl.BlockSpec((tl, z.shape[1]), lambda i: (i, 0)), full(w123), full(vecs), full(w4),
                  pl.BlockSpec((tl, 1), lambda i: (i, 0)), full(absd)] + cast_specs,
        out_specs=[pl.BlockSpec((tl, n_out), lambda i: (i, 0))] + cast_specs,
        out_shape=[jax.ShapeDtypeStruct((seq, n_out), F32)] + [jax.ShapeDtypeStruct(a.shape, BF16) for a in casts],
        scratch_shapes=[pltpu.VMEM((LANES, n_out), BF16)] * 2,
        compiler_params=_cparams(("arbitrary",)),
        name="hyena_filter_mlp",
    )(z, w123, vecs, w4, t_col, absd, *casts)
    return out[0], tuple(out[1:])


_TOEPLITZ_BLOCKS = {2: ((0, -1, 1),), 4: ((0, -1, 1), (-2, -3, -1), (2, 1, 3))}


def _csub(x, y):
    return x[0] - y[0], x[1] - y[1]


def _cadd(x, y):
    return x[0] + y[0], x[1] + y[1]


def _cmul(k, x):
    return k[0] * x[0] - k[1] * x[1], k[0] * x[1] + k[1] * x[0]


def _toeplitz2(coef, u, v):
    p = _cmul(coef[0], _cadd(u, v))
    return _cadd(p, _cmul(coef[1], v)), _cadd(p, _cmul(coef[2], u))


def _filter_dft_kernel(hf_ref, hb_ref, bias_ref, f_ref, *rest, blk, n_cast):
    cast_in, k_ref, cast_out, r_ref = rest[:n_cast], rest[n_cast], rest[n_cast + 1:2 * n_cast + 1], rest[-1]
    width = hf_ref.shape[1]

    for src, dst in zip(cast_in, cast_out):
        dst[...] = src[...].astype(BF16)

    @pl.when((pl.program_id(1) == 0) & (pl.program_id(2) == 0))
    def _():
        for c, ref in enumerate((hf_ref, hb_ref)):
            for part in range(HY_SPLIT):
                col = (HY_SPLIT * c + part) * width
                r_ref[:, col:col + width] = ref[part * blk:(part + 1) * blk, :].astype(BF16)

    @pl.when(pl.program_id(2) == 0)
    def _():
        r = r_ref[...]
        re = jnp.dot(f_ref[:DFT_TF, :], r, preferred_element_type=F32)
        im = jnp.dot(f_ref[DFT_TF:, :], r, preferred_element_type=F32)

        def transform(c, j):
            col = (HY_SPLIT * c + j) * width
            return re[:, col:col + width], im[:, col:col + width]

        row = lax.broadcasted_iota(jnp.int32, (DFT_TF, width), 0)
        sgn = jnp.where((row & 1) == 0, 1.0, -1.0)
        scale = 1.0 / blk
        (f0r, f0i), (b0r, b0i) = transform(0, 0), transform(1, 0)
        seg = {0: ((f0r + b0r + bias_ref[...]) * scale, (f0i - b0i) * scale)}
        for d in range(1, HY_SPLIT):
            for c, ref, conj in ((0, hf_ref, 1.0), (1, hb_ref, -1.0)):
                (ar, ai), (pr, pi) = transform(c, d), transform(c, d - 1)
                edge = ref[(d - 1) * blk:(d - 1) * blk + 1, :]
                seg[d if c == 0 else -d] = ((ar - sgn * pi) * scale, (ai + sgn * (pr - edge)) * (conj * scale))

        blocks = _TOEPLITZ_BLOCKS[HY_SPLIT]
        first = [seg[d] for d in blocks[0]]
        n = 0
        for g, offsets in enumerate(blocks):
            a, b, c = [seg[d] for d in offsets] if g == 0 else [_csub(seg[d], x) for d, x in zip(offsets, first)]
            for cr, ci in (a, _csub(b, a), _csub(c, a)):
                k_ref[n] = cr
                k_ref[n + 1] = ci
                n += 2


def _filter_dft(hfilt, bias, f_tab, width, casts=()):
    seq = hfilt.shape[0]
    blk = seq // HY_SPLIT
    nf = blk // DFT_TF
    n_spec = 2 * 3 * len(_TOEPLITZ_BLOCKS[HY_SPLIT])
    assert DFT_TF % 2 == 0
    pace = FILT_CAST_STEPS // (HY_ORDER * nf)
    cast_specs = _cast_specs(casts, HY_ORDER * nf * pace, lambda o, f, c: (o * nf + f) * pace + c)
    kern = functools.partial(_filter_dft_kernel, blk=blk, n_cast=len(casts))
    once = pl.Buffered(1)
    out = pl.pallas_call(
        kern,
        grid=(HY_ORDER, nf, pace),
        in_specs=[pl.BlockSpec((seq, width), lambda o, f, c: (0, 2 * o), pipeline_mode=once),
                  pl.BlockSpec((seq, width), lambda o, f, c: (0, 2 * o + 1), pipeline_mode=once),
                  pl.BlockSpec((None, 1, width), lambda o, f, c: (o, 0, 0)),
                  pl.BlockSpec((2 * DFT_TF, blk), lambda o, f, c: (f, 0))] + cast_specs,
        out_specs=[pl.BlockSpec((None, n_spec, DFT_TF, width), lambda o, f, c: (o, 0, f, 0))] + cast_specs,
        out_shape=[jax.ShapeDtypeStruct((HY_ORDER, n_spec, blk, width), F32)]
                  + [jax.ShapeDtypeStruct(a.shape, BF16) for a in casts],
        scratch_shapes=[pltpu.VMEM((blk, 2 * HY_SPLIT * width), BF16)],
        compiler_params=_cparams(("arbitrary", "arbitrary", "arbitrary")),
        name="hyena_filter_dft",
    )(hfilt, hfilt, bias, f_tab, *casts)
    return out[0], tuple(out[1:])


def _short_conv(u_ref, w_ref, b_ref, part, width, emit):
    seq = u_ref.shape[0]
    sl = slice(part * width, (part + 1) * width)
    w0, w1, w2, b = w_ref[0:1, sl], w_ref[1:2, sl], w_ref[2:3, sl], b_ref[:, sl]
    n = SHIFT_ROWS
    r = lax.broadcasted_iota(jnp.int32, (n, n), 0)
    c = lax.broadcasted_iota(jnp.int32, (n, n), 1)
    down = (c == r - 1).astype(BF16)
    up = (c == r + 1).astype(BF16)
    sub = SUBLANES
    row = lax.broadcasted_iota(jnp.int32, (sub, width), 0)
    for k in range(seq // n):
        ub = u_ref[k * n:(k + 1) * n, :]
        prev = jnp.dot(down, ub, preferred_element_type=F32)
        nxt = jnp.dot(up, ub, preferred_element_type=F32)
        if k > 0:
            edge = u_ref[k * n - BF16_ROWS:k * n, :].astype(F32)[BF16_ROWS - 1:BF16_ROWS]
            prev = jnp.concatenate([prev[:sub] + jnp.where(row == 0, edge, 0.0), prev[sub:]], axis=0)
        if k < seq // n - 1:
            edge = u_ref[(k + 1) * n:(k + 1) * n + BF16_ROWS, :].astype(F32)[0:1]
            nxt = jnp.concatenate([nxt[:-sub], nxt[-sub:] + jnp.where(row == sub - 1, edge, 0.0)], axis=0)
        emit(k * n, prev * w0 + ub.astype(F32) * w1 + nxt * w2 + b)


def _hyena_kernel(u0_ref, u1_ref, u2_ref, w_ref, b_ref, f_ref, g_ref, k_ref, o_ref, z_ref, acc_ref,
                  *, nf, width, blk):
    j = pl.program_id(1)
    cw = HY_CHUNK
    gw = HY_SPLIT * cw
    groups = width // cw

    def cols(g, i):
        return slice(g * gw + i * cw, g * gw + (i + 1) * cw)

    def place(row0, rows):
        return slice(row0 % blk, row0 % blk + rows), row0 // blk

    def first_input(row0, v):
        rows, i = place(row0, v.shape[0])
        for g in range(groups):
            z_ref[rows, cols(g, i)] = v[:, g * cw:(g + 1) * cw].astype(BF16)

    def next_input(row0, v):
        rows, i = place(row0, v.shape[0])
        for g in range(groups):
            z_ref[rows, cols(g, i)] = (v[:, g * cw:(g + 1) * cw] * acc_ref[rows, cols(g, i)]).astype(BF16)

    def result(row0, v):
        rows, i = place(row0, v.shape[0])
        for g in range(groups):
            o_ref[row0:row0 + v.shape[0], g * cw:(g + 1) * cw] = (
                v[:, g * cw:(g + 1) * cw] * acc_ref[rows, cols(g, i)]).astype(BF16)

    @pl.when(j == 0)
    def _():
        _short_conv(u0_ref, w_ref, b_ref, 0, width, first_input)
        acc_ref[...] = jnp.zeros_like(acc_ref)

    spans = [slice(g * gw, (g + 1) * gw) for g in range(groups)]
    uvs = [jnp.dot(f_ref[...], z_ref[:, span], preferred_element_type=F32) for span in spans]
    ys_all = []
    for g, uv in enumerate(uvs):
        ch = slice(g * cw, (g + 1) * cw)
        z = [(uv[:DFT_TF, i * cw:(i + 1) * cw], uv[DFT_TF:, i * cw:(i + 1) * cw]) for i in range(HY_SPLIT)]
        coef = [[(k_ref[2 * (3 * b + t), :, ch], k_ref[2 * (3 * b + t) + 1, :, ch]) for t in range(3)]
                for b in range(len(_TOEPLITZ_BLOCKS[HY_SPLIT]))]
        if HY_SPLIT == 2:
            ys = _toeplitz2(coef[0], z[0], z[1])
        else:
            p1 = _toeplitz2(coef[0], _cadd(z[0], z[2]), _cadd(z[1], z[3]))
            p2 = _toeplitz2(coef[1], z[2], z[3])
            p3 = _toeplitz2(coef[2], z[0], z[1])
            ys = (_cadd(p1[0], p2[0]), _cadd(p1[1], p2[1]), _cadd(p1[0], p3[0]), _cadd(p1[1], p3[1]))
        ys_all.append(jnp.concatenate([jnp.concatenate([r for r, _ in ys], axis=1),
                                       jnp.concatenate([i for _, i in ys], axis=1)], axis=0).astype(BF16))
    for span, y in zip(spans, ys_all):
        acc_ref[:, span] += jnp.dot(g_ref[...], y, preferred_element_type=F32)

    @pl.when(j == nf - 1)
    def _():
        _short_conv(u1_ref, w_ref, b_ref, 1, width, next_input)
        acc_ref[...] = jnp.zeros_like(acc_ref)

    @pl.when(j == 2 * nf - 1)
    def _():
        _short_conv(u2_ref, w_ref, b_ref, 2, width, result)


def _hyena(p5, conv_w, conv_b, f_tab, g_tab, spectra, col_block, width):
    batch, seq, _ = p5.shape
    blk = seq // HY_SPLIT
    nf = blk // DFT_TF
    n_spec = spectra.shape[1]
    assert HY_ORDER == 2 and n_spec == 6 * len(_TOEPLITZ_BLOCKS[HY_SPLIT]) and blk % SHIFT_ROWS == 0
    assert conv_w.shape[0] == HY_SHORT == 3 and width % HY_CHUNK == 0
    kern = functools.partial(_hyena_kernel, nf=nf, width=width, blk=blk)
    u_spec = lambda part: pl.BlockSpec((None, seq, width), lambda b, j: (b, 0, col_block + part))
    return pl.pallas_call(
        kern,
        grid=(batch, HY_ORDER * nf),
        in_specs=[u_spec(0), u_spec(1), u_spec(2),
                  pl.BlockSpec(conv_w.shape, lambda b, j: (0, 0)),
                  pl.BlockSpec(conv_b.shape, lambda b, j: (0, 0)),
                  pl.BlockSpec((2 * DFT_TF, blk), lambda b, j: (j % nf, 0)),
                  pl.BlockSpec((blk, 2 * DFT_TF), lambda b, j: (0, j % nf)),
                  pl.BlockSpec((None, n_spec, DFT_TF, width), lambda b, j: (j // nf, 0, j % nf, 0))],
        out_specs=pl.BlockSpec((None, seq, width), lambda b, j: (b, 0, 0)),
        out_shape=jax.ShapeDtypeStruct((batch, seq, width), BF16),
        scratch_shapes=[pltpu.VMEM((blk, HY_SPLIT * width), BF16), pltpu.VMEM((blk, HY_SPLIT * width), F32)],
        compiler_params=_cparams(("parallel", "arbitrary")),
        name="hyena_conv",
    )(p5, p5, p5, conv_w, conv_b, f_tab, g_tab, spectra)


def _merge_kernel(x_ref, g_ref, ya_ref, yb_ref, yc_ref, wga_ref, wgb_ref, wgc_ref, wa_ref, wb_ref, wc_ref,
                  wo_ref, o_ref, h_ref, m_ref, *, nj):
    s = pl.program_id(0)
    slot = s % 2

    @pl.when(s == 0)
    def _():
        m_ref[1] = jnp.zeros(m_ref.shape[1:], BF16)
        o_ref[...] = jnp.zeros_like(o_ref)

    @pl.when(s % nj == 0)
    def _():
        h_ref[...] = _rms(x_ref[...], g_ref[...]).astype(BF16)

    first = (s - 1) % nj == 0
    base = jnp.where(first, x_ref[...], o_ref[...])
    o_ref[...] = base + jnp.dot(m_ref[1 - slot], wo_ref[...], preferred_element_type=F32)

    h = h_ref[...]

    def branch(wg_ref, y_ref, w_ref):
        gate = jax.nn.sigmoid(jnp.dot(h, wg_ref[...], preferred_element_type=F32))
        return gate * jnp.dot(y_ref[...], w_ref[...], preferred_element_type=F32)

    merged = branch(wga_ref, ya_ref, wa_ref) + branch(wgb_ref, yb_ref, wb_ref) + branch(wgc_ref, yc_ref, wc_ref)
    m_ref[slot] = merged.astype(BF16)


def _merge(x, g, ya, yb, yc, w_gate, gate_col, w_a, w_b, w_c, w_o):
    t, d = x.shape
    nj = d // MERGE_TN
    g0 = gate_col // MERGE_TN
    n = (t // MERGE_TM) * nj
    cur = lambda s: jnp.minimum(s, n - 1)
    prev = lambda s: jnp.maximum(s - 1, 0)
    rows = lambda a: pl.BlockSpec((MERGE_TM, a.shape[1]), lambda s: (cur(s) // nj, 0))
    cols = lambda a, off: pl.BlockSpec((a.shape[0], MERGE_TN), lambda s: (0, cur(s) % nj + off))
    return pl.pallas_call(
        functools.partial(_merge_kernel, nj=nj),
        grid=(n + 1,),
        in_specs=[pl.BlockSpec((MERGE_TM, d), lambda s: (cur(s) // nj, 0)),
                  pl.BlockSpec((1, d), lambda s: (0, 0)), rows(ya), rows(yb), rows(yc),
                  cols(w_gate, g0), cols(w_gate, g0 + nj), cols(w_gate, g0 + 2 * nj),
                  cols(w_a, 0), cols(w_b, 0), cols(w_c, 0),
                  pl.BlockSpec((MERGE_TN, d), lambda s: (prev(s) % nj, 0))],
        out_specs=pl.BlockSpec((MERGE_TM, d), lambda s: (prev(s) // nj, 0)),
        out_shape=jax.ShapeDtypeStruct((t, d), F32),
        scratch_shapes=[pltpu.VMEM((MERGE_TM, d), BF16), pltpu.VMEM((2, MERGE_TM, MERGE_TN), BF16)],
        compiler_params=_cparams(("arbitrary",)),
        name="merge",
    )(x, g, ya, yb, yc, w_gate, w_gate, w_gate, w_a, w_b, w_c, w_o)


def _rope_tables(seq):
    half = ROT_DIM // 2
    inv = jnp.power(ROPE_THETA, -jnp.arange(0, ROT_DIM, 2, dtype=F32) / ROT_DIM)
    ang = jnp.arange(seq, dtype=F32)[:, None] * inv[None, :]
    cos, sin = jnp.cos(ang), jnp.sin(ang)
    ones = jnp.ones((seq, HEAD_DIM - ROT_DIM), F32)
    cos_t = jnp.concatenate([cos, cos, ones], axis=1)
    sin_t = jnp.concatenate([-sin, sin, 0.0 * ones], axis=1)
    assert cos_t.shape == (seq, HEAD_DIM) and half * 2 == ROT_DIM
    return cos_t, sin_t


def _hyena_positional_features(seq, emb):
    bands = (emb - 1) // 2
    t = jnp.linspace(0.0, 1.0, seq, dtype=F32)[:, None]
    w = 2.0 * math.pi * jnp.arange(seq, dtype=F32)[:, None] / seq
    f = jnp.linspace(1e-4, bands - 1, bands, dtype=F32)[None, :]
    return jnp.concatenate([t, jnp.cos(f * w), -jnp.sin(f * w)], axis=-1)


def _pad_to(a, shape):
    return jnp.pad(a, [(0, s - d) for s, d in zip(shape, a.shape)])


def kernel(x, mem, g_ff1, w_ff1_in, w_ff1_out, g_mix, w_in, a_gq, a_gk, hy_conv_w, hy_conv_b, hy_f_w1, hy_f_b1,
           hy_f_w2, hy_f_b2, hy_f_w3, hy_f_b3, hy_f_w4, hy_f_freq, hy_bias, g_mem, w_mem_kv, m_gq, m_gk,
           w_br_a, w_br_b, w_br_c, w_out, g_ff2, w_ff2_in, w_ff2_out, g_post):
    batch, seq, d = x.shape
    depth = g_ff1.shape[0]
    dswa_width = len(DSWA_GROUPS) * DSWA_HEADS_PER_GROUP * HEAD_DIM
    hy_width = hy_bias.shape[-1]
    mem_width = MEM_HEADS * HEAD_DIM
    n_cols = 3 * dswa_width + (HY_ORDER + 1) * hy_width + mem_width
    n_rope = 2 * dswa_width // PROJ_TN
    n_plain = (n_cols - mem_width) // PROJ_TN - n_rope
    assert 2 * dswa_width % PROJ_TN == 0 and mem_width == PROJ_TN and n_cols % PROJ_TN == 0
    assert 3 * dswa_width % hy_width == 0 and n_cols % MERGE_TN == 0 and seq & (seq - 1) == 0
    scale = 1.0 / math.sqrt(HEAD_DIM)

    cos_t, sin_t = _rope_tables(seq)
    f_tab, g_tab = _dft_tables(seq // HY_SPLIT)
    emb, hidden = hy_f_w1.shape[1:]
    z_feat = _pad_to(_hyena_positional_features(seq, emb), (seq, LANES))
    t_col = jnp.linspace(0.0, 1.0, seq, dtype=F32)[:, None]
    deltas = jnp.linspace(math.log(HY_TARGET) / HY_SLOW_DECAY, math.log(HY_TARGET) / HY_FAST_DECAY, hy_width, dtype=F32)
    absd = jnp.abs(deltas)[None, :]
    row = lambda v: v.reshape(1, -1)

    xt = x.reshape(batch * seq, d)
    for l in range(depth):
        w123 = jnp.stack([_pad_to(w.astype(F32), (LANES, LANES)) for w in (hy_f_w1[l], hy_f_w2[l], hy_f_w3[l])])
        vecs = _pad_to(jnp.stack([hy_f_b1[l], hy_f_b2[l], hy_f_b3[l], hy_f_freq[l]]).astype(F32), (SUBLANES, LANES))
        hfilt, (w_ff1_out_b,) = _filter_mlp(z_feat, w123, vecs, hy_f_w4[l], t_col, absd, casts=(w_ff1_out[l],))
        spectra, (w_ff1_in_b,) = _filter_dft(
            hfilt, hy_bias[l].reshape(HY_ORDER, 1, hy_width), f_tab, hy_width, casts=(w_ff1_in[l],))

        xt, (w_ff2_in_b, w_ff2_out_b, w_in_b) = _ffn(
            xt, row(g_ff1[l]), w_ff1_in_b, w_ff1_out_b, row(g_post[l]), False,
            casts=(w_ff2_in[l], w_ff2_out[l], w_in[l]))

        heads = dswa_width // HEAD_DIM
        gain_cols = jnp.concatenate([jnp.tile(a_gq[l], heads) * scale, jnp.tile(a_gk[l], heads),
                                     jnp.ones((n_cols - 2 * dswa_width - mem_width,), F32),
                                     jnp.tile(m_gq[l], MEM_HEADS) * scale])[None, :]
        p5 = _mixproj(xt, row(g_mix[l]), w_in_b, gain_cols, cos_t, sin_t, n_cols, n_rope, n_plain)
        p5 = p5.reshape(batch, seq, n_cols)

        blk = dswa_width // HEAD_DIM
        y_a, (w_out_b, w_br_a_b, w_br_b_b, w_br_c_b, w_mem_kv_b) = _attn(
            p5, batch, seq, 0, blk, 2 * blk, casts=(w_out[l], w_br_a[l], w_br_b[l], w_br_c[l], w_mem_kv[l]))

        y_b = _hyena(p5, hy_conv_w[l], row(hy_conv_b[l]), f_tab, g_tab, spectra, 3 * dswa_width // hy_width, hy_width)

        y_c = _memattn(p5, mem, row(g_mem[l]), w_mem_kv_b, row(m_gk[l]), (n_cols - mem_width) // mem_width)

        t = batch * seq
        xt = _merge(xt, row(g_mix[l]), y_a.reshape(t, -1), y_b.reshape(t, -1), y_c.reshape(t, -1),
                    w_in_b, n_cols, w_br_a_b, w_br_b_b, w_br_c_b, w_out_b)

        xt, _ = _ffn(xt, row(g_ff2[l]), w_ff2_in_b, w_ff2_out_b, row(g_post[l]), True)
    return xt.reshape(batch, seq, d)
```
